```python
import math
import jax, jax.numpy as jnp
from jax import lax
import numpy as np

D_MODEL = 1024
BATCH = 8
SEQ = 2048
DEPTH = 1
DEC_BATCH = 128
DEC_SEQ = 4
PAST_LEN = 16384
PAGE_SIZE = 128

H_A = 4
DK_A = 128
DV_A = 128
D_A = H_A * DK_A
DV_TOT_A = H_A * DV_A
HGRN_CHUNK = 64
H_B = 8
HD_B = 64
D_B = H_B * HD_B
LORA_W = 64
LORA_A = 64
LORA_G = 128
RW_START = 2 * D_A + 2 * DV_TOT_A
RW_COLS = 3 * D_B + LORA_W + LORA_A + LORA_G
GATE_START = RW_START + RW_COLS
D_IN = GATE_START + 2 * D_MODEL
N_EXPERTS = 32
TOP_K = 4
D_FF = 1024
SWIGLU_LIMIT = 7.0
SWIGLU_ALPHA = 1.702
MOE_BLOCK = 128
EPS = 1e-6
RWKV_GN_EPS = 64e-5

kernel_name = "hgrn2_rwkv7_gated_moe_decode_step"


def rmsnorm(x, w):
    x32 = x.astype(jnp.float32)
    y = x32 * lax.rsqrt(jnp.mean(x32 * x32, axis=-1, keepdims=True) + EPS)
    return (y * w.astype(jnp.float32)).astype(x.dtype)


def hgrn2_chunked(q, k, v, log_f, s0):
    B, T, H, K = q.shape
    V = v.shape[-1]
    C = math.gcd(T, HGRN_CHUNK)
    n = T // C

    def to_blocks(a):
        return a.reshape(B, n, C, H, a.shape[-1]).transpose(1, 0, 3, 2, 4)

    causal = jnp.tril(jnp.ones((C, C), dtype=bool))[:, :, None]

    def step(S, blk):
        qc, kc, vc, gc = blk
        b = jnp.cumsum(gc, axis=2)
        o = jnp.einsum("bhtk,bhkv->bhtv", qc * jnp.exp(b), S)
        decay = jnp.exp(jnp.where(causal, b[:, :, :, None, :] - b[:, :, None, :, :], -jnp.inf))
        scores = jnp.einsum("bhtk,bhsk,bhtsk->bhts", qc, kc, decay)
        o = o + jnp.einsum("bhts,bhsv->bhtv", scores, vc)
        b_last = b[:, :, -1]
        S = S * jnp.exp(b_last)[..., None] + jnp.einsum(
            "bhsk,bhsv->bhkv", kc * jnp.exp(b_last[:, :, None] - b), vc)
        return S, o

    S, o = lax.scan(step, s0, (to_blocks(q), to_blocks(k), to_blocks(v), to_blocks(log_f)))
    return o.transpose(1, 0, 3, 2, 4).reshape(B, T, H, V), S


def rwkv7_scan(r, w, k, v, kk, a, s0):
    def step(S, inp):
        r_t, w_t, k_t, v_t, kk_t, a_t = inp
        sa = jnp.einsum("bhvk,bhk->bhv", S, -kk_t)
        S = (S * w_t[:, :, None, :] + sa[..., None] * (kk_t * a_t)[:, :, None, :]
             + v_t[..., None] * k_t[:, :, None, :])
        return S, jnp.einsum("bhvk,bhk->bhv", S, r_t)

    xs = tuple(jnp.moveaxis(t, 1, 0) for t in (r, w, k, v, kk, a))
    S, y = lax.scan(step, s0, xs)
    return jnp.moveaxis(y, 0, 1), S


def moe(x2d, router_w, router_b, w_gu, b_gu, w_down, b_down):
    N, D = x2d.shape
    logits = x2d.astype(jnp.float32) @ router_w.astype(jnp.float32) + router_b.astype(jnp.float32)
    top_val, top_idx = lax.top_k(logits, TOP_K)
    gates = jax.nn.softmax(top_val, axis=-1).astype(x2d.dtype)
    NK = N * TOP_K
    e_flat = top_idx.reshape(-1).astype(jnp.int32)
    tok_flat = jnp.arange(NK, dtype=jnp.int32) // TOP_K
    order = jnp.argsort(e_flat * NK + jnp.arange(NK, dtype=jnp.int32))
    e_s, tok_s, g_s = e_flat[order], tok_flat[order], gates.reshape(-1)[order]
    counts = jnp.zeros((N_EXPERTS,), jnp.int32).at[e_flat].add(1)
    start = jnp.cumsum(counts) - counts
    padded = (counts + MOE_BLOCK - 1) // MOE_BLOCK * MOE_BLOCK
    pend = jnp.cumsum(padded)
    pstart = pend - padded
    dest = pstart[e_s] + (jnp.arange(NK, dtype=jnp.int32) - start[e_s])
    n_blocks = (NK + N_EXPERTS * (MOE_BLOCK - 1) + MOE_BLOCK - 1) // MOE_BLOCK
    M = n_blocks * MOE_BLOCK
    buf_tok = jnp.full((M,), N, jnp.int32).at[dest].set(tok_s)
    x_pad = jnp.concatenate([x2d, jnp.zeros((1, D), x2d.dtype)], axis=0)
    block_expert = jnp.minimum(
        jnp.searchsorted(pend, jnp.arange(n_blocks, dtype=jnp.int32) * MOE_BLOCK, side="right"),
        N_EXPERTS - 1).astype(jnp.int32)

    def expert_block(args):
        tok_b, e = args
        h = x_pad[tok_b] @ w_gu[e] + b_gu[e]
        glu = jnp.minimum(h[:, 0::2], SWIGLU_LIMIT)
        lin = jnp.clip(h[:, 1::2], -SWIGLU_LIMIT, SWIGLU_LIMIT)
        act = glu * jax.nn.sigmoid(SWIGLU_ALPHA * glu) * (lin + 1.0)
        return act @ w_down[e] + b_down[e]

    out = lax.map(expert_block, (buf_tok.reshape(n_blocks, MOE_BLOCK), block_expert)).reshape(M, D)
    return jax.ops.segment_sum(out[dest] * g_s[:, None], tok_s, num_segments=N)


def decoder_layer(x, s_hgrn, s_rwkv, shift_prev, lb, norm1_w, w_in, hgrn_gnorm_w, rwkv_mu,
                  rwkv_w0, rwkv_w2, rwkv_a0, rwkv_a2, rwkv_g2, rwkv_k_k, rwkv_k_a, rwkv_r_k,
                  rwkv_ln_w, rwkv_ln_b, w_branch_a, w_branch_b, w_out, norm2_w, router_w, router_b,
                  expert_w_gu, expert_b_gu, expert_w_down, expert_b_down):
    B, T, _ = x.shape
    f32 = jnp.float32
    xn = rmsnorm(x, norm1_w)
    proj = xn @ w_in
    hq = proj[..., :D_A].astype(f32)
    hf = proj[..., D_A:2 * D_A].astype(f32)
    hi = proj[..., 2 * D_A:2 * D_A + DV_TOT_A].astype(f32)
    hg = proj[..., 2 * D_A + DV_TOT_A:RW_START].astype(f32)
    rw = proj[..., RW_START:GATE_START].astype(f32)
    gate_a = proj[..., GATE_START:GATE_START + D_MODEL]
    gate_b = proj[..., GATE_START + D_MODEL:]

    lb = lb.astype(f32)
    f = lb + (1.0 - lb) * jax.nn.sigmoid(hf)
    o_a, s_hgrn_new = hgrn2_chunked(
        jax.nn.silu(hq).reshape(B, T, H_A, DK_A), (1.0 - f).reshape(B, T, H_A, DK_A),
        hi.reshape(B, T, H_A, DV_A), jnp.log(f).reshape(B, T, H_A, DK_A), s_hgrn.astype(f32))
    o_a = (o_a * lax.rsqrt(jnp.mean(o_a * o_a, axis=-1, keepdims=True) + EPS)
           * hgrn_gnorm_w.astype(f32) * jax.nn.silu(hg.reshape(B, T, H_A, DV_A)))
    o_a = o_a.reshape(B, T, DV_TOT_A)

    prev = (shift_prev.astype(xn.dtype) @ w_in[:, RW_START:GATE_START]).astype(f32)
    rw_prev = jnp.concatenate([prev[:, None], rw[:, :-1]], axis=1)
    rw = rw + (rw_prev - rw) * rwkv_mu.astype(f32)
    r, k, v, wlo, alo, glo = jnp.split(
        rw, [D_B, 2 * D_B, 3 * D_B, 3 * D_B + LORA_W, 3 * D_B + LORA_W + LORA_A], axis=-1)
    w_log = -jax.nn.softplus(-(rwkv_w0.astype(f32) + jnp.tanh(wlo) @ rwkv_w2.astype(f32))) - 0.5
    decay = jnp.exp(-jnp.exp(w_log))
    a = jax.nn.sigmoid(rwkv_a0.astype(f32) + alo @ rwkv_a2.astype(f32))
    g = jax.nn.sigmoid(glo) @ rwkv_g2.astype(f32)
    kk = (k * rwkv_k_k.astype(f32)).reshape(B, T, H_B, HD_B)
    kk = kk / jnp.maximum(jnp.sqrt(jnp.sum(kk * kk, axis=-1, keepdims=True)), 1e-12)
    k = k * (1.0 + (a - 1.0) * rwkv_k_a.astype(f32))
    r = r.reshape(B, T, H_B, HD_B)
    k = k.reshape(B, T, H_B, HD_B)
    v = v.reshape(B, T, H_B, HD_B)
    y, s_rwkv_new = rwkv7_scan(r, decay.reshape(B, T, H_B, HD_B), k, v, kk,
                               a.reshape(B, T, H_B, HD_B), s_rwkv.astype(f32))
    mu = jnp.mean(y, axis=-1, keepdims=True)
    var = jnp.mean(jnp.square(y - mu), axis=-1, keepdims=True)
    y = ((y - mu) * lax.rsqrt(var + RWKV_GN_EPS) * rwkv_ln_w.astype(f32).reshape(H_B, HD_B)
         + rwkv_ln_b.astype(f32).reshape(H_B, HD_B))
    y = y + jnp.sum(r * k * rwkv_r_k.astype(f32), axis=-1, keepdims=True) * v
    o_b = y.reshape(B, T, D_B) * g

    merged = (jax.nn.sigmoid(gate_a) * (o_a.astype(x.dtype) @ w_branch_a)
              + jax.nn.sigmoid(gate_b) * (o_b.astype(x.dtype) @ w_branch_b))
    x = x + merged @ w_out

    h = moe(rmsnorm(x, norm2_w).reshape(B * T, D_MODEL), router_w, router_b,
            expert_w_gu, expert_b_gu, expert_w_down, expert_b_down)
    x = x + h.reshape(B, T, D_MODEL).astype(x.dtype)
    return x, s_hgrn_new.astype(x.dtype), s_rwkv_new.astype(x.dtype), xn[:, -1]


def setup_inputs(seed: int = 0) -> dict:
    key = jax.random.key(seed)
    ks = jax.random.split(key, 32)

    def nrm(k, shape, scale):
        return jax.random.normal(k, shape, jnp.float32) * scale

    return {
        "x_prompt": nrm(ks[0], (BATCH, SEQ, D_MODEL), 1.0),
        "x_sample": nrm(ks[1], (DEC_BATCH, DEC_SEQ, D_MODEL), 1.0),
        "state_hgrn": nrm(ks[2], (DEPTH, DEC_BATCH, H_A, DK_A, DV_A), 0.5),
        "state_rwkv": nrm(ks[3], (DEPTH, DEC_BATCH, H_B, HD_B, HD_B), 0.3),
        "state_shift": nrm(ks[4], (DEPTH, DEC_BATCH, D_MODEL), 1.0),
        "norm1_w": 1.0 + nrm(ks[5], (DEPTH, D_MODEL), 0.02),
        "w_in": nrm(ks[6], (DEPTH, D_MODEL, D_IN), D_MODEL ** -0.5),
        "hgrn_lb_logits": nrm(ks[7], (DEPTH + 1, D_A), 0.5),
        "hgrn_gnorm_w": 1.0 + nrm(ks[8], (DEPTH, DV_A), 0.02),
        "rwkv_mu": jax.random.uniform(ks[9], (DEPTH, RW_COLS), jnp.float32),
        "rwkv_w0": nrm(ks[10], (DEPTH, D_B), 0.5),
        "rwkv_w2": nrm(ks[11], (DEPTH, LORA_W, D_B), 0.5 * LORA_W ** -0.5),
        "rwkv_a0": nrm(ks[12], (DEPTH, D_B), 0.1),
        "rwkv_a2": nrm(ks[13], (DEPTH, LORA_A, D_B), 0.5 * LORA_A ** -0.5),
        "rwkv_g2": nrm(ks[14], (DEPTH, LORA_G, D_B), LORA_G ** -0.5),
        "rwkv_k_k": 0.85 + nrm(ks[15], (DEPTH, D_B), 0.02),
        "rwkv_k_a": 1.0 + nrm(ks[16], (DEPTH, D_B), 0.02),
        "rwkv_r_k": nrm(ks[17], (DEPTH, H_B, HD_B), 0.1),
        "rwkv_ln_w": 1.0 + nrm(ks[18], (DEPTH, D_B), 0.02),
        "rwkv_ln_b": nrm(ks[19], (DEPTH, D_B), 0.02),
        "w_branch_a": nrm(ks[20], (DEPTH, DV_TOT_A, D_MODEL), DV_TOT_A ** -0.5),
        "w_branch_b": nrm(ks[21], (DEPTH, D_B, D_MODEL), D_B ** -0.5),
        "w_out": nrm(ks[22], (DEPTH, D_MODEL, D_MODEL), D_MODEL ** -0.5),
        "norm2_w": 1.0 + nrm(ks[23], (DEPTH, D_MODEL), 0.02),
        "router_w": nrm(ks[24], (DEPTH, D_MODEL, N_EXPERTS), D_MODEL ** -0.5),
        "router_b": nrm(ks[25], (DEPTH, N_EXPERTS), 0.01),
        "expert_w_gu": nrm(ks[26], (DEPTH, N_EXPERTS, D_MODEL, 2 * D_FF), D_MODEL ** -0.5),
        "expert_b_gu": nrm(ks[27], (DEPTH, N_EXPERTS, 2 * D_FF), 0.01),
        "expert_w_down": nrm(ks[28], (DEPTH, N_EXPERTS, D_FF, D_MODEL), D_FF ** -0.5),
        "expert_b_down": nrm(ks[29], (DEPTH, N_EXPERTS, D_MODEL), 0.01),
        "final_norm_w": 1.0 + nrm(ks[30], (D_MODEL,), 0.02),
    }


def reference(x_prompt, x_sample, state_hgrn, state_rwkv, state_shift, norm1_w, w_in, hgrn_lb_logits,
              hgrn_gnorm_w, rwkv_mu, rwkv_w0, rwkv_w2, rwkv_a0, rwkv_a2, rwkv_g2, rwkv_k_k, rwkv_k_a,
              rwkv_r_k, rwkv_ln_w, rwkv_ln_b, w_branch_a, w_branch_b, w_out, norm2_w, router_w, router_b,
              expert_w_gu, expert_b_gu, expert_w_down, expert_b_down, final_norm_w):
    lb_all = jnp.cumsum(jax.nn.softmax(hgrn_lb_logits.astype(jnp.float32), axis=0), axis=0)
    bp = x_prompt.shape[0]
    xp, xs = x_prompt, x_sample
    hgrn_p, rwkv_p, shift_p, hgrn_s, rwkv_s, shift_s = [], [], [], [], [], []
    for l in range(DEPTH):
        layer_w = (norm1_w[l], w_in[l], hgrn_gnorm_w[l], rwkv_mu[l], rwkv_w0[l], rwkv_w2[l], rwkv_a0[l],
                   rwkv_a2[l], rwkv_g2[l], rwkv_k_k[l], rwkv_k_a[l], rwkv_r_k[l], rwkv_ln_w[l], rwkv_ln_b[l],
                   w_branch_a[l], w_branch_b[l], w_out[l], norm2_w[l], router_w[l], router_b[l],
                   expert_w_gu[l], expert_b_gu[l], expert_w_down[l], expert_b_down[l])
        xp, hp, rp, sp = decoder_layer(
            xp, jnp.zeros((bp, H_A, DK_A, DV_A), jnp.float32), jnp.zeros((bp, H_B, HD_B, HD_B), jnp.float32),
            jnp.zeros((bp, D_MODEL), xp.dtype), lb_all[l], *layer_w)
        xs, hs, rs, ss = decoder_layer(xs, state_hgrn[l], state_rwkv[l], state_shift[l], lb_all[l], *layer_w)
        hgrn_p.append(hp); rwkv_p.append(rp); shift_p.append(sp)
        hgrn_s.append(hs); rwkv_s.append(rs); shift_s.append(ss)
    y_prompt = rmsnorm(xp, final_norm_w)
    y_sample = rmsnorm(xs, final_norm_w)
    return (y_prompt, y_sample, jnp.stack(hgrn_p), jnp.stack(rwkv_p), jnp.stack(shift_p),
            jnp.stack(hgrn_s), jnp.stack(rwkv_s), jnp.stack(shift_s))
```

```python
import functools

import jax
import jax.numpy as jnp
from jax import lax
from jax.experimental import pallas as pl
from jax.experimental.pallas import tpu as pltpu

F32 = jnp.float32
BF16 = jnp.bfloat16

H_A = 4
DK_A = 128
H_B = 8
HD_B = 64
LORA_W = 64
LORA_A = 64
LORA_G = 128
N_EXPERTS = 32
TOP_K = 4
SWIGLU_LIMIT = 7.0
SWIGLU_ALPHA = 1.702
EPS = 1e-6
RWKV_GN_EPS = 64e-5

LANES = 128
SUBLANES = 8
HGRN_CHUNK = 64
HGRN_SUB = 8
RWKV_CHUNK = 64
RWKV_GROUP = 4
RWKV_BASE = 8
MOE_ROWS = 256
VMEM_LIMIT = 48 * 1024 * 1024

NN = (((1,), (0,)), ((), ()))
NT = (((1,), (1,)), ((), ()))
TN = (((0,), (0,)), ((), ()))


def _params(sem):
    return pltpu.CompilerParams(dimension_semantics=sem, vmem_limit_bytes=VMEM_LIMIT)


def _dg(a, b, dims):
    return lax.dot_general(a, b, dims, preferred_element_type=F32)


def _split2(x):
    hi = x.astype(BF16)
    lo = (x - hi.astype(F32)).astype(BF16)
    return hi, lo


def _mm(a, b, dims=NN, passes=1):
    if passes == 1:
        return _dg(a.astype(BF16), b.astype(BF16), dims)
    ah, al = _split2(a)
    bh, bl = _split2(b)
    return _dg(ah, bh, dims) + (_dg(ah, bl, dims) + _dg(al, bh, dims))


def _mm_exact_lhs(sel_bf16, x, dims=NN):
    hi = x.astype(BF16)
    r1 = x - hi.astype(F32)
    mid = r1.astype(BF16)
    lo = (r1 - mid.astype(F32)).astype(BF16)
    return _dg(sel_bf16, hi, dims) + (_dg(sel_bf16, mid, dims) + _dg(sel_bf16, lo, dims))


def _seg_sum(x, seg_bf16):
    hi = x.astype(BF16)
    r1 = x - hi.astype(F32)
    mid = r1.astype(BF16)
    lo = (r1 - mid.astype(F32)).astype(BF16)
    return _dg(hi, seg_bf16, NN) + (_dg(mid, seg_bf16, NN) + _dg(lo, seg_bf16, NN))


def _iota(shape, dim):
    return lax.broadcasted_iota(jnp.int32, shape, dim)


def _sigmoid(x):
    return 1.0 / (1.0 + jnp.exp(-x))


def _stack_rows(rows):
    n = rows[0].shape[1]
    sub = _iota((SUBLANES, n), 0)
    out = jnp.zeros((SUBLANES, n), rows[0].dtype)
    for j, r in enumerate(rows):
        out = jnp.where(sub == j, jnp.broadcast_to(r, (SUBLANES, n)), out)
    return out


def _pad_rows(x, rows):
    if x.shape[0] == rows:
        return x
    return jnp.concatenate([x, jnp.zeros((rows - x.shape[0], x.shape[1]), x.dtype)], axis=0)


def _norm_proj_kernel(x_ref, nw_ref, w_ref, o_ref, *, normalize):
    x = x_ref[...]
    if normalize:
        x = x * lax.rsqrt(jnp.mean(x * x, axis=-1, keepdims=True) + EPS) * nw_ref[...]
    o_ref[...] = _dg(x.astype(BF16), w_ref[...], NN)


def _norm_proj(x, nw, w_bf16, *, normalize, tm):
    n, d = x.shape
    nc = w_bf16.shape[1]
    return pl.pallas_call(
        functools.partial(_norm_proj_kernel, normalize=normalize),
        out_shape=jax.ShapeDtypeStruct((n, nc), F32),
        grid=(n // tm,),
        in_specs=[pl.BlockSpec((tm, d), lambda i: (i, 0)),
                  pl.BlockSpec((1, d), lambda i: (0, 0)),
                  pl.BlockSpec((d, nc), lambda i: (0, 0))],
        out_specs=pl.BlockSpec((tm, nc), lambda i: (i, 0)),
        compiler_params=_params(("arbitrary",)),
        name="norm_proj",
    )(x, nw, w_bf16)


def _rms_rows_kernel(x_ref, w_ref, o_ref):
    x = x_ref[...]
    o_ref[...] = x * lax.rsqrt(jnp.mean(x * x, axis=-1, keepdims=True) + EPS) * w_ref[...]


def _rms_rows(x, w):
    return pl.pallas_call(
        _rms_rows_kernel,
        out_shape=jax.ShapeDtypeStruct(x.shape, F32),
        name="rms_rows",
    )(x, w)


def _hgrn_chunk(hq, hf, hi, lbv, st, *, rows, t_valid):
    f = lbv + (1.0 - lbv) * _sigmoid(hf)
    g = jnp.log(f)
    k = 1.0 - f
    q = hq * _sigmoid(hq)
    v = hi
    if t_valid < rows:
        valid = _iota((rows, 1), 0) < t_valid
        g = jnp.where(valid, g, 0.0)
        k = jnp.where(valid, k, 0.0)
        q = jnp.where(valid, q, 0.0)
        v = jnp.where(valid, v, 0.0)
    row = _iota((rows, rows), 0)
    col = _iota((rows, rows), 1)
    tril = (col <= row).astype(BF16)
    b = _mm_exact_lhs(tril, g)
    o = _mm(q * jnp.exp(b), st, NT)
    sub = min(HGRN_SUB, rows)
    m = rows // 2
    scores = None
    while m >= sub:
        nb = rows // m
        starts = [jnp.zeros((1, DK_A), F32) if j == 0 else b[j * m - 1:j * m] for j in range(nb)]
        ends = [b[j * m + m - 1:j * m + m] for j in range(nb)]
        b_start = jnp.concatenate([jnp.broadcast_to(s, (m, DK_A)) for s in starts], axis=0)
        b_end = jnp.concatenate([jnp.broadcast_to(s, (m, DK_A)) for s in ends], axis=0)
        a_m = _mm(q * jnp.exp(b - b_start), k * jnp.exp(b_end - b), NT)
        shift = m.bit_length() - 1
        rb = row >> shift
        pair = ((rb & 1) == 1) & ((col >> shift) == rb - 1)
        a_m = jnp.where(pair, a_m, 0.0)
        scores = a_m if scores is None else scores + a_m
        m //= 2
    if scores is not None:
        o = o + _mm(scores, v, NN)
    rloc = _iota((rows, 1), 0) & (sub - 1)
    for d in range(min(sub, t_valid)):
        kr = k if d == 0 else pltpu.roll(k, d, 0)
        br = b if d == 0 else pltpu.roll(b, d, 0)
        vr = v if d == 0 else pltpu.roll(v, d, 0)
        e = jnp.exp(jnp.where(rloc >= d, b - br, -jnp.inf))
        o = o + jnp.sum(q * kr * e, axis=-1, keepdims=True) * vr
    b_last = b[rows - 1:rows]
    st_new = st * jnp.exp(b_last) + _mm(v, k * jnp.exp(b_last - b), TN)
    return o, st_new


def _hgrn_kernel(*refs, chunk, n_chunks, t_valid, has_s0):
    if has_s0:
        q_ref, f_ref, i_ref, g_ref, lb_ref, gw_ref, s0_ref, o_ref, s_ref, st_scr = refs
    else:
        q_ref, f_ref, i_ref, g_ref, lb_ref, gw_ref, o_ref, s_ref, st_scr = refs
    ti = pl.program_id(2)

    @pl.when(ti == 0)
    def _():
        if has_s0:
            st_scr[...] = s0_ref[0, 0].T
        else:
            st_scr[...] = jnp.zeros_like(st_scr)

    lbv = lb_ref[...]
    gw = gw_ref[...]

    def body(c, carry):
        if n_chunks == 1:
            sl = slice(None)
        else:
            sl = pl.ds(pl.multiple_of(c * chunk, chunk), chunk)
        hq = _pad_rows(q_ref[0, sl, :], chunk)
        hf = _pad_rows(f_ref[0, sl, :], chunk)
        hi = _pad_rows(i_ref[0, sl, :], chunk)
        hg = g_ref[0, sl, :]
        o, st_new = _hgrn_chunk(hq, hf, hi, lbv, st_scr[...], rows=chunk, t_valid=t_valid)
        st_scr[...] = st_new
        o = o[:t_valid]
        o = o * lax.rsqrt(jnp.mean(o * o, axis=-1, keepdims=True) + EPS) * gw
        o_ref[0, sl, :] = o * (hg * _sigmoid(hg))
        return carry

    if n_chunks == 1:
        body(0, 0)
    else:
        lax.fori_loop(0, n_chunks, body, 0)

    @pl.when(ti == pl.num_programs(2) - 1)
    def _():
        s_ref[0, 0] = st_scr[...].T


def _hgrn(ph, lb, gw, s0, *, batch, seq, tt):
    has_s0 = s0 is not None
    chunk = min(HGRN_CHUNK, max(seq, SUBLANES))
    tt = min(tt, seq)
    n_chunks = max(tt // chunk, 1)
    t_valid = min(chunk, seq)
    col = lambda off: pl.BlockSpec((1, tt, DK_A), lambda b, h, i, off=off: (b, i, off + h))
    in_specs = [col(0), col(H_A), col(2 * H_A), col(3 * H_A),
                pl.BlockSpec((1, DK_A), lambda b, h, i: (0, h)),
                pl.BlockSpec((1, DK_A), lambda b, h, i: (0, 0))]
    args = [ph, ph, ph, ph, lb, gw]
    if has_s0:
        in_specs.append(pl.BlockSpec((1, 1, DK_A, DK_A), lambda b, h, i: (b, h, 0, 0)))
        args.append(s0)
    return pl.pallas_call(
        functools.partial(_hgrn_kernel, chunk=chunk, n_chunks=n_chunks, t_valid=t_valid, has_s0=has_s0),
        out_shape=(jax.ShapeDtypeStruct((batch, seq, H_A * DK_A), F32),
                   jax.ShapeDtypeStruct((batch, H_A, DK_A, DK_A), F32)),
        grid=(batch, H_A, seq // tt),
        in_specs=in_specs,
        out_specs=(pl.BlockSpec((1, tt, DK_A), lambda b, h, i: (b, i, h)),
                   pl.BlockSpec((1, 1, DK_A, DK_A), lambda b, h, i: (b, h, 0, 0))),
        scratch_shapes=[pltpu.VMEM((DK_A, DK_A), F32)],
        compiler_params=_params(("arbitrary", "arbitrary", "arbitrary")),
        name="hgrn_scan",
    )(*args)


def _rwkv_prep_kernel(*refs, seq, has_first):
    if has_first:
        (rw_ref, first_ref, mu_ref, w0_ref, w2_ref, a0_ref, a2_ref, g2_ref, kk_ref, ka_ref, rk_ref, seg_ref,
         r_o, lw_o, k_o, v_o, ah_o, bh_o, bonus_o, g_o, carry) = refs
    else:
        (rw_ref, mu_ref, w0_ref, w2_ref, a0_ref, a2_ref, g2_ref, kk_ref, ka_ref, rk_ref, seg_ref,
         r_o, lw_o, k_o, v_o, ah_o, bh_o, bonus_o, g_o, carry) = refs
    d = r_o.shape[1]
    x = rw_ref[...]
    tm = x.shape[0]
    i = pl.program_id(0)

    @pl.when(i == 0)
    def _():
        carry[...] = jnp.zeros_like(carry)

    rowi = _iota((tm, 1), 0)
    prev = pltpu.roll(x, 1, 0)
    prev = jnp.where(rowi == 0, carry[0:1, :], prev)
    seq_start = ((rowi + i * tm) & (seq - 1)) == 0
    if has_first:
        prev = jnp.where(seq_start, first_ref[...], prev)
    else:
        prev = jnp.where(seq_start, 0.0, prev)
    carry[0:1, :] = x[tm - 1:tm, :]
    x = x + (prev - x) * mu_ref[...]
    r = x[:, 0:d]
    k = x[:, d:2 * d]
    v = x[:, 2 * d:3 * d]
    wlo = x[:, 3 * d:3 * d + LORA_W]
    alo = x[:, 3 * d + LORA_W:3 * d + LORA_W + LORA_A]
    glo = x[:, 3 * d + LORA_W + LORA_A:]
    seg = seg_ref[...]
    z = -(w0_ref[...] + _mm(jnp.tanh(wlo), w2_ref[...]))
    softplus = jnp.maximum(z, 0.0) + jnp.log(1.0 + jnp.exp(-jnp.abs(z)))
    lw_o[...] = -jnp.exp(-softplus - 0.5)
    a = _sigmoid(a0_ref[...] + _mm(alo, a2_ref[...]))
    g_o[...] = _mm(_sigmoid(glo), g2_ref[...])
    kk = k * kk_ref[...]
    kk = kk / jnp.maximum(jnp.sqrt(_seg_sum(kk * kk, seg)), 1e-12)
    k2 = k * (1.0 + (a - 1.0) * ka_ref[...])
    r_o[...] = r
    k_o[...] = k2
    v_o[...] = v
    ah_o[...] = -kk
    bh_o[...] = kk * a
    bonus_o[...] = _seg_sum(r * k2 * rk_ref[...], seg) * v


def _rwkv_prep(rw, first, p, *, seq, tm):
    n, wcols = rw.shape
    d = H_B * HD_B
    has_first = first is not None
    row = lambda c: pl.BlockSpec((tm, c), lambda i: (i, 0))
    full = lambda a: pl.BlockSpec(a.shape, lambda i: (0, 0))
    small = [p["mu"], p["w0"], p["w2"], p["a0"], p["a2"], p["g2"], p["k_k"], p["k_a"], p["r_k"], p["seg"]]
    args = [rw] + ([first] if has_first else []) + small
    in_specs = [row(wcols)] + ([row(wcols)] if has_first else []) + [full(a) for a in small]
    return pl.pallas_call(
        functools.partial(_rwkv_prep_kernel, seq=seq, has_first=has_first),
        out_shape=tuple(jax.ShapeDtypeStruct((n, d), F32) for _ in range(8)),
        grid=(n // tm,),
        in_specs=in_specs,
        out_specs=tuple(row(d) for _ in range(8)),
        scratch_shapes=[pltpu.VMEM((SUBLANES, wcols), F32)],
        compiler_params=_params(("arbitrary",)),
        name="rwkv_prep",
    )(*args)


def _tri_inv(a_bd, n, rows, passes):
    row = _iota((n, n), 0)
    col = _iota((n, n), 1)
    t = row & (rows - 1)
    i = col & (rows - 1)
    eye = (row == col).astype(F32)
    bshift = RWKV_BASE.bit_length() - 1
    d1 = jnp.where((t >> bshift) == (i >> bshift), a_bd, 0.0)
    n1 = eye + d1
    d2 = _mm(d1, d1, NN, passes)
    n2 = n1 + _mm(n1, d2, NN, passes)
    d4 = _mm(d2, d2, NN, passes)
    tinv = n2 + _mm(n2, d4, NN, passes)
    m = RWKV_BASE
    while m < rows:
        s = m.bit_length() - 1
        tb = t >> s
        ib = i >> s
        off = jnp.where(((tb >> 1) == (ib >> 1)) & ((tb & 1) == 1) & ((ib & 1) == 0), a_bd, 0.0)
        tinv = tinv + _mm(_mm(tinv, off, NN, passes), tinv, NN, passes)
        m *= 2
    return tinv


def _rwkv_chunk(r, lw, k, v, ah, bh, st, *, rows, t_valid, passes):
    w = r.shape[1]
    n = RWKV_GROUP * rows
    if t_valid < rows:
        valid = _iota((rows, 1), 0) < t_valid
        r, lw, k, v, ah, bh = (jnp.where(valid, x, 0.0) for x in (r, lw, k, v, ah, bh))
    lrow = _iota((rows, rows), 0)
    lcol = _iota((rows, rows), 1)
    tril = (lcol <= lrow).astype(BF16)
    cum = _mm_exact_lhs(tril, lw)
    gl = cum[rows - 1:rows]
    e_neg = jnp.exp(-cum)
    e_end = jnp.exp(gl - cum)
    qa = ah * jnp.exp(cum - lw)
    qr = r * jnp.exp(cum)
    hs = HD_B.bit_length() - 1
    rs = rows.bit_length() - 1
    bd_mask = (_iota((n, w), 0) >> rs) == (_iota((n, w), 1) >> hs)

    def expand(x):
        return jnp.where(bd_mask, jnp.concatenate([x] * RWKV_GROUP, axis=0), 0.0)

    ke = jnp.concatenate([expand(bh * e_neg), expand(k * e_neg)], axis=0)
    kge = jnp.concatenate([expand(bh * e_end), expand(k * e_end)], axis=0)
    a_a = _mm(qa, ke, NT, passes)
    a_r = _mm(qr, ke, NT, 1)
    t_idx = _iota((rows, 2 * n), 0)
    i_idx = _iota((rows, 2 * n), 1) & (rows - 1)
    a_a = jnp.where(i_idx < t_idx, a_a, 0.0)
    a_r = jnp.where(i_idx <= t_idx, a_r, 0.0)
    p1 = _mm(qa, st, NT, passes)
    y_carry = _mm(qr, st, NT, 1)
    v_bd = expand(v)
    rhs = p1 + _mm(a_a[:, n:], v_bd, NN, passes)
    nmask = (_iota((n, n), 0) >> rs) == (_iota((n, n), 1) >> rs)
    a_bd = jnp.where(nmask, jnp.concatenate([a_a[:, :n]] * RWKV_GROUP, axis=0), 0.0)
    tinv = _tri_inv(a_bd, n, rows, passes)
    c_bd = _mm(tinv, expand(rhs), NN, passes)
    cv = jnp.concatenate([c_bd, v_bd], axis=0)
    y = y_carry + _mm(a_r, cv, NN, 1)
    st_new = st * jnp.exp(gl) + _mm(cv, kge, TN, passes)
    return y, st_new


def _rwkv_scan_kernel(*refs, chunk, n_chunks, t_valid, has_s0, passes):
    if has_s0:
        r_ref, lw_ref, k_ref, v_ref, ah_ref, bh_ref, s0_ref, y_ref, s_ref, st_scr = refs
    else:
        r_ref, lw_ref, k_ref, v_ref, ah_ref, bh_ref, y_ref, s_ref, st_scr = refs
    ti = pl.program_id(2)
    w = RWKV_GROUP * HD_B

    @pl.when(ti == 0)
    def _():
        st_scr[...] = jnp.zeros_like(st_scr)
        if has_s0:
            for h in range(RWKV_GROUP):
                st_scr[h * HD_B:(h + 1) * HD_B, h * HD_B:(h + 1) * HD_B] = s0_ref[0, h]

    def body(c, carry):
        if n_chunks == 1:
            sl = slice(None)
        else:
            sl = pl.ds(pl.multiple_of(c * chunk, chunk), chunk)
        ins = [_pad_rows(ref[0, sl, :], chunk) for ref in (r_ref, lw_ref, k_ref, v_ref, ah_ref, bh_ref)]
        y, st_new = _rwkv_chunk(*ins, st_scr[...], rows=chunk, t_valid=t_valid, passes=passes)
        st_scr[...] = st_new
        y_ref[0, sl, :] = y[:t_valid]
        return carry

    if n_chunks == 1:
        body(0, 0)
    else:
        lax.fori_loop(0, n_chunks, body, 0)

    @pl.when(ti == pl.num_programs(2) - 1)
    def _():
        for h in range(RWKV_GROUP):
            s_ref[0, h] = st_scr[h * HD_B:(h + 1) * HD_B, h * HD_B:(h + 1) * HD_B]


def _rwkv_scan(r, lw, k, v, ah, bh, s0, *, batch, seq, tt, passes):
    has_s0 = s0 is not None
    w = RWKV_GROUP * HD_B
    groups = H_B // RWKV_GROUP
    chunk = min(RWKV_CHUNK, max(seq, SUBLANES))
    tt = min(tt, seq)
    n_chunks = max(tt // chunk, 1)
    t_valid = min(chunk, seq)
    tok = pl.BlockSpec((1, tt, w), lambda b, g, i: (b, i, g))
    st_spec = pl.BlockSpec((1, RWKV_GROUP, HD_B, HD_B), lambda b, g, i: (b, g, 0, 0))
    args = [r, lw, k, v, ah, bh]
    in_specs = [tok] * 6
    if has_s0:
        args.append(s0)
        in_specs.append(st_spec)
    return pl.pallas_call(
        functools.partial(_rwkv_scan_kernel, chunk=chunk, n_chunks=n_chunks, t_valid=t_valid,
                          has_s0=has_s0, passes=passes),
        out_shape=(jax.ShapeDtypeStruct((batch, seq, H_B * HD_B), F32),
                   jax.ShapeDtypeStruct((batch, H_B, HD_B, HD_B), F32)),
        grid=(batch, groups, seq // tt),
        in_specs=in_specs,
        out_specs=(tok, st_spec),
        scratch_shapes=[pltpu.VMEM((w, w), F32)],
        compiler_params=_params(("arbitrary", "arbitrary", "arbitrary")),
        name="rwkv_scan",
    )(*args)


def _merge_kernel(*refs, blocks_a):
    group_a = refs[0:7]
    group_b = refs[7:14]
    (lnw_ref, lnb_ref, seg_ref, wa_ref, wb_ref, wo_ref, n2_ref, rwt_ref, rb_ref,
     x1_o, xn_o, idx_o, gate_o, cnt_o) = refs[14:]
    from_b = pl.program_id(0) >= blocks_a
    y, bonus, g, oa, ga, gb, x = (jnp.where(from_b, rb_[...], ra_[...]) for ra_, rb_ in zip(group_a, group_b))
    seg = seg_ref[...]
    inv = 1.0 / HD_B
    mu = _seg_sum(y, seg) * inv
    dlt = y - mu
    var = _seg_sum(dlt * dlt, seg) * inv
    yn = dlt * lax.rsqrt(var + RWKV_GN_EPS) * lnw_ref[...] + lnb_ref[...]
    ob = (yn + bonus) * g
    merged = _sigmoid(ga) * _mm(oa, wa_ref[...]) + _sigmoid(gb) * _mm(ob, wb_ref[...])
    x1 = x + _mm(merged, wo_ref[...])
    x1_o[...] = x1
    xn = x1 * lax.rsqrt(jnp.mean(x1 * x1, axis=-1, keepdims=True) + EPS) * n2_ref[...]
    xn_o[...] = xn
    logits = _mm(rwt_ref[...], xn, NT, 3) + rb_ref[...]
    tm = x1.shape[0]
    eio = _iota((N_EXPERTS, tm), 0)
    idx_rows, val_rows = [], []
    cnt = jnp.zeros((N_EXPERTS, 1), F32)
    for _ in range(TOP_K):
        mx = jnp.max(logits, axis=0, keepdims=True)
        pick = jnp.min(jnp.where(logits == mx, eio, N_EXPERTS), axis=0, keepdims=True)
        hit = eio == pick
        cnt = cnt + jnp.sum(hit.astype(F32), axis=1, keepdims=True)
        logits = jnp.where(hit, -jnp.inf, logits)
        idx_rows.append(pick)
        val_rows.append(mx)
    ex = [jnp.exp(vr - val_rows[0]) for vr in val_rows]
    den = ex[0] + ex[1] + ex[2] + ex[3]
    idx_o[...] = _stack_rows(idx_rows)
    gate_o[...] = _stack_rows([e / den for e in ex])

    @pl.when(pl.program_id(0) == 0)
    def _():
        cnt_o[...] = jnp.zeros_like(cnt_o)

    cnt_o[...] += jnp.broadcast_to(cnt, cnt_o.shape)


def _merge(tok_a, tok_b, p, *, tm):
    n_a, d = tok_a[5].shape
    n_b = tok_b[5].shape[0]
    n = n_a + n_b
    blocks_a = n_a // tm
    last_a = blocks_a - 1
    ia = lambda i: jnp.minimum(i, last_a)
    ib = lambda i: jnp.maximum(i - blocks_a, 0)

    def group_specs(tok, im):
        y, _, _, oa, gates, _ = tok
        dm = gates.shape[1] // 2
        row = lambda c: pl.BlockSpec((tm, c), lambda i: (im(i), 0))
        return [row(y.shape[1]), row(y.shape[1]), row(y.shape[1]), row(oa.shape[1]),
                pl.BlockSpec((tm, dm), lambda i: (im(i), 0)), pl.BlockSpec((tm, dm), lambda i: (im(i), 1)),
                row(d)]

    def group_args(tok):
        y, bonus, g, oa, gates, x = tok
        return [y, bonus, g, oa, gates, gates, x]

    full = lambda a: pl.BlockSpec(a.shape, lambda i: (0, 0))
    small = [p["ln_w"], p["ln_b"], p["seg"], p["wa"], p["wb"], p["wo"], p["n2"], p["rwt"], p["rb"]]
    out_row = pl.BlockSpec((tm, d), lambda i: (i, 0))
    out_col = pl.BlockSpec((SUBLANES, tm), lambda i: (0, i))
    return pl.pallas_call(
        functools.partial(_merge_kernel, blocks_a=blocks_a),
        out_shape=(jax.ShapeDtypeStruct((n, d), F32), jax.ShapeDtypeStruct((n, d), F32),
                   jax.ShapeDtypeStruct((SUBLANES, n), jnp.int32),
                   jax.ShapeDtypeStruct((SUBLANES, n), F32),
                   jax.ShapeDtypeStruct((N_EXPERTS, LANES), F32)),
        grid=(n // tm,),
        in_specs=group_specs(tok_a, ia) + group_specs(tok_b, ib) + [full(a) for a in small],
        out_specs=(out_row, out_row, out_col, out_col,
                   pl.BlockSpec((N_EXPERTS, LANES), lambda i: (0, 0))),
        compiler_params=_params(("arbitrary",)),
        name="merge_router",
    )(*(group_args(tok_a) + group_args(tok_b) + small))


def _rank_kernel(idx_ref, pstart_ref, dest_o, carry):
    i = pl.program_id(0)

    @pl.when(i == 0)
    def _():
        carry[...] = pstart_ref[...]

    tm = idx_ref.shape[1]
    eio = _iota((N_EXPERTS, tm), 0)
    idx = idx_ref[...]
    hits = [eio == idx[j:j + 1, :] for j in range(TOP_K)]
    sel = hits[0] | hits[1] | hits[2] | hits[3]
    upper = (_iota((tm, tm), 0) < _iota((tm, tm), 1)).astype(BF16)
    rank = _dg(sel.astype(BF16), upper, NN) + carry[:, 0:1]
    rows = [jnp.sum(jnp.where(h, rank, 0.0), axis=0, keepdims=True) for h in hits]
    dest_o[...] = _stack_rows(rows).astype(jnp.int32)
    carry[...] += jnp.sum(sel.astype(F32), axis=1, keepdims=True)


def _rank(idx4, pstart, *, tm):
    n = idx4.shape[1]
    return pl.pallas_call(
        _rank_kernel,
        out_shape=jax.ShapeDtypeStruct((SUBLANES, n), jnp.int32),
        grid=(n // tm,),
        in_specs=[pl.BlockSpec((SUBLANES, tm), lambda i: (0, i)),
                  pl.BlockSpec((N_EXPERTS, LANES), lambda i: (0, 0))],
        out_specs=pl.BlockSpec((SUBLANES, tm), lambda i: (0, i)),
        scratch_shapes=[pltpu.VMEM((N_EXPERTS, LANES), F32)],
        compiler_params=_params(("arbitrary",)),
        name="moe_rank",
    )(idx4, pstart)


def _scatter_kernel(pad_lo, pad_hi, na_ref, dest_ref, x_ref, xs_hbm, zeros, sem, zsem):
    tm = x_ref.shape[0]
    zrows = zeros.shape[0]
    nblk = xs_hbm.shape[0] // zrows

    @pl.when(pl.program_id(0) == 0)
    def _():
        zeros[...] = jnp.zeros_like(zeros)
        zrow = lambda r: pltpu.make_async_copy(zeros.at[0], xs_hbm.at[r], zsem)
        zblk = lambda b: pltpu.make_async_copy(zeros, xs_hbm.at[pl.ds(b * zrows, zrows)], zsem)

        def per_expert(op):
            def f(e, c):
                lax.fori_loop(pad_lo[e], pad_hi[e], lambda r, c2: (op(zrow(r)), c2)[1], 0)
                return c
            return f

        lax.fori_loop(0, N_EXPERTS, per_expert(lambda cp: cp.start()), 0)
        lax.fori_loop(na_ref[0], nblk, lambda b, c: (zblk(b).start(), c)[1], 0)
        lax.fori_loop(0, N_EXPERTS, per_expert(lambda cp: cp.wait()), 0)
        lax.fori_loop(na_ref[0], nblk, lambda b, c: (zblk(b).wait(), c)[1], 0)

    def copy(t, j):
        return pltpu.make_async_copy(x_ref.at[t], xs_hbm.at[dest_ref[j, t]], sem)

    def issue(t, c):
        for j in range(TOP_K):
            copy(t, j).start()
        return c

    def drain(t, c):
        for j in range(TOP_K):
            copy(t, j).wait()
        return c

    lax.fori_loop(0, tm, issue, 0)
    lax.fori_loop(0, tm, drain, 0)


def _scatter(pad_lo, pad_hi, n_active, dest4, x3, m_total, *, tm):
    n = x3.shape[0]
    tail = x3.shape[1:]
    return pl.pallas_call(
        _scatter_kernel,
        out_shape=jax.ShapeDtypeStruct((m_total,) + tail, x3.dtype),
        grid_spec=pltpu.PrefetchScalarGridSpec(
            num_scalar_prefetch=3,
            grid=(n // tm,),
            in_specs=[pl.BlockSpec((SUBLANES, tm), lambda i, *_: (0, i), memory_space=pltpu.SMEM),
                      pl.BlockSpec((tm,) + tail, lambda i, *_: (i, 0, 0))],
            out_specs=pl.BlockSpec(memory_space=pl.ANY),
            scratch_shapes=[pltpu.VMEM((MOE_ROWS,) + tail, x3.dtype),
                            pltpu.SemaphoreType.DMA, pltpu.SemaphoreType.DMA]),
        compiler_params=_params(("arbitrary",)),
        name="moe_scatter",
    )(pad_lo, pad_hi, n_active, dest4, x3)


def _expert_kernel(be_ref, na_ref, x_ref, wg_ref, wl_ref, bg_ref, bl_ref, wd_ref, bd_ref, y_ref):
    @pl.when(pl.program_id(0) < na_ref[0])
    def _():
        x = x_ref[...].astype(BF16)
        hg = _dg(x, wg_ref[0], NN) + bg_ref[0]
        hl = _dg(x, wl_ref[0], NN) + bl_ref[0]
        glu = jnp.minimum(hg, SWIGLU_LIMIT)
        lin = jnp.clip(hl, -SWIGLU_LIMIT, SWIGLU_LIMIT)
        act = glu * _sigmoid(SWIGLU_ALPHA * glu) * (lin + 1.0)
        y_ref[...] = _dg(act.astype(BF16), wd_ref[0], NN) + bd_ref[0]

    @pl.when(pl.program_id(0) >= na_ref[0])
    def _():
        y_ref[...] = jnp.zeros_like(y_ref)


def _experts(block_expert, n_active, xs, wg, wl, bg, bl, wd, bd):
    m_total, d = xs.shape
    dff = wg.shape[2]
    nblk = m_total // MOE_ROWS
    wspec = lambda r, c: pl.BlockSpec((1, r, c), lambda b, be, na: (be[b], 0, 0))
    return pl.pallas_call(
        _expert_kernel,
        out_shape=jax.ShapeDtypeStruct((m_total, d), F32),
        grid_spec=pltpu.PrefetchScalarGridSpec(
            num_scalar_prefetch=2,
            grid=(nblk,),
            in_specs=[pl.BlockSpec((MOE_ROWS, d), lambda b, be, na: (b, 0)),
                      wspec(d, dff), wspec(d, dff), wspec(1, dff), wspec(1, dff),
                      wspec(dff, d), wspec(1, d)],
            out_specs=pl.BlockSpec((MOE_ROWS, d), lambda b, be, na: (b, 0))),
        compiler_params=_params(("arbitrary",)),
        name="moe_experts",
    )(block_expert, n_active, xs, wg, wl, bg, bl, wd, bd)


def _combine_kernel(dest_ref, gate_ref, ys_hbm, o_ref, buf, sem):
    tm = o_ref.shape[0]

    def copy(t, j):
        return pltpu.make_async_copy(ys_hbm.at[dest_ref[j, t]], buf.at[j, t], sem)

    def issue(t, c):
        for j in range(TOP_K):
            copy(t, j).start()
        return c

    def drain(t, c):
        for j in range(TOP_K):
            copy(t, j).wait()
        return c

    def mix(t, c):
        acc = gate_ref[0, t] * buf[0, t]
        for j in range(1, TOP_K):
            acc = acc + gate_ref[j, t] * buf[j, t]
        o_ref[t] = acc
        return c

    lax.fori_loop(0, tm, issue, 0)
    lax.fori_loop(0, tm, drain, 0)
    lax.fori_loop(0, tm, mix, 0)


def _combine(dest4, gate4, ys3, n, *, tm):
    tail = ys3.shape[1:]
    smem = lambda: pl.BlockSpec((SUBLANES, tm), lambda i: (0, i), memory_space=pltpu.SMEM)
    return pl.pallas_call(
        _combine_kernel,
        out_shape=jax.ShapeDtypeStruct((n,) + tail, F32),
        grid=(n // tm,),
        in_specs=[smem(), smem(), pl.BlockSpec(memory_space=pl.ANY)],
        out_specs=pl.BlockSpec((tm,) + tail, lambda i: (i, 0, 0)),
        scratch_shapes=[pltpu.VMEM((TOP_K, tm) + tail, F32), pltpu.SemaphoreType.DMA],
        compiler_params=_params(("arbitrary",)),
        name="moe_combine",
    )(dest4, gate4, ys3)


def _final_kernel(x_ref, h_ref, w_ref, o_ref):
    x = x_ref[...] + h_ref[...]
    o_ref[...] = x * lax.rsqrt(jnp.mean(x * x, axis=-1, keepdims=True) + EPS) * w_ref[...]


def _final(x1, h, w, *, tm):
    n, d = x1.shape
    row = pl.BlockSpec((tm, d), lambda i: (i, 0))
    return pl.pallas_call(
        _final_kernel,
        out_shape=jax.ShapeDtypeStruct((n, d), F32),
        grid=(n // tm,),
        in_specs=[row, row, pl.BlockSpec((1, d), lambda i: (0, 0))],
        out_specs=row,
        compiler_params=_params(("arbitrary",)),
        name="final_norm",
    )(x1, h, w)


def _mixers(x, s_hgrn, s_rwkv, shift_prev, wts):
    batch, seq, d = x.shape
    n = batch * seq
    x2 = x.reshape(n, d)
    tm = min(512, n)
    ph = _norm_proj(x2, wts["n1"], wts["w_h"], normalize=True, tm=tm)
    prw = _norm_proj(x2, wts["n1"], wts["w_rw"], normalize=True, tm=tm)
    pg = _norm_proj(x2, wts["n1"], wts["w_g"], normalize=True, tm=tm)
    oa, s_hgrn_new = _hgrn(ph.reshape(batch, seq, -1), wts["lb"], wts["gw"], s_hgrn,
                           batch=batch, seq=seq, tt=256)
    first = None
    if shift_prev is not None:
        prev = _norm_proj(shift_prev, wts["n1"], wts["w_rw"], normalize=False, tm=shift_prev.shape[0])
        first = jnp.repeat(prev, seq, axis=0)
    r, lw, k, v, ah, bh, bonus, g = _rwkv_prep(prw, first, wts, seq=seq, tm=min(256, n))
    to3 = lambda a: a.reshape(batch, seq, -1)
    y, s_rwkv_new = _rwkv_scan(to3(r), to3(lw), to3(k), to3(v), to3(ah), to3(bh), s_rwkv,
                               batch=batch, seq=seq, tt=256, passes=3)
    return (y.reshape(n, -1), bonus, g, oa.reshape(n, -1), pg, x2), s_hgrn_new, s_rwkv_new


def kernel(x_prompt, x_sample, state_hgrn, state_rwkv, state_shift, norm1_w, w_in, hgrn_lb_logits,
           hgrn_gnorm_w, rwkv_mu, rwkv_w0, rwkv_w2, rwkv_a0, rwkv_a2, rwkv_g2, rwkv_k_k, rwkv_k_a,
           rwkv_r_k, rwkv_ln_w, rwkv_ln_b, w_branch_a, w_branch_b, w_out, norm2_w, router_w, router_b,
           expert_w_gu, expert_b_gu, expert_w_down, expert_b_down, final_norm_w):
    bp, tp, d = x_prompt.shape
    bs, ts, _ = x_sample.shape
    n_p, n_s = bp * tp, bs * ts
    n = n_p + n_s
    d_a = H_A * DK_A
    d_b = H_B * HD_B
    rw_start = 4 * d_a
    gate_start = rw_start + 3 * d_b + LORA_W + LORA_A + LORA_G
    row2 = lambda a: a.reshape(1, -1)
    lb = jnp.cumsum(jax.nn.softmax(hgrn_lb_logits.astype(F32), axis=0), axis=0)[0]
    hid = jnp.arange(d_b, dtype=jnp.int32) // HD_B
    wts = {
        "n1": row2(norm1_w[0]),
        "w_h": w_in[0][:, :rw_start].astype(BF16),
        "w_rw": w_in[0][:, rw_start:gate_start].astype(BF16),
        "w_g": w_in[0][:, gate_start:].astype(BF16),
        "lb": row2(lb), "gw": row2(hgrn_gnorm_w[0]),
        "mu": row2(rwkv_mu[0]), "w0": row2(rwkv_w0[0]), "w2": rwkv_w2[0], "a0": row2(rwkv_a0[0]),
        "a2": rwkv_a2[0], "g2": rwkv_g2[0], "k_k": row2(rwkv_k_k[0]), "k_a": row2(rwkv_k_a[0]),
        "r_k": row2(rwkv_r_k[0]), "seg": (hid[:, None] == hid[None, :]).astype(BF16),
        "ln_w": row2(rwkv_ln_w[0]), "ln_b": row2(rwkv_ln_b[0]),
        "wa": w_branch_a[0].astype(BF16), "wb": w_branch_b[0].astype(BF16), "wo": w_out[0].astype(BF16),
        "n2": row2(norm2_w[0]), "rwt": router_w[0].T, "rb": router_b[0].reshape(-1, 1),
    }
    tok_p, hgrn_p, rwkv_p = _mixers(x_prompt, None, None, None, wts)
    tok_s, hgrn_s, rwkv_s = _mixers(x_sample, state_hgrn[0], state_rwkv[0], state_shift[0], wts)
    shift = _rms_rows(jnp.concatenate([x_prompt[:, -1], x_sample[:, -1]], axis=0), wts["n1"])

    x1, xn2, idx4, gate4, counts = _merge(tok_p, tok_s, wts, tm=256)

    cnt = counts[:, 0].astype(jnp.int32)
    padded = (cnt + MOE_ROWS - 1) // MOE_ROWS * MOE_ROWS
    pend = jnp.cumsum(padded)
    pstart = pend - padded
    nblk = (n * TOP_K + N_EXPERTS * (MOE_ROWS - 1) + MOE_ROWS - 1) // MOE_ROWS
    m_total = nblk * MOE_ROWS
    block_expert = jnp.minimum(
        jnp.searchsorted(pend, jnp.arange(nblk, dtype=jnp.int32) * MOE_ROWS, side="right"),
        N_EXPERTS - 1).astype(jnp.int32)
    n_active = (pend[-1:] // MOE_ROWS).astype(jnp.int32)
    dest4 = _rank(idx4, jnp.broadcast_to(pstart.astype(F32)[:, None], (N_EXPERTS, LANES)), tm=512)

    xs = _scatter(pstart + cnt, pend, n_active, dest4, xn2.reshape(n, d // LANES, LANES), m_total, tm=256)
    wg = expert_w_gu[0][:, :, 0::2].astype(BF16)
    wl = expert_w_gu[0][:, :, 1::2].astype(BF16)
    bg = expert_b_gu[0][:, None, 0::2]
    bl = expert_b_gu[0][:, None, 1::2]
    wd = expert_w_down[0].astype(BF16)
    bd = expert_b_down[0][:, None, :]
    ys = _experts(block_expert, n_active, xs.reshape(m_total, d), wg, wl, bg, bl, wd, bd)
    h = _combine(dest4, gate4, ys.reshape(m_total, d // LANES, LANES), n, tm=256)
    yout = _final(x1, h.reshape(n, d), row2(final_norm_w), tm=512)

    return (yout[:n_p].reshape(bp, tp, d), yout[n_p:].reshape(bs, ts, d),
            hgrn_p[None], rwkv_p[None], shift[None, :bp],
            hgrn_s[None], rwkv_s[None], shift[None, bp:])
```

```python
import functools

import jax
import jax.numpy as jnp
from jax import lax
from jax.experimental import pallas as pl
from jax.experimental.pallas import tpu as pltpu

F32 = jnp.float32
BF16 = jnp.bfloat16

H_A = 4
DK_A = 128
H_B = 8
HD_B = 64
LORA_W = 64
LORA_A = 64
LORA_G = 128
N_EXPERTS = 32
TOP_K = 4
SWIGLU_LIMIT = 7.0
SWIGLU_ALPHA = 1.702
EPS = 1e-6
RWKV_GN_EPS = 64e-5

LANES = 128
SUBLANES = 8
HGRN_CHUNK = 64
HGRN_SUB = 8
RWKV_CHUNK = 64
RWKV_GROUP = 4
RWKV_BASE = 8
MOE_ROWS = 256
VMEM_LIMIT = 48 * 1024 * 1024

NN = (((1,), (0,)), ((), ()))
NT = (((1,), (1,)), ((), ()))
TN = (((0,), (0,)), ((), ()))


def _params(sem):
    return pltpu.CompilerParams(dimension_semantics=sem, vmem_limit_bytes=VMEM_LIMIT)


def _dg(a, b, dims):
    return lax.dot_general(a, b, dims, preferred_element_type=F32)


def _split2(x):
    hi = x.astype(BF16)
    lo = (x - hi.astype(F32)).astype(BF16)
    return hi, lo


def _mm(a, b, dims=NN, passes=1):
    if passes == 1:
        return _dg(a.astype(BF16), b.astype(BF16), dims)
    ah, al = _split2(a)
    bh, bl = _split2(b)
    return _dg(ah, bh, dims) + (_dg(ah, bl, dims) + _dg(al, bh, dims))


def _mm_exact_lhs(sel_bf16, x, dims=NN):
    hi = x.astype(BF16)
    r1 = x - hi.astype(F32)
    mid = r1.astype(BF16)
    lo = (r1 - mid.astype(F32)).astype(BF16)
    return _dg(sel_bf16, hi, dims) + (_dg(sel_bf16, mid, dims) + _dg(sel_bf16, lo, dims))


def _seg_sum(x, seg_bf16):
    hi = x.astype(BF16)
    r1 = x - hi.astype(F32)
    mid = r1.astype(BF16)
    lo = (r1 - mid.astype(F32)).astype(BF16)
    return _dg(hi, seg_bf16, NN) + (_dg(mid, seg_bf16, NN) + _dg(lo, seg_bf16, NN))


def _iota(shape, dim):
    return lax.broadcasted_iota(jnp.int32, shape, dim)


def _sigmoid(x):
    return 1.0 / (1.0 + jnp.exp(-x))


def _tok_load(ref):
    tm = ref.shape[0] // SUBLANES
    return jnp.concatenate([ref[pl.ds(c, tm, stride=SUBLANES), :] for c in range(SUBLANES)], axis=1)


def _tok_store(ref, val):
    tm = val.shape[0]
    for c in range(SUBLANES):
        ref[pl.ds(c, tm, stride=SUBLANES), :] = val[:, c * LANES:(c + 1) * LANES]


def _stack_rows(rows):
    n = rows[0].shape[1]
    sub = _iota((SUBLANES, n), 0)
    out = jnp.zeros((SUBLANES, n), rows[0].dtype)
    for j, r in enumerate(rows):
        out = jnp.where(sub == j, jnp.broadcast_to(r, (SUBLANES, n)), out)
    return out


def _pad_rows(x, rows):
    if x.shape[0] == rows:
        return x
    return jnp.concatenate([x, jnp.zeros((rows - x.shape[0], x.shape[1]), x.dtype)], axis=0)


def _norm_proj_kernel(x_ref, nw_ref, w_ref, o_ref, *, normalize):
    x = x_ref[...]
    if normalize:
        x = x * lax.rsqrt(jnp.mean(x * x, axis=-1, keepdims=True) + EPS) * nw_ref[...]
    o_ref[...] = _dg(x.astype(BF16), w_ref[...], NN)


def _norm_proj(x, nw, w_bf16, *, normalize, tm):
    n, d = x.shape
    nc = w_bf16.shape[1]
    return pl.pallas_call(
        functools.partial(_norm_proj_kernel, normalize=normalize),
        out_shape=jax.ShapeDtypeStruct((n, nc), F32),
        grid=(n // tm,),
        in_specs=[pl.BlockSpec((tm, d), lambda i: (i, 0)),
                  pl.BlockSpec((1, d), lambda i: (0, 0)),
                  pl.BlockSpec((d, nc), lambda i: (0, 0))],
        out_specs=pl.BlockSpec((tm, nc), lambda i: (i, 0)),
        compiler_params=_params(("arbitrary",)),
        name="norm_proj",
    )(x, nw, w_bf16)


def _rms_rows_kernel(x_ref, w_ref, o_ref):
    x = x_ref[...]
    o_ref[...] = x * lax.rsqrt(jnp.mean(x * x, axis=-1, keepdims=True) + EPS) * w_ref[...]


def _rms_rows(x, w):
    return pl.pallas_call(
        _rms_rows_kernel,
        out_shape=jax.ShapeDtypeStruct(x.shape, F32),
        name="rms_rows",
    )(x, w)


def _hgrn_chunk(hq, hf, hi, lbv, st, *, rows, t_valid):
    f = lbv + (1.0 - lbv) * _sigmoid(hf)
    g = jnp.log(f)
    k = 1.0 - f
    q = hq * _sigmoid(hq)
    v = hi
    if t_valid < rows:
        valid = _iota((rows, 1), 0) < t_valid
        g = jnp.where(valid, g, 0.0)
        k = jnp.where(valid, k, 0.0)
        q = jnp.where(valid, q, 0.0)
        v = jnp.where(valid, v, 0.0)
    row = _iota((rows, rows), 0)
    col = _iota((rows, rows), 1)
    tril = (col <= row).astype(BF16)
    b = _mm_exact_lhs(tril, g)
    o = _mm(q * jnp.exp(b), st, NT)
    sub = min(HGRN_SUB, rows)
    m = rows // 2
    scores = None
    while m >= sub:
        nb = rows // m
        starts = [jnp.zeros((1, DK_A), F32) if j == 0 else b[j * m - 1:j * m] for j in range(nb)]
        ends = [b[j * m + m - 1:j * m + m] for j in range(nb)]
        b_start = jnp.concatenate([jnp.broadcast_to(s, (m, DK_A)) for s in starts], axis=0)
        b_end = jnp.concatenate([jnp.broadcast_to(s, (m, DK_A)) for s in ends], axis=0)
        a_m = _mm(q * jnp.exp(b - b_start), k * jnp.exp(b_end - b), NT)
        shift = m.bit_length() - 1
        rb = row >> shift
        pair = ((rb & 1) == 1) & ((col >> shift) == rb - 1)
        a_m = jnp.where(pair, a_m, 0.0)
        scores = a_m if scores is None else scores + a_m
        m //= 2
    if scores is not None:
        o = o + _mm(scores, v, NN)
    rloc = _iota((rows, 1), 0) & (sub - 1)
    for d in range(min(sub, t_valid)):
        kr = k if d == 0 else pltpu.roll(k, d, 0)
        br = b if d == 0 else pltpu.roll(b, d, 0)
        vr = v if d == 0 else pltpu.roll(v, d, 0)
        e = jnp.exp(jnp.where(rloc >= d, b - br, -jnp.inf))
        o = o + jnp.sum(q * kr * e, axis=-1, keepdims=True) * vr
    b_last = b[rows - 1:rows]
    st_new = st * jnp.exp(b_last) + _mm(v, k * jnp.exp(b_last - b), TN)
    return o, st_new


def _hgrn_kernel(*refs, chunk, n_chunks, t_valid, has_s0):
    if has_s0:
        q_ref, f_ref, i_ref, g_ref, lb_ref, gw_ref, s0_ref, o_ref, s_ref, st_scr = refs
    else:
        q_ref, f_ref, i_ref, g_ref, lb_ref, gw_ref, o_ref, s_ref, st_scr = refs
    ti = pl.program_id(2)

    @pl.when(ti == 0)
    def _():
        if has_s0:
            st_scr[...] = s0_ref[0, 0].T
        else:
            st_scr[...] = jnp.zeros_like(st_scr)

    lbv = lb_ref[...]
    gw = gw_ref[...]

    def body(c, carry):
        if n_chunks == 1:
            sl = slice(None)
        else:
            sl = pl.ds(pl.multiple_of(c * chunk, chunk), chunk)
        hq = _pad_rows(q_ref[0, sl, :], chunk)
        hf = _pad_rows(f_ref[0, sl, :], chunk)
        hi = _pad_rows(i_ref[0, sl, :], chunk)
        hg = g_ref[0, sl, :]
        o, st_new = _hgrn_chunk(hq, hf, hi, lbv, st_scr[...], rows=chunk, t_valid=t_valid)
        st_scr[...] = st_new
        o = o[:t_valid]
        o = o * lax.rsqrt(jnp.mean(o * o, axis=-1, keepdims=True) + EPS) * gw
        o_ref[0, sl, :] = o * (hg * _sigmoid(hg))
        return carry

    if n_chunks == 1:
        body(0, 0)
    else:
        lax.fori_loop(0, n_chunks, body, 0)

    @pl.when(ti == pl.num_programs(2) - 1)
    def _():
        s_ref[0, 0] = st_scr[...].T


def _hgrn(ph, lb, gw, s0, *, batch, seq, tt):
    has_s0 = s0 is not None
    chunk = min(HGRN_CHUNK, max(seq, SUBLANES))
    tt = min(tt, seq)
    n_chunks = max(tt // chunk, 1)
    t_valid = min(chunk, seq)
    col = lambda off: pl.BlockSpec((1, tt, DK_A), lambda b, h, i, off=off: (b, i, off + h))
    in_specs = [col(0), col(H_A), col(2 * H_A), col(3 * H_A),
                pl.BlockSpec((1, DK_A), lambda b, h, i: (0, h)),
                pl.BlockSpec((1, DK_A), lambda b, h, i: (0, 0))]
    args = [ph, ph, ph, ph, lb, gw]
    if has_s0:
        in_specs.append(pl.BlockSpec((1, 1, DK_A, DK_A), lambda b, h, i: (b, h, 0, 0)))
        args.append(s0)
    return pl.pallas_call(
        functools.partial(_hgrn_kernel, chunk=chunk, n_chunks=n_chunks, t_valid=t_valid, has_s0=has_s0),
        out_shape=(jax.ShapeDtypeStruct((batch, seq, H_A * DK_A), F32),
                   jax.ShapeDtypeStruct((batch, H_A, DK_A, DK_A), F32)),
        grid=(batch, H_A, seq // tt),
        in_specs=in_specs,
        out_specs=(pl.BlockSpec((1, tt, DK_A), lambda b, h, i: (b, i, h)),
                   pl.BlockSpec((1, 1, DK_A, DK_A), lambda b, h, i: (b, h, 0, 0))),
        scratch_shapes=[pltpu.VMEM((DK_A, DK_A), F32)],
        compiler_params=_params(("arbitrary", "arbitrary", "arbitrary")),
        name="hgrn_scan",
    )(*args)


def _rwkv_prep_kernel(*refs, seq, has_first):
    if has_first:
        (rw_ref, first_ref, mu_ref, w0_ref, w2_ref, a0_ref, a2_ref, g2_ref, kk_ref, ka_ref, rk_ref, seg_ref,
         r_o, lw_o, k_o, v_o, ah_o, bh_o, bonus_o, g_o, carry) = refs
    else:
        (rw_ref, mu_ref, w0_ref, w2_ref, a0_ref, a2_ref, g2_ref, kk_ref, ka_ref, rk_ref, seg_ref,
         r_o, lw_o, k_o, v_o, ah_o, bh_o, bonus_o, g_o, carry) = refs
    d = r_o.shape[1]
    x = rw_ref[...]
    tm = x.shape[0]
    i = pl.program_id(0)

    @pl.when(i == 0)
    def _():
        carry[...] = jnp.zeros_like(carry)

    rowi = _iota((tm, 1), 0)
    prev = pltpu.roll(x, 1, 0)
    prev = jnp.where(rowi == 0, carry[0:1, :], prev)
    seq_start = ((rowi + i * tm) & (seq - 1)) == 0
    if has_first:
        prev = jnp.where(seq_start, first_ref[...], prev)
    else:
        prev = jnp.where(seq_start, 0.0, prev)
    carry[0:1, :] = x[tm - 1:tm, :]
    x = x + (prev - x) * mu_ref[...]
    r = x[:, 0:d]
    k = x[:, d:2 * d]
    v = x[:, 2 * d:3 * d]
    wlo = x[:, 3 * d:3 * d + LORA_W]
    alo = x[:, 3 * d + LORA_W:3 * d + LORA_W + LORA_A]
    glo = x[:, 3 * d + LORA_W + LORA_A:]
    seg = seg_ref[...]
    z = -(w0_ref[...] + _mm(jnp.tanh(wlo), w2_ref[...]))
    softplus = jnp.maximum(z, 0.0) + jnp.log(1.0 + jnp.exp(-jnp.abs(z)))
    lw_o[...] = -jnp.exp(-softplus - 0.5)
    a = _sigmoid(a0_ref[...] + _mm(alo, a2_ref[...]))
    g_o[...] = _mm(_sigmoid(glo), g2_ref[...])
    kk = k * kk_ref[...]
    kk = kk / jnp.maximum(jnp.sqrt(_seg_sum(kk * kk, seg)), 1e-12)
    k2 = k * (1.0 + (a - 1.0) * ka_ref[...])
    r_o[...] = r
    k_o[...] = k2
    v_o[...] = v
    ah_o[...] = -kk
    bh_o[...] = kk * a
    bonus_o[...] = _seg_sum(r * k2 * rk_ref[...], seg) * v


def _rwkv_prep(rw, first, p, *, seq, tm):
    n, wcols = rw.shape
    d = H_B * HD_B
    has_first = first is not None
    row = lambda c: pl.BlockSpec((tm, c), lambda i: (i, 0))
    full = lambda a: pl.BlockSpec(a.shape, lambda i: (0, 0))
    small = [p["mu"], p["w0"], p["w2"], p["a0"], p["a2"], p["g2"], p["k_k"], p["k_a"], p["r_k"], p["seg"]]
    args = [rw] + ([first] if has_first else []) + small
    in_specs = [row(wcols)] + ([row(wcols)] if has_first else []) + [full(a) for a in small]
    return pl.pallas_call(
        functools.partial(_rwkv_prep_kernel, seq=seq, has_first=has_first),
        out_shape=tuple(jax.ShapeDtypeStruct((n, d), F32) for _ in range(8)),
        grid=(n // tm,),
        in_specs=in_specs,
        out_specs=tuple(row(d) for _ in range(8)),
        scratch_shapes=[pltpu.VMEM((SUBLANES, wcols), F32)],
        compiler_params=_params(("arbitrary",)),
        name="rwkv_prep",
    )(*args)


def _tri_inv(a_bd, n, rows, passes):
    row = _iota((n, n), 0)
    col = _iota((n, n), 1)
    t = row & (rows - 1)
    i = col & (rows - 1)
    eye = (row == col).astype(F32)
    bshift = RWKV_BASE.bit_length() - 1
    d1 = jnp.where((t >> bshift) == (i >> bshift), a_bd, 0.0)
    n1 = eye + d1
    d2 = _mm(d1, d1, NN, passes)
    n2 = n1 + _mm(n1, d2, NN, passes)
    d4 = _mm(d2, d2, NN, passes)
    tinv = n2 + _mm(n2, d4, NN, passes)
    m = RWKV_BASE
    while m < rows:
        s = m.bit_length() - 1
        tb = t >> s
        ib = i >> s
        off = jnp.where(((tb >> 1) == (ib >> 1)) & ((tb & 1) == 1) & ((ib & 1) == 0), a_bd, 0.0)
        tinv = tinv + _mm(_mm(tinv, off, NN, passes), tinv, NN, passes)
        m *= 2
    return tinv


def _rwkv_chunk(r, lw, k, v, ah, bh, st, *, rows, t_valid, passes):
    w = r.shape[1]
    n = RWKV_GROUP * rows
    if t_valid < rows:
        valid = _iota((rows, 1), 0) < t_valid
        r, lw, k, v, ah, bh = (jnp.where(valid, x, 0.0) for x in (r, lw, k, v, ah, bh))
    lrow = _iota((rows, rows), 0)
    lcol = _iota((rows, rows), 1)
    tril = (lcol <= lrow).astype(BF16)
    cum = _mm_exact_lhs(tril, lw)
    gl = cum[rows - 1:rows]
    e_neg = jnp.exp(-cum)
    e_end = jnp.exp(gl - cum)
    qa = ah * jnp.exp(cum - lw)
    qr = r * jnp.exp(cum)
    hs = HD_B.bit_length() - 1
    rs = rows.bit_length() - 1
    bd_mask = (_iota((n, w), 0) >> rs) == (_iota((n, w), 1) >> hs)

    def expand(x):
        return jnp.where(bd_mask, jnp.concatenate([x] * RWKV_GROUP, axis=0), 0.0)

    ke = jnp.concatenate([expand(bh * e_neg), expand(k * e_neg)], axis=0)
    kge = jnp.concatenate([expand(bh * e_end), expand(k * e_end)], axis=0)
    a_a = _mm(qa, ke, NT, passes)
    a_r = _mm(qr, ke, NT, 1)
    t_idx = _iota((rows, 2 * n), 0)
    i_idx = _iota((rows, 2 * n), 1) & (rows - 1)
    a_a = jnp.where(i_idx < t_idx, a_a, 0.0)
    a_r = jnp.where(i_idx <= t_idx, a_r, 0.0)
    p1 = _mm(qa, st, NT, passes)
    y_carry = _mm(qr, st, NT, 1)
    v_bd = expand(v)
    rhs = p1 + _mm(a_a[:, n:], v_bd, NN, passes)
    nmask = (_iota((n, n), 0) >> rs) == (_iota((n, n), 1) >> rs)
    a_bd = jnp.where(nmask, jnp.concatenate([a_a[:, :n]] * RWKV_GROUP, axis=0), 0.0)
    tinv = _tri_inv(a_bd, n, rows, passes)
    c_bd = _mm(tinv, expand(rhs), NN, passes)
    cv = jnp.concatenate([c_bd, v_bd], axis=0)
    y = y_carry + _mm(a_r, cv, NN, 1)
    st_new = st * jnp.exp(gl) + _mm(cv, kge, TN, passes)
    return y, st_new


def _rwkv_scan_kernel(*refs, chunk, n_chunks, t_valid, has_s0, passes):
    if has_s0:
        r_ref, lw_ref, k_ref, v_ref, ah_ref, bh_ref, s0_ref, y_ref, s_ref, st_scr = refs
    else:
        r_ref, lw_ref, k_ref, v_ref, ah_ref, bh_ref, y_ref, s_ref, st_scr = refs
    ti = pl.program_id(2)
    w = RWKV_GROUP * HD_B

    @pl.when(ti == 0)
    def _():
        st_scr[...] = jnp.zeros_like(st_scr)
        if has_s0:
            for h in range(RWKV_GROUP):
                st_scr[h * HD_B:(h + 1) * HD_B, h * HD_B:(h + 1) * HD_B] = s0_ref[0, h]

    def body(c, carry):
        if n_chunks == 1:
            sl = slice(None)
        else:
            sl = pl.ds(pl.multiple_of(c * chunk, chunk), chunk)
        ins = [_pad_rows(ref[0, sl, :], chunk) for ref in (r_ref, lw_ref, k_ref, v_ref, ah_ref, bh_ref)]
        y, st_new = _rwkv_chunk(*ins, st_scr[...], rows=chunk, t_valid=t_valid, passes=passes)
        st_scr[...] = st_new
        y_ref[0, sl, :] = y[:t_valid]
        return carry

    if n_chunks == 1:
        body(0, 0)
    else:
        lax.fori_loop(0, n_chunks, body, 0)

    @pl.when(ti == pl.num_programs(2) - 1)
    def _():
        for h in range(RWKV_GROUP):
            s_ref[0, h] = st_scr[h * HD_B:(h + 1) * HD_B, h * HD_B:(h + 1) * HD_B]


def _rwkv_scan(r, lw, k, v, ah, bh, s0, *, batch, seq, tt, passes):
    has_s0 = s0 is not None
    w = RWKV_GROUP * HD_B
    groups = H_B // RWKV_GROUP
    chunk = min(RWKV_CHUNK, max(seq, SUBLANES))
    tt = min(tt, seq)
    n_chunks = max(tt // chunk, 1)
    t_valid = min(chunk, seq)
    tok = pl.BlockSpec((1, tt, w), lambda b, g, i: (b, i, g))
    st_spec = pl.BlockSpec((1, RWKV_GROUP, HD_B, HD_B), lambda b, g, i: (b, g, 0, 0))
    args = [r, lw, k, v, ah, bh]
    in_specs = [tok] * 6
    if has_s0:
        args.append(s0)
        in_specs.append(st_spec)
    return pl.pallas_call(
        functools.partial(_rwkv_scan_kernel, chunk=chunk, n_chunks=n_chunks, t_valid=t_valid,
                          has_s0=has_s0, passes=passes),
        out_shape=(jax.ShapeDtypeStruct((batch, seq, H_B * HD_B), F32),
                   jax.ShapeDtypeStruct((batch, H_B, HD_B, HD_B), F32)),
        grid=(batch, groups, seq // tt),
        in_specs=in_specs,
        out_specs=(tok, st_spec),
        scratch_shapes=[pltpu.VMEM((w, w), F32)],
        compiler_params=_params(("arbitrary", "arbitrary", "arbitrary")),
        name="rwkv_scan",
    )(*args)


def _merge_kernel(*refs, blocks_a):
    group_a = refs[0:7]
    group_b = refs[7:14]
    (lnw_ref, lnb_ref, seg_ref, wa_ref, wb_ref, wo_ref, n2_ref, rwt_ref, rb_ref,
     x1_o, xn_o, idx_o, gate_o, cnt_o) = refs[14:]
    from_b = pl.program_id(0) >= blocks_a
    y, bonus, g, oa, ga, gb, x = (jnp.where(from_b, rb_[...], ra_[...]) for ra_, rb_ in zip(group_a, group_b))
    seg = seg_ref[...]
    inv = 1.0 / HD_B
    mu = _seg_sum(y, seg) * inv
    dlt = y - mu
    var = _seg_sum(dlt * dlt, seg) * inv
    yn = dlt * lax.rsqrt(var + RWKV_GN_EPS) * lnw_ref[...] + lnb_ref[...]
    ob = (yn + bonus) * g
    merged = _sigmoid(ga) * _mm(oa, wa_ref[...]) + _sigmoid(gb) * _mm(ob, wb_ref[...])
    x1 = x + _mm(merged, wo_ref[...])
    x1_o[...] = x1
    xn = x1 * lax.rsqrt(jnp.mean(x1 * x1, axis=-1, keepdims=True) + EPS) * n2_ref[...]
    _tok_store(xn_o, xn)
    logits = _mm(rwt_ref[...], xn, NT, 3) + rb_ref[...]
    tm = x1.shape[0]
    eio = _iota((N_EXPERTS, tm), 0)
    idx_rows, val_rows = [], []
    cnt = jnp.zeros((N_EXPERTS, 1), F32)
    for _ in range(TOP_K):
        mx = jnp.max(logits, axis=0, keepdims=True)
        pick = jnp.min(jnp.where(logits == mx, eio, N_EXPERTS), axis=0, keepdims=True)
        hit = eio == pick
        cnt = cnt + jnp.sum(hit.astype(F32), axis=1, keepdims=True)
        logits = jnp.where(hit, -jnp.inf, logits)
        idx_rows.append(pick)
        val_rows.append(mx)
    ex = [jnp.exp(vr - val_rows[0]) for vr in val_rows]
    den = ex[0] + ex[1] + ex[2] + ex[3]
    idx_o[...] = _stack_rows(idx_rows)
    gate_o[...] = _stack_rows([e / den for e in ex])

    @pl.when(pl.program_id(0) == 0)
    def _():
        cnt_o[...] = jnp.zeros_like(cnt_o)

    cnt_o[...] += jnp.broadcast_to(cnt, cnt_o.shape)


def _merge(tok_a, tok_b, p, *, tm):
    n_a, d = tok_a[5].shape
    n_b = tok_b[5].shape[0]
    n = n_a + n_b
    blocks_a = n_a // tm
    last_a = blocks_a - 1
    ia = lambda i: jnp.minimum(i, last_a)
    ib = lambda i: jnp.maximum(i - blocks_a, 0)

    def group_specs(tok, im):
        y, _, _, oa, gates, _ = tok
        dm = gates.shape[1] // 2
        row = lambda c: pl.BlockSpec((tm, c), lambda i: (im(i), 0))
        return [row(y.shape[1]), row(y.shape[1]), row(y.shape[1]), row(oa.shape[1]),
                pl.BlockSpec((tm, dm), lambda i: (im(i), 0)), pl.BlockSpec((tm, dm), lambda i: (im(i), 1)),
                row(d)]

    def group_args(tok):
        y, bonus, g, oa, gates, x = tok
        return [y, bonus, g, oa, gates, gates, x]

    full = lambda a: pl.BlockSpec(a.shape, lambda i: (0, 0))
    small = [p["ln_w"], p["ln_b"], p["seg"], p["wa"], p["wb"], p["wo"], p["n2"], p["rwt"], p["rb"]]
    out_row = pl.BlockSpec((tm, d), lambda i: (i, 0))
    out_col = pl.BlockSpec((SUBLANES, tm), lambda i: (0, i))
    return pl.pallas_call(
        functools.partial(_merge_kernel, blocks_a=blocks_a),
        out_shape=(jax.ShapeDtypeStruct((n, d), F32), jax.ShapeDtypeStruct((n * SUBLANES, LANES), F32),
                   jax.ShapeDtypeStruct((SUBLANES, n), jnp.int32),
                   jax.ShapeDtypeStruct((SUBLANES, n), F32),
                   jax.ShapeDtypeStruct((N_EXPERTS, LANES), F32)),
        grid=(n // tm,),
        in_specs=group_specs(tok_a, ia) + group_specs(tok_b, ib) + [full(a) for a in small],
        out_specs=(out_row, pl.BlockSpec((tm * SUBLANES, LANES), lambda i: (i, 0)), out_col, out_col,
                   pl.BlockSpec((N_EXPERTS, LANES), lambda i: (0, 0))),
        compiler_params=_params(("arbitrary",)),
        name="merge_router",
    )(*(group_args(tok_a) + group_args(tok_b) + small))


def _rank_kernel(idx_ref, pstart_ref, dest_o, carry):
    i = pl.program_id(0)

    @pl.when(i == 0)
    def _():
        carry[...] = pstart_ref[...]

    tm = idx_ref.shape[1]
    eio = _iota((N_EXPERTS, tm), 0)
    idx = idx_ref[...]
    hits = [eio == idx[j:j + 1, :] for j in range(TOP_K)]
    sel = hits[0] | hits[1] | hits[2] | hits[3]
    upper = (_iota((tm, tm), 0) < _iota((tm, tm), 1)).astype(BF16)
    rank = _dg(sel.astype(BF16), upper, NN) + carry[:, 0:1]
    rows = [jnp.sum(jnp.where(h, rank, 0.0), axis=0, keepdims=True) for h in hits]
    dest_o[...] = _stack_rows(rows).astype(jnp.int32)
    carry[...] += jnp.sum(sel.astype(F32), axis=1, keepdims=True)


def _rank(idx4, pstart, *, tm):
    n = idx4.shape[1]
    return pl.pallas_call(
        _rank_kernel,
        out_shape=jax.ShapeDtypeStruct((SUBLANES, n), jnp.int32),
        grid=(n // tm,),
        in_specs=[pl.BlockSpec((SUBLANES, tm), lambda i: (0, i)),
                  pl.BlockSpec((N_EXPERTS, LANES), lambda i: (0, 0))],
        out_specs=pl.BlockSpec((SUBLANES, tm), lambda i: (0, i)),
        scratch_shapes=[pltpu.VMEM((N_EXPERTS, LANES), F32)],
        compiler_params=_params(("arbitrary",)),
        name="moe_rank",
    )(idx4, pstart)


def _scatter_kernel(pad_lo, pad_hi, na_ref, dest_ref, x_ref, xs_hbm, zeros, sem, zsem):
    tm = x_ref.shape[0]
    zrows = zeros.shape[0]
    nblk = xs_hbm.shape[0] // zrows

    @pl.when(pl.program_id(0) == 0)
    def _():
        zeros[...] = jnp.zeros_like(zeros)
        zrow = lambda r: pltpu.make_async_copy(zeros.at[0], xs_hbm.at[r], zsem)
        zblk = lambda b: pltpu.make_async_copy(zeros, xs_hbm.at[pl.ds(b * zrows, zrows)], zsem)

        def per_expert(op):
            def f(e, c):
                lax.fori_loop(pad_lo[e], pad_hi[e], lambda r, c2: (op(zrow(r)), c2)[1], 0)
                return c
            return f

        lax.fori_loop(0, N_EXPERTS, per_expert(lambda cp: cp.start()), 0)
        lax.fori_loop(na_ref[0], nblk, lambda b, c: (zblk(b).start(), c)[1], 0)
        lax.fori_loop(0, N_EXPERTS, per_expert(lambda cp: cp.wait()), 0)
        lax.fori_loop(na_ref[0], nblk, lambda b, c: (zblk(b).wait(), c)[1], 0)

    def copy(t, j):
        return pltpu.make_async_copy(x_ref.at[t], xs_hbm.at[dest_ref[j, t]], sem)

    def issue(t, c):
        for j in range(TOP_K):
            copy(t, j).start()
        return c

    def drain(t, c):
        for j in range(TOP_K):
            copy(t, j).wait()
        return c

    lax.fori_loop(0, tm, issue, 0)
    lax.fori_loop(0, tm, drain, 0)


def _scatter(pad_lo, pad_hi, n_active, dest4, x3, m_total, *, tm):
    n = x3.shape[0]
    tail = x3.shape[1:]
    return pl.pallas_call(
        _scatter_kernel,
        out_shape=jax.ShapeDtypeStruct((m_total,) + tail, x3.dtype),
        grid_spec=pltpu.PrefetchScalarGridSpec(
            num_scalar_prefetch=3,
            grid=(n // tm,),
            in_specs=[pl.BlockSpec((SUBLANES, tm), lambda i, *_: (0, i), memory_space=pltpu.SMEM),
                      pl.BlockSpec((tm,) + tail, lambda i, *_: (i, 0, 0))],
            out_specs=pl.BlockSpec(memory_space=pl.ANY),
            scratch_shapes=[pltpu.VMEM((MOE_ROWS,) + tail, x3.dtype),
                            pltpu.SemaphoreType.DMA, pltpu.SemaphoreType.DMA]),
        compiler_params=_params(("arbitrary",)),
        name="moe_scatter",
    )(pad_lo, pad_hi, n_active, dest4, x3)


def _expert_kernel(be_ref, na_ref, x_ref, wgu_ref, bgu_ref, wd_ref, bd_ref, y_ref, wgu_bf, wd_perm, wd_bf):
    b = pl.program_id(0)
    active = b < na_ref[0]
    half = LANES // 2
    dff = wd_ref.shape[1]

    @pl.when(active & ((b == 0) | (be_ref[b] != be_ref[jnp.maximum(b - 1, 0)])))
    def _():
        wgu_bf[...] = wgu_ref[0].astype(BF16)
        for c in range(wd_perm.shape[0]):
            cols = slice(c * LANES, (c + 1) * LANES)
            for p in range(dff // LANES):
                r0 = p * LANES
                wd_perm[c, pl.ds(r0, half, stride=2), :] = wd_ref[0, r0:r0 + half, cols]
                wd_perm[c, pl.ds(r0 + 1, half, stride=2), :] = wd_ref[0, r0 + half:r0 + LANES, cols]
            wd_bf[:, cols] = wd_perm[c].astype(BF16)

    @pl.when(active)
    def _():
        x = _tok_load(x_ref).astype(BF16)
        h = _dg(x, wgu_bf[...], NN) + bgu_ref[0]
        even = (_iota((1, LANES), 1) & 1) == 0
        acts = []
        for p in range(dff // LANES):
            ha = h[:, 2 * p * LANES:(2 * p + 1) * LANES]
            hb = h[:, (2 * p + 1) * LANES:(2 * p + 2) * LANES]
            hg = jnp.where(even, ha, pltpu.roll(hb, 1, 1))
            hl = jnp.where(even, pltpu.roll(ha, LANES - 1, 1), hb)
            glu = jnp.minimum(hg, SWIGLU_LIMIT)
            lin = jnp.clip(hl, -SWIGLU_LIMIT, SWIGLU_LIMIT)
            acts.append((glu * _sigmoid(SWIGLU_ALPHA * glu) * (lin + 1.0)).astype(BF16))
        act = jnp.concatenate(acts, axis=1)
        _tok_store(y_ref, _dg(act, wd_bf[...], NN) + bd_ref[0])

    @pl.when(jnp.logical_not(active))
    def _():
        y_ref[...] = jnp.zeros_like(y_ref)


def _experts(block_expert, n_active, xs2, w_gu, b_gu, w_down, b_down):
    d = w_gu.shape[1]
    dff = w_down.shape[1]
    per_tok = d // LANES
    m_total = xs2.shape[0] // per_tok
    nblk = m_total // MOE_ROWS
    wspec = lambda r, c: pl.BlockSpec((1, r, c), lambda b, be, na: (be[b], 0, 0))
    tok = pl.BlockSpec((MOE_ROWS * per_tok, LANES), lambda b, be, na: (b, 0))
    return pl.pallas_call(
        _expert_kernel,
        out_shape=jax.ShapeDtypeStruct(xs2.shape, F32),
        grid_spec=pltpu.PrefetchScalarGridSpec(
            num_scalar_prefetch=2,
            grid=(nblk,),
            in_specs=[tok, wspec(d, 2 * dff), wspec(1, 2 * dff), wspec(dff, d), wspec(1, d)],
            out_specs=tok,
            scratch_shapes=[pltpu.VMEM((d, 2 * dff), BF16), pltpu.VMEM((d // LANES, dff, LANES), F32),
                            pltpu.VMEM((dff, d), BF16)]),
        compiler_params=_params(("arbitrary",)),
        name="moe_experts",
    )(block_expert, n_active, xs2, w_gu, b_gu, w_down, b_down)


def _combine_kernel(dest_ref, gate_ref, ys_hbm, o_ref, buf, sem):
    tm = o_ref.shape[0]

    def copy(t, j):
        return pltpu.make_async_copy(ys_hbm.at[dest_ref[j, t]], buf.at[j, t], sem)

    def issue(t, c):
        for j in range(TOP_K):
            copy(t, j).start()
        return c

    def drain(t, c):
        for j in range(TOP_K):
            copy(t, j).wait()
        return c

    def mix(t, c):
        acc = gate_ref[0, t] * buf[0, t]
        for j in range(1, TOP_K):
            acc = acc + gate_ref[j, t] * buf[j, t]
        o_ref[t] = acc
        return c

    lax.fori_loop(0, tm, issue, 0)
    lax.fori_loop(0, tm, drain, 0)
    lax.fori_loop(0, tm, mix, 0)


def _combine(dest4, gate4, ys3, n, *, tm):
    tail = ys3.shape[1:]
    smem = lambda: pl.BlockSpec((SUBLANES, tm), lambda i: (0, i), memory_space=pltpu.SMEM)
    return pl.pallas_call(
        _combine_kernel,
        out_shape=jax.ShapeDtypeStruct((n,) + tail, F32),
        grid=(n // tm,),
        in_specs=[smem(), smem(), pl.BlockSpec(memory_space=pl.ANY)],
        out_specs=pl.BlockSpec((tm,) + tail, lambda i: (i, 0, 0)),
        scratch_shapes=[pltpu.VMEM((TOP_K, tm) + tail, F32), pltpu.SemaphoreType.DMA],
        compiler_params=_params(("arbitrary",)),
        name="moe_combine",
    )(dest4, gate4, ys3)


def _final_kernel(x_ref, h_ref, w_ref, o_ref):
    x = x_ref[...] + _tok_load(h_ref)
    o_ref[...] = x * lax.rsqrt(jnp.mean(x * x, axis=-1, keepdims=True) + EPS) * w_ref[...]


def _final(x1, h, w, *, tm):
    n, d = x1.shape
    row = pl.BlockSpec((tm, d), lambda i: (i, 0))
    return pl.pallas_call(
        _final_kernel,
        out_shape=jax.ShapeDtypeStruct((n, d), F32),
        grid=(n // tm,),
        in_specs=[row, pl.BlockSpec((tm * SUBLANES, LANES), lambda i: (i, 0)),
                  pl.BlockSpec((1, d), lambda i: (0, 0))],
        out_specs=row,
        compiler_params=_params(("arbitrary",)),
        name="final_norm",
    )(x1, h, w)


def _mixers(x, s_hgrn, s_rwkv, shift_prev, wts):
    batch, seq, d = x.shape
    n = batch * seq
    x2 = x.reshape(n, d)
    tm = min(512, n)
    ph = _norm_proj(x2, wts["n1"], wts["w_h"], normalize=True, tm=tm)
    prw = _norm_proj(x2, wts["n1"], wts["w_rw"], normalize=True, tm=tm)
    pg = _norm_proj(x2, wts["n1"], wts["w_g"], normalize=True, tm=tm)
    oa, s_hgrn_new = _hgrn(ph.reshape(batch, seq, -1), wts["lb"], wts["gw"], s_hgrn,
                           batch=batch, seq=seq, tt=256)
    first = None
    if shift_prev is not None:
        prev = _norm_proj(shift_prev, wts["n1"], wts["w_rw"], normalize=False, tm=shift_prev.shape[0])
        first = jnp.repeat(prev, seq, axis=0)
    r, lw, k, v, ah, bh, bonus, g = _rwkv_prep(prw, first, wts, seq=seq, tm=min(256, n))
    to3 = lambda a: a.reshape(batch, seq, -1)
    y, s_rwkv_new = _rwkv_scan(to3(r), to3(lw), to3(k), to3(v), to3(ah), to3(bh), s_rwkv,
                               batch=batch, seq=seq, tt=256, passes=3)
    return (y.reshape(n, -1), bonus, g, oa.reshape(n, -1), pg, x2), s_hgrn_new, s_rwkv_new


def kernel(x_prompt, x_sample, state_hgrn, state_rwkv, state_shift, norm1_w, w_in, hgrn_lb_logits,
           hgrn_gnorm_w, rwkv_mu, rwkv_w0, rwkv_w2, rwkv_a0, rwkv_a2, rwkv_g2, rwkv_k_k, rwkv_k_a,
           rwkv_r_k, rwkv_ln_w, rwkv_ln_b, w_branch_a, w_branch_b, w_out, norm2_w, router_w, router_b,
           expert_w_gu, expert_b_gu, expert_w_down, expert_b_down, final_norm_w):
    bp, tp, d = x_prompt.shape
    bs, ts, _ = x_sample.shape
    n_p, n_s = bp * tp, bs * ts
    n = n_p + n_s
    d_a = H_A * DK_A
    d_b = H_B * HD_B
    rw_start = 4 * d_a
    gate_start = rw_start + 3 * d_b + LORA_W + LORA_A + LORA_G
    row2 = lambda a: a.reshape(1, -1)
    lb = jnp.cumsum(jax.nn.softmax(hgrn_lb_logits.astype(F32), axis=0), axis=0)[0]
    hid = jnp.arange(d_b, dtype=jnp.int32) // HD_B
    wts = {
        "n1": row2(norm1_w[0]),
        "w_h": w_in[0][:, :rw_start].astype(BF16),
        "w_rw": w_in[0][:, rw_start:gate_start].astype(BF16),
        "w_g": w_in[0][:, gate_start:].astype(BF16),
        "lb": row2(lb), "gw": row2(hgrn_gnorm_w[0]),
        "mu": row2(rwkv_mu[0]), "w0": row2(rwkv_w0[0]), "w2": rwkv_w2[0], "a0": row2(rwkv_a0[0]),
        "a2": rwkv_a2[0], "g2": rwkv_g2[0], "k_k": row2(rwkv_k_k[0]), "k_a": row2(rwkv_k_a[0]),
        "r_k": row2(rwkv_r_k[0]), "seg": (hid[:, None] == hid[None, :]).astype(BF16),
        "ln_w": row2(rwkv_ln_w[0]), "ln_b": row2(rwkv_ln_b[0]),
        "wa": w_branch_a[0].astype(BF16), "wb": w_branch_b[0].astype(BF16), "wo": w_out[0].astype(BF16),
        "n2": row2(norm2_w[0]), "rwt": router_w[0].T, "rb": router_b[0].reshape(-1, 1),
    }
    tok_p, hgrn_p, rwkv_p = _mixers(x_prompt, None, None, None, wts)
    tok_s, hgrn_s, rwkv_s = _mixers(x_sample, state_hgrn[0], state_rwkv[0], state_shift[0], wts)
    shift = _rms_rows(jnp.concatenate([x_prompt[:, -1], x_sample[:, -1]], axis=0), wts["n1"])

    x1, xn2, idx4, gate4, counts = _merge(tok_p, tok_s, wts, tm=256)

    cnt = counts[:, 0].astype(jnp.int32)
    padded = (cnt + MOE_ROWS - 1) // MOE_ROWS * MOE_ROWS
    pend = jnp.cumsum(padded)
    pstart = pend - padded
    nblk = (n * TOP_K + N_EXPERTS * (MOE_ROWS - 1) + MOE_ROWS - 1) // MOE_ROWS
    m_total = nblk * MOE_ROWS
    block_row = jnp.arange(nblk, dtype=jnp.int32)[:, None] * MOE_ROWS
    block_expert = jnp.minimum(jnp.sum((pend[None, :] <= block_row).astype(jnp.int32), axis=1), N_EXPERTS - 1)
    n_active = (pend[-1:] // MOE_ROWS).astype(jnp.int32)
    dest4 = _rank(idx4, jnp.broadcast_to(pstart.astype(F32)[:, None], (N_EXPERTS, LANES)), tm=512)

    per_tok = d // LANES
    xs = _scatter(pstart + cnt, pend, n_active, dest4, xn2.reshape(n, per_tok, LANES), m_total, tm=256)
    ys = _experts(block_expert, n_active, xs.reshape(m_total * per_tok, LANES), expert_w_gu[0],
                  expert_b_gu[0][:, None, :], expert_w_down[0], expert_b_down[0][:, None, :])
    h = _combine(dest4, gate4, ys.reshape(m_total, per_tok, LANES), n, tm=256)
    yout = _final(x1, h.reshape(n * per_tok, LANES), row2(final_norm_w), tm=512)

    return (yout[:n_p].reshape(bp, tp, d), yout[n_p:].reshape(bs, ts, d),
            hgrn_p[None], rwkv_p[None], shift[None, :bp],
            hgrn_s[None], rwkv_s[None], shift[None, bp:])
```

```python
import functools

import jax
import jax.numpy as jnp
from jax import lax
from jax.experimental import pallas as pl
from jax.experimental.pallas import tpu as pltpu

F32 = jnp.float32
BF16 = jnp.bfloat16

H_A = 4
DK_A = 128
H_B = 8
HD_B = 64
LORA_W = 64
LORA_A = 64
LORA_G = 128
N_EXPERTS = 32
TOP_K = 4
SWIGLU_LIMIT = 7.0
SWIGLU_ALPHA = 1.702
EPS = 1e-6
RWKV_GN_EPS = 64e-5

LANES = 128
SUBLANES = 8
HGRN_CHUNK = 64
HGRN_SUB = 8
RWKV_CHUNK = 64
RWKV_GROUP = 4
RWKV_BASE = 8
MOE_ROWS = 512
LONG_SEQS_PER_STEP = 2
SHORT_SEQS_PER_STEP = 4
VMEM_LIMIT = 52 * 1024 * 1024

NN = (((1,), (0,)), ((), ()))
NT = (((1,), (1,)), ((), ()))
TN = (((0,), (0,)), ((), ()))


def _params(sem):
    return pltpu.CompilerParams(dimension_semantics=sem, vmem_limit_bytes=VMEM_LIMIT)


def _dg(a, b, dims):
    return lax.dot_general(a, b, dims, preferred_element_type=F32)


def _split2(x):
    hi = x.astype(BF16)
    lo = (x - hi.astype(F32)).astype(BF16)
    return hi, lo


def _mm(a, b, dims=NN, passes=1):
    if passes == 1:
        return _dg(a.astype(BF16), b.astype(BF16), dims)
    ah, al = _split2(a)
    bh, bl = _split2(b)
    return _dg(ah, bh, dims) + (_dg(ah, bl, dims) + _dg(al, bh, dims))


def _mm_exact_lhs(sel_bf16, x, dims=NN):
    hi = x.astype(BF16)
    r1 = x - hi.astype(F32)
    mid = r1.astype(BF16)
    lo = (r1 - mid.astype(F32)).astype(BF16)
    return _dg(sel_bf16, hi, dims) + (_dg(sel_bf16, mid, dims) + _dg(sel_bf16, lo, dims))


def _seg_sum(x, seg_bf16):
    hi = x.astype(BF16)
    r1 = x - hi.astype(F32)
    mid = r1.astype(BF16)
    lo = (r1 - mid.astype(F32)).astype(BF16)
    return _dg(hi, seg_bf16, NN) + (_dg(mid, seg_bf16, NN) + _dg(lo, seg_bf16, NN))


def _iota(shape, dim):
    return lax.broadcasted_iota(jnp.int32, shape, dim)


def _sigmoid(x):
    return 1.0 / (1.0 + jnp.exp(-x))


def _tok_load(ref):
    tm = ref.shape[0] // SUBLANES
    return jnp.concatenate([ref[pl.ds(c, tm, stride=SUBLANES), :] for c in range(SUBLANES)], axis=1)


def _tok_store(ref, val):
    tm = val.shape[0]
    for c in range(SUBLANES):
        ref[pl.ds(c, tm, stride=SUBLANES), :] = val[:, c * LANES:(c + 1) * LANES]


def _stack_rows(rows):
    n = rows[0].shape[1]
    sub = _iota((SUBLANES, n), 0)
    out = jnp.zeros((SUBLANES, n), rows[0].dtype)
    for j, r in enumerate(rows):
        out = jnp.where(sub == j, jnp.broadcast_to(r, (SUBLANES, n)), out)
    return out


def _pad_rows(x, rows):
    if x.shape[0] == rows:
        return x
    return jnp.concatenate([x, jnp.zeros((rows - x.shape[0], x.shape[1]), x.dtype)], axis=0)


def _norm_proj_kernel(x_ref, nw_ref, w_ref, o_ref, *, normalize):
    x = x_ref[...]
    if normalize:
        x = x * lax.rsqrt(jnp.mean(x * x, axis=-1, keepdims=True) + EPS) * nw_ref[...]
    o_ref[...] = _dg(x.astype(BF16), w_ref[...], NN)


def _norm_proj(x, nw, w_bf16, *, normalize, tm):
    n, d = x.shape
    nc = w_bf16.shape[1]
    return pl.pallas_call(
        functools.partial(_norm_proj_kernel, normalize=normalize),
        out_shape=jax.ShapeDtypeStruct((n, nc), F32),
        grid=(n // tm,),
        in_specs=[pl.BlockSpec((tm, d), lambda i: (i, 0)),
                  pl.BlockSpec((1, d), lambda i: (0, 0)),
                  pl.BlockSpec((d, nc), lambda i: (0, 0))],
        out_specs=pl.BlockSpec((tm, nc), lambda i: (i, 0)),
        compiler_params=_params(("arbitrary",)),
        name="norm_proj",
    )(x, nw, w_bf16)


def _rms_rows_kernel(x_ref, w_ref, o_ref):
    x = x_ref[...]
    o_ref[...] = x * lax.rsqrt(jnp.mean(x * x, axis=-1, keepdims=True) + EPS) * w_ref[...]


def _rms_rows(x, w):
    return pl.pallas_call(
        _rms_rows_kernel,
        out_shape=jax.ShapeDtypeStruct(x.shape, F32),
        name="rms_rows",
    )(x, w)


def _hgrn_chunks(hqs, hfs, his, lbvs, sts, *, rows, t_valid):
    n_u = len(hqs)
    fs = [lbv + (1.0 - lbv) * _sigmoid(hf) for lbv, hf in zip(lbvs, hfs)]
    gs = [jnp.log(f) for f in fs]
    ks = [1.0 - f for f in fs]
    qs = [hq * _sigmoid(hq) for hq in hqs]
    vs = list(his)
    if t_valid < rows:
        valid = _iota((rows, 1), 0) < t_valid
        gs, ks, qs, vs = ([jnp.where(valid, x, 0.0) for x in xs] for xs in (gs, ks, qs, vs))
    row = _iota((rows, rows), 0)
    col = _iota((rows, rows), 1)
    tril = (col <= row).astype(BF16)
    bs = [_mm_exact_lhs(tril, g) for g in gs]
    sub = min(HGRN_SUB, rows)
    m = rows // 2
    scores = [None] * n_u
    while m >= sub:
        nb = rows // m
        shift = m.bit_length() - 1
        rb = row >> shift
        pair = ((rb & 1) == 1) & ((col >> shift) == rb - 1)
        qes, kes = [], []
        for q, k, b in zip(qs, ks, bs):
            starts = [jnp.zeros((1, DK_A), F32) if j == 0 else b[j * m - 1:j * m] for j in range(nb)]
            ends = [b[j * m + m - 1:j * m + m] for j in range(nb)]
            b_start = jnp.concatenate([jnp.broadcast_to(s, (m, DK_A)) for s in starts], axis=0)
            b_end = jnp.concatenate([jnp.broadcast_to(s, (m, DK_A)) for s in ends], axis=0)
            qes.append(q * jnp.exp(b - b_start))
            kes.append(k * jnp.exp(b_end - b))
        a_ms = [jnp.where(pair, _mm(qe, ke, NT), 0.0) for qe, ke in zip(qes, kes)]
        scores = [a if s is None else s + a for s, a in zip(scores, a_ms)]
        m //= 2
    os_ = [jnp.zeros((rows, DK_A), F32)] * n_u
    if scores[0] is not None:
        os_ = [_mm(s, v, NN) for s, v in zip(scores, vs)]
    rloc = _iota((rows, 1), 0) & (sub - 1)
    for d in range(min(sub, t_valid)):
        sh = (lambda x: x) if d == 0 else (lambda x: pltpu.roll(x, d, 0))
        es = [jnp.exp(jnp.where(rloc >= d, b - sh(b), -jnp.inf)) for b in bs]
        os_ = [o + jnp.sum(q * sh(k) * e, axis=-1, keepdims=True) * sh(v)
               for o, q, k, v, e in zip(os_, qs, ks, vs, es)]
    b_lasts = [b[rows - 1:rows] for b in bs]
    kds = [k * jnp.exp(bl - b) for k, bl, b in zip(ks, b_lasts, bs)]
    qbs = [q * jnp.exp(b) for q, b in zip(qs, bs)]
    os_ = [o + _mm(qb, st, NT) for o, qb, st in zip(os_, qbs, sts)]
    st_new = [st * jnp.exp(bl) + _mm(v, kd, TN) for st, bl, v, kd in zip(sts, b_lasts, vs, kds)]
    return os_, st_new


def _hgrn_kernel(*refs, chunk, n_chunks, t_valid, has_s0, sb):
    if has_s0:
        ph_ref, lb_ref, gw_ref, s0_ref, o_ref, s_ref, st_scr = refs
    else:
        ph_ref, lb_ref, gw_ref, o_ref, s_ref, st_scr = refs
    ti = pl.program_id(1)
    d_a = H_A * DK_A
    units = [(s, h) for s in range(sb) for h in range(H_A)]

    @pl.when(ti == 0)
    def _():
        for s, h in units:
            st_scr[s, h] = s0_ref[s, h].T if has_s0 else jnp.zeros((DK_A, DK_A), F32)

    gw = gw_ref[...]

    def body(c, carry):
        if n_chunks == 1:
            sl = slice(None)
        else:
            sl = pl.ds(pl.multiple_of(c * chunk, chunk), chunk)
        cols = lambda part, h: slice(part * d_a + h * DK_A, part * d_a + (h + 1) * DK_A)
        part = lambda p: [_pad_rows(ph_ref[s, sl, cols(p, h)], chunk) for s, h in units]
        os_, st_new = _hgrn_chunks(part(0), part(1), part(2),
                                   [lb_ref[:, h * DK_A:(h + 1) * DK_A] for _, h in units],
                                   [st_scr[s, h] for s, h in units], rows=chunk, t_valid=t_valid)
        for (s, h), o, st in zip(units, os_, st_new):
            st_scr[s, h] = st
            hg = ph_ref[s, sl, cols(3, h)]
            o = o[:t_valid]
            o = o * lax.rsqrt(jnp.mean(o * o, axis=-1, keepdims=True) + EPS) * gw
            o_ref[s, sl, h * DK_A:(h + 1) * DK_A] = o * (hg * _sigmoid(hg))
        return carry

    if n_chunks == 1:
        body(0, 0)
    else:
        lax.fori_loop(0, n_chunks, body, 0)

    @pl.when(ti == pl.num_programs(1) - 1)
    def _():
        for s, h in units:
            s_ref[s, h] = st_scr[s, h].T


def _hgrn(ph, lb, gw, s0, *, batch, seq, tt, sb):
    has_s0 = s0 is not None
    chunk = min(HGRN_CHUNK, max(seq, SUBLANES))
    tt = min(tt, seq)
    n_chunks = max(tt // chunk, 1)
    t_valid = min(chunk, seq)
    d_a = H_A * DK_A
    st_spec = pl.BlockSpec((sb, H_A, DK_A, DK_A), lambda b, i: (b, 0, 0, 0))
    in_specs = [pl.BlockSpec((sb, tt, 4 * d_a), lambda b, i: (b, i, 0)),
                pl.BlockSpec((1, d_a), lambda b, i: (0, 0)),
                pl.BlockSpec((1, DK_A), lambda b, i: (0, 0))]
    args = [ph, lb, gw]
    if has_s0:
        in_specs.append(st_spec)
        args.append(s0)
    return pl.pallas_call(
        functools.partial(_hgrn_kernel, chunk=chunk, n_chunks=n_chunks, t_valid=t_valid, has_s0=has_s0, sb=sb),
        out_shape=(jax.ShapeDtypeStruct((batch, seq, d_a), F32),
                   jax.ShapeDtypeStruct((batch, H_A, DK_A, DK_A), F32)),
        grid=(batch // sb, seq // tt),
        in_specs=in_specs,
        out_specs=(pl.BlockSpec((sb, tt, d_a), lambda b, i: (b, i, 0)), st_spec),
        scratch_shapes=[pltpu.VMEM((sb, H_A, DK_A, DK_A), F32)],
        compiler_params=_params(("arbitrary", "arbitrary")),
        name="hgrn_scan",
    )(*args)


def _rwkv_prep_kernel(*refs, seq, has_first):
    if has_first:
        (rw_ref, first_ref, mu_ref, w0_ref, w2_ref, a0_ref, a2_ref, g2_ref, kk_ref, ka_ref, rk_ref, seg_ref,
         r_o, lw_o, k_o, v_o, ah_o, bh_o, bonus_o, g_o, carry) = refs
    else:
        (rw_ref, mu_ref, w0_ref, w2_ref, a0_ref, a2_ref, g2_ref, kk_ref, ka_ref, rk_ref, seg_ref,
         r_o, lw_o, k_o, v_o, ah_o, bh_o, bonus_o, g_o, carry) = refs
    d = r_o.shape[1]
    x = rw_ref[...]
    tm = x.shape[0]
    i = pl.program_id(0)

    @pl.when(i == 0)
    def _():
        carry[...] = jnp.zeros_like(carry)

    rowi = _iota((tm, 1), 0)
    prev = pltpu.roll(x, 1, 0)
    prev = jnp.where(rowi == 0, carry[0:1, :], prev)
    seq_start = ((rowi + i * tm) & (seq - 1)) == 0
    if has_first:
        prev = jnp.where(seq_start, first_ref[...], prev)
    else:
        prev = jnp.where(seq_start, 0.0, prev)
    carry[0:1, :] = x[tm - 1:tm, :]
    x = x + (prev - x) * mu_ref[...]
    r = x[:, 0:d]
    k = x[:, d:2 * d]
    v = x[:, 2 * d:3 * d]
    wlo = x[:, 3 * d:3 * d + LORA_W]
    alo = x[:, 3 * d + LORA_W:3 * d + LORA_W + LORA_A]
    glo = x[:, 3 * d + LORA_W + LORA_A:]
    seg = seg_ref[...]
    z = -(w0_ref[...] + _mm(jnp.tanh(wlo), w2_ref[...]))
    softplus = jnp.maximum(z, 0.0) + jnp.log(1.0 + jnp.exp(-jnp.abs(z)))
    lw_o[...] = -jnp.exp(-softplus - 0.5)
    a = _sigmoid(a0_ref[...] + _mm(alo, a2_ref[...]))
    g_o[...] = _mm(_sigmoid(glo), g2_ref[...])
    kk = k * kk_ref[...]
    kk = kk / jnp.maximum(jnp.sqrt(_seg_sum(kk * kk, seg)), 1e-12)
    k2 = k * (1.0 + (a - 1.0) * ka_ref[...])
    r_o[...] = r
    k_o[...] = k2
    v_o[...] = v
    ah_o[...] = -kk
    bh_o[...] = kk * a
    bonus_o[...] = _seg_sum(r * k2 * rk_ref[...], seg) * v


def _rwkv_prep(rw, first, p, *, seq, tm):
    n, wcols = rw.shape
    d = H_B * HD_B
    has_first = first is not None
    row = lambda c: pl.BlockSpec((tm, c), lambda i: (i, 0))
    full = lambda a: pl.BlockSpec(a.shape, lambda i: (0, 0))
    small = [p["mu"], p["w0"], p["w2"], p["a0"], p["a2"], p["g2"], p["k_k"], p["k_a"], p["r_k"], p["seg"]]
    args = [rw] + ([first] if has_first else []) + small
    in_specs = [row(wcols)] + ([row(wcols)] if has_first else []) + [full(a) for a in small]
    return pl.pallas_call(
        functools.partial(_rwkv_prep_kernel, seq=seq, has_first=has_first),
        out_shape=tuple(jax.ShapeDtypeStruct((n, d), F32) for _ in range(8)),
        grid=(n // tm,),
        in_specs=in_specs,
        out_specs=tuple(row(d) for _ in range(8)),
        scratch_shapes=[pltpu.VMEM((SUBLANES, wcols), F32)],
        compiler_params=_params(("arbitrary",)),
        name="rwkv_prep",
    )(*args)


def _tri_inv(a_bds, n, rows, passes):
    mm = lambda xs, ys: [_mm(x, y, NN, passes) for x, y in zip(xs, ys)]
    row = _iota((n, n), 0)
    col = _iota((n, n), 1)
    t = row & (rows - 1)
    i = col & (rows - 1)
    eye = (row == col).astype(F32)
    bshift = RWKV_BASE.bit_length() - 1
    base = (t >> bshift) == (i >> bshift)
    d1 = [jnp.where(base, a, 0.0) for a in a_bds]
    n1 = [eye + d for d in d1]
    d2 = mm(d1, d1)
    n2 = [x + y for x, y in zip(n1, mm(n1, d2))]
    d4 = mm(d2, d2)
    tinv = [x + y for x, y in zip(n2, mm(n2, d4))]
    m = RWKV_BASE
    while m < rows:
        s = m.bit_length() - 1
        tb = t >> s
        ib = i >> s
        sib = ((tb >> 1) == (ib >> 1)) & ((tb & 1) == 1) & ((ib & 1) == 0)
        off = [jnp.where(sib, a, 0.0) for a in a_bds]
        tinv = [x + y for x, y in zip(tinv, mm(mm(tinv, off), tinv))]
        m *= 2
    return tinv


def _rwkv_chunks(ins, sts, *, rows, t_valid, passes):
    w = ins[0][0].shape[1]
    n = RWKV_GROUP * rows
    if t_valid < rows:
        valid = _iota((rows, 1), 0) < t_valid
        ins = [tuple(jnp.where(valid, x, 0.0) for x in u) for u in ins]
    r, lw, k, v, ah, bh = ([u[j] for u in ins] for j in range(6))
    lrow = _iota((rows, rows), 0)
    lcol = _iota((rows, rows), 1)
    tril = (lcol <= lrow).astype(BF16)
    hs = HD_B.bit_length() - 1
    rs = rows.bit_length() - 1
    bd_mask = (_iota((n, w), 0) >> rs) == (_iota((n, w), 1) >> hs)
    nmask = (_iota((n, n), 0) >> rs) == (_iota((n, n), 1) >> rs)
    t_idx = _iota((rows, 2 * n), 0)
    i_idx = _iota((rows, 2 * n), 1) & (rows - 1)
    strict = i_idx < t_idx
    incl = i_idx <= t_idx

    def expand(x):
        return jnp.where(bd_mask, jnp.concatenate([x] * RWKV_GROUP, axis=0), 0.0)

    cum = [_mm_exact_lhs(tril, x) for x in lw]
    gl = [c[rows - 1:rows] for c in cum]
    e_neg = [jnp.exp(-c) for c in cum]
    e_end = [jnp.exp(g - c) for g, c in zip(gl, cum)]
    qa = [a * jnp.exp(c - l) for a, c, l in zip(ah, cum, lw)]
    qr = [x * jnp.exp(c) for x, c in zip(r, cum)]
    ke = [jnp.concatenate([expand(b * e), expand(x * e)], axis=0) for b, x, e in zip(bh, k, e_neg)]
    kge = [jnp.concatenate([expand(b * e), expand(x * e)], axis=0) for b, x, e in zip(bh, k, e_end)]
    a_a = [jnp.where(strict, _mm(q, x, NT, passes), 0.0) for q, x in zip(qa, ke)]
    a_r = [jnp.where(incl, _mm(q, x, NT, 1), 0.0) for q, x in zip(qr, ke)]
    v_bd = [expand(x) for x in v]
    w_kv = [_mm(a[:, n:], x, NN, passes) for a, x in zip(a_a, v_bd)]
    a_bd = [jnp.where(nmask, jnp.concatenate([a[:, :n]] * RWKV_GROUP, axis=0), 0.0) for a in a_a]
    tinv = _tri_inv(a_bd, n, rows, passes)
    p1 = [_mm(q, st, NT, passes) for q, st in zip(qa, sts)]
    y_carry = [_mm(q, st, NT, 1) for q, st in zip(qr, sts)]
    c_bd = [_mm(t, expand(p + x), NN, passes) for t, p, x in zip(tinv, p1, w_kv)]
    cv = [jnp.concatenate([c, x], axis=0) for c, x in zip(c_bd, v_bd)]
    st_new = [st * jnp.exp(g) + _mm(c, x, TN, passes) for st, g, c, x in zip(sts, gl, cv, kge)]
    ys = [yc + _mm(a, c, NN, 1) for yc, a, c in zip(y_carry, a_r, cv)]
    return ys, st_new


def _rwkv_scan_kernel(*refs, chunk, n_chunks, t_valid, has_s0, passes, sb):
    if has_s0:
        r_ref, lw_ref, k_ref, v_ref, ah_ref, bh_ref, s0_ref, y_ref, s_ref, st_scr = refs
    else:
        r_ref, lw_ref, k_ref, v_ref, ah_ref, bh_ref, y_ref, s_ref, st_scr = refs
    ti = pl.program_id(1)
    w = RWKV_GROUP * HD_B
    groups = H_B // RWKV_GROUP
    units = [(s, g) for s in range(sb) for g in range(groups)]
    blk = lambda h: slice(h * HD_B, (h + 1) * HD_B)

    @pl.when(ti == 0)
    def _():
        st_scr[...] = jnp.zeros_like(st_scr)
        if has_s0:
            for s, g in units:
                for h in range(RWKV_GROUP):
                    st_scr[s, g, blk(h), blk(h)] = s0_ref[s, g * RWKV_GROUP + h]

    def body(c, carry):
        if n_chunks == 1:
            sl = slice(None)
        else:
            sl = pl.ds(pl.multiple_of(c * chunk, chunk), chunk)
        cols = lambda g: slice(g * w, (g + 1) * w)
        in_refs = (r_ref, lw_ref, k_ref, v_ref, ah_ref, bh_ref)
        ins = [tuple(_pad_rows(ref[s, sl, cols(g)], chunk) for ref in in_refs) for s, g in units]
        ys, st_new = _rwkv_chunks(ins, [st_scr[s, g] for s, g in units],
                                  rows=chunk, t_valid=t_valid, passes=passes)
        for (s, g), y, st in zip(units, ys, st_new):
            st_scr[s, g] = st
            y_ref[s, sl, cols(g)] = y[:t_valid]
        return carry

    if n_chunks == 1:
        body(0, 0)
    else:
        lax.fori_loop(0, n_chunks, body, 0)

    @pl.when(ti == pl.num_programs(1) - 1)
    def _():
        for s, g in units:
            for h in range(RWKV_GROUP):
                s_ref[s, g * RWKV_GROUP + h] = st_scr[s, g, blk(h), blk(h)]


def _rwkv_scan(r, lw, k, v, ah, bh, s0, *, batch, seq, tt, passes, sb):
    has_s0 = s0 is not None
    w = RWKV_GROUP * HD_B
    d_b = H_B * HD_B
    chunk = min(RWKV_CHUNK, max(seq, SUBLANES))
    tt = min(tt, seq)
    n_chunks = max(tt // chunk, 1)
    t_valid = min(chunk, seq)
    tok = pl.BlockSpec((sb, tt, d_b), lambda b, i: (b, i, 0))
    st_spec = pl.BlockSpec((sb, H_B, HD_B, HD_B), lambda b, i: (b, 0, 0, 0))
    args = [r, lw, k, v, ah, bh]
    in_specs = [tok] * 6
    if has_s0:
        args.append(s0)
        in_specs.append(st_spec)
    return pl.pallas_call(
        functools.partial(_rwkv_scan_kernel, chunk=chunk, n_chunks=n_chunks, t_valid=t_valid,
                          has_s0=has_s0, passes=passes, sb=sb),
        out_shape=(jax.ShapeDtypeStruct((batch, seq, d_b), F32),
                   jax.ShapeDtypeStruct((batch, H_B, HD_B, HD_B), F32)),
        grid=(batch // sb, seq // tt),
        in_specs=in_specs,
        out_specs=(tok, st_spec),
        scratch_shapes=[pltpu.VMEM((sb, H_B // RWKV_GROUP, w, w), F32)],
        compiler_params=_params(("arbitrary", "arbitrary")),
        name="rwkv_scan",
    )(*args)


def _merge_kernel(*refs, blocks_a):
    group_a = refs[0:7]
    group_b = refs[7:14]
    (lnw_ref, lnb_ref, seg_ref, wa_ref, wb_ref, wo_ref, n2_ref, rwt_ref, rb_ref,
     x1_o, xn_o, idx_o, gate_o, cnt_o) = refs[14:]
    from_b = pl.program_id(0) >= blocks_a
    y, bonus, g, oa, ga, gb, x = (jnp.where(from_b, rb_[...], ra_[...]) for ra_, rb_ in zip(group_a, group_b))
    seg = seg_ref[...]
    inv = 1.0 / HD_B
    mu = _seg_sum(y, seg) * inv
    dlt = y - mu
    var = _seg_sum(dlt * dlt, seg) * inv
    yn = dlt * lax.rsqrt(var + RWKV_GN_EPS) * lnw_ref[...] + lnb_ref[...]
    ob = (yn + bonus) * g
    merged = _sigmoid(ga) * _mm(oa, wa_ref[...]) + _sigmoid(gb) * _mm(ob, wb_ref[...])
    x1 = x + _mm(merged, wo_ref[...])
    x1_o[...] = x1
    xn = x1 * lax.rsqrt(jnp.mean(x1 * x1, axis=-1, keepdims=True) + EPS) * n2_ref[...]
    _tok_store(xn_o, xn)
    logits = _mm(rwt_ref[...], xn, NT, 3) + rb_ref[...]
    tm = x1.shape[0]
    eio = _iota((N_EXPERTS, tm), 0)
    idx_rows, val_rows = [], []
    cnt = jnp.zeros((N_EXPERTS, 1), F32)
    for _ in range(TOP_K):
        mx = jnp.max(logits, axis=0, keepdims=True)
        pick = jnp.min(jnp.where(logits == mx, eio, N_EXPERTS), axis=0, keepdims=True)
        hit = eio == pick
        cnt = cnt + jnp.sum(hit.astype(F32), axis=1, keepdims=True)
        logits = jnp.where(hit, -jnp.inf, logits)
        idx_rows.append(pick)
        val_rows.append(mx)
    ex = [jnp.exp(vr - val_rows[0]) for vr in val_rows]
    den = ex[0] + ex[1] + ex[2] + ex[3]
    idx_o[...] = _stack_rows(idx_rows)
    gate_o[...] = _stack_rows([e / den for e in ex])

    @pl.when(pl.program_id(0) == 0)
    def _():
        cnt_o[...] = jnp.zeros_like(cnt_o)

    cnt_o[...] += jnp.broadcast_to(cnt, cnt_o.shape)


def _merge(tok_a, tok_b, p, *, tm):
    n_a, d = tok_a[5].shape
    n_b = tok_b[5].shape[0]
    n = n_a + n_b
    blocks_a = n_a // tm
    last_a = blocks_a - 1
    ia = lambda i: jnp.minimum(i, last_a)
    ib = lambda i: jnp.maximum(i - blocks_a, 0)

    def group_specs(tok, im):
        y, _, _, oa, gates, _ = tok
        dm = gates.shape[1] // 2
        row = lambda c: pl.BlockSpec((tm, c), lambda i: (im(i), 0))
        return [row(y.shape[1]), row(y.shape[1]), row(y.shape[1]), row(oa.shape[1]),
                pl.BlockSpec((tm, dm), lambda i: (im(i), 0)), pl.BlockSpec((tm, dm), lambda i: (im(i), 1)),
                row(d)]

    def group_args(tok):
        y, bonus, g, oa, gates, x = tok
        return [y, bonus, g, oa, gates, gates, x]

    full = lambda a: pl.BlockSpec(a.shape, lambda i: (0, 0))
    small = [p["ln_w"], p["ln_b"], p["seg"], p["wa"], p["wb"], p["wo"], p["n2"], p["rwt"], p["rb"]]
    out_row = pl.BlockSpec((tm, d), lambda i: (i, 0))
    out_col = pl.BlockSpec((SUBLANES, tm), lambda i: (0, i))
    return pl.pallas_call(
        functools.partial(_merge_kernel, blocks_a=blocks_a),
        out_shape=(jax.ShapeDtypeStruct((n, d), F32), jax.ShapeDtypeStruct((n * SUBLANES, LANES), F32),
                   jax.ShapeDtypeStruct((SUBLANES, n), jnp.int32),
                   jax.ShapeDtypeStruct((SUBLANES, n), F32),
                   jax.ShapeDtypeStruct((N_EXPERTS, LANES), F32)),
        grid=(n // tm,),
        in_specs=group_specs(tok_a, ia) + group_specs(tok_b, ib) + [full(a) for a in small],
        out_specs=(out_row, pl.BlockSpec((tm * SUBLANES, LANES), lambda i: (i, 0)), out_col, out_col,
                   pl.BlockSpec((N_EXPERTS, LANES), lambda i: (0, 0))),
        compiler_params=_params(("arbitrary",)),
        name="merge_router",
    )(*(group_args(tok_a) + group_args(tok_b) + small))


def _rank_kernel(idx_ref, pstart_ref, dest_o, carry):
    i = pl.program_id(0)

    @pl.when(i == 0)
    def _():
        carry[...] = pstart_ref[...]

    tm = idx_ref.shape[1]
    eio = _iota((N_EXPERTS, tm), 0)
    idx = idx_ref[...]
    hits = [eio == idx[j:j + 1, :] for j in range(TOP_K)]
    sel = hits[0] | hits[1] | hits[2] | hits[3]
    upper = (_iota((tm, tm), 0) < _iota((tm, tm), 1)).astype(BF16)
    rank = _dg(sel.astype(BF16), upper, NN) + carry[:, 0:1]
    rows = [jnp.sum(jnp.where(h, rank, 0.0), axis=0, keepdims=True) for h in hits]
    dest_o[...] = _stack_rows(rows).astype(jnp.int32)
    carry[...] += jnp.sum(sel.astype(F32), axis=1, keepdims=True)


def _rank(idx4, pstart, *, tm):
    n = idx4.shape[1]
    return pl.pallas_call(
        _rank_kernel,
        out_shape=jax.ShapeDtypeStruct((SUBLANES, n), jnp.int32),
        grid=(n // tm,),
        in_specs=[pl.BlockSpec((SUBLANES, tm), lambda i: (0, i)),
                  pl.BlockSpec((N_EXPERTS, LANES), lambda i: (0, 0))],
        out_specs=pl.BlockSpec((SUBLANES, tm), lambda i: (0, i)),
        scratch_shapes=[pltpu.VMEM((N_EXPERTS, LANES), F32)],
        compiler_params=_params(("arbitrary",)),
        name="moe_rank",
    )(idx4, pstart)


def _scatter_kernel(pad_lo, pad_hi, na_ref, dest_ref, x_ref, xs_hbm, zeros, sem, zsem):
    tm = x_ref.shape[0]
    zrows = zeros.shape[0]
    nblk = xs_hbm.shape[0] // zrows

    @pl.when(pl.program_id(0) == 0)
    def _():
        zeros[...] = jnp.zeros_like(zeros)
        zrow = lambda r: pltpu.make_async_copy(zeros.at[0], xs_hbm.at[r], zsem)
        zblk = lambda b: pltpu.make_async_copy(zeros, xs_hbm.at[pl.ds(b * zrows, zrows)], zsem)

        def per_expert(op):
            def f(e, c):
                lax.fori_loop(pad_lo[e], pad_hi[e], lambda r, c2: (op(zrow(r)), c2)[1], 0)
                return c
            return f

        lax.fori_loop(0, N_EXPERTS, per_expert(lambda cp: cp.start()), 0)
        lax.fori_loop(na_ref[0], nblk, lambda b, c: (zblk(b).start(), c)[1], 0)
        lax.fori_loop(0, N_EXPERTS, per_expert(lambda cp: cp.wait()), 0)
        lax.fori_loop(na_ref[0], nblk, lambda b, c: (zblk(b).wait(), c)[1], 0)

    def copy(t, j):
        return pltpu.make_async_copy(x_ref.at[t], xs_hbm.at[dest_ref[j, t]], sem)

    def issue(t, c):
        for j in range(TOP_K):
            copy(t, j).start()
        return c

    def drain(t, c):
        for j in range(TOP_K):
            copy(t, j).wait()
        return c

    lax.fori_loop(0, tm, issue, 0)
    lax.fori_loop(0, tm, drain, 0)


def _scatter(pad_lo, pad_hi, n_active, dest4, x3, m_total, *, tm):
    n = x3.shape[0]
    tail = x3.shape[1:]
    return pl.pallas_call(
        _scatter_kernel,
        out_shape=jax.ShapeDtypeStruct((m_total,) + tail, x3.dtype),
        grid_spec=pltpu.PrefetchScalarGridSpec(
            num_scalar_prefetch=3,
            grid=(n // tm,),
            in_specs=[pl.BlockSpec((SUBLANES, tm), lambda i, *_: (0, i), memory_space=pltpu.SMEM),
                      pl.BlockSpec((tm,) + tail, lambda i, *_: (i, 0, 0))],
            out_specs=pl.BlockSpec(memory_space=pl.ANY),
            scratch_shapes=[pltpu.VMEM((MOE_ROWS,) + tail, x3.dtype),
                            pltpu.SemaphoreType.DMA, pltpu.SemaphoreType.DMA]),
        compiler_params=_params(("arbitrary",)),
        name="moe_scatter",
    )(pad_lo, pad_hi, n_active, dest4, x3)


def _expert_kernel(be_ref, na_ref, x_ref, wgu_ref, bgu_ref, wd_ref, bd_ref, y_ref, wgu_bf, wd_perm, wd_bf):
    b = pl.program_id(0)
    active = b < na_ref[0]
    half = LANES // 2
    dff = wd_ref.shape[1]

    @pl.when(active & ((b == 0) | (be_ref[b] != be_ref[jnp.maximum(b - 1, 0)])))
    def _():
        wgu_bf[...] = wgu_ref[0].astype(BF16)
        for c in range(wd_ref.shape[2] // LANES):
            cols = slice(c * LANES, (c + 1) * LANES)
            for p in range(dff // LANES):
                r0 = p * LANES
                wd_perm[pl.ds(r0, half, stride=2), :] = wd_ref[0, r0:r0 + half, cols]
                wd_perm[pl.ds(r0 + 1, half, stride=2), :] = wd_ref[0, r0 + half:r0 + LANES, cols]
            wd_bf[:, cols] = wd_perm[...].astype(BF16)

    @pl.when(active)
    def _():
        x = _tok_load(x_ref).astype(BF16)
        h = _dg(x, wgu_bf[...], NN) + bgu_ref[0]
        even = (_iota((1, LANES), 1) & 1) == 0
        acts = []
        for p in range(dff // LANES):
            ha = h[:, 2 * p * LANES:(2 * p + 1) * LANES]
            hb = h[:, (2 * p + 1) * LANES:(2 * p + 2) * LANES]
            hg = jnp.where(even, ha, pltpu.roll(hb, 1, 1))
            hl = jnp.where(even, pltpu.roll(ha, LANES - 1, 1), hb)
            glu = jnp.minimum(hg, SWIGLU_LIMIT)
            lin = jnp.clip(hl, -SWIGLU_LIMIT, SWIGLU_LIMIT)
            acts.append((glu * _sigmoid(SWIGLU_ALPHA * glu) * (lin + 1.0)).astype(BF16))
        act = jnp.concatenate(acts, axis=1)
        _tok_store(y_ref, _dg(act, wd_bf[...], NN) + bd_ref[0])

    @pl.when(jnp.logical_not(active))
    def _():
        y_ref[...] = jnp.zeros_like(y_ref)


def _experts(block_expert, n_active, xs2, w_gu, b_gu, w_down, b_down):
    d = w_gu.shape[1]
    dff = w_down.shape[1]
    per_tok = d // LANES
    m_total = xs2.shape[0] // per_tok
    nblk = m_total // MOE_ROWS
    wspec = lambda r, c: pl.BlockSpec((1, r, c), lambda b, be, na: (be[b], 0, 0))
    tok = pl.BlockSpec((MOE_ROWS * per_tok, LANES), lambda b, be, na: (b, 0))
    return pl.pallas_call(
        _expert_kernel,
        out_shape=jax.ShapeDtypeStruct(xs2.shape, F32),
        grid_spec=pltpu.PrefetchScalarGridSpec(
            num_scalar_prefetch=2,
            grid=(nblk,),
            in_specs=[tok, wspec(d, 2 * dff), wspec(1, 2 * dff), wspec(dff, d), wspec(1, d)],
            out_specs=tok,
            scratch_shapes=[pltpu.VMEM((d, 2 * dff), BF16), pltpu.VMEM((dff, LANES), F32),
                            pltpu.VMEM((dff, d), BF16)]),
        compiler_params=_params(("arbitrary",)),
        name="moe_experts",
    )(block_expert, n_active, xs2, w_gu, b_gu, w_down, b_down)


def _combine_kernel(dest_ref, gate_ref, ys_hbm, o_ref, buf, sem):
    tm = o_ref.shape[0]

    def copy(t, j):
        return pltpu.make_async_copy(ys_hbm.at[dest_ref[j, t]], buf.at[j, t], sem)

    def issue(t, c):
        for j in range(TOP_K):
            copy(t, j).start()
        return c

    def drain(t, c):
        for j in range(TOP_K):
            copy(t, j).wait()
        return c

    def mix(t, c):
        acc = gate_ref[0, t] * buf[0, t]
        for j in range(1, TOP_K):
            acc = acc + gate_ref[j, t] * buf[j, t]
        o_ref[t] = acc
        return c

    lax.fori_loop(0, tm, issue, 0)
    lax.fori_loop(0, tm, drain, 0)
    lax.fori_loop(0, tm, mix, 0)


def _combine(dest4, gate4, ys3, n, *, tm):
    tail = ys3.shape[1:]
    smem = lambda: pl.BlockSpec((SUBLANES, tm), lambda i: (0, i), memory_space=pltpu.SMEM)
    return pl.pallas_call(
        _combine_kernel,
        out_shape=jax.ShapeDtypeStruct((n,) + tail, F32),
        grid=(n // tm,),
        in_specs=[smem(), smem(), pl.BlockSpec(memory_space=pl.ANY)],
        out_specs=pl.BlockSpec((tm,) + tail, lambda i: (i, 0, 0)),
        scratch_shapes=[pltpu.VMEM((TOP_K, tm) + tail, F32), pltpu.SemaphoreType.DMA],
        compiler_params=_params(("arbitrary",)),
        name="moe_combine",
    )(dest4, gate4, ys3)


def _final_kernel(x_ref, h_ref, w_ref, o_ref):
    x = x_ref[...] + _tok_load(h_ref)
    o_ref[...] = x * lax.rsqrt(jnp.mean(x * x, axis=-1, keepdims=True) + EPS) * w_ref[...]


def _final(x1, h, w, *, tm):
    n, d = x1.shape
    row = pl.BlockSpec((tm, d), lambda i: (i, 0))
    return pl.pallas_call(
        _final_kernel,
        out_shape=jax.ShapeDtypeStruct((n, d), F32),
        grid=(n // tm,),
        in_specs=[row, pl.BlockSpec((tm * SUBLANES, LANES), lambda i: (i, 0)),
                  pl.BlockSpec((1, d), lambda i: (0, 0))],
        out_specs=row,
        compiler_params=_params(("arbitrary",)),
        name="final_norm",
    )(x1, h, w)


def _mixers(x, s_hgrn, s_rwkv, shift_prev, wts):
    batch, seq, d = x.shape
    n = batch * seq
    x2 = x.reshape(n, d)
    tm = min(512, n)
    seqs_per_step = LONG_SEQS_PER_STEP if seq >= HGRN_CHUNK else SHORT_SEQS_PER_STEP
    ph = _norm_proj(x2, wts["n1"], wts["w_h"], normalize=True, tm=tm)
    prw = _norm_proj(x2, wts["n1"], wts["w_rw"], normalize=True, tm=tm)
    pg = _norm_proj(x2, wts["n1"], wts["w_g"], normalize=True, tm=tm)
    oa, s_hgrn_new = _hgrn(ph.reshape(batch, seq, -1), wts["lb"], wts["gw"], s_hgrn,
                           batch=batch, seq=seq, tt=256, sb=seqs_per_step)
    first = None
    if shift_prev is not None:
        prev = _norm_proj(shift_prev, wts["n1"], wts["w_rw"], normalize=False, tm=shift_prev.shape[0])
        first = jnp.repeat(prev, seq, axis=0)
    r, lw, k, v, ah, bh, bonus, g = _rwkv_prep(prw, first, wts, seq=seq, tm=min(256, n))
    to3 = lambda a: a.reshape(batch, seq, -1)
    y, s_rwkv_new = _rwkv_scan(to3(r), to3(lw), to3(k), to3(v), to3(ah), to3(bh), s_rwkv,
                               batch=batch, seq=seq, tt=256, passes=1, sb=seqs_per_step)
    return (y.reshape(n, -1), bonus, g, oa.reshape(n, -1), pg, x2), s_hgrn_new, s_rwkv_new


def kernel(x_prompt, x_sample, state_hgrn, state_rwkv, state_shift, norm1_w, w_in, hgrn_lb_logits,
           hgrn_gnorm_w, rwkv_mu, rwkv_w0, rwkv_w2, rwkv_a0, rwkv_a2, rwkv_g2, rwkv_k_k, rwkv_k_a,
           rwkv_r_k, rwkv_ln_w, rwkv_ln_b, w_branch_a, w_branch_b, w_out, norm2_w, router_w, router_b,
           expert_w_gu, expert_b_gu, expert_w_down, expert_b_down, final_norm_w):
    bp, tp, d = x_prompt.shape
    bs, ts, _ = x_sample.shape
    n_p, n_s = bp * tp, bs * ts
    n = n_p + n_s
    d_a = H_A * DK_A
    d_b = H_B * HD_B
    rw_start = 4 * d_a
    gate_start = rw_start + 3 * d_b + LORA_W + LORA_A + LORA_G
    row2 = lambda a: a.reshape(1, -1)
    lb = jnp.cumsum(jax.nn.softmax(hgrn_lb_logits.astype(F32), axis=0), axis=0)[0]
    hid = jnp.arange(d_b, dtype=jnp.int32) // HD_B
    wts = {
        "n1": row2(norm1_w[0]),
        "w_h": w_in[0][:, :rw_start].astype(BF16),
        "w_rw": w_in[0][:, rw_start:gate_start].astype(BF16),
        "w_g": w_in[0][:, gate_start:].astype(BF16),
        "lb": row2(lb), "gw": row2(hgrn_gnorm_w[0]),
        "mu": row2(rwkv_mu[0]), "w0": row2(rwkv_w0[0]), "w2": rwkv_w2[0], "a0": row2(rwkv_a0[0]),
        "a2": rwkv_a2[0], "g2": rwkv_g2[0], "k_k": row2(rwkv_k_k[0]), "k_a": row2(rwkv_k_a[0]),
        "r_k": row2(rwkv_r_k[0]), "seg": (hid[:, None] == hid[None, :]).astype(BF16),
        "ln_w": row2(rwkv_ln_w[0]), "ln_b": row2(rwkv_ln_b[0]),
        "wa": w_branch_a[0].astype(BF16), "wb": w_branch_b[0].astype(BF16), "wo": w_out[0].astype(BF16),
        "n2": row2(norm2_w[0]), "rwt": router_w[0].T, "rb": router_b[0].reshape(-1, 1),
    }
    tok_p, hgrn_p, rwkv_p = _mixers(x_prompt, None, None, None, wts)
    tok_s, hgrn_s, rwkv_s = _mixers(x_sample, state_hgrn[0], state_rwkv[0], state_shift[0], wts)
    shift = _rms_rows(jnp.concatenate([x_prompt[:, -1], x_sample[:, -1]], axis=0), wts["n1"])

    x1, xn2, idx4, gate4, counts = _merge(tok_p, tok_s, wts, tm=256)

    cnt = counts[:, 0].astype(jnp.int32)
    padded = (cnt + MOE_ROWS - 1) // MOE_ROWS * MOE_ROWS
    pend = jnp.cumsum(padded)
    pstart = pend - padded
    nblk = (n * TOP_K + N_EXPERTS * (MOE_ROWS - 1) + MOE_ROWS - 1) // MOE_ROWS
    m_total = nblk * MOE_ROWS
    block_row = jnp.arange(nblk, dtype=jnp.int32)[:, None] * MOE_ROWS
    block_expert = jnp.minimum(jnp.sum((pend[None, :] <= block_row).astype(jnp.int32), axis=1), N_EXPERTS - 1)
    n_active = (pend[-1:] // MOE_ROWS).astype(jnp.int32)
    dest4 = _rank(idx4, jnp.broadcast_to(pstart.astype(F32)[:, None], (N_EXPERTS, LANES)), tm=512)

    per_tok = d // LANES
    xs = _scatter(pstart + cnt, pend, n_active, dest4, xn2.reshape(n, per_tok, LANES), m_total, tm=256)
    ys = _experts(block_expert, n_active, xs.reshape(m_total * per_tok, LANES), expert_w_gu[0],
                  expert_b_gu[0][:, None, :], expert_w_down[0], expert_b_down[0][:, None, :])
    h = _combine(dest4, gate4, ys.reshape(m_total, per_tok, LANES), n, tm=256)
    yout = _final(x1, h.reshape(n * per_tok, LANES), row2(final_norm_w), tm=512)

    return (yout[:n_p].reshape(bp, tp, d), yout[n_p:].reshape(bs, ts, d),
            hgrn_p[None], rwkv_p[None], shift[None, :bp],
            hgrn_s[None], rwkv_s[None], shift[None, bp:])
```

```python
import functools

import jax
import jax.numpy as jnp
from jax import lax
from jax.experimental import pallas as pl
from jax.experimental.pallas import tpu as pltpu

F32 = jnp.float32
BF16 = jnp.bfloat16

H_A = 4
DK_A = 128
H_B = 8
HD_B = 64
LORA_W = 64
LORA_A = 64
LORA_G = 128
N_EXPERTS = 32
TOP_K = 4
SWIGLU_LIMIT = 7.0
SWIGLU_ALPHA = 1.702
EPS = 1e-6
RWKV_GN_EPS = 64e-5

LANES = 128
SUBLANES = 8
HGRN_CHUNK = 64
HGRN_SUB = 8
RWKV_CHUNK = 64
RWKV_GROUP = 4
RWKV_BASE = 8
MOE_ROWS = 512
DMA_UNROLL = 4
LONG_SEQS_PER_STEP = 4
SHORT_SEQS_PER_STEP = 4
VMEM_LIMIT = 52 * 1024 * 1024

NN = (((1,), (0,)), ((), ()))
NT = (((1,), (1,)), ((), ()))
TN = (((0,), (0,)), ((), ()))


def _params(sem):
    return pltpu.CompilerParams(dimension_semantics=sem, vmem_limit_bytes=VMEM_LIMIT)


def _dg(a, b, dims):
    return lax.dot_general(a, b, dims, preferred_element_type=F32)


def _split2(x):
    hi = x.astype(BF16)
    lo = (x - hi.astype(F32)).astype(BF16)
    return hi, lo


def _mm(a, b, dims=NN, passes=1):
    if passes == 1:
        return _dg(a.astype(BF16), b.astype(BF16), dims)
    ah, al = _split2(a)
    bh, bl = _split2(b)
    return _dg(ah, bh, dims) + (_dg(ah, bl, dims) + _dg(al, bh, dims))


def _mm_exact_lhs(sel_bf16, x, dims=NN):
    hi = x.astype(BF16)
    r1 = x - hi.astype(F32)
    mid = r1.astype(BF16)
    lo = (r1 - mid.astype(F32)).astype(BF16)
    return _dg(sel_bf16, hi, dims) + (_dg(sel_bf16, mid, dims) + _dg(sel_bf16, lo, dims))


def _seg_sum(x, seg_bf16):
    w = seg_bf16.shape[0]
    hi, lo = _split2(x)
    parts = [_dg(hi[:, c:c + w], seg_bf16, NN) + _dg(lo[:, c:c + w], seg_bf16, NN)
             for c in range(0, x.shape[1], w)]
    return parts[0] if len(parts) == 1 else jnp.concatenate(parts, axis=1)


def _iota(shape, dim):
    return lax.broadcasted_iota(jnp.int32, shape, dim)


def _sigmoid(x):
    return 1.0 / (1.0 + jnp.exp(-x))


def _tok_load(ref):
    tm = ref.shape[0] // SUBLANES
    return jnp.concatenate([ref[pl.ds(c, tm, stride=SUBLANES), :] for c in range(SUBLANES)], axis=1)


def _tok_store(ref, val):
    tm = val.shape[0]
    for c in range(SUBLANES):
        ref[pl.ds(c, tm, stride=SUBLANES), :] = val[:, c * LANES:(c + 1) * LANES]


def _stack_rows(rows):
    n = rows[0].shape[1]
    sub = _iota((SUBLANES, n), 0)
    out = jnp.zeros((SUBLANES, n), rows[0].dtype)
    for j, r in enumerate(rows):
        out = jnp.where(sub == j, jnp.broadcast_to(r, (SUBLANES, n)), out)
    return out


def _pad_rows(x, rows):
    if x.shape[0] == rows:
        return x
    return jnp.concatenate([x, jnp.zeros((rows - x.shape[0], x.shape[1]), x.dtype)], axis=0)


def _norm_proj_kernel(x_ref, nw_ref, w_ref, o_ref, *, normalize):
    x = x_ref[...]
    if normalize:
        x = x * lax.rsqrt(jnp.mean(x * x, axis=-1, keepdims=True) + EPS) * nw_ref[...]
    o_ref[...] = _dg(x.astype(BF16), w_ref[...], NN)


def _norm_proj(x, nw, w_bf16, *, normalize, tm):
    n, d = x.shape
    nc = w_bf16.shape[1]
    return pl.pallas_call(
        functools.partial(_norm_proj_kernel, normalize=normalize),
        out_shape=jax.ShapeDtypeStruct((n, nc), F32),
        grid=(n // tm,),
        in_specs=[pl.BlockSpec((tm, d), lambda i: (i, 0)),
                  pl.BlockSpec((1, d), lambda i: (0, 0)),
                  pl.BlockSpec((d, nc), lambda i: (0, 0))],
        out_specs=pl.BlockSpec((tm, nc), lambda i: (i, 0)),
        compiler_params=_params(("arbitrary",)),
        name="norm_proj",
    )(x, nw, w_bf16)


def _rms_rows_kernel(x_ref, w_ref, o_ref):
    x = x_ref[...]
    o_ref[...] = x * lax.rsqrt(jnp.mean(x * x, axis=-1, keepdims=True) + EPS) * w_ref[...]


def _rms_rows(x, w):
    return pl.pallas_call(
        _rms_rows_kernel,
        out_shape=jax.ShapeDtypeStruct(x.shape, F32),
        name="rms_rows",
    )(x, w)


def _hgrn_chunks(hqs, hfs, his, lbvs, sts, *, rows, t_valid):
    n_u = len(hqs)
    fs = [lbv + (1.0 - lbv) * _sigmoid(hf) for lbv, hf in zip(lbvs, hfs)]
    gs = [jnp.log(f) for f in fs]
    ks = [1.0 - f for f in fs]
    qs = [hq * _sigmoid(hq) for hq in hqs]
    vs = list(his)
    if t_valid < rows:
        valid = _iota((rows, 1), 0) < t_valid
        gs, ks, qs, vs = ([jnp.where(valid, x, 0.0) for x in xs] for xs in (gs, ks, qs, vs))
    row = _iota((rows, rows), 0)
    col = _iota((rows, rows), 1)
    tril = (col <= row).astype(BF16)
    bs = [_mm_exact_lhs(tril, g) for g in gs]
    sub = min(HGRN_SUB, rows)
    m = rows // 2
    scores = [None] * n_u
    while m >= sub:
        nb = rows // m
        shift = m.bit_length() - 1
        rb = row >> shift
        pair = ((rb & 1) == 1) & ((col >> shift) == rb - 1)
        qes, kes = [], []
        for q, k, b in zip(qs, ks, bs):
            starts = [jnp.zeros((1, DK_A), F32) if j == 0 else b[j * m - 1:j * m] for j in range(nb)]
            ends = [b[j * m + m - 1:j * m + m] for j in range(nb)]
            b_start = jnp.concatenate([jnp.broadcast_to(s, (m, DK_A)) for s in starts], axis=0)
            b_end = jnp.concatenate([jnp.broadcast_to(s, (m, DK_A)) for s in ends], axis=0)
            qes.append(q * jnp.exp(b - b_start))
            kes.append(k * jnp.exp(b_end - b))
        a_ms = [jnp.where(pair, _mm(qe, ke, NT), 0.0) for qe, ke in zip(qes, kes)]
        scores = [a if s is None else s + a for s, a in zip(scores, a_ms)]
        m //= 2
    os_ = [jnp.zeros((rows, DK_A), F32)] * n_u
    if scores[0] is not None:
        os_ = [_mm(s, v, NN) for s, v in zip(scores, vs)]
    rloc = _iota((rows, 1), 0) & (sub - 1)
    for d in range(min(sub, t_valid)):
        sh = (lambda x: x) if d == 0 else (lambda x: pltpu.roll(x, d, 0))
        es = [jnp.exp(jnp.where(rloc >= d, b - sh(b), -jnp.inf)) for b in bs]
        os_ = [o + jnp.sum(q * sh(k) * e, axis=-1, keepdims=True) * sh(v)
               for o, q, k, v, e in zip(os_, qs, ks, vs, es)]
    b_lasts = [b[rows - 1:rows] for b in bs]
    kds = [k * jnp.exp(bl - b) for k, bl, b in zip(ks, b_lasts, bs)]
    qbs = [q * jnp.exp(b) for q, b in zip(qs, bs)]
    os_ = [o + _mm(qb, st, NT) for o, qb, st in zip(os_, qbs, sts)]
    st_new = [st * jnp.exp(bl) + _mm(v, kd, TN) for st, bl, v, kd in zip(sts, b_lasts, vs, kds)]
    return os_, st_new


def _hgrn_kernel(*refs, chunk, n_chunks, t_valid, has_s0, sb):
    if has_s0:
        ph_ref, lb_ref, gw_ref, s0_ref, o_ref, s_ref, st_scr = refs
    else:
        ph_ref, lb_ref, gw_ref, o_ref, s_ref, st_scr = refs
    ti = pl.program_id(1)
    d_a = H_A * DK_A
    units = [(s, h) for s in range(sb) for h in range(H_A)]

    @pl.when(ti == 0)
    def _():
        for s, h in units:
            st_scr[s, h] = s0_ref[s, h].T if has_s0 else jnp.zeros((DK_A, DK_A), F32)

    gw = gw_ref[...]

    def body(c, carry):
        if n_chunks == 1:
            sl = slice(None)
        else:
            sl = pl.ds(pl.multiple_of(c * chunk, chunk), chunk)
        cols = lambda part, h: slice(part * d_a + h * DK_A, part * d_a + (h + 1) * DK_A)
        part = lambda p: [_pad_rows(ph_ref[s, sl, cols(p, h)], chunk) for s, h in units]
        os_, st_new = _hgrn_chunks(part(0), part(1), part(2),
                                   [lb_ref[:, h * DK_A:(h + 1) * DK_A] for _, h in units],
                                   [st_scr[s, h] for s, h in units], rows=chunk, t_valid=t_valid)
        for (s, h), o, st in zip(units, os_, st_new):
            st_scr[s, h] = st
            hg = ph_ref[s, sl, cols(3, h)]
            o = o[:t_valid]
            o = o * lax.rsqrt(jnp.mean(o * o, axis=-1, keepdims=True) + EPS) * gw
            o_ref[s, sl, h * DK_A:(h + 1) * DK_A] = o * (hg * _sigmoid(hg))
        return carry

    if n_chunks == 1:
        body(0, 0)
    else:
        lax.fori_loop(0, n_chunks, body, 0)

    @pl.when(ti == pl.num_programs(1) - 1)
    def _():
        for s, h in units:
            s_ref[s, h] = st_scr[s, h].T


def _hgrn(ph, lb, gw, s0, *, batch, seq, tt, sb):
    has_s0 = s0 is not None
    chunk = min(HGRN_CHUNK, max(seq, SUBLANES))
    tt = min(tt, seq)
    n_chunks = max(tt // chunk, 1)
    t_valid = min(chunk, seq)
    d_a = H_A * DK_A
    st_spec = pl.BlockSpec((sb, H_A, DK_A, DK_A), lambda b, i: (b, 0, 0, 0))
    in_specs = [pl.BlockSpec((sb, tt, 4 * d_a), lambda b, i: (b, i, 0)),
                pl.BlockSpec((1, d_a), lambda b, i: (0, 0)),
                pl.BlockSpec((1, DK_A), lambda b, i: (0, 0))]
    args = [ph, lb, gw]
    if has_s0:
        in_specs.append(st_spec)
        args.append(s0)
    return pl.pallas_call(
        functools.partial(_hgrn_kernel, chunk=chunk, n_chunks=n_chunks, t_valid=t_valid, has_s0=has_s0, sb=sb),
        out_shape=(jax.ShapeDtypeStruct((batch, seq, d_a), F32),
                   jax.ShapeDtypeStruct((batch, H_A, DK_A, DK_A), F32)),
        grid=(batch // sb, seq // tt),
        in_specs=in_specs,
        out_specs=(pl.BlockSpec((sb, tt, d_a), lambda b, i: (b, i, 0)), st_spec),
        scratch_shapes=[pltpu.VMEM((sb, H_A, DK_A, DK_A), F32)],
        compiler_params=_params(("arbitrary", "arbitrary")),
        name="hgrn_scan",
    )(*args)


def _rwkv_prep_kernel(*refs, seq, has_first):
    if has_first:
        (rw_ref, first_ref, mu_ref, w0_ref, w2_ref, a0_ref, a2_ref, g2_ref, kk_ref, ka_ref, rk_ref, seg_ref,
         r_o, lw_o, k_o, v_o, ah_o, bh_o, bonus_o, g_o, carry) = refs
    else:
        (rw_ref, mu_ref, w0_ref, w2_ref, a0_ref, a2_ref, g2_ref, kk_ref, ka_ref, rk_ref, seg_ref,
         r_o, lw_o, k_o, v_o, ah_o, bh_o, bonus_o, g_o, carry) = refs
    d = r_o.shape[1]
    x = rw_ref[...]
    tm = x.shape[0]
    i = pl.program_id(0)

    @pl.when(i == 0)
    def _():
        carry[...] = jnp.zeros_like(carry)

    rowi = _iota((tm, 1), 0)
    prev = pltpu.roll(x, 1, 0)
    prev = jnp.where(rowi == 0, carry[0:1, :], prev)
    seq_start = ((rowi + i * tm) & (seq - 1)) == 0
    if has_first:
        prev = jnp.where(seq_start, first_ref[...], prev)
    else:
        prev = jnp.where(seq_start, 0.0, prev)
    carry[0:1, :] = x[tm - 1:tm, :]
    x = x + (prev - x) * mu_ref[...]
    r = x[:, 0:d]
    k = x[:, d:2 * d]
    v = x[:, 2 * d:3 * d]
    wlo = x[:, 3 * d:3 * d + LORA_W]
    alo = x[:, 3 * d + LORA_W:3 * d + LORA_W + LORA_A]
    glo = x[:, 3 * d + LORA_W + LORA_A:]
    seg = seg_ref[...]
    z = -(w0_ref[...] + _mm(jnp.tanh(wlo), w2_ref[...]))
    softplus = jnp.maximum(z, 0.0) + jnp.log(1.0 + jnp.exp(-jnp.abs(z)))
    lw_o[...] = -jnp.exp(-softplus - 0.5)
    a = _sigmoid(a0_ref[...] + _mm(alo, a2_ref[...]))
    g_o[...] = _mm(_sigmoid(glo), g2_ref[...])
    kk = k * kk_ref[...]
    kk = kk / jnp.maximum(jnp.sqrt(_seg_sum(kk * kk, seg)), 1e-12)
    k2 = k * (1.0 + (a - 1.0) * ka_ref[...])
    r_o[...] = r
    k_o[...] = k2
    v_o[...] = v
    ah_o[...] = -kk
    bh_o[...] = kk * a
    bonus_o[...] = _seg_sum(r * k2 * rk_ref[...], seg) * v


def _rwkv_prep(rw, first, p, *, seq, tm):
    n, wcols = rw.shape
    d = H_B * HD_B
    has_first = first is not None
    row = lambda c: pl.BlockSpec((tm, c), lambda i: (i, 0))
    full = lambda a: pl.BlockSpec(a.shape, lambda i: (0, 0))
    small = [p["mu"], p["w0"], p["w2"], p["a0"], p["a2"], p["g2"], p["k_k"], p["k_a"], p["r_k"], p["seg"]]
    args = [rw] + ([first] if has_first else []) + small
    in_specs = [row(wcols)] + ([row(wcols)] if has_first else []) + [full(a) for a in small]
    return pl.pallas_call(
        functools.partial(_rwkv_prep_kernel, seq=seq, has_first=has_first),
        out_shape=tuple(jax.ShapeDtypeStruct((n, d), F32) for _ in range(8)),
        grid=(n // tm,),
        in_specs=in_specs,
        out_specs=tuple(row(d) for _ in range(8)),
        scratch_shapes=[pltpu.VMEM((SUBLANES, wcols), F32)],
        compiler_params=_params(("arbitrary",)),
        name="rwkv_prep",
    )(*args)


def _tri_inv(a_bds, n, rows, passes):
    mm = lambda xs, ys: [_mm(x, y, NN, passes) for x, y in zip(xs, ys)]
    row = _iota((n, n), 0)
    col = _iota((n, n), 1)
    t = row & (rows - 1)
    i = col & (rows - 1)
    eye = (row == col).astype(F32)
    bshift = RWKV_BASE.bit_length() - 1
    base = (t >> bshift) == (i >> bshift)
    d1 = [jnp.where(base, a, 0.0) for a in a_bds]
    n1 = [eye + d for d in d1]
    d2 = mm(d1, d1)
    n2 = [x + y for x, y in zip(n1, mm(n1, d2))]
    d4 = mm(d2, d2)
    tinv = [x + y for x, y in zip(n2, mm(n2, d4))]
    m = RWKV_BASE
    while m < rows:
        s = m.bit_length() - 1
        tb = t >> s
        ib = i >> s
        sib = ((tb >> 1) == (ib >> 1)) & ((tb & 1) == 1) & ((ib & 1) == 0)
        off = [jnp.where(sib, a, 0.0) for a in a_bds]
        tinv = [x + y for x, y in zip(tinv, mm(mm(tinv, off), tinv))]
        m *= 2
    return tinv


def _rwkv_chunks(ins, sts, *, rows, t_valid, passes):
    w = ins[0][0].shape[1]
    n = RWKV_GROUP * rows
    if t_valid < rows:
        valid = _iota((rows, 1), 0) < t_valid
        ins = [tuple(jnp.where(valid, x, 0.0) for x in u) for u in ins]
    r, lw, k, v, ah, bh = ([u[j] for u in ins] for j in range(6))
    lrow = _iota((rows, rows), 0)
    lcol = _iota((rows, rows), 1)
    tril = (lcol <= lrow).astype(BF16)
    hs = HD_B.bit_length() - 1
    rs = rows.bit_length() - 1
    bd_mask = (_iota((n, w), 0) >> rs) == (_iota((n, w), 1) >> hs)
    nmask = (_iota((n, n), 0) >> rs) == (_iota((n, n), 1) >> rs)
    t_idx = _iota((rows, 2 * n), 0)
    i_idx = _iota((rows, 2 * n), 1) & (rows - 1)
    strict = i_idx < t_idx
    incl = i_idx <= t_idx

    def expand(x):
        return jnp.where(bd_mask, jnp.concatenate([x] * RWKV_GROUP, axis=0), 0.0)

    cum = [_mm_exact_lhs(tril, x) for x in lw]
    gl = [c[rows - 1:rows] for c in cum]
    e_neg = [jnp.exp(-c) for c in cum]
    e_end = [jnp.exp(g - c) for g, c in zip(gl, cum)]
    qa = [a * jnp.exp(c - l) for a, c, l in zip(ah, cum, lw)]
    qr = [x * jnp.exp(c) for x, c in zip(r, cum)]
    ke = [jnp.concatenate([expand(b * e), expand(x * e)], axis=0) for b, x, e in zip(bh, k, e_neg)]
    kge = [jnp.concatenate([expand(b * e), expand(x * e)], axis=0) for b, x, e in zip(bh, k, e_end)]
    a_a = [jnp.where(strict, _mm(q, x, NT, passes), 0.0) for q, x in zip(qa, ke)]
    a_r = [jnp.where(incl, _mm(q, x, NT, 1), 0.0) for q, x in zip(qr, ke)]
    v_bd = [expand(x) for x in v]
    w_kv = [_mm(a[:, n:], x, NN, passes) for a, x in zip(a_a, v_bd)]
    a_bd = [jnp.where(nmask, jnp.concatenate([a[:, :n]] * RWKV_GROUP, axis=0), 0.0) for a in a_a]
    tinv = _tri_inv(a_bd, n, rows, passes)
    p1 = [_mm(q, st, NT, passes) for q, st in zip(qa, sts)]
    y_carry = [_mm(q, st, NT, 1) for q, st in zip(qr, sts)]
    c_bd = [_mm(t, expand(p + x), NN, passes) for t, p, x in zip(tinv, p1, w_kv)]
    cv = [jnp.concatenate([c, x], axis=0) for c, x in zip(c_bd, v_bd)]
    st_new = [st * jnp.exp(g) + _mm(c, x, TN, passes) for st, g, c, x in zip(sts, gl, cv, kge)]
    ys = [yc + _mm(a, c, NN, 1) for yc, a, c in zip(y_carry, a_r, cv)]
    return ys, st_new


def _rwkv_scan_kernel(*refs, chunk, n_chunks, t_valid, has_s0, passes, sb):
    if has_s0:
        r_ref, lw_ref, k_ref, v_ref, ah_ref, bh_ref, s0_ref, y_ref, s_ref, st_scr = refs
    else:
        r_ref, lw_ref, k_ref, v_ref, ah_ref, bh_ref, y_ref, s_ref, st_scr = refs
    ti = pl.program_id(1)
    w = RWKV_GROUP * HD_B
    groups = H_B // RWKV_GROUP
    units = [(s, g) for s in range(sb) for g in range(groups)]
    blk = lambda h: slice(h * HD_B, (h + 1) * HD_B)

    @pl.when(ti == 0)
    def _():
        st_scr[...] = jnp.zeros_like(st_scr)
        if has_s0:
            for s, g in units:
                for h in range(RWKV_GROUP):
                    st_scr[s, g, blk(h), blk(h)] = s0_ref[s, g * RWKV_GROUP + h]

    def body(c, carry):
        if n_chunks == 1:
            sl = slice(None)
        else:
            sl = pl.ds(pl.multiple_of(c * chunk, chunk), chunk)
        cols = lambda g: slice(g * w, (g + 1) * w)
        in_refs = (r_ref, lw_ref, k_ref, v_ref, ah_ref, bh_ref)
        ins = [tuple(_pad_rows(ref[s, sl, cols(g)], chunk) for ref in in_refs) for s, g in units]
        ys, st_new = _rwkv_chunks(ins, [st_scr[s, g] for s, g in units],
                                  rows=chunk, t_valid=t_valid, passes=passes)
        for (s, g), y, st in zip(units, ys, st_new):
            st_scr[s, g] = st
            y_ref[s, sl, cols(g)] = y[:t_valid]
        return carry

    if n_chunks == 1:
        body(0, 0)
    else:
        lax.fori_loop(0, n_chunks, body, 0)

    @pl.when(ti == pl.num_programs(1) - 1)
    def _():
        for s, g in units:
            for h in range(RWKV_GROUP):
                s_ref[s, g * RWKV_GROUP + h] = st_scr[s, g, blk(h), blk(h)]


def _rwkv_scan(r, lw, k, v, ah, bh, s0, *, batch, seq, tt, passes, sb):
    has_s0 = s0 is not None
    w = RWKV_GROUP * HD_B
    d_b = H_B * HD_B
    chunk = min(RWKV_CHUNK, max(seq, SUBLANES))
    tt = min(tt, seq)
    n_chunks = max(tt // chunk, 1)
    t_valid = min(chunk, seq)
    tok = pl.BlockSpec((sb, tt, d_b), lambda b, i: (b, i, 0))
    st_spec = pl.BlockSpec((sb, H_B, HD_B, HD_B), lambda b, i: (b, 0, 0, 0))
    args = [r, lw, k, v, ah, bh]
    in_specs = [tok] * 6
    if has_s0:
        args.append(s0)
        in_specs.append(st_spec)
    return pl.pallas_call(
        functools.partial(_rwkv_scan_kernel, chunk=chunk, n_chunks=n_chunks, t_valid=t_valid,
                          has_s0=has_s0, passes=passes, sb=sb),
        out_shape=(jax.ShapeDtypeStruct((batch, seq, d_b), F32),
                   jax.ShapeDtypeStruct((batch, H_B, HD_B, HD_B), F32)),
        grid=(batch // sb, seq // tt),
        in_specs=in_specs,
        out_specs=(tok, st_spec),
        scratch_shapes=[pltpu.VMEM((sb, H_B // RWKV_GROUP, w, w), F32)],
        compiler_params=_params(("arbitrary", "arbitrary")),
        name="rwkv_scan",
    )(*args)


def _merge_kernel(*refs, blocks_a):
    group_a = refs[0:7]
    group_b = refs[7:14]
    (lnw_ref, lnb_ref, seg_ref, wa_ref, wb_ref, wo_ref, n2_ref, rwt_ref, rb_ref,
     x1_o, xn_o, idx_o, gate_o, cnt_o) = refs[14:]
    from_b = pl.program_id(0) >= blocks_a
    y, bonus, g, oa, ga, gb, x = (jnp.where(from_b, rb_[...], ra_[...]) for ra_, rb_ in zip(group_a, group_b))
    seg = seg_ref[...]
    inv = 1.0 / HD_B
    mu = _seg_sum(y, seg) * inv
    dlt = y - mu
    var = _seg_sum(dlt * dlt, seg) * inv
    yn = dlt * lax.rsqrt(var + RWKV_GN_EPS) * lnw_ref[...] + lnb_ref[...]
    ob = (yn + bonus) * g
    merged = _sigmoid(ga) * _mm(oa, wa_ref[...]) + _sigmoid(gb) * _mm(ob, wb_ref[...])
    x1 = x + _mm(merged, wo_ref[...])
    x1_o[...] = x1
    xn = x1 * lax.rsqrt(jnp.mean(x1 * x1, axis=-1, keepdims=True) + EPS) * n2_ref[...]
    _tok_store(xn_o, xn)
    logits = _mm(rwt_ref[...], xn, NT, 3) + rb_ref[...]
    tm = x1.shape[0]
    eio = _iota((N_EXPERTS, tm), 0)
    idx_rows, val_rows = [], []
    cnt = jnp.zeros((N_EXPERTS, 1), F32)
    for _ in range(TOP_K):
        mx = jnp.max(logits, axis=0, keepdims=True)
        pick = jnp.min(jnp.where(logits == mx, eio, N_EXPERTS), axis=0, keepdims=True)
        hit = eio == pick
        cnt = cnt + jnp.sum(hit.astype(F32), axis=1, keepdims=True)
        logits = jnp.where(hit, -jnp.inf, logits)
        idx_rows.append(pick)
        val_rows.append(mx)
    ex = [jnp.exp(vr - val_rows[0]) for vr in val_rows]
    den = ex[0] + ex[1] + ex[2] + ex[3]
    idx_o[...] = _stack_rows(idx_rows)
    gate_o[...] = _stack_rows([e / den for e in ex])

    @pl.when(pl.program_id(0) == 0)
    def _():
        cnt_o[...] = jnp.zeros_like(cnt_o)

    cnt_o[...] += jnp.broadcast_to(cnt, cnt_o.shape)


def _merge(tok_a, tok_b, p, *, tm):
    n_a, d = tok_a[5].shape
    n_b = tok_b[5].shape[0]
    n = n_a + n_b
    blocks_a = n_a // tm
    last_a = blocks_a - 1
    ia = lambda i: jnp.minimum(i, last_a)
    ib = lambda i: jnp.maximum(i - blocks_a, 0)

    def group_specs(tok, im):
        y, _, _, oa, gates, _ = tok
        dm = gates.shape[1] // 2
        row = lambda c: pl.BlockSpec((tm, c), lambda i: (im(i), 0))
        return [row(y.shape[1]), row(y.shape[1]), row(y.shape[1]), row(oa.shape[1]),
                pl.BlockSpec((tm, dm), lambda i: (im(i), 0)), pl.BlockSpec((tm, dm), lambda i: (im(i), 1)),
                row(d)]

    def group_args(tok):
        y, bonus, g, oa, gates, x = tok
        return [y, bonus, g, oa, gates, gates, x]

    full = lambda a: pl.BlockSpec(a.shape, lambda i: (0, 0))
    small = [p["ln_w"], p["ln_b"], p["seg"], p["wa"], p["wb"], p["wo"], p["n2"], p["rwt"], p["rb"]]
    out_row = pl.BlockSpec((tm, d), lambda i: (i, 0))
    out_col = pl.BlockSpec((SUBLANES, tm), lambda i: (0, i))
    return pl.pallas_call(
        functools.partial(_merge_kernel, blocks_a=blocks_a),
        out_shape=(jax.ShapeDtypeStruct((n, d), F32), jax.ShapeDtypeStruct((n * SUBLANES, LANES), F32),
                   jax.ShapeDtypeStruct((SUBLANES, n), jnp.int32),
                   jax.ShapeDtypeStruct((SUBLANES, n), F32),
                   jax.ShapeDtypeStruct((N_EXPERTS, LANES), F32)),
        grid=(n // tm,),
        in_specs=group_specs(tok_a, ia) + group_specs(tok_b, ib) + [full(a) for a in small],
        out_specs=(out_row, pl.BlockSpec((tm * SUBLANES, LANES), lambda i: (i, 0)), out_col, out_col,
                   pl.BlockSpec((N_EXPERTS, LANES), lambda i: (0, 0))),
        compiler_params=_params(("arbitrary",)),
        name="merge_router",
    )(*(group_args(tok_a) + group_args(tok_b) + small))


def _rank_kernel(idx_ref, pstart_ref, dest_o, carry):
    i = pl.program_id(0)

    @pl.when(i == 0)
    def _():
        carry[...] = pstart_ref[...]

    tm = idx_ref.shape[1]
    eio = _iota((N_EXPERTS, tm), 0)
    idx = idx_ref[...]
    hits = [eio == idx[j:j + 1, :] for j in range(TOP_K)]
    sel = hits[0] | hits[1] | hits[2] | hits[3]
    upper = (_iota((tm, tm), 0) < _iota((tm, tm), 1)).astype(BF16)
    rank = _dg(sel.astype(BF16), upper, NN) + carry[:, 0:1]
    rows = [jnp.sum(jnp.where(h, rank, 0.0), axis=0, keepdims=True) for h in hits]
    dest_o[...] = _stack_rows(rows).astype(jnp.int32)
    carry[...] += jnp.sum(sel.astype(F32), axis=1, keepdims=True)


def _rank(idx4, pstart, *, tm):
    n = idx4.shape[1]
    return pl.pallas_call(
        _rank_kernel,
        out_shape=jax.ShapeDtypeStruct((SUBLANES, n), jnp.int32),
        grid=(n // tm,),
        in_specs=[pl.BlockSpec((SUBLANES, tm), lambda i: (0, i)),
                  pl.BlockSpec((N_EXPERTS, LANES), lambda i: (0, 0))],
        out_specs=pl.BlockSpec((SUBLANES, tm), lambda i: (0, i)),
        scratch_shapes=[pltpu.VMEM((N_EXPERTS, LANES), F32)],
        compiler_params=_params(("arbitrary",)),
        name="moe_rank",
    )(idx4, pstart)


def _scatter_kernel(pad_lo, pad_hi, na_ref, dest_ref, x_ref, xs_hbm, zeros, sem, zsem):
    tm = x_ref.shape[0]
    zrows = zeros.shape[0]
    nblk = xs_hbm.shape[0] // zrows

    @pl.when(pl.program_id(0) == 0)
    def _():
        zeros[...] = jnp.zeros_like(zeros)
        zrun = lambda r, size: pltpu.make_async_copy(zeros.at[pl.ds(0, size)], xs_hbm.at[pl.ds(r, size)], zsem)

        def per_expert(op):
            def f(e, c):
                lo = pad_lo[e]
                n_pad = pad_hi[e] - lo
                for bit in reversed(range(zrows.bit_length() - 1)):
                    @pl.when(((n_pad >> bit) & 1) == 1)
                    def _(lo=lo, bit=bit):
                        op(zrun(lo, 1 << bit))
                    lo = lo + (n_pad & (1 << bit))
                return c
            return f

        lax.fori_loop(0, N_EXPERTS, per_expert(lambda cp: cp.start()), 0)
        lax.fori_loop(na_ref[0], nblk, lambda b, c: (zrun(b * zrows, zrows).start(), c)[1], 0)
        lax.fori_loop(0, N_EXPERTS, per_expert(lambda cp: cp.wait()), 0)
        lax.fori_loop(na_ref[0], nblk, lambda b, c: (zrun(b * zrows, zrows).wait(), c)[1], 0)

    def issue(t, c):
        for j in range(TOP_K):
            pltpu.make_async_copy(x_ref.at[t], xs_hbm.at[dest_ref[j, t]], sem).start()
        return c

    lax.fori_loop(0, tm, issue, 0, unroll=DMA_UNROLL)
    for j in range(TOP_K):
        pltpu.make_async_copy(x_ref, xs_hbm.at[pl.ds(0, tm)], sem).wait()


def _scatter(pad_lo, pad_hi, n_active, dest4, x3, m_total, *, tm):
    n = x3.shape[0]
    tail = x3.shape[1:]
    return pl.pallas_call(
        _scatter_kernel,
        out_shape=jax.ShapeDtypeStruct((m_total,) + tail, x3.dtype),
        grid_spec=pltpu.PrefetchScalarGridSpec(
            num_scalar_prefetch=3,
            grid=(n // tm,),
            in_specs=[pl.BlockSpec((SUBLANES, tm), lambda i, *_: (0, i), memory_space=pltpu.SMEM),
                      pl.BlockSpec((tm,) + tail, lambda i, *_: (i, 0, 0))],
            out_specs=pl.BlockSpec(memory_space=pl.ANY),
            scratch_shapes=[pltpu.VMEM((MOE_ROWS,) + tail, x3.dtype),
                            pltpu.SemaphoreType.DMA, pltpu.SemaphoreType.DMA]),
        compiler_params=_params(("arbitrary",)),
        name="moe_scatter",
    )(pad_lo, pad_hi, n_active, dest4, x3)


def _expert_kernel(be_ref, na_ref, x_ref, wgu_ref, bgu_ref, wd_ref, bd_ref, y_ref, wgu_bf, wd_perm, wd_bf):
    b = pl.program_id(0)
    active = b < na_ref[0]
    half = LANES // 2
    dff = wd_ref.shape[1]

    @pl.when(active & ((b == 0) | (be_ref[b] != be_ref[jnp.maximum(b - 1, 0)])))
    def _():
        wgu_bf[...] = wgu_ref[0].astype(BF16)
        for c in range(wd_ref.shape[2] // LANES):
            cols = slice(c * LANES, (c + 1) * LANES)
            for p in range(dff // LANES):
                r0 = p * LANES
                wd_perm[pl.ds(r0, half, stride=2), :] = wd_ref[0, r0:r0 + half, cols]
                wd_perm[pl.ds(r0 + 1, half, stride=2), :] = wd_ref[0, r0 + half:r0 + LANES, cols]
            wd_bf[:, cols] = wd_perm[...].astype(BF16)

    @pl.when(active)
    def _():
        x = _tok_load(x_ref).astype(BF16)
        h = _dg(x, wgu_bf[...], NN) + bgu_ref[0]
        even = (_iota((1, LANES), 1) & 1) == 0
        acts = []
        for p in range(dff // LANES):
            ha = h[:, 2 * p * LANES:(2 * p + 1) * LANES]
            hb = h[:, (2 * p + 1) * LANES:(2 * p + 2) * LANES]
            hg = jnp.where(even, ha, pltpu.roll(hb, 1, 1))
            hl = jnp.where(even, pltpu.roll(ha, LANES - 1, 1), hb)
            glu = jnp.minimum(hg, SWIGLU_LIMIT)
            lin = jnp.clip(hl, -SWIGLU_LIMIT, SWIGLU_LIMIT)
            acts.append((glu * _sigmoid(SWIGLU_ALPHA * glu) * (lin + 1.0)).astype(BF16))
        act = jnp.concatenate(acts, axis=1)
        _tok_store(y_ref, _dg(act, wd_bf[...], NN) + bd_ref[0])

    @pl.when(jnp.logical_not(active))
    def _():
        y_ref[...] = jnp.zeros_like(y_ref)


def _experts(block_expert, n_active, xs2, w_gu, b_gu, w_down, b_down):
    d = w_gu.shape[1]
    dff = w_down.shape[1]
    per_tok = d // LANES
    m_total = xs2.shape[0] // per_tok
    nblk = m_total // MOE_ROWS
    wspec = lambda r, c: pl.BlockSpec((1, r, c), lambda b, be, na: (be[b], 0, 0))
    tok = pl.BlockSpec((MOE_ROWS * per_tok, LANES), lambda b, be, na: (b, 0))
    return pl.pallas_call(
        _expert_kernel,
        out_shape=jax.ShapeDtypeStruct(xs2.shape, F32),
        grid_spec=pltpu.PrefetchScalarGridSpec(
            num_scalar_prefetch=2,
            grid=(nblk,),
            in_specs=[tok, wspec(d, 2 * dff), wspec(1, 2 * dff), wspec(dff, d), wspec(1, d)],
            out_specs=tok,
            scratch_shapes=[pltpu.VMEM((d, 2 * dff), BF16), pltpu.VMEM((dff, LANES), F32),
                            pltpu.VMEM((dff, d), BF16)]),
        compiler_params=_params(("arbitrary",)),
        name="moe_experts",
    )(block_expert, n_active, xs2, w_gu, b_gu, w_down, b_down)


def _combine_kernel(dest_ref, gate_ref, ys_hbm, o_ref, buf, sems):
    tm = o_ref.shape[0]
    half = tm // 2

    def issue(h):
        def f(t, c):
            for j in range(TOP_K):
                pltpu.make_async_copy(ys_hbm.at[dest_ref[j, t]], buf.at[j, t], sems.at[h]).start()
            return c
        return f

    def mix(t, c):
        acc = gate_ref[0, t] * buf[0, t]
        for j in range(1, TOP_K):
            acc = acc + gate_ref[j, t] * buf[j, t]
        o_ref[t] = acc
        return c

    for h in range(2):
        lax.fori_loop(h * half, (h + 1) * half, issue(h), 0, unroll=DMA_UNROLL)
    for h in range(2):
        rows = pl.ds(h * half, half)
        for j in range(TOP_K):
            pltpu.make_async_copy(ys_hbm.at[pl.ds(0, half)], buf.at[j, rows], sems.at[h]).wait()
        lax.fori_loop(h * half, (h + 1) * half, mix, 0, unroll=DMA_UNROLL)


def _combine(dest4, gate4, ys3, n, *, tm):
    tail = ys3.shape[1:]
    smem = lambda: pl.BlockSpec((SUBLANES, tm), lambda i: (0, i), memory_space=pltpu.SMEM)
    return pl.pallas_call(
        _combine_kernel,
        out_shape=jax.ShapeDtypeStruct((n,) + tail, F32),
        grid=(n // tm,),
        in_specs=[smem(), smem(), pl.BlockSpec(memory_space=pl.ANY)],
        out_specs=pl.BlockSpec((tm,) + tail, lambda i: (i, 0, 0)),
        scratch_shapes=[pltpu.VMEM((TOP_K, tm) + tail, F32), pltpu.SemaphoreType.DMA((2,))],
        compiler_params=_params(("arbitrary",)),
        name="moe_combine",
    )(dest4, gate4, ys3)


def _final_kernel(x_ref, h_ref, w_ref, oa_ref, ob_ref, *, blocks_a):
    x = x_ref[...] + _tok_load(h_ref)
    y = x * lax.rsqrt(jnp.mean(x * x, axis=-1, keepdims=True) + EPS) * w_ref[...]
    first = pl.program_id(0) < blocks_a

    @pl.when(first)
    def _():
        oa_ref[...] = y

    @pl.when(jnp.logical_not(first))
    def _():
        ob_ref[...] = y


def _final(x1, h, w, n_a, *, tm):
    n, d = x1.shape
    blocks_a = n_a // tm
    row = pl.BlockSpec((tm, d), lambda i: (i, 0))
    return pl.pallas_call(
        functools.partial(_final_kernel, blocks_a=blocks_a),
        out_shape=(jax.ShapeDtypeStruct((n_a, d), F32), jax.ShapeDtypeStruct((n - n_a, d), F32)),
        grid=(n // tm,),
        in_specs=[row, pl.BlockSpec((tm * SUBLANES, LANES), lambda i: (i, 0)),
                  pl.BlockSpec((1, d), lambda i: (0, 0))],
        out_specs=(pl.BlockSpec((tm, d), lambda i: (jnp.minimum(i, blocks_a - 1), 0)),
                   pl.BlockSpec((tm, d), lambda i: (jnp.maximum(i - blocks_a, 0), 0))),
        compiler_params=_params(("arbitrary",)),
        name="final_norm",
    )(x1, h, w)


def _mixers(x, s_hgrn, s_rwkv, shift_prev, wts):
    batch, seq, d = x.shape
    n = batch * seq
    x2 = x.reshape(n, d)
    tm = min(512, n)
    seqs_per_step = LONG_SEQS_PER_STEP if seq >= HGRN_CHUNK else SHORT_SEQS_PER_STEP
    ph = _norm_proj(x2, wts["n1"], wts["w_h"], normalize=True, tm=tm)
    prw = _norm_proj(x2, wts["n1"], wts["w_rw"], normalize=True, tm=tm)
    pg = _norm_proj(x2, wts["n1"], wts["w_g"], normalize=True, tm=tm)
    oa, s_hgrn_new = _hgrn(ph.reshape(batch, seq, -1), wts["lb"], wts["gw"], s_hgrn,
                           batch=batch, seq=seq, tt=256, sb=seqs_per_step)
    first = None
    if shift_prev is not None:
        prev = _norm_proj(shift_prev, wts["n1"], wts["w_rw"], normalize=False, tm=shift_prev.shape[0])
        first = jnp.repeat(prev, seq, axis=0)
    r, lw, k, v, ah, bh, bonus, g = _rwkv_prep(prw, first, wts, seq=seq, tm=min(256, n))
    to3 = lambda a: a.reshape(batch, seq, -1)
    y, s_rwkv_new = _rwkv_scan(to3(r), to3(lw), to3(k), to3(v), to3(ah), to3(bh), s_rwkv,
                               batch=batch, seq=seq, tt=128, passes=1, sb=seqs_per_step)
    return (y.reshape(n, -1), bonus, g, oa.reshape(n, -1), pg, x2), s_hgrn_new, s_rwkv_new


def kernel(x_prompt, x_sample, state_hgrn, state_rwkv, state_shift, norm1_w, w_in, hgrn_lb_logits,
           hgrn_gnorm_w, rwkv_mu, rwkv_w0, rwkv_w2, rwkv_a0, rwkv_a2, rwkv_g2, rwkv_k_k, rwkv_k_a,
           rwkv_r_k, rwkv_ln_w, rwkv_ln_b, w_branch_a, w_branch_b, w_out, norm2_w, router_w, router_b,
           expert_w_gu, expert_b_gu, expert_w_down, expert_b_down, final_norm_w):
    bp, tp, d = x_prompt.shape
    bs, ts, _ = x_sample.shape
    n_p, n_s = bp * tp, bs * ts
    n = n_p + n_s
    d_a = H_A * DK_A
    d_b = H_B * HD_B
    rw_start = 4 * d_a
    gate_start = rw_start + 3 * d_b + LORA_W + LORA_A + LORA_G
    row2 = lambda a: a.reshape(1, -1)
    lb = jnp.cumsum(jax.nn.softmax(hgrn_lb_logits.astype(F32), axis=0), axis=0)[0]
    hid = jnp.arange(RWKV_GROUP * HD_B, dtype=jnp.int32) // HD_B
    wts = {
        "n1": row2(norm1_w[0]),
        "w_h": w_in[0][:, :rw_start].astype(BF16),
        "w_rw": w_in[0][:, rw_start:gate_start].astype(BF16),
        "w_g": w_in[0][:, gate_start:].astype(BF16),
        "lb": row2(lb), "gw": row2(hgrn_gnorm_w[0]),
        "mu": row2(rwkv_mu[0]), "w0": row2(rwkv_w0[0]), "w2": rwkv_w2[0], "a0": row2(rwkv_a0[0]),
        "a2": rwkv_a2[0], "g2": rwkv_g2[0], "k_k": row2(rwkv_k_k[0]), "k_a": row2(rwkv_k_a[0]),
        "r_k": row2(rwkv_r_k[0]), "seg": (hid[:, None] == hid[None, :]).astype(BF16),
        "ln_w": row2(rwkv_ln_w[0]), "ln_b": row2(rwkv_ln_b[0]),
        "wa": w_branch_a[0].astype(BF16), "wb": w_branch_b[0].astype(BF16), "wo": w_out[0].astype(BF16),
        "n2": row2(norm2_w[0]), "rwt": router_w[0].T, "rb": router_b[0].reshape(-1, 1),
    }
    tok_p, hgrn_p, rwkv_p = _mixers(x_prompt, None, None, None, wts)
    tok_s, hgrn_s, rwkv_s = _mixers(x_sample, state_hgrn[0], state_rwkv[0], state_shift[0], wts)
    shift = _rms_rows(jnp.concatenate([x_prompt[:, -1], x_sample[:, -1]], axis=0), wts["n1"])

    x1, xn2, idx4, gate4, counts = _merge(tok_p, tok_s, wts, tm=256)

    cnt = counts[:, 0].astype(jnp.int32)
    padded = (cnt + MOE_ROWS - 1) // MOE_ROWS * MOE_ROWS
    pend = jnp.cumsum(padded)
    pstart = pend - padded
    nblk = (n * TOP_K + N_EXPERTS * (MOE_ROWS - 1) + MOE_ROWS - 1) // MOE_ROWS
    m_total = nblk * MOE_ROWS
    block_row = jnp.arange(nblk, dtype=jnp.int32)[:, None] * MOE_ROWS
    block_expert = jnp.minimum(jnp.sum((pend[None, :] <= block_row).astype(jnp.int32), axis=1), N_EXPERTS - 1)
    n_active = (pend[-1:] // MOE_ROWS).astype(jnp.int32)
    dest4 = _rank(idx4, jnp.broadcast_to(pstart.astype(F32)[:, None], (N_EXPERTS, LANES)), tm=512)

    per_tok = d // LANES
    xs = _scatter(pstart + cnt, pend, n_active, dest4, xn2.reshape(n, per_tok, LANES), m_total, tm=256)
    ys = _experts(block_expert, n_active, xs.reshape(m_total * per_tok, LANES), expert_w_gu[0],
                  expert_b_gu[0][:, None, :], expert_w_down[0], expert_b_down[0][:, None, :])
    h = _combine(dest4, gate4, ys.reshape(m_total, per_tok, LANES), n, tm=256)
    y_p, y_s = _final(x1, h.reshape(n * per_tok, LANES), row2(final_norm_w), n_p, tm=512)

    return (y_p.reshape(bp, tp, d), y_s.reshape(bs, ts, d),
            hgrn_p[None], rwkv_p[None], shift[None, :bp],
            hgrn_s[None], rwkv_s[None], shift[None, bp:])
```

```python
import functools

import jax
import jax.numpy as jnp
from jax import lax
from jax.experimental import pallas as pl
from jax.experimental.pallas import tpu as pltpu

F32 = jnp.float32
BF16 = jnp.bfloat16

H_A = 4
DK_A = 128
H_B = 8
HD_B = 64
LORA_W = 64
LORA_A = 64
LORA_G = 128
N_EXPERTS = 32
TOP_K = 4
SWIGLU_LIMIT = 7.0
SWIGLU_ALPHA = 1.702
EPS = 1e-6
RWKV_GN_EPS = 64e-5

LANES = 128
SUBLANES = 8
HGRN_CHUNK = 64
HGRN_SUB = 8
RWKV_CHUNK = 64
RWKV_GROUP = 4
RWKV_BASE = 8
MOE_ROWS = 384
DMA_UNROLL = 4
LONG_SEQS_PER_STEP = 4
SHORT_SEQS_PER_STEP = 4
VMEM_LIMIT = 52 * 1024 * 1024

NN = (((1,), (0,)), ((), ()))
NT = (((1,), (1,)), ((), ()))
TN = (((0,), (0,)), ((), ()))


def _params(sem):
    return pltpu.CompilerParams(dimension_semantics=sem, vmem_limit_bytes=VMEM_LIMIT)


def _dg(a, b, dims):
    return lax.dot_general(a, b, dims, preferred_element_type=F32)


def _split2(x):
    hi = x.astype(BF16)
    lo = (x - hi.astype(F32)).astype(BF16)
    return hi, lo


def _mm(a, b, dims=NN, passes=1):
    if passes == 1:
        return _dg(a.astype(BF16), b.astype(BF16), dims)
    ah, al = _split2(a)
    bh, bl = _split2(b)
    return _dg(ah, bh, dims) + (_dg(ah, bl, dims) + _dg(al, bh, dims))


def _mm_exact_lhs(sel_bf16, x, dims=NN):
    hi = x.astype(BF16)
    r1 = x - hi.astype(F32)
    mid = r1.astype(BF16)
    lo = (r1 - mid.astype(F32)).astype(BF16)
    return _dg(sel_bf16, hi, dims) + (_dg(sel_bf16, mid, dims) + _dg(sel_bf16, lo, dims))


def _seg_sum(x, seg_bf16):
    w = seg_bf16.shape[0]
    hi, lo = _split2(x)
    parts = [_dg(hi[:, c:c + w], seg_bf16, NN) + _dg(lo[:, c:c + w], seg_bf16, NN)
             for c in range(0, x.shape[1], w)]
    return parts[0] if len(parts) == 1 else jnp.concatenate(parts, axis=1)


def _iota(shape, dim):
    return lax.broadcasted_iota(jnp.int32, shape, dim)


def _sigmoid(x):
    return 1.0 / (1.0 + jnp.exp(-x))


def _tok_load(ref):
    tm = ref.shape[0] // SUBLANES
    return jnp.concatenate([ref[pl.ds(c, tm, stride=SUBLANES), :] for c in range(SUBLANES)], axis=1)


def _tok_store(ref, val):
    tm = val.shape[0]
    for c in range(SUBLANES):
        ref[pl.ds(c, tm, stride=SUBLANES), :] = val[:, c * LANES:(c + 1) * LANES]


def _stack_rows(rows):
    n = rows[0].shape[1]
    sub = _iota((SUBLANES, n), 0)
    out = jnp.zeros((SUBLANES, n), rows[0].dtype)
    for j, r in enumerate(rows):
        out = jnp.where(sub == j, jnp.broadcast_to(r, (SUBLANES, n)), out)
    return out


def _pad_rows(x, rows):
    if x.shape[0] == rows:
        return x
    return jnp.concatenate([x, jnp.zeros((rows - x.shape[0], x.shape[1]), x.dtype)], axis=0)


def _norm_proj_kernel(x_ref, nw_ref, w_ref, o_ref, *, normalize):
    x = x_ref[...]
    if normalize:
        x = x * lax.rsqrt(jnp.mean(x * x, axis=-1, keepdims=True) + EPS) * nw_ref[...]
    o_ref[...] = _dg(x.astype(BF16), w_ref[...], NN)


def _norm_proj(x, nw, w_bf16, *, normalize, tm):
    n, d = x.shape
    nc = w_bf16.shape[1]
    return pl.pallas_call(
        functools.partial(_norm_proj_kernel, normalize=normalize),
        out_shape=jax.ShapeDtypeStruct((n, nc), F32),
        grid=(n // tm,),
        in_specs=[pl.BlockSpec((tm, d), lambda i: (i, 0)),
                  pl.BlockSpec((1, d), lambda i: (0, 0)),
                  pl.BlockSpec((d, nc), lambda i: (0, 0))],
        out_specs=pl.BlockSpec((tm, nc), lambda i: (i, 0)),
        compiler_params=_params(("arbitrary",)),
        name="norm_proj",
    )(x, nw, w_bf16)


def _rms_rows_kernel(x_ref, w_ref, o_ref):
    x = x_ref[...]
    o_ref[...] = x * lax.rsqrt(jnp.mean(x * x, axis=-1, keepdims=True) + EPS) * w_ref[...]


def _rms_rows(x, w):
    return pl.pallas_call(
        _rms_rows_kernel,
        out_shape=jax.ShapeDtypeStruct(x.shape, F32),
        name="rms_rows",
    )(x, w)


def _hgrn_chunks(hqs, hfs, his, lbvs, sts, *, rows, t_valid):
    n_u = len(hqs)
    fs = [lbv + (1.0 - lbv) * _sigmoid(hf) for lbv, hf in zip(lbvs, hfs)]
    gs = [jnp.log(f) for f in fs]
    ks = [1.0 - f for f in fs]
    qs = [hq * _sigmoid(hq) for hq in hqs]
    vs = list(his)
    if t_valid < rows:
        valid = _iota((rows, 1), 0) < t_valid
        gs, ks, qs, vs = ([jnp.where(valid, x, 0.0) for x in xs] for xs in (gs, ks, qs, vs))
    row = _iota((rows, rows), 0)
    col = _iota((rows, rows), 1)
    tril = (col <= row).astype(BF16)
    bs = [_mm_exact_lhs(tril, g) for g in gs]
    sub = min(HGRN_SUB, rows)
    m = rows // 2
    scores = [None] * n_u
    while m >= sub:
        nb = rows // m
        shift = m.bit_length() - 1
        rb = row >> shift
        pair = ((rb & 1) == 1) & ((col >> shift) == rb - 1)
        qes, kes = [], []
        for q, k, b in zip(qs, ks, bs):
            starts = [jnp.zeros((1, DK_A), F32) if j == 0 else b[j * m - 1:j * m] for j in range(nb)]
            ends = [b[j * m + m - 1:j * m + m] for j in range(nb)]
            b_start = jnp.concatenate([jnp.broadcast_to(s, (m, DK_A)) for s in starts], axis=0)
            b_end = jnp.concatenate([jnp.broadcast_to(s, (m, DK_A)) for s in ends], axis=0)
            qes.append(q * jnp.exp(b - b_start))
            kes.append(k * jnp.exp(b_end - b))
        a_ms = [jnp.where(pair, _mm(qe, ke, NT), 0.0) for qe, ke in zip(qes, kes)]
        scores = [a if s is None else s + a for s, a in zip(scores, a_ms)]
        m //= 2
    os_ = [jnp.zeros((rows, DK_A), F32)] * n_u
    if scores[0] is not None:
        os_ = [_mm(s, v, NN) for s, v in zip(scores, vs)]
    rloc = _iota((rows, 1), 0) & (sub - 1)
    for d in range(min(sub, t_valid)):
        sh = (lambda x: x) if d == 0 else (lambda x: pltpu.roll(x, d, 0))
        es = [jnp.exp(jnp.where(rloc >= d, b - sh(b), -jnp.inf)) for b in bs]
        os_ = [o + jnp.sum(q * sh(k) * e, axis=-1, keepdims=True) * sh(v)
               for o, q, k, v, e in zip(os_, qs, ks, vs, es)]
    b_lasts = [b[rows - 1:rows] for b in bs]
    kds = [k * jnp.exp(bl - b) for k, bl, b in zip(ks, b_lasts, bs)]
    qbs = [q * jnp.exp(b) for q, b in zip(qs, bs)]
    os_ = [o + _mm(qb, st, NT) for o, qb, st in zip(os_, qbs, sts)]
    st_new = [st * jnp.exp(bl) + _mm(v, kd, TN) for st, bl, v, kd in zip(sts, b_lasts, vs, kds)]
    return os_, st_new


def _hgrn_kernel(*refs, chunk, n_chunks, t_valid, has_s0, sb):
    if has_s0:
        ph_ref, lb_ref, gw_ref, s0_ref, o_ref, s_ref, st_scr = refs
    else:
        ph_ref, lb_ref, gw_ref, o_ref, s_ref, st_scr = refs
    ti = pl.program_id(1)
    d_a = H_A * DK_A
    units = [(s, h) for s in range(sb) for h in range(H_A)]

    @pl.when(ti == 0)
    def _():
        for s, h in units:
            st_scr[s, h] = s0_ref[s, h].T if has_s0 else jnp.zeros((DK_A, DK_A), F32)

    gw = gw_ref[...]

    def body(c, carry):
        if n_chunks == 1:
            sl = slice(None)
        else:
            sl = pl.ds(pl.multiple_of(c * chunk, chunk), chunk)
        cols = lambda part, h: slice(part * d_a + h * DK_A, part * d_a + (h + 1) * DK_A)
        part = lambda p: [_pad_rows(ph_ref[s, sl, cols(p, h)], chunk) for s, h in units]
        os_, st_new = _hgrn_chunks(part(0), part(1), part(2),
                                   [lb_ref[:, h * DK_A:(h + 1) * DK_A] for _, h in units],
                                   [st_scr[s, h] for s, h in units], rows=chunk, t_valid=t_valid)
        for (s, h), o, st in zip(units, os_, st_new):
            st_scr[s, h] = st
            hg = ph_ref[s, sl, cols(3, h)]
            o = o[:t_valid]
            o = o * lax.rsqrt(jnp.mean(o * o, axis=-1, keepdims=True) + EPS) * gw
            o_ref[s, sl, h * DK_A:(h + 1) * DK_A] = o * (hg * _sigmoid(hg))
        return carry

    if n_chunks == 1:
        body(0, 0)
    else:
        lax.fori_loop(0, n_chunks, body, 0)

    @pl.when(ti == pl.num_programs(1) - 1)
    def _():
        for s, h in units:
            s_ref[s, h] = st_scr[s, h].T


def _hgrn(ph, lb, gw, s0, *, batch, seq, tt, sb):
    has_s0 = s0 is not None
    chunk = min(HGRN_CHUNK, max(seq, SUBLANES))
    tt = min(tt, seq)
    n_chunks = max(tt // chunk, 1)
    t_valid = min(chunk, seq)
    d_a = H_A * DK_A
    st_spec = pl.BlockSpec((sb, H_A, DK_A, DK_A), lambda b, i: (b, 0, 0, 0))
    in_specs = [pl.BlockSpec((sb, tt, 4 * d_a), lambda b, i: (b, i, 0)),
                pl.BlockSpec((1, d_a), lambda b, i: (0, 0)),
                pl.BlockSpec((1, DK_A), lambda b, i: (0, 0))]
    args = [ph, lb, gw]
    if has_s0:
        in_specs.append(st_spec)
        args.append(s0)
    return pl.pallas_call(
        functools.partial(_hgrn_kernel, chunk=chunk, n_chunks=n_chunks, t_valid=t_valid, has_s0=has_s0, sb=sb),
        out_shape=(jax.ShapeDtypeStruct((batch, seq, d_a), F32),
                   jax.ShapeDtypeStruct((batch, H_A, DK_A, DK_A), F32)),
        grid=(batch // sb, seq // tt),
        in_specs=in_specs,
        out_specs=(pl.BlockSpec((sb, tt, d_a), lambda b, i: (b, i, 0)), st_spec),
        scratch_shapes=[pltpu.VMEM((sb, H_A, DK_A, DK_A), F32)],
        compiler_params=_params(("arbitrary", "arbitrary")),
        name="hgrn_scan",
    )(*args)


def _rwkv_prep_kernel(*refs, seq, has_first):
    if has_first:
        (rw_ref, first_ref, mu_ref, w0_ref, w2_ref, a0_ref, a2_ref, g2_ref, kk_ref, ka_ref, rk_ref, seg_ref,
         r_o, lw_o, k_o, v_o, ah_o, bh_o, bonus_o, g_o, carry) = refs
    else:
        (rw_ref, mu_ref, w0_ref, w2_ref, a0_ref, a2_ref, g2_ref, kk_ref, ka_ref, rk_ref, seg_ref,
         r_o, lw_o, k_o, v_o, ah_o, bh_o, bonus_o, g_o, carry) = refs
    d = r_o.shape[1]
    x = rw_ref[...]
    tm = x.shape[0]
    i = pl.program_id(0)

    @pl.when(i == 0)
    def _():
        carry[...] = jnp.zeros_like(carry)

    rowi = _iota((tm, 1), 0)
    prev = pltpu.roll(x, 1, 0)
    prev = jnp.where(rowi == 0, carry[0:1, :], prev)
    seq_start = ((rowi + i * tm) & (seq - 1)) == 0
    if has_first:
        prev = jnp.where(seq_start, first_ref[...], prev)
    else:
        prev = jnp.where(seq_start, 0.0, prev)
    carry[0:1, :] = x[tm - 1:tm, :]
    x = x + (prev - x) * mu_ref[...]
    r = x[:, 0:d]
    k = x[:, d:2 * d]
    v = x[:, 2 * d:3 * d]
    wlo = x[:, 3 * d:3 * d + LORA_W]
    alo = x[:, 3 * d + LORA_W:3 * d + LORA_W + LORA_A]
    glo = x[:, 3 * d + LORA_W + LORA_A:]
    seg = seg_ref[...]
    z = -(w0_ref[...] + _mm(jnp.tanh(wlo), w2_ref[...]))
    softplus = jnp.maximum(z, 0.0) + jnp.log(1.0 + jnp.exp(-jnp.abs(z)))
    lw_o[...] = -jnp.exp(-softplus - 0.5)
    a = _sigmoid(a0_ref[...] + _mm(alo, a2_ref[...]))
    g_o[...] = _mm(_sigmoid(glo), g2_ref[...])
    kk = k * kk_ref[...]
    kk = kk / jnp.maximum(jnp.sqrt(_seg_sum(kk * kk, seg)), 1e-12)
    k2 = k * (1.0 + (a - 1.0) * ka_ref[...])
    r_o[...] = r
    k_o[...] = k2
    v_o[...] = v
    ah_o[...] = -kk
    bh_o[...] = kk * a
    bonus_o[...] = _seg_sum(r * k2 * rk_ref[...], seg) * v


def _rwkv_prep(rw, first, p, *, seq, tm):
    n, wcols = rw.shape
    d = H_B * HD_B
    has_first = first is not None
    row = lambda c: pl.BlockSpec((tm, c), lambda i: (i, 0))
    full = lambda a: pl.BlockSpec(a.shape, lambda i: (0, 0))
    small = [p["mu"], p["w0"], p["w2"], p["a0"], p["a2"], p["g2"], p["k_k"], p["k_a"], p["r_k"], p["seg"]]
    args = [rw] + ([first] if has_first else []) + small
    in_specs = [row(wcols)] + ([row(wcols)] if has_first else []) + [full(a) for a in small]
    return pl.pallas_call(
        functools.partial(_rwkv_prep_kernel, seq=seq, has_first=has_first),
        out_shape=tuple(jax.ShapeDtypeStruct((n, d), F32) for _ in range(8)),
        grid=(n // tm,),
        in_specs=in_specs,
        out_specs=tuple(row(d) for _ in range(8)),
        scratch_shapes=[pltpu.VMEM((SUBLANES, wcols), F32)],
        compiler_params=_params(("arbitrary",)),
        name="rwkv_prep",
    )(*args)


def _tri_inv(a_bds, n, rows, passes):
    mm = lambda xs, ys: [_mm(x, y, NN, passes) for x, y in zip(xs, ys)]
    row = _iota((n, n), 0)
    col = _iota((n, n), 1)
    t = row & (rows - 1)
    i = col & (rows - 1)
    eye = (row == col).astype(F32)
    bshift = RWKV_BASE.bit_length() - 1
    base = (t >> bshift) == (i >> bshift)
    d1 = [jnp.where(base, a, 0.0) for a in a_bds]
    n1 = [eye + d for d in d1]
    d2 = mm(d1, d1)
    n2 = [x + y for x, y in zip(n1, mm(n1, d2))]
    d4 = mm(d2, d2)
    tinv = [x + y for x, y in zip(n2, mm(n2, d4))]
    m = RWKV_BASE
    while m < rows:
        s = m.bit_length() - 1
        tb = t >> s
        ib = i >> s
        sib = ((tb >> 1) == (ib >> 1)) & ((tb & 1) == 1) & ((ib & 1) == 0)
        off = [jnp.where(sib, a, 0.0) for a in a_bds]
        nb = n // m

        def odd(x, m=m, nb=nb):
            return jnp.concatenate([x[j * m:(j + 1) * m] for j in range(1, nb, 2)], axis=0)

        def spread(y, m=m, nb=nb):
            zero = jnp.zeros((m, n), F32)
            return jnp.concatenate([p for j in range(nb // 2) for p in (zero, y[j * m:(j + 1) * m])], axis=0)

        upd = mm(mm([odd(x) for x in tinv], off), tinv)
        tinv = [x + spread(y) for x, y in zip(tinv, upd)]
        m *= 2
    return tinv


def _rwkv_chunks(ins, sts, *, rows, t_valid, passes):
    w = ins[0][0].shape[1]
    n = RWKV_GROUP * rows
    if t_valid < rows:
        valid = _iota((rows, 1), 0) < t_valid
        ins = [tuple(jnp.where(valid, x, 0.0) for x in u) for u in ins]
    r, lw, k, v, ah, bh = ([u[j] for u in ins] for j in range(6))
    lrow = _iota((rows, rows), 0)
    lcol = _iota((rows, rows), 1)
    tril = (lcol <= lrow).astype(BF16)
    hs = HD_B.bit_length() - 1
    rs = rows.bit_length() - 1
    bd_mask = (_iota((n, w), 0) >> rs) == (_iota((n, w), 1) >> hs)
    nmask = (_iota((n, n), 0) >> rs) == (_iota((n, n), 1) >> rs)
    t_idx = _iota((rows, 2 * n), 0)
    i_idx = _iota((rows, 2 * n), 1) & (rows - 1)
    strict = i_idx < t_idx
    incl = i_idx <= t_idx

    def expand(x):
        return jnp.where(bd_mask, jnp.concatenate([x] * RWKV_GROUP, axis=0), 0.0)

    cum = [_mm_exact_lhs(tril, x) for x in lw]
    gl = [c[rows - 1:rows] for c in cum]
    e_neg = [jnp.exp(-c) for c in cum]
    e_end = [jnp.exp(g - c) for g, c in zip(gl, cum)]
    qa = [a * jnp.exp(c - l) for a, c, l in zip(ah, cum, lw)]
    qr = [x * jnp.exp(c) for x, c in zip(r, cum)]
    ke = [jnp.concatenate([expand(b * e), expand(x * e)], axis=0) for b, x, e in zip(bh, k, e_neg)]
    kge = [jnp.concatenate([expand(b * e), expand(x * e)], axis=0) for b, x, e in zip(bh, k, e_end)]
    a_a = [jnp.where(strict, _mm(q, x, NT, passes), 0.0) for q, x in zip(qa, ke)]
    a_r = [jnp.where(incl, _mm(q, x, NT, 1), 0.0) for q, x in zip(qr, ke)]
    v_bd = [expand(x) for x in v]
    w_kv = [_mm(a[:, n:], x, NN, passes) for a, x in zip(a_a, v_bd)]
    a_bd = [jnp.where(nmask, jnp.concatenate([a[:, :n]] * RWKV_GROUP, axis=0), 0.0) for a in a_a]
    tinv = _tri_inv(a_bd, n, rows, passes)
    p1 = [_mm(q, st, NT, passes) for q, st in zip(qa, sts)]
    y_carry = [_mm(q, st, NT, 1) for q, st in zip(qr, sts)]
    c_bd = [_mm(t, expand(p + x), NN, passes) for t, p, x in zip(tinv, p1, w_kv)]
    cv = [jnp.concatenate([c, x], axis=0) for c, x in zip(c_bd, v_bd)]
    st_new = [st * jnp.exp(g) + _mm(c, x, TN, passes) for st, g, c, x in zip(sts, gl, cv, kge)]
    ys = [yc + _mm(a, c, NN, 1) for yc, a, c in zip(y_carry, a_r, cv)]
    return ys, st_new


def _rwkv_scan_kernel(*refs, chunk, n_chunks, t_valid, has_s0, passes, sb):
    if has_s0:
        r_ref, lw_ref, k_ref, v_ref, ah_ref, bh_ref, s0_ref, y_ref, s_ref, st_scr = refs
    else:
        r_ref, lw_ref, k_ref, v_ref, ah_ref, bh_ref, y_ref, s_ref, st_scr = refs
    ti = pl.program_id(1)
    w = RWKV_GROUP * HD_B
    groups = H_B // RWKV_GROUP
    units = [(s, g) for s in range(sb) for g in range(groups)]
    blk = lambda h: slice(h * HD_B, (h + 1) * HD_B)

    @pl.when(ti == 0)
    def _():
        st_scr[...] = jnp.zeros_like(st_scr)
        if has_s0:
            for s, g in units:
                for h in range(RWKV_GROUP):
                    st_scr[s, g, blk(h), blk(h)] = s0_ref[s, g * RWKV_GROUP + h]

    def body(c, carry):
        if n_chunks == 1:
            sl = slice(None)
        else:
            sl = pl.ds(pl.multiple_of(c * chunk, chunk), chunk)
        cols = lambda g: slice(g * w, (g + 1) * w)
        in_refs = (r_ref, lw_ref, k_ref, v_ref, ah_ref, bh_ref)
        ins = [tuple(_pad_rows(ref[s, sl, cols(g)], chunk) for ref in in_refs) for s, g in units]
        ys, st_new = _rwkv_chunks(ins, [st_scr[s, g] for s, g in units],
                                  rows=chunk, t_valid=t_valid, passes=passes)
        for (s, g), y, st in zip(units, ys, st_new):
            st_scr[s, g] = st
            y_ref[s, sl, cols(g)] = y[:t_valid]
        return carry

    if n_chunks == 1:
        body(0, 0)
    else:
        lax.fori_loop(0, n_chunks, body, 0)

    @pl.when(ti == pl.num_programs(1) - 1)
    def _():
        for s, g in units:
            for h in range(RWKV_GROUP):
                s_ref[s, g * RWKV_GROUP + h] = st_scr[s, g, blk(h), blk(h)]


def _rwkv_scan(r, lw, k, v, ah, bh, s0, *, batch, seq, tt, passes, sb):
    has_s0 = s0 is not None
    w = RWKV_GROUP * HD_B
    d_b = H_B * HD_B
    chunk = min(RWKV_CHUNK, max(seq, SUBLANES))
    tt = min(tt, seq)
    n_chunks = max(tt // chunk, 1)
    t_valid = min(chunk, seq)
    tok = pl.BlockSpec((sb, tt, d_b), lambda b, i: (b, i, 0))
    st_spec = pl.BlockSpec((sb, H_B, HD_B, HD_B), lambda b, i: (b, 0, 0, 0))
    args = [r, lw, k, v, ah, bh]
    in_specs = [tok] * 6
    if has_s0:
        args.append(s0)
        in_specs.append(st_spec)
    return pl.pallas_call(
        functools.partial(_rwkv_scan_kernel, chunk=chunk, n_chunks=n_chunks, t_valid=t_valid,
                          has_s0=has_s0, passes=passes, sb=sb),
        out_shape=(jax.ShapeDtypeStruct((batch, seq, d_b), F32),
                   jax.ShapeDtypeStruct((batch, H_B, HD_B, HD_B), F32)),
        grid=(batch // sb, seq // tt),
        in_specs=in_specs,
        out_specs=(tok, st_spec),
        scratch_shapes=[pltpu.VMEM((sb, H_B // RWKV_GROUP, w, w), F32)],
        compiler_params=_params(("arbitrary", "arbitrary")),
        name="rwkv_scan",
    )(*args)


def _merge_kernel(*refs, blocks_a):
    group_a = refs[0:7]
    group_b = refs[7:14]
    (lnw_ref, lnb_ref, seg_ref, wa_ref, wb_ref, wo_ref, n2_ref, rwt_ref, rb_ref,
     x1_o, xn_o, idx_o, gate_o, cnt_o) = refs[14:]
    from_b = pl.program_id(0) >= blocks_a
    y, bonus, g, oa, ga, gb, x = (jnp.where(from_b, rb_[...], ra_[...]) for ra_, rb_ in zip(group_a, group_b))
    seg = seg_ref[...]
    inv = 1.0 / HD_B
    mu = _seg_sum(y, seg) * inv
    dlt = y - mu
    var = _seg_sum(dlt * dlt, seg) * inv
    yn = dlt * lax.rsqrt(var + RWKV_GN_EPS) * lnw_ref[...] + lnb_ref[...]
    ob = (yn + bonus) * g
    merged = _sigmoid(ga) * _mm(oa, wa_ref[...]) + _sigmoid(gb) * _mm(ob, wb_ref[...])
    x1 = x + _mm(merged, wo_ref[...])
    x1_o[...] = x1
    xn = x1 * lax.rsqrt(jnp.mean(x1 * x1, axis=-1, keepdims=True) + EPS) * n2_ref[...]
    _tok_store(xn_o, xn)
    logits = _mm(rwt_ref[...], xn, NT, 3) + rb_ref[...]
    tm = x1.shape[0]
    eio = _iota((N_EXPERTS, tm), 0)
    idx_rows, val_rows = [], []
    cnt = jnp.zeros((N_EXPERTS, 1), F32)
    for _ in range(TOP_K):
        mx = jnp.max(logits, axis=0, keepdims=True)
        pick = jnp.min(jnp.where(logits == mx, eio, N_EXPERTS), axis=0, keepdims=True)
        hit = eio == pick
        cnt = cnt + jnp.sum(hit.astype(F32), axis=1, keepdims=True)
        logits = jnp.where(hit, -jnp.inf, logits)
        idx_rows.append(pick)
        val_rows.append(mx)
    ex = [jnp.exp(vr - val_rows[0]) for vr in val_rows]
    den = ex[0] + ex[1] + ex[2] + ex[3]
    idx_o[...] = _stack_rows(idx_rows)
    gate_o[...] = _stack_rows([e / den for e in ex])

    @pl.when(pl.program_id(0) == 0)
    def _():
        cnt_o[...] = jnp.zeros_like(cnt_o)

    cnt_o[...] += jnp.broadcast_to(cnt, cnt_o.shape)


def _merge(tok_a, tok_b, p, *, tm):
    n_a, d = tok_a[5].shape
    n_b = tok_b[5].shape[0]
    n = n_a + n_b
    blocks_a = n_a // tm
    last_a = blocks_a - 1
    ia = lambda i: jnp.minimum(i, last_a)
    ib = lambda i: jnp.maximum(i - blocks_a, 0)

    def group_specs(tok, im):
        y, _, _, oa, gates, _ = tok
        dm = gates.shape[1] // 2
        row = lambda c: pl.BlockSpec((tm, c), lambda i: (im(i), 0))
        return [row(y.shape[1]), row(y.shape[1]), row(y.shape[1]), row(oa.shape[1]),
                pl.BlockSpec((tm, dm), lambda i: (im(i), 0)), pl.BlockSpec((tm, dm), lambda i: (im(i), 1)),
                row(d)]

    def group_args(tok):
        y, bonus, g, oa, gates, x = tok
        return [y, bonus, g, oa, gates, gates, x]

    full = lambda a: pl.BlockSpec(a.shape, lambda i: (0, 0))
    small = [p["ln_w"], p["ln_b"], p["seg"], p["wa"], p["wb"], p["wo"], p["n2"], p["rwt"], p["rb"]]
    out_row = pl.BlockSpec((tm, d), lambda i: (i, 0))
    out_col = pl.BlockSpec((SUBLANES, tm), lambda i: (0, i))
    return pl.pallas_call(
        functools.partial(_merge_kernel, blocks_a=blocks_a),
        out_shape=(jax.ShapeDtypeStruct((n, d), F32), jax.ShapeDtypeStruct((n * SUBLANES, LANES), F32),
                   jax.ShapeDtypeStruct((SUBLANES, n), jnp.int32),
                   jax.ShapeDtypeStruct((SUBLANES, n), F32),
                   jax.ShapeDtypeStruct((N_EXPERTS, LANES), F32)),
        grid=(n // tm,),
        in_specs=group_specs(tok_a, ia) + group_specs(tok_b, ib) + [full(a) for a in small],
        out_specs=(out_row, pl.BlockSpec((tm * SUBLANES, LANES), lambda i: (i, 0)), out_col, out_col,
                   pl.BlockSpec((N_EXPERTS, LANES), lambda i: (0, 0))),
        compiler_params=_params(("arbitrary",)),
        name="merge_router",
    )(*(group_args(tok_a) + group_args(tok_b) + small))


def _rank_kernel(idx_ref, pstart_ref, dest_o, carry):
    i = pl.program_id(0)

    @pl.when(i == 0)
    def _():
        carry[...] = pstart_ref[...]

    tm = idx_ref.shape[1]
    eio = _iota((N_EXPERTS, tm), 0)
    idx = idx_ref[...]
    hits = [eio == idx[j:j + 1, :] for j in range(TOP_K)]
    sel = hits[0] | hits[1] | hits[2] | hits[3]
    upper = (_iota((tm, tm), 0) < _iota((tm, tm), 1)).astype(BF16)
    rank = _dg(sel.astype(BF16), upper, NN) + carry[:, 0:1]
    rows = [jnp.sum(jnp.where(h, rank, 0.0), axis=0, keepdims=True) for h in hits]
    dest_o[...] = _stack_rows(rows).astype(jnp.int32)
    carry[...] += jnp.sum(sel.astype(F32), axis=1, keepdims=True)


def _rank(idx4, pstart, *, tm):
    n = idx4.shape[1]
    return pl.pallas_call(
        _rank_kernel,
        out_shape=jax.ShapeDtypeStruct((SUBLANES, n), jnp.int32),
        grid=(n // tm,),
        in_specs=[pl.BlockSpec((SUBLANES, tm), lambda i: (0, i)),
                  pl.BlockSpec((N_EXPERTS, LANES), lambda i: (0, 0))],
        out_specs=pl.BlockSpec((SUBLANES, tm), lambda i: (0, i)),
        scratch_shapes=[pltpu.VMEM((N_EXPERTS, LANES), F32)],
        compiler_params=_params(("arbitrary",)),
        name="moe_rank",
    )(idx4, pstart)


def _scatter_kernel(pad_lo, pad_hi, na_ref, dest_ref, x_ref, xs_hbm, zeros, sem, zsem):
    tm = x_ref.shape[0]
    zrows = zeros.shape[0]
    nblk = xs_hbm.shape[0] // zrows

    @pl.when(pl.program_id(0) == 0)
    def _():
        zeros[...] = jnp.zeros_like(zeros)
        zrun = lambda r, size: pltpu.make_async_copy(zeros.at[pl.ds(0, size)], xs_hbm.at[pl.ds(r, size)], zsem)

        def per_expert(op):
            def f(e, c):
                lo = pad_lo[e]
                n_pad = pad_hi[e] - lo
                for bit in reversed(range((zrows - 1).bit_length())):
                    @pl.when(((n_pad >> bit) & 1) == 1)
                    def _(lo=lo, bit=bit):
                        op(zrun(lo, 1 << bit))
                    lo = lo + (n_pad & (1 << bit))
                return c
            return f

        lax.fori_loop(0, N_EXPERTS, per_expert(lambda cp: cp.start()), 0)
        lax.fori_loop(na_ref[0], nblk, lambda b, c: (zrun(b * zrows, zrows).start(), c)[1], 0)
        lax.fori_loop(0, N_EXPERTS, per_expert(lambda cp: cp.wait()), 0)
        lax.fori_loop(na_ref[0], nblk, lambda b, c: (zrun(b * zrows, zrows).wait(), c)[1], 0)

    def issue(t, c):
        for j in range(TOP_K):
            pltpu.make_async_copy(x_ref.at[t], xs_hbm.at[dest_ref[j, t]], sem).start(priority=j % 2)
        return c

    lax.fori_loop(0, tm, issue, 0, unroll=DMA_UNROLL)
    for j in range(TOP_K):
        pltpu.make_async_copy(x_ref, xs_hbm.at[pl.ds(0, tm)], sem).wait()


def _scatter(pad_lo, pad_hi, n_active, dest4, x3, m_total, *, tm):
    n = x3.shape[0]
    tail = x3.shape[1:]
    return pl.pallas_call(
        _scatter_kernel,
        out_shape=jax.ShapeDtypeStruct((m_total,) + tail, x3.dtype),
        grid_spec=pltpu.PrefetchScalarGridSpec(
            num_scalar_prefetch=3,
            grid=(n // tm,),
            in_specs=[pl.BlockSpec((SUBLANES, tm), lambda i, *_: (0, i), memory_space=pltpu.SMEM),
                      pl.BlockSpec((tm,) + tail, lambda i, *_: (i, 0, 0))],
            out_specs=pl.BlockSpec(memory_space=pl.ANY),
            scratch_shapes=[pltpu.VMEM((MOE_ROWS,) + tail, x3.dtype),
                            pltpu.SemaphoreType.DMA, pltpu.SemaphoreType.DMA]),
        compiler_params=_params(("arbitrary",)),
        name="moe_scatter",
    )(pad_lo, pad_hi, n_active, dest4, x3)


def _expert_kernel(be_ref, na_ref, x_ref, wgu_ref, bgu_ref, wd_ref, bd_ref, y_ref, wgu_bf, wd_perm, wd_bf):
    b = pl.program_id(0)
    active = b < na_ref[0]
    half = LANES // 2
    dff = wd_ref.shape[1]

    @pl.when(active & ((b == 0) | (be_ref[b] != be_ref[jnp.maximum(b - 1, 0)])))
    def _():
        wgu_bf[...] = wgu_ref[0].astype(BF16)
        for c in range(wd_ref.shape[2] // LANES):
            cols = slice(c * LANES, (c + 1) * LANES)
            for p in range(dff // LANES):
                r0 = p * LANES
                wd_perm[pl.ds(r0, half, stride=2), :] = wd_ref[0, r0:r0 + half, cols]
                wd_perm[pl.ds(r0 + 1, half, stride=2), :] = wd_ref[0, r0 + half:r0 + LANES, cols]
            wd_bf[:, cols] = wd_perm[...].astype(BF16)

    @pl.when(active)
    def _():
        x = _tok_load(x_ref).astype(BF16)
        h = _dg(x, wgu_bf[...], NN) + bgu_ref[0]
        even = (_iota((1, LANES), 1) & 1) == 0
        acts = []
        for p in range(dff // LANES):
            ha = h[:, 2 * p * LANES:(2 * p + 1) * LANES]
            hb = h[:, (2 * p + 1) * LANES:(2 * p + 2) * LANES]
            hg = jnp.where(even, ha, pltpu.roll(hb, 1, 1))
            hl = jnp.where(even, pltpu.roll(ha, LANES - 1, 1), hb)
            glu = jnp.minimum(hg, SWIGLU_LIMIT)
            lin = jnp.clip(hl, -SWIGLU_LIMIT, SWIGLU_LIMIT)
            acts.append((glu * _sigmoid(SWIGLU_ALPHA * glu) * (lin + 1.0)).astype(BF16))
        act = jnp.concatenate(acts, axis=1)
        _tok_store(y_ref, _dg(act, wd_bf[...], NN) + bd_ref[0])

    @pl.when(jnp.logical_not(active))
    def _():
        y_ref[...] = jnp.zeros_like(y_ref)


def _experts(block_expert, n_active, xs2, w_gu, b_gu, w_down, b_down):
    d = w_gu.shape[1]
    dff = w_down.shape[1]
    per_tok = d // LANES
    m_total = xs2.shape[0] // per_tok
    nblk = m_total // MOE_ROWS
    wspec = lambda r, c: pl.BlockSpec((1, r, c), lambda b, be, na: (be[b], 0, 0))
    tok = pl.BlockSpec((MOE_ROWS * per_tok, LANES), lambda b, be, na: (b, 0))
    return pl.pallas_call(
        _expert_kernel,
        out_shape=jax.ShapeDtypeStruct(xs2.shape, F32),
        grid_spec=pltpu.PrefetchScalarGridSpec(
            num_scalar_prefetch=2,
            grid=(nblk,),
            in_specs=[tok, wspec(d, 2 * dff), wspec(1, 2 * dff), wspec(dff, d), wspec(1, d)],
            out_specs=tok,
            scratch_shapes=[pltpu.VMEM((d, 2 * dff), BF16), pltpu.VMEM((dff, LANES), F32),
                            pltpu.VMEM((dff, d), BF16)]),
        compiler_params=_params(("arbitrary",)),
        name="moe_experts",
    )(block_expert, n_active, xs2, w_gu, b_gu, w_down, b_down)


def _combine_kernel(dest_ref, gate_ref, ys_hbm, x_ref, w_ref, oa_ref, ob_ref, buf, mixed, sems, *, blocks_a):
    tm = x_ref.shape[0]
    half = tm // 2

    def issue(h):
        def f(t, c):
            for j in range(TOP_K):
                pltpu.make_async_copy(ys_hbm.at[dest_ref[j, t]], buf.at[j, t], sems.at[h]).start(priority=j % 2)
            return c
        return f

    def mix(t, c):
        acc = gate_ref[0, t] * buf[0, t]
        for j in range(1, TOP_K):
            acc = acc + gate_ref[j, t] * buf[j, t]
        mixed[pl.ds(pl.multiple_of(t * SUBLANES, SUBLANES), SUBLANES), :] = acc
        return c

    for h in range(2):
        lax.fori_loop(h * half, (h + 1) * half, issue(h), 0, unroll=DMA_UNROLL)
    for h in range(2):
        rows = pl.ds(h * half, half)
        for j in range(TOP_K):
            pltpu.make_async_copy(ys_hbm.at[pl.ds(0, half)], buf.at[j, rows], sems.at[h]).wait()
        lax.fori_loop(h * half, (h + 1) * half, mix, 0, unroll=DMA_UNROLL)

    x = x_ref[...] + _tok_load(mixed)
    y = x * lax.rsqrt(jnp.mean(x * x, axis=-1, keepdims=True) + EPS) * w_ref[...]
    first = pl.program_id(0) < blocks_a

    @pl.when(first)
    def _():
        oa_ref[...] = y

    @pl.when(jnp.logical_not(first))
    def _():
        ob_ref[...] = y


def _combine(dest4, gate4, ys3, x1, w, n_a, *, tm):
    n, d = x1.shape
    tail = ys3.shape[1:]
    blocks_a = n_a // tm
    smem = lambda: pl.BlockSpec((SUBLANES, tm), lambda i: (0, i), memory_space=pltpu.SMEM)
    return pl.pallas_call(
        functools.partial(_combine_kernel, blocks_a=blocks_a),
        out_shape=(jax.ShapeDtypeStruct((n_a, d), F32), jax.ShapeDtypeStruct((n - n_a, d), F32)),
        grid=(n // tm,),
        in_specs=[smem(), smem(), pl.BlockSpec(memory_space=pl.ANY),
                  pl.BlockSpec((tm, d), lambda i: (i, 0)), pl.BlockSpec((1, d), lambda i: (0, 0))],
        out_specs=(pl.BlockSpec((tm, d), lambda i: (jnp.minimum(i, blocks_a - 1), 0)),
                   pl.BlockSpec((tm, d), lambda i: (jnp.maximum(i - blocks_a, 0), 0))),
        scratch_shapes=[pltpu.VMEM((TOP_K, tm) + tail, F32), pltpu.VMEM((tm * SUBLANES, LANES), F32),
                        pltpu.SemaphoreType.DMA((2,))],
        compiler_params=_params(("arbitrary",)),
        name="moe_combine",
    )(dest4, gate4, ys3, x1, w)


def _mixers(x, s_hgrn, s_rwkv, shift_prev, wts):
    batch, seq, d = x.shape
    n = batch * seq
    x2 = x.reshape(n, d)
    tm = min(512, n)
    seqs_per_step = LONG_SEQS_PER_STEP if seq >= HGRN_CHUNK else SHORT_SEQS_PER_STEP
    ph = _norm_proj(x2, wts["n1"], wts["w_h"], normalize=True, tm=tm)
    prw = _norm_proj(x2, wts["n1"], wts["w_rw"], normalize=True, tm=tm)
    pg = _norm_proj(x2, wts["n1"], wts["w_g"], normalize=True, tm=tm)
    oa, s_hgrn_new = _hgrn(ph.reshape(batch, seq, -1), wts["lb"], wts["gw"], s_hgrn,
                           batch=batch, seq=seq, tt=256, sb=seqs_per_step)
    first = None
    if shift_prev is not None:
        prev = _norm_proj(shift_prev, wts["n1"], wts["w_rw"], normalize=False, tm=shift_prev.shape[0])
        first = jnp.repeat(prev, seq, axis=0)
    r, lw, k, v, ah, bh, bonus, g = _rwkv_prep(prw, first, wts, seq=seq, tm=min(256, n))
    to3 = lambda a: a.reshape(batch, seq, -1)
    y, s_rwkv_new = _rwkv_scan(to3(r), to3(lw), to3(k), to3(v), to3(ah), to3(bh), s_rwkv,
                               batch=batch, seq=seq, tt=128, passes=1, sb=seqs_per_step)
    return (y.reshape(n, -1), bonus, g, oa.reshape(n, -1), pg, x2), s_hgrn_new, s_rwkv_new


def kernel(x_prompt, x_sample, state_hgrn, state_rwkv, state_shift, norm1_w, w_in, hgrn_lb_logits,
           hgrn_gnorm_w, rwkv_mu, rwkv_w0, rwkv_w2, rwkv_a0, rwkv_a2, rwkv_g2, rwkv_k_k, rwkv_k_a,
           rwkv_r_k, rwkv_ln_w, rwkv_ln_b, w_branch_a, w_branch_b, w_out, norm2_w, router_w, router_b,
           expert_w_gu, expert_b_gu, expert_w_down, expert_b_down, final_norm_w):
    bp, tp, d = x_prompt.shape
    bs, ts, _ = x_sample.shape
    n_p, n_s = bp * tp, bs * ts
    n = n_p + n_s
    d_a = H_A * DK_A
    d_b = H_B * HD_B
    rw_start = 4 * d_a
    gate_start = rw_start + 3 * d_b + LORA_W + LORA_A + LORA_G
    row2 = lambda a: a.reshape(1, -1)
    lb = jnp.cumsum(jax.nn.softmax(hgrn_lb_logits.astype(F32), axis=0), axis=0)[0]
    hid = jnp.arange(RWKV_GROUP * HD_B, dtype=jnp.int32) // HD_B
    wts = {
        "n1": row2(norm1_w[0]),
        "w_h": w_in[0][:, :rw_start].astype(BF16),
        "w_rw": w_in[0][:, rw_start:gate_start].astype(BF16),
        "w_g": w_in[0][:, gate_start:].astype(BF16),
        "lb": row2(lb), "gw": row2(hgrn_gnorm_w[0]),
        "mu": row2(rwkv_mu[0]), "w0": row2(rwkv_w0[0]), "w2": rwkv_w2[0], "a0": row2(rwkv_a0[0]),
        "a2": rwkv_a2[0], "g2": rwkv_g2[0], "k_k": row2(rwkv_k_k[0]), "k_a": row2(rwkv_k_a[0]),
        "r_k": row2(rwkv_r_k[0]), "seg": (hid[:, None] == hid[None, :]).astype(BF16),
        "ln_w": row2(rwkv_ln_w[0]), "ln_b": row2(rwkv_ln_b[0]),
        "wa": w_branch_a[0].astype(BF16), "wb": w_branch_b[0].astype(BF16), "wo": w_out[0].astype(BF16),
        "n2": row2(norm2_w[0]), "rwt": router_w[0].T, "rb": router_b[0].reshape(-1, 1),
    }
    tok_p, hgrn_p, rwkv_p = _mixers(x_prompt, None, None, None, wts)
    tok_s, hgrn_s, rwkv_s = _mixers(x_sample, state_hgrn[0], state_rwkv[0], state_shift[0], wts)
    shift = _rms_rows(jnp.concatenate([x_prompt[:, -1], x_sample[:, -1]], axis=0), wts["n1"])

    x1, xn2, idx4, gate4, counts = _merge(tok_p, tok_s, wts, tm=256)

    cnt = counts[:, 0].astype(jnp.int32)
    padded = (cnt + MOE_ROWS - 1) // MOE_ROWS * MOE_ROWS
    pend = jnp.cumsum(padded)
    pstart = pend - padded
    nblk = (n * TOP_K + N_EXPERTS * (MOE_ROWS - 1) + MOE_ROWS - 1) // MOE_ROWS
    m_total = nblk * MOE_ROWS
    block_row = jnp.arange(nblk, dtype=jnp.int32)[:, None] * MOE_ROWS
    block_expert = jnp.minimum(jnp.sum((pend[None, :] <= block_row).astype(jnp.int32), axis=1), N_EXPERTS - 1)
    n_active = (pend[-1:] // MOE_ROWS).astype(jnp.int32)
    dest4 = _rank(idx4, jnp.broadcast_to(pstart.astype(F32)[:, None], (N_EXPERTS, LANES)), tm=512)

    per_tok = d // LANES
    xs = _scatter(pstart + cnt, pend, n_active, dest4, xn2.reshape(n, per_tok, LANES), m_total, tm=256)
    ys = _experts(block_expert, n_active, xs.reshape(m_total * per_tok, LANES), expert_w_gu[0],
                  expert_b_gu[0][:, None, :], expert_w_down[0], expert_b_down[0][:, None, :])
    y_p, y_s = _combine(dest4, gate4, ys.reshape(m_total, per_tok, LANES), x1, row2(final_norm_w), n_p, tm=256)

    return (y_p.reshape(bp, tp, d), y_s.reshape(bs, ts, d),
            hgrn_p[None], rwkv_p[None], shift[None, :bp],
            hgrn_s[None], rwkv_s[None], shift[None, bp:])
```

```python
import functools

import jax
import jax.numpy as jnp
from jax import lax
from jax.experimental import pallas as pl
from jax.experimental.pallas import tpu as pltpu

F32 = jnp.float32
BF16 = jnp.bfloat16

H_A = 4
DK_A = 128
H_B = 8
HD_B = 64
LORA_W = 64
LORA_A = 64
LORA_G = 128
N_EXPERTS = 32
TOP_K = 4
SWIGLU_LIMIT = 7.0
SWIGLU_ALPHA = 1.702
EPS = 1e-6
RWKV_GN_EPS = 64e-5

LANES = 128
SUBLANES = 8
HGRN_CHUNK = 64
HGRN_SUB = 8
RWKV_CHUNK = 64
RWKV_GROUP = 4
RWKV_BASE = 8
MOE_ROWS = 384
DMA_UNROLL = 4
LONG_SEQS_PER_STEP = 4
SHORT_SEQS_PER_STEP = 4
VMEM_LIMIT = 52 * 1024 * 1024

NN = (((1,), (0,)), ((), ()))
NT = (((1,), (1,)), ((), ()))
TN = (((0,), (0,)), ((), ()))


def _params(sem):
    return pltpu.CompilerParams(dimension_semantics=sem, vmem_limit_bytes=VMEM_LIMIT)


def _dg(a, b, dims):
    return lax.dot_general(a, b, dims, preferred_element_type=F32)


def _split2(x):
    hi = x.astype(BF16)
    lo = (x - hi.astype(F32)).astype(BF16)
    return hi, lo


def _mm(a, b, dims=NN, passes=1):
    if passes == 1:
        return _dg(a.astype(BF16), b.astype(BF16), dims)
    ah, al = _split2(a)
    bh, bl = _split2(b)
    return _dg(ah, bh, dims) + (_dg(ah, bl, dims) + _dg(al, bh, dims))


def _mm_exact_lhs(sel_bf16, x, dims=NN):
    hi = x.astype(BF16)
    r1 = x - hi.astype(F32)
    mid = r1.astype(BF16)
    lo = (r1 - mid.astype(F32)).astype(BF16)
    return _dg(sel_bf16, hi, dims) + (_dg(sel_bf16, mid, dims) + _dg(sel_bf16, lo, dims))


def _seg_sum(x, seg_bf16):
    w = seg_bf16.shape[0]
    hi, lo = _split2(x)
    parts = [_dg(hi[:, c:c + w], seg_bf16, NN) + _dg(lo[:, c:c + w], seg_bf16, NN)
             for c in range(0, x.shape[1], w)]
    return parts[0] if len(parts) == 1 else jnp.concatenate(parts, axis=1)


def _iota(shape, dim):
    return lax.broadcasted_iota(jnp.int32, shape, dim)


def _sigmoid(x):
    return 1.0 / (1.0 + jnp.exp(-x))


def _tok_load(ref):
    tm = ref.shape[0] // SUBLANES
    return jnp.concatenate([ref[pl.ds(c, tm, stride=SUBLANES), :] for c in range(SUBLANES)], axis=1)


def _tok_store(ref, val):
    tm = val.shape[0]
    for c in range(SUBLANES):
        ref[pl.ds(c, tm, stride=SUBLANES), :] = val[:, c * LANES:(c + 1) * LANES]


def _stack_rows(rows):
    n = rows[0].shape[1]
    sub = _iota((SUBLANES, n), 0)
    out = jnp.zeros((SUBLANES, n), rows[0].dtype)
    for j, r in enumerate(rows):
        out = jnp.where(sub == j, jnp.broadcast_to(r, (SUBLANES, n)), out)
    return out


def _pad_rows(x, rows):
    if x.shape[0] == rows:
        return x
    return jnp.concatenate([x, jnp.zeros((rows - x.shape[0], x.shape[1]), x.dtype)], axis=0)


def _norm_proj_kernel(x_ref, nw_ref, w_ref, o_ref, *, normalize):
    x = x_ref[...]
    if normalize:
        x = x * lax.rsqrt(jnp.mean(x * x, axis=-1, keepdims=True) + EPS) * nw_ref[...]
    o_ref[...] = _dg(x.astype(BF16), w_ref[...], NN)


def _norm_proj(x, nw, w_bf16, *, normalize, tm):
    n, d = x.shape
    nc = w_bf16.shape[1]
    return pl.pallas_call(
        functools.partial(_norm_proj_kernel, normalize=normalize),
        out_shape=jax.ShapeDtypeStruct((n, nc), F32),
        grid=(n // tm,),
        in_specs=[pl.BlockSpec((tm, d), lambda i: (i, 0)),
                  pl.BlockSpec((1, d), lambda i: (0, 0)),
                  pl.BlockSpec((d, nc), lambda i: (0, 0))],
        out_specs=pl.BlockSpec((tm, nc), lambda i: (i, 0)),
        compiler_params=_params(("arbitrary",)),
        name="norm_proj",
    )(x, nw, w_bf16)


def _rms_rows_kernel(x_ref, w_ref, o_ref):
    x = x_ref[...]
    o_ref[...] = x * lax.rsqrt(jnp.mean(x * x, axis=-1, keepdims=True) + EPS) * w_ref[...]


def _rms_rows(x, w):
    return pl.pallas_call(
        _rms_rows_kernel,
        out_shape=jax.ShapeDtypeStruct(x.shape, F32),
        name="rms_rows",
    )(x, w)


def _hgrn_chunks(hqs, hfs, his, lbvs, sts, *, rows, t_valid):
    n_u = len(hqs)
    fs = [lbv + (1.0 - lbv) * _sigmoid(hf) for lbv, hf in zip(lbvs, hfs)]
    gs = [jnp.log(f) for f in fs]
    ks = [1.0 - f for f in fs]
    qs = [hq * _sigmoid(hq) for hq in hqs]
    vs = list(his)
    if t_valid < rows:
        valid = _iota((rows, 1), 0) < t_valid
        gs, ks, qs, vs = ([jnp.where(valid, x, 0.0) for x in xs] for xs in (gs, ks, qs, vs))
    row = _iota((rows, rows), 0)
    col = _iota((rows, rows), 1)
    tril = (col <= row).astype(BF16)
    bs = [_mm_exact_lhs(tril, g) for g in gs]
    sub = min(HGRN_SUB, rows)
    m = rows // 2
    scores = [None] * n_u
    while m >= sub:
        nb = rows // m
        shift = m.bit_length() - 1
        rb = row >> shift
        pair = ((rb & 1) == 1) & ((col >> shift) == rb - 1)
        qes, kes = [], []
        for q, k, b in zip(qs, ks, bs):
            starts = [jnp.zeros((1, DK_A), F32) if j == 0 else b[j * m - 1:j * m] for j in range(nb)]
            ends = [b[j * m + m - 1:j * m + m] for j in range(nb)]
            b_start = jnp.concatenate([jnp.broadcast_to(s, (m, DK_A)) for s in starts], axis=0)
            b_end = jnp.concatenate([jnp.broadcast_to(s, (m, DK_A)) for s in ends], axis=0)
            qes.append(q * jnp.exp(b - b_start))
            kes.append(k * jnp.exp(b_end - b))
        a_ms = [jnp.where(pair, _mm(qe, ke, NT), 0.0) for qe, ke in zip(qes, kes)]
        scores = [a if s is None else s + a for s, a in zip(scores, a_ms)]
        m //= 2
    os_ = [jnp.zeros((rows, DK_A), F32)] * n_u
    if scores[0] is not None:
        os_ = [_mm(s, v, NN) for s, v in zip(scores, vs)]
    rloc = _iota((rows, 1), 0) & (sub - 1)
    for d in range(min(sub, t_valid)):
        sh = (lambda x: x) if d == 0 else (lambda x: pltpu.roll(x, d, 0))
        es = [jnp.exp(jnp.where(rloc >= d, b - sh(b), -jnp.inf)) for b in bs]
        os_ = [o + jnp.sum(q * sh(k) * e, axis=-1, keepdims=True) * sh(v)
               for o, q, k, v, e in zip(os_, qs, ks, vs, es)]
    b_lasts = [b[rows - 1:rows] for b in bs]
    kds = [k * jnp.exp(bl - b) for k, bl, b in zip(ks, b_lasts, bs)]
    qbs = [q * jnp.exp(b) for q, b in zip(qs, bs)]
    os_ = [o + _mm(qb, st, NT) for o, qb, st in zip(os_, qbs, sts)]
    st_new = [st * jnp.exp(bl) + _mm(v, kd, TN) for st, bl, v, kd in zip(sts, b_lasts, vs, kds)]
    return os_, st_new


def _hgrn_kernel(*refs, chunk, n_chunks, t_valid, has_s0, sb):
    if has_s0:
        ph_ref, lb_ref, gw_ref, s0_ref, o_ref, s_ref, st_scr = refs
    else:
        ph_ref, lb_ref, gw_ref, o_ref, s_ref, st_scr = refs
    ti = pl.program_id(1)
    d_a = H_A * DK_A
    units = [(s, h) for s in range(sb) for h in range(H_A)]

    @pl.when(ti == 0)
    def _():
        for s, h in units:
            st_scr[s, h] = s0_ref[s, h].T if has_s0 else jnp.zeros((DK_A, DK_A), F32)

    gw = gw_ref[...]

    def body(c, carry):
        if n_chunks == 1:
            sl = slice(None)
        else:
            sl = pl.ds(pl.multiple_of(c * chunk, chunk), chunk)
        cols = lambda part, h: slice(part * d_a + h * DK_A, part * d_a + (h + 1) * DK_A)
        part = lambda p: [_pad_rows(ph_ref[s, sl, cols(p, h)], chunk) for s, h in units]
        os_, st_new = _hgrn_chunks(part(0), part(1), part(2),
                                   [lb_ref[:, h * DK_A:(h + 1) * DK_A] for _, h in units],
                                   [st_scr[s, h] for s, h in units], rows=chunk, t_valid=t_valid)
        for (s, h), o, st in zip(units, os_, st_new):
            st_scr[s, h] = st
            hg = ph_ref[s, sl, cols(3, h)]
            o = o[:t_valid]
            o = o * lax.rsqrt(jnp.mean(o * o, axis=-1, keepdims=True) + EPS) * gw
            o_ref[s, sl, h * DK_A:(h + 1) * DK_A] = o * (hg * _sigmoid(hg))
        return carry

    if n_chunks == 1:
        body(0, 0)
    else:
        lax.fori_loop(0, n_chunks, body, 0)

    @pl.when(ti == pl.num_programs(1) - 1)
    def _():
        for s, h in units:
            s_ref[s, h] = st_scr[s, h].T


def _hgrn(ph, lb, gw, s0, *, batch, seq, tt, sb):
    has_s0 = s0 is not None
    chunk = min(HGRN_CHUNK, max(seq, SUBLANES))
    tt = min(tt, seq)
    n_chunks = max(tt // chunk, 1)
    t_valid = min(chunk, seq)
    d_a = H_A * DK_A
    st_spec = pl.BlockSpec((sb, H_A, DK_A, DK_A), lambda b, i: (b, 0, 0, 0))
    in_specs = [pl.BlockSpec((sb, tt, 4 * d_a), lambda b, i: (b, i, 0)),
                pl.BlockSpec((1, d_a), lambda b, i: (0, 0)),
                pl.BlockSpec((1, DK_A), lambda b, i: (0, 0))]
    args = [ph, lb, gw]
    if has_s0:
        in_specs.append(st_spec)
        args.append(s0)
    return pl.pallas_call(
        functools.partial(_hgrn_kernel, chunk=chunk, n_chunks=n_chunks, t_valid=t_valid, has_s0=has_s0, sb=sb),
        out_shape=(jax.ShapeDtypeStruct((batch, seq, d_a), F32),
                   jax.ShapeDtypeStruct((batch, H_A, DK_A, DK_A), F32)),
        grid=(batch // sb, seq // tt),
        in_specs=in_specs,
        out_specs=(pl.BlockSpec((sb, tt, d_a), lambda b, i: (b, i, 0)), st_spec),
        scratch_shapes=[pltpu.VMEM((sb, H_A, DK_A, DK_A), F32)],
        compiler_params=_params(("arbitrary", "arbitrary")),
        name="hgrn_scan",
    )(*args)


def _rwkv_prep_kernel(*refs, seq, has_first):
    if has_first:
        (x_ref, n1_ref, wrw_ref, first_ref, mu_ref, w0_ref, w2_ref, a0_ref, a2_ref, g2_ref, kk_ref, ka_ref,
         rk_ref, seg_ref, r_o, lw_o, k_o, v_o, ah_o, bh_o, bonus_o, g_o, carry) = refs
    else:
        (x_ref, n1_ref, wrw_ref, mu_ref, w0_ref, w2_ref, a0_ref, a2_ref, g2_ref, kk_ref, ka_ref,
         rk_ref, seg_ref, r_o, lw_o, k_o, v_o, ah_o, bh_o, bonus_o, g_o, carry) = refs
    d = r_o.shape[1]
    xin = x_ref[...]
    xin = xin * lax.rsqrt(jnp.mean(xin * xin, axis=-1, keepdims=True) + EPS) * n1_ref[...]
    x = _dg(xin.astype(BF16), wrw_ref[...], NN)
    tm = x.shape[0]
    i = pl.program_id(0)

    @pl.when(i == 0)
    def _():
        carry[...] = jnp.zeros_like(carry)

    rowi = _iota((tm, 1), 0)
    prev = pltpu.roll(x, 1, 0)
    prev = jnp.where(rowi == 0, carry[0:1, :], prev)
    seq_start = ((rowi + i * tm) & (seq - 1)) == 0
    if has_first:
        prev = jnp.where(seq_start, first_ref[...], prev)
    else:
        prev = jnp.where(seq_start, 0.0, prev)
    carry[0:1, :] = x[tm - 1:tm, :]
    x = x + (prev - x) * mu_ref[...]
    r = x[:, 0:d]
    k = x[:, d:2 * d]
    v = x[:, 2 * d:3 * d]
    wlo = x[:, 3 * d:3 * d + LORA_W]
    alo = x[:, 3 * d + LORA_W:3 * d + LORA_W + LORA_A]
    glo = x[:, 3 * d + LORA_W + LORA_A:]
    seg = seg_ref[...]
    z = -(w0_ref[...] + _mm(jnp.tanh(wlo), w2_ref[...]))
    softplus = jnp.maximum(z, 0.0) + jnp.log(1.0 + jnp.exp(-jnp.abs(z)))
    lw_o[...] = -jnp.exp(-softplus - 0.5)
    a = _sigmoid(a0_ref[...] + _mm(alo, a2_ref[...]))
    g_o[...] = _mm(_sigmoid(glo), g2_ref[...])
    kk = k * kk_ref[...]
    kk = kk / jnp.maximum(jnp.sqrt(_seg_sum(kk * kk, seg)), 1e-12)
    k2 = k * (1.0 + (a - 1.0) * ka_ref[...])
    r_o[...] = r
    k_o[...] = k2
    v_o[...] = v
    ah_o[...] = -kk
    bh_o[...] = kk * a
    bonus_o[...] = _seg_sum(r * k2 * rk_ref[...], seg) * v


def _rwkv_prep(x2, first, p, *, seq, tm):
    n, dm = x2.shape
    wcols = p["w_rw"].shape[1]
    d = H_B * HD_B
    has_first = first is not None
    row = lambda c: pl.BlockSpec((tm, c), lambda i: (i, 0))
    full = lambda a: pl.BlockSpec(a.shape, lambda i: (0, 0))
    small = [p["mu"], p["w0"], p["w2"], p["a0"], p["a2"], p["g2"], p["k_k"], p["k_a"], p["r_k"], p["seg"]]
    args = [x2, p["n1"], p["w_rw"]] + ([first] if has_first else []) + small
    in_specs = ([row(dm), full(p["n1"]), full(p["w_rw"])] + ([row(wcols)] if has_first else [])
                + [full(a) for a in small])
    return pl.pallas_call(
        functools.partial(_rwkv_prep_kernel, seq=seq, has_first=has_first),
        out_shape=tuple(jax.ShapeDtypeStruct((n, d), F32) for _ in range(8)),
        grid=(n // tm,),
        in_specs=in_specs,
        out_specs=tuple(row(d) for _ in range(8)),
        scratch_shapes=[pltpu.VMEM((SUBLANES, wcols), F32)],
        compiler_params=_params(("arbitrary",)),
        name="rwkv_prep",
    )(*args)


def _tri_inv(a_bds, n, rows, passes):
    mm = lambda xs, ys: [_mm(x, y, NN, passes) for x, y in zip(xs, ys)]
    row = _iota((n, n), 0)
    col = _iota((n, n), 1)
    t = row & (rows - 1)
    i = col & (rows - 1)
    eye = (row == col).astype(F32)
    bshift = RWKV_BASE.bit_length() - 1
    base = (t >> bshift) == (i >> bshift)
    d1 = [jnp.where(base, a, 0.0) for a in a_bds]
    n1 = [eye + d for d in d1]
    d2 = mm(d1, d1)
    n2 = [x + y for x, y in zip(n1, mm(n1, d2))]
    d4 = mm(d2, d2)
    tinv = [x + y for x, y in zip(n2, mm(n2, d4))]
    m = RWKV_BASE
    while m < rows:
        s = m.bit_length() - 1
        tb = t >> s
        ib = i >> s
        sib = ((tb >> 1) == (ib >> 1)) & ((tb & 1) == 1) & ((ib & 1) == 0)
        off = [jnp.where(sib, a, 0.0) for a in a_bds]
        nb = n // m

        def odd(x, m=m, nb=nb):
            return jnp.concatenate([x[j * m:(j + 1) * m] for j in range(1, nb, 2)], axis=0)

        def spread(y, m=m, nb=nb):
            zero = jnp.zeros((m, n), F32)
            return jnp.concatenate([p for j in range(nb // 2) for p in (zero, y[j * m:(j + 1) * m])], axis=0)

        upd = mm(mm([odd(x) for x in tinv], off), tinv)
        tinv = [x + spread(y) for x, y in zip(tinv, upd)]
        m *= 2
    return tinv


def _rwkv_chunks(ins, sts, *, rows, t_valid, passes):
    w = ins[0][0].shape[1]
    n = RWKV_GROUP * rows
    if t_valid < rows:
        valid = _iota((rows, 1), 0) < t_valid
        ins = [tuple(jnp.where(valid, x, 0.0) for x in u) for u in ins]
    r, lw, k, v, ah, bh = ([u[j] for u in ins] for j in range(6))
    lrow = _iota((rows, rows), 0)
    lcol = _iota((rows, rows), 1)
    tril = (lcol <= lrow).astype(BF16)
    hs = HD_B.bit_length() - 1
    rs = rows.bit_length() - 1
    bd_mask = (_iota((n, w), 0) >> rs) == (_iota((n, w), 1) >> hs)
    nmask = (_iota((n, n), 0) >> rs) == (_iota((n, n), 1) >> rs)
    t_idx = _iota((rows, 2 * n), 0)
    i_idx = _iota((rows, 2 * n), 1) & (rows - 1)
    strict = i_idx < t_idx
    incl = i_idx <= t_idx

    def expand(x):
        return jnp.where(bd_mask, jnp.concatenate([x] * RWKV_GROUP, axis=0), 0.0)

    cum = [_mm_exact_lhs(tril, x) for x in lw]
    gl = [c[rows - 1:rows] for c in cum]
    e_neg = [jnp.exp(-c) for c in cum]
    e_end = [jnp.exp(g - c) for g, c in zip(gl, cum)]
    qa = [a * jnp.exp(c - l) for a, c, l in zip(ah, cum, lw)]
    qr = [x * jnp.exp(c) for x, c in zip(r, cum)]
    ke = [jnp.concatenate([expand(b * e), expand(x * e)], axis=0) for b, x, e in zip(bh, k, e_neg)]
    kge = [jnp.concatenate([expand(b * e), expand(x * e)], axis=0) for b, x, e in zip(bh, k, e_end)]
    a_a = [jnp.where(strict, _mm(q, x, NT, passes), 0.0) for q, x in zip(qa, ke)]
    a_r = [jnp.where(incl, _mm(q, x, NT, 1), 0.0) for q, x in zip(qr, ke)]
    v_bd = [expand(x) for x in v]
    w_kv = [_mm(a[:, n:], x, NN, passes) for a, x in zip(a_a, v_bd)]
    a_bd = [jnp.where(nmask, jnp.concatenate([a[:, :n]] * RWKV_GROUP, axis=0), 0.0) for a in a_a]
    tinv = _tri_inv(a_bd, n, rows, passes)
    p1 = [_mm(q, st, NT, passes) for q, st in zip(qa, sts)]
    y_carry = [_mm(q, st, NT, 1) for q, st in zip(qr, sts)]
    c_bd = [_mm(t, expand(p + x), NN, passes) for t, p, x in zip(tinv, p1, w_kv)]
    cv = [jnp.concatenate([c, x], axis=0) for c, x in zip(c_bd, v_bd)]
    st_new = [st * jnp.exp(g) + _mm(c, x, TN, passes) for st, g, c, x in zip(sts, gl, cv, kge)]
    ys = [yc + _mm(a, c, NN, 1) for yc, a, c in zip(y_carry, a_r, cv)]
    return ys, st_new


def _rwkv_scan_kernel(*refs, chunk, n_chunks, t_valid, has_s0, passes, sb):
    if has_s0:
        r_ref, lw_ref, k_ref, v_ref, ah_ref, bh_ref, s0_ref, y_ref, s_ref, st_scr = refs
    else:
        r_ref, lw_ref, k_ref, v_ref, ah_ref, bh_ref, y_ref, s_ref, st_scr = refs
    ti = pl.program_id(1)
    w = RWKV_GROUP * HD_B
    groups = H_B // RWKV_GROUP
    units = [(s, g) for s in range(sb) for g in range(groups)]
    blk = lambda h: slice(h * HD_B, (h + 1) * HD_B)

    @pl.when(ti == 0)
    def _():
        st_scr[...] = jnp.zeros_like(st_scr)
        if has_s0:
            for s, g in units:
                for h in range(RWKV_GROUP):
                    st_scr[s, g, blk(h), blk(h)] = s0_ref[s, g * RWKV_GROUP + h]

    def body(c, carry):
        if n_chunks == 1:
            sl = slice(None)
        else:
            sl = pl.ds(pl.multiple_of(c * chunk, chunk), chunk)
        cols = lambda g: slice(g * w, (g + 1) * w)
        in_refs = (r_ref, lw_ref, k_ref, v_ref, ah_ref, bh_ref)
        ins = [tuple(_pad_rows(ref[s, sl, cols(g)], chunk) for ref in in_refs) for s, g in units]
        ys, st_new = _rwkv_chunks(ins, [st_scr[s, g] for s, g in units],
                                  rows=chunk, t_valid=t_valid, passes=passes)
        for (s, g), y, st in zip(units, ys, st_new):
            st_scr[s, g] = st
            y_ref[s, sl, cols(g)] = y[:t_valid]
        return carry

    if n_chunks == 1:
        body(0, 0)
    else:
        lax.fori_loop(0, n_chunks, body, 0)

    @pl.when(ti == pl.num_programs(1) - 1)
    def _():
        for s, g in units:
            for h in range(RWKV_GROUP):
                s_ref[s, g * RWKV_GROUP + h] = st_scr[s, g, blk(h), blk(h)]


def _rwkv_scan(r, lw, k, v, ah, bh, s0, *, batch, seq, tt, passes, sb):
    has_s0 = s0 is not None
    w = RWKV_GROUP * HD_B
    d_b = H_B * HD_B
    chunk = min(RWKV_CHUNK, max(seq, SUBLANES))
    tt = min(tt, seq)
    n_chunks = max(tt // chunk, 1)
    t_valid = min(chunk, seq)
    tok = pl.BlockSpec((sb, tt, d_b), lambda b, i: (b, i, 0))
    st_spec = pl.BlockSpec((sb, H_B, HD_B, HD_B), lambda b, i: (b, 0, 0, 0))
    args = [r, lw, k, v, ah, bh]
    in_specs = [tok] * 6
    if has_s0:
        args.append(s0)
        in_specs.append(st_spec)
    return pl.pallas_call(
        functools.partial(_rwkv_scan_kernel, chunk=chunk, n_chunks=n_chunks, t_valid=t_valid,
                          has_s0=has_s0, passes=passes, sb=sb),
        out_shape=(jax.ShapeDtypeStruct((batch, seq, d_b), F32),
                   jax.ShapeDtypeStruct((batch, H_B, HD_B, HD_B), F32)),
        grid=(batch // sb, seq // tt),
        in_specs=in_specs,
        out_specs=(tok, st_spec),
        scratch_shapes=[pltpu.VMEM((sb, H_B // RWKV_GROUP, w, w), F32)],
        compiler_params=_params(("arbitrary", "arbitrary")),
        name="rwkv_scan",
    )(*args)


def _merge_kernel(*refs, blocks_a):
    (lnw_ref, lnb_ref, seg_ref, wa_ref, wb_ref, wo_ref, n2_ref, rwt_ref, rb_ref,
     x1_o, xn_o, idx_o, gate_o, cnt_o) = refs[14:]
    from_b = pl.program_id(0) >= blocks_a

    @pl.when(pl.program_id(0) == 0)
    def _():
        cnt_o[...] = jnp.zeros_like(cnt_o)

    def body(group):
        y, bonus, g, oa, ga, gb, x = (ref[...] for ref in group)
        seg = seg_ref[...]
        inv = 1.0 / HD_B
        mu = _seg_sum(y, seg) * inv
        dlt = y - mu
        var = _seg_sum(dlt * dlt, seg) * inv
        yn = dlt * lax.rsqrt(var + RWKV_GN_EPS) * lnw_ref[...] + lnb_ref[...]
        ob = (yn + bonus) * g
        merged = _sigmoid(ga) * _mm(oa, wa_ref[...]) + _sigmoid(gb) * _mm(ob, wb_ref[...])
        x1 = x + _mm(merged, wo_ref[...])
        x1_o[...] = x1
        xn = x1 * lax.rsqrt(jnp.mean(x1 * x1, axis=-1, keepdims=True) + EPS) * n2_ref[...]
        _tok_store(xn_o, xn)
        logits = _mm(rwt_ref[...], xn, NT, 3) + rb_ref[...]
        tm = x1.shape[0]
        eio = _iota((N_EXPERTS, tm), 0)
        idx_rows, val_rows = [], []
        cnt = jnp.zeros((N_EXPERTS, 1), F32)
        for _ in range(TOP_K):
            mx = jnp.max(logits, axis=0, keepdims=True)
            pick = jnp.min(jnp.where(logits == mx, eio, N_EXPERTS), axis=0, keepdims=True)
            hit = eio == pick
            cnt = cnt + jnp.sum(hit.astype(F32), axis=1, keepdims=True)
            logits = jnp.where(hit, -jnp.inf, logits)
            idx_rows.append(pick)
            val_rows.append(mx)
        ex = [jnp.exp(vr - val_rows[0]) for vr in val_rows]
        den = ex[0] + ex[1] + ex[2] + ex[3]
        idx_o[...] = _stack_rows(idx_rows)
        gate_o[...] = _stack_rows([e / den for e in ex])
        cnt_o[...] += jnp.broadcast_to(cnt, cnt_o.shape)

    pl.when(jnp.logical_not(from_b))(lambda: body(refs[0:7]))
    pl.when(from_b)(lambda: body(refs[7:14]))


def _merge(tok_a, tok_b, p, *, tm):
    n_a, d = tok_a[5].shape
    n_b = tok_b[5].shape[0]
    n = n_a + n_b
    blocks_a = n_a // tm
    last_a = blocks_a - 1
    ia = lambda i: jnp.minimum(i, last_a)
    ib = lambda i: jnp.maximum(i - blocks_a, 0)

    def group_specs(tok, im):
        y, _, _, oa, gates, _ = tok
        dm = gates.shape[1] // 2
        row = lambda c: pl.BlockSpec((tm, c), lambda i: (im(i), 0))
        return [row(y.shape[1]), row(y.shape[1]), row(y.shape[1]), row(oa.shape[1]),
                pl.BlockSpec((tm, dm), lambda i: (im(i), 0)), pl.BlockSpec((tm, dm), lambda i: (im(i), 1)),
                row(d)]

    def group_args(tok):
        y, bonus, g, oa, gates, x = tok
        return [y, bonus, g, oa, gates, gates, x]

    full = lambda a: pl.BlockSpec(a.shape, lambda i: (0, 0))
    small = [p["ln_w"], p["ln_b"], p["seg"], p["wa"], p["wb"], p["wo"], p["n2"], p["rwt"], p["rb"]]
    out_row = pl.BlockSpec((tm, d), lambda i: (i, 0))
    out_col = pl.BlockSpec((SUBLANES, tm), lambda i: (0, i))
    return pl.pallas_call(
        functools.partial(_merge_kernel, blocks_a=blocks_a),
        out_shape=(jax.ShapeDtypeStruct((n, d), F32), jax.ShapeDtypeStruct((n * SUBLANES, LANES), F32),
                   jax.ShapeDtypeStruct((SUBLANES, n), jnp.int32),
                   jax.ShapeDtypeStruct((SUBLANES, n), F32),
                   jax.ShapeDtypeStruct((N_EXPERTS, LANES), F32)),
        grid=(n // tm,),
        in_specs=group_specs(tok_a, ia) + group_specs(tok_b, ib) + [full(a) for a in small],
        out_specs=(out_row, pl.BlockSpec((tm * SUBLANES, LANES), lambda i: (i, 0)), out_col, out_col,
                   pl.BlockSpec((N_EXPERTS, LANES), lambda i: (0, 0))),
        compiler_params=_params(("arbitrary",)),
        name="merge_router",
    )(*(group_args(tok_a) + group_args(tok_b) + small))


def _rank_kernel(idx_ref, pstart_ref, dest_o, carry):
    i = pl.program_id(0)

    @pl.when(i == 0)
    def _():
        carry[...] = pstart_ref[...]

    tm = idx_ref.shape[1]
    eio = _iota((N_EXPERTS, tm), 0)
    idx = idx_ref[...]
    hits = [eio == idx[j:j + 1, :] for j in range(TOP_K)]
    sel = hits[0] | hits[1] | hits[2] | hits[3]
    upper = (_iota((tm, tm), 0) < _iota((tm, tm), 1)).astype(BF16)
    rank = _dg(sel.astype(BF16), upper, NN) + carry[:, 0:1]
    rows = [jnp.sum(jnp.where(h, rank, 0.0), axis=0, keepdims=True) for h in hits]
    dest_o[...] = _stack_rows(rows).astype(jnp.int32)
    carry[...] += jnp.sum(sel.astype(F32), axis=1, keepdims=True)


def _rank(idx4, pstart, *, tm):
    n = idx4.shape[1]
    return pl.pallas_call(
        _rank_kernel,
        out_shape=jax.ShapeDtypeStruct((SUBLANES, n), jnp.int32),
        grid=(n // tm,),
        in_specs=[pl.BlockSpec((SUBLANES, tm), lambda i: (0, i)),
                  pl.BlockSpec((N_EXPERTS, LANES), lambda i: (0, 0))],
        out_specs=pl.BlockSpec((SUBLANES, tm), lambda i: (0, i)),
        scratch_shapes=[pltpu.VMEM((N_EXPERTS, LANES), F32)],
        compiler_params=_params(("arbitrary",)),
        name="moe_rank",
    )(idx4, pstart)


def _scatter_kernel(pad_lo, pad_hi, na_ref, dest_ref, x_ref, xs_hbm, zeros, sem, zsem):
    tm = x_ref.shape[0]
    zrows = zeros.shape[0]
    nblk = xs_hbm.shape[0] // zrows

    @pl.when(pl.program_id(0) == 0)
    def _():
        zeros[...] = jnp.zeros_like(zeros)
        zrun = lambda r, size: pltpu.make_async_copy(zeros.at[pl.ds(0, size)], xs_hbm.at[pl.ds(r, size)], zsem)

        def per_expert(op):
            def f(e, c):
                lo = pad_lo[e]
                n_pad = pad_hi[e] - lo
                for bit in reversed(range((zrows - 1).bit_length())):
                    @pl.when(((n_pad >> bit) & 1) == 1)
                    def _(lo=lo, bit=bit):
                        op(zrun(lo, 1 << bit))
                    lo = lo + (n_pad & (1 << bit))
                return c
            return f

        lax.fori_loop(0, N_EXPERTS, per_expert(lambda cp: cp.start()), 0)
        lax.fori_loop(na_ref[0], nblk, lambda b, c: (zrun(b * zrows, zrows).start(), c)[1], 0)
        lax.fori_loop(0, N_EXPERTS, per_expert(lambda cp: cp.wait()), 0)
        lax.fori_loop(na_ref[0], nblk, lambda b, c: (zrun(b * zrows, zrows).wait(), c)[1], 0)

    def issue(t, c):
        for j in range(TOP_K):
            pltpu.make_async_copy(x_ref.at[t], xs_hbm.at[dest_ref[j, t]], sem).start(priority=j % 2)
        return c

    lax.fori_loop(0, tm, issue, 0, unroll=DMA_UNROLL)
    for j in range(TOP_K):
        pltpu.make_async_copy(x_ref, xs_hbm.at[pl.ds(0, tm)], sem).wait()


def _scatter(pad_lo, pad_hi, n_active, dest4, x3, m_total, *, tm):
    n = x3.shape[0]
    tail = x3.shape[1:]
    return pl.pallas_call(
        _scatter_kernel,
        out_shape=jax.ShapeDtypeStruct((m_total,) + tail, x3.dtype),
        grid_spec=pltpu.PrefetchScalarGridSpec(
            num_scalar_prefetch=3,
            grid=(n // tm,),
            in_specs=[pl.BlockSpec((SUBLANES, tm), lambda i, *_: (0, i), memory_space=pltpu.SMEM),
                      pl.BlockSpec((tm,) + tail, lambda i, *_: (i, 0, 0))],
            out_specs=pl.BlockSpec(memory_space=pl.ANY),
            scratch_shapes=[pltpu.VMEM((MOE_ROWS,) + tail, x3.dtype),
                            pltpu.SemaphoreType.DMA, pltpu.SemaphoreType.DMA]),
        compiler_params=_params(("arbitrary",)),
        name="moe_scatter",
    )(pad_lo, pad_hi, n_active, dest4, x3)


def _expert_kernel(be_ref, na_ref, x_ref, wgu_ref, bgu_ref, wd_ref, bd_ref, y_ref, wgu_bf, wd_perm, wd_bf):
    b = pl.program_id(0)
    active = b < na_ref[0]
    half = LANES // 2
    dff = wd_ref.shape[1]

    @pl.when(active & ((b == 0) | (be_ref[b] != be_ref[jnp.maximum(b - 1, 0)])))
    def _():
        wgu_bf[...] = wgu_ref[0].astype(BF16)
        for c in range(wd_ref.shape[2] // LANES):
            cols = slice(c * LANES, (c + 1) * LANES)
            for p in range(dff // LANES):
                r0 = p * LANES
                wd_perm[pl.ds(r0, half, stride=2), :] = wd_ref[0, r0:r0 + half, cols]
                wd_perm[pl.ds(r0 + 1, half, stride=2), :] = wd_ref[0, r0 + half:r0 + LANES, cols]
            wd_bf[:, cols] = wd_perm[...].astype(BF16)

    @pl.when(active)
    def _():
        x = _tok_load(x_ref).astype(BF16)
        h = _dg(x, wgu_bf[...], NN) + bgu_ref[0]
        even = (_iota((1, LANES), 1) & 1) == 0
        acts = []
        for p in range(dff // LANES):
            ha = h[:, 2 * p * LANES:(2 * p + 1) * LANES]
            hb = h[:, (2 * p + 1) * LANES:(2 * p + 2) * LANES]
            hg = jnp.where(even, ha, pltpu.roll(hb, 1, 1))
            hl = jnp.where(even, pltpu.roll(ha, LANES - 1, 1), hb)
            glu = jnp.minimum(hg, SWIGLU_LIMIT)
            lin = jnp.clip(hl, -SWIGLU_LIMIT, SWIGLU_LIMIT)
            acts.append((glu * _sigmoid(SWIGLU_ALPHA * glu) * (lin + 1.0)).astype(BF16))
        act = jnp.concatenate(acts, axis=1)
        _tok_store(y_ref, _dg(act, wd_bf[...], NN) + bd_ref[0])

    @pl.when(jnp.logical_not(active))
    def _():
        y_ref[...] = jnp.zeros_like(y_ref)


def _experts(block_expert, n_active, xs2, w_gu, b_gu, w_down, b_down):
    d = w_gu.shape[1]
    dff = w_down.shape[1]
    per_tok = d // LANES
    m_total = xs2.shape[0] // per_tok
    nblk = m_total // MOE_ROWS
    wspec = lambda r, c: pl.BlockSpec((1, r, c), lambda b, be, na: (be[b], 0, 0))
    tok = pl.BlockSpec((MOE_ROWS * per_tok, LANES), lambda b, be, na: (b, 0))
    return pl.pallas_call(
        _expert_kernel,
        out_shape=jax.ShapeDtypeStruct(xs2.shape, F32),
        grid_spec=pltpu.PrefetchScalarGridSpec(
            num_scalar_prefetch=2,
            grid=(nblk,),
            in_specs=[tok, wspec(d, 2 * dff), wspec(1, 2 * dff), wspec(dff, d), wspec(1, d)],
            out_specs=tok,
            scratch_shapes=[pltpu.VMEM((d, 2 * dff), BF16), pltpu.VMEM((dff, LANES), F32),
                            pltpu.VMEM((dff, d), BF16)]),
        compiler_params=_params(("arbitrary",)),
        name="moe_experts",
    )(block_expert, n_active, xs2, w_gu, b_gu, w_down, b_down)


def _combine_kernel(dest_ref, gate_ref, ys_hbm, x_ref, w_ref, oa_ref, ob_ref, buf, mixed, sems, *, blocks_a):
    tm = x_ref.shape[0]
    half = tm // 2

    def issue(h):
        def f(t, c):
            for j in range(TOP_K):
                pltpu.make_async_copy(ys_hbm.at[dest_ref[j, t]], buf.at[j, t], sems.at[h]).start(priority=j % 2)
            return c
        return f

    def mix(t, c):
        acc = gate_ref[0, t] * buf[0, t]
        for j in range(1, TOP_K):
            acc = acc + gate_ref[j, t] * buf[j, t]
        mixed[pl.ds(pl.multiple_of(t * SUBLANES, SUBLANES), SUBLANES), :] = acc
        return c

    for h in range(2):
        lax.fori_loop(h * half, (h + 1) * half, issue(h), 0, unroll=DMA_UNROLL)
    for h in range(2):
        rows = pl.ds(h * half, half)
        for j in range(TOP_K):
            pltpu.make_async_copy(ys_hbm.at[pl.ds(0, half)], buf.at[j, rows], sems.at[h]).wait()
        lax.fori_loop(h * half, (h + 1) * half, mix, 0, unroll=DMA_UNROLL)

    x = x_ref[...] + _tok_load(mixed)
    y = x * lax.rsqrt(jnp.mean(x * x, axis=-1, keepdims=True) + EPS) * w_ref[...]
    first = pl.program_id(0) < blocks_a

    @pl.when(first)
    def _():
        oa_ref[...] = y

    @pl.when(jnp.logical_not(first))
    def _():
        ob_ref[...] = y


def _combine(dest4, gate4, ys3, x1, w, n_a, *, tm):
    n, d = x1.shape
    tail = ys3.shape[1:]
    blocks_a = n_a // tm
    smem = lambda: pl.BlockSpec((SUBLANES, tm), lambda i: (0, i), memory_space=pltpu.SMEM)
    return pl.pallas_call(
        functools.partial(_combine_kernel, blocks_a=blocks_a),
        out_shape=(jax.ShapeDtypeStruct((n_a, d), F32), jax.ShapeDtypeStruct((n - n_a, d), F32)),
        grid=(n // tm,),
        in_specs=[smem(), smem(), pl.BlockSpec(memory_space=pl.ANY),
                  pl.BlockSpec((tm, d), lambda i: (i, 0)), pl.BlockSpec((1, d), lambda i: (0, 0))],
        out_specs=(pl.BlockSpec((tm, d), lambda i: (jnp.minimum(i, blocks_a - 1), 0)),
                   pl.BlockSpec((tm, d), lambda i: (jnp.maximum(i - blocks_a, 0), 0))),
        scratch_shapes=[pltpu.VMEM((TOP_K, tm) + tail, F32), pltpu.VMEM((tm * SUBLANES, LANES), F32),
                        pltpu.SemaphoreType.DMA((2,))],
        compiler_params=_params(("arbitrary",)),
        name="moe_combine",
    )(dest4, gate4, ys3, x1, w)


def _mixers(x, s_hgrn, s_rwkv, shift_prev, wts):
    batch, seq, d = x.shape
    n = batch * seq
    x2 = x.reshape(n, d)
    tm = min(512, n)
    seqs_per_step = LONG_SEQS_PER_STEP if seq >= HGRN_CHUNK else SHORT_SEQS_PER_STEP
    ph = _norm_proj(x2, wts["n1"], wts["w_h"], normalize=True, tm=tm)
    pg = _norm_proj(x2, wts["n1"], wts["w_g"], normalize=True, tm=tm)
    oa, s_hgrn_new = _hgrn(ph.reshape(batch, seq, -1), wts["lb"], wts["gw"], s_hgrn,
                           batch=batch, seq=seq, tt=256, sb=seqs_per_step)
    first = None
    if shift_prev is not None:
        prev = _norm_proj(shift_prev, wts["n1"], wts["w_rw"], normalize=False, tm=shift_prev.shape[0])
        first = jnp.repeat(prev, seq, axis=0)
    r, lw, k, v, ah, bh, bonus, g = _rwkv_prep(x2, first, wts, seq=seq, tm=min(256, n))
    to3 = lambda a: a.reshape(batch, seq, -1)
    y, s_rwkv_new = _rwkv_scan(to3(r), to3(lw), to3(k), to3(v), to3(ah), to3(bh), s_rwkv,
                               batch=batch, seq=seq, tt=128, passes=1, sb=seqs_per_step)
    return (y.reshape(n, -1), bonus, g, oa.reshape(n, -1), pg, x2), s_hgrn_new, s_rwkv_new


def kernel(x_prompt, x_sample, state_hgrn, state_rwkv, state_shift, norm1_w, w_in, hgrn_lb_logits,
           hgrn_gnorm_w, rwkv_mu, rwkv_w0, rwkv_w2, rwkv_a0, rwkv_a2, rwkv_g2, rwkv_k_k, rwkv_k_a,
           rwkv_r_k, rwkv_ln_w, rwkv_ln_b, w_branch_a, w_branch_b, w_out, norm2_w, router_w, router_b,
           expert_w_gu, expert_b_gu, expert_w_down, expert_b_down, final_norm_w):
    bp, tp, d = x_prompt.shape
    bs, ts, _ = x_sample.shape
    n_p, n_s = bp * tp, bs * ts
    n = n_p + n_s
    d_a = H_A * DK_A
    d_b = H_B * HD_B
    rw_start = 4 * d_a
    gate_start = rw_start + 3 * d_b + LORA_W + LORA_A + LORA_G
    row2 = lambda a: a.reshape(1, -1)
    lb = jnp.cumsum(jax.nn.softmax(hgrn_lb_logits.astype(F32), axis=0), axis=0)[0]
    hid = jnp.arange(RWKV_GROUP * HD_B, dtype=jnp.int32) // HD_B
    wts = {
        "n1": row2(norm1_w[0]),
        "w_h": w_in[0][:, :rw_start].astype(BF16),
        "w_rw": w_in[0][:, rw_start:gate_start].astype(BF16),
        "w_g": w_in[0][:, gate_start:].astype(BF16),
        "lb": row2(lb), "gw": row2(hgrn_gnorm_w[0]),
        "mu": row2(rwkv_mu[0]), "w0": row2(rwkv_w0[0]), "w2": rwkv_w2[0], "a0": row2(rwkv_a0[0]),
        "a2": rwkv_a2[0], "g2": rwkv_g2[0], "k_k": row2(rwkv_k_k[0]), "k_a": row2(rwkv_k_a[0]),
        "r_k": row2(rwkv_r_k[0]), "seg": (hid[:, None] == hid[None, :]).astype(BF16),
        "ln_w": row2(rwkv_ln_w[0]), "ln_b": row2(rwkv_ln_b[0]),
        "wa": w_branch_a[0].astype(BF16), "wb": w_branch_b[0].astype(BF16), "wo": w_out[0].astype(BF16),
        "n2": row2(norm2_w[0]), "rwt": router_w[0].T, "rb": router_b[0].reshape(-1, 1),
    }
    tok_p, hgrn_p, rwkv_p = _mixers(x_prompt, None, None, None, wts)
    tok_s, hgrn_s, rwkv_s = _mixers(x_sample, state_hgrn[0], state_rwkv[0], state_shift[0], wts)
    shift = _rms_rows(jnp.concatenate([x_prompt[:, -1], x_sample[:, -1]], axis=0), wts["n1"])

    x1, xn2, idx4, gate4, counts = _merge(tok_p, tok_s, wts, tm=256)

    cnt = counts[:, 0].astype(jnp.int32)
    padded = (cnt + MOE_ROWS - 1) // MOE_ROWS * MOE_ROWS
    pend = jnp.cumsum(padded)
    pstart = pend - padded
    nblk = (n * TOP_K + N_EXPERTS * (MOE_ROWS - 1) + MOE_ROWS - 1) // MOE_ROWS
    m_total = nblk * MOE_ROWS
    block_row = jnp.arange(nblk, dtype=jnp.int32)[:, None] * MOE_ROWS
    block_expert = jnp.minimum(jnp.sum((pend[None, :] <= block_row).astype(jnp.int32), axis=1), N_EXPERTS - 1)
    n_active = (pend[-1:] // MOE_ROWS).astype(jnp.int32)
    dest4 = _rank(idx4, jnp.broadcast_to(pstart.astype(F32)[:, None], (N_EXPERTS, LANES)), tm=512)

    per_tok = d // LANES
    xs = _scatter(pstart + cnt, pend, n_active, dest4, xn2.reshape(n, per_tok, LANES), m_total, tm=256)
    ys = _experts(block_expert, n_active, xs.reshape(m_total * per_tok, LANES), expert_w_gu[0],
                  expert_b_gu[0][:, None, :], expert_w_down[0], expert_b_down[0][:, None, :])
    y_p, y_s = _combine(dest4, gate4, ys.reshape(m_total, per_tok, LANES), x1, row2(final_norm_w), n_p, tm=256)

    return (y_p.reshape(bp, tp, d), y_s.reshape(bs, ts, d),
            hgrn_p[None], rwkv_p[None], shift[None, :bp],
            hgrn_s[None], rwkv_s[None], shift[None, bp:])
```

```python
import functools

import jax
import jax.numpy as jnp
from jax import lax
from jax.experimental import pallas as pl
from jax.experimental.pallas import tpu as pltpu

F32 = jnp.float32
BF16 = jnp.bfloat16

H_A = 4
DK_A = 128
H_B = 8
HD_B = 64
LORA_W = 64
LORA_A = 64
LORA_G = 128
N_EXPERTS = 32
TOP_K = 4
SWIGLU_LIMIT = 7.0
SWIGLU_ALPHA = 1.702
EPS = 1e-6
RWKV_GN_EPS = 64e-5

LANES = 128
SUBLANES = 8
HGRN_CHUNK = 64
HGRN_SUB = 8
RWKV_CHUNK = 64
RWKV_GROUP = 4
RWKV_BASE = 8
MOE_ROWS = 384
DMA_UNROLL = 4
LONG_SEQS_PER_STEP = 4
SHORT_SEQS_PER_STEP = 4
VMEM_LIMIT = 52 * 1024 * 1024

NN = (((1,), (0,)), ((), ()))
NT = (((1,), (1,)), ((), ()))
TN = (((0,), (0,)), ((), ()))


def _params(sem):
    return pltpu.CompilerParams(dimension_semantics=sem, vmem_limit_bytes=VMEM_LIMIT)


def _dg(a, b, dims):
    return lax.dot_general(a, b, dims, preferred_element_type=F32)


def _split2(x):
    hi = x.astype(BF16)
    lo = (x - hi.astype(F32)).astype(BF16)
    return hi, lo


def _mm(a, b, dims=NN, passes=1):
    if passes == 1:
        return _dg(a.astype(BF16), b.astype(BF16), dims)
    ah, al = _split2(a)
    bh, bl = _split2(b)
    return _dg(ah, bh, dims) + (_dg(ah, bl, dims) + _dg(al, bh, dims))


def _mm_exact_lhs(sel_bf16, x, dims=NN):
    hi = x.astype(BF16)
    r1 = x - hi.astype(F32)
    mid = r1.astype(BF16)
    lo = (r1 - mid.astype(F32)).astype(BF16)
    return _dg(sel_bf16, hi, dims) + (_dg(sel_bf16, mid, dims) + _dg(sel_bf16, lo, dims))


def _seg_sum(x, seg_bf16):
    w = seg_bf16.shape[0]
    hi, lo = _split2(x)
    parts = [_dg(hi[:, c:c + w], seg_bf16, NN) + _dg(lo[:, c:c + w], seg_bf16, NN)
             for c in range(0, x.shape[1], w)]
    return parts[0] if len(parts) == 1 else jnp.concatenate(parts, axis=1)


def _iota(shape, dim):
    return lax.broadcasted_iota(jnp.int32, shape, dim)


def _sigmoid(x):
    return 1.0 / (1.0 + jnp.exp(-x))


def _tok_load(ref):
    tm = ref.shape[0] // SUBLANES
    return jnp.concatenate([ref[pl.ds(c, tm, stride=SUBLANES), :] for c in range(SUBLANES)], axis=1)


def _tok_store(ref, val):
    tm = val.shape[0]
    for c in range(SUBLANES):
        ref[pl.ds(c, tm, stride=SUBLANES), :] = val[:, c * LANES:(c + 1) * LANES]


def _stack_rows(rows):
    n = rows[0].shape[1]
    sub = _iota((SUBLANES, n), 0)
    out = jnp.zeros((SUBLANES, n), rows[0].dtype)
    for j, r in enumerate(rows):
        out = jnp.where(sub == j, jnp.broadcast_to(r, (SUBLANES, n)), out)
    return out


def _pad_rows(x, rows):
    if x.shape[0] == rows:
        return x
    return jnp.concatenate([x, jnp.zeros((rows - x.shape[0], x.shape[1]), x.dtype)], axis=0)


def _norm_proj_kernel(x_ref, nw_ref, w_ref, o_ref, *, normalize):
    x = x_ref[...]
    if normalize:
        x = x * lax.rsqrt(jnp.mean(x * x, axis=-1, keepdims=True) + EPS) * nw_ref[...]
    o_ref[...] = _dg(x.astype(BF16), w_ref[...], NN)


def _norm_proj(x, nw, w_bf16, *, normalize, tm):
    n, d = x.shape
    nc = w_bf16.shape[1]
    return pl.pallas_call(
        functools.partial(_norm_proj_kernel, normalize=normalize),
        out_shape=jax.ShapeDtypeStruct((n, nc), F32),
        grid=(n // tm,),
        in_specs=[pl.BlockSpec((tm, d), lambda i: (i, 0)),
                  pl.BlockSpec((1, d), lambda i: (0, 0)),
                  pl.BlockSpec((d, nc), lambda i: (0, 0))],
        out_specs=pl.BlockSpec((tm, nc), lambda i: (i, 0)),
        compiler_params=_params(("arbitrary",)),
        name="norm_proj",
    )(x, nw, w_bf16)


def _rms_rows_kernel(x_ref, w_ref, o_ref):
    x = x_ref[...]
    o_ref[...] = x * lax.rsqrt(jnp.mean(x * x, axis=-1, keepdims=True) + EPS) * w_ref[...]


def _rms_rows(x, w):
    return pl.pallas_call(
        _rms_rows_kernel,
        out_shape=jax.ShapeDtypeStruct(x.shape, F32),
        name="rms_rows",
    )(x, w)


def _hgrn_chunks(hqs, hfs, his, lbvs, sts, *, rows, t_valid):
    n_u = len(hqs)
    fs = [lbv + (1.0 - lbv) * _sigmoid(hf) for lbv, hf in zip(lbvs, hfs)]
    gs = [jnp.log(f) for f in fs]
    ks = [1.0 - f for f in fs]
    qs = [hq * _sigmoid(hq) for hq in hqs]
    vs = list(his)
    if t_valid < rows:
        valid = _iota((rows, 1), 0) < t_valid
        gs, ks, qs, vs = ([jnp.where(valid, x, 0.0) for x in xs] for xs in (gs, ks, qs, vs))
    row = _iota((rows, rows), 0)
    col = _iota((rows, rows), 1)
    tril = (col <= row).astype(BF16)
    bs = [_mm_exact_lhs(tril, g) for g in gs]
    sub = min(HGRN_SUB, rows)
    m = rows // 2
    scores = [None] * n_u
    while m >= sub:
        nb = rows // m
        shift = m.bit_length() - 1
        rb = row >> shift
        pair = ((rb & 1) == 1) & ((col >> shift) == rb - 1)
        qes, kes = [], []
        for q, k, b in zip(qs, ks, bs):
            starts = [jnp.zeros((1, DK_A), F32) if j == 0 else b[j * m - 1:j * m] for j in range(nb)]
            ends = [b[j * m + m - 1:j * m + m] for j in range(nb)]
            b_start = jnp.concatenate([jnp.broadcast_to(s, (m, DK_A)) for s in starts], axis=0)
            b_end = jnp.concatenate([jnp.broadcast_to(s, (m, DK_A)) for s in ends], axis=0)
            qes.append(q * jnp.exp(b - b_start))
            kes.append(k * jnp.exp(b_end - b))
        a_ms = [jnp.where(pair, _mm(qe, ke, NT), 0.0) for qe, ke in zip(qes, kes)]
        scores = [a if s is None else s + a for s, a in zip(scores, a_ms)]
        m //= 2
    os_ = [jnp.zeros((rows, DK_A), F32)] * n_u
    if scores[0] is not None:
        os_ = [_mm(s, v, NN) for s, v in zip(scores, vs)]
    rloc = _iota((rows, 1), 0) & (sub - 1)
    for d in range(min(sub, t_valid)):
        sh = (lambda x: x) if d == 0 else (lambda x: pltpu.roll(x, d, 0))
        es = [jnp.exp(jnp.where(rloc >= d, b - sh(b), -jnp.inf)) for b in bs]
        os_ = [o + jnp.sum(q * sh(k) * e, axis=-1, keepdims=True) * sh(v)
               for o, q, k, v, e in zip(os_, qs, ks, vs, es)]
    b_lasts = [b[rows - 1:rows] for b in bs]
    kds = [k * jnp.exp(bl - b) for k, bl, b in zip(ks, b_lasts, bs)]
    qbs = [q * jnp.exp(b) for q, b in zip(qs, bs)]
    os_ = [o + _mm(qb, st, NT) for o, qb, st in zip(os_, qbs, sts)]
    st_new = [st * jnp.exp(bl) + _mm(v, kd, TN) for st, bl, v, kd in zip(sts, b_lasts, vs, kds)]
    return os_, st_new


def _hgrn_kernel(*refs, chunk, n_chunks, t_valid, has_s0, sb):
    if has_s0:
        ph_ref, lb_ref, gw_ref, s0_ref, o_ref, s_ref, st_scr = refs
    else:
        ph_ref, lb_ref, gw_ref, o_ref, s_ref, st_scr = refs
    ti = pl.program_id(1)
    d_a = H_A * DK_A
    units = [(s, h) for s in range(sb) for h in range(H_A)]

    @pl.when(ti == 0)
    def _():
        for s, h in units:
            st_scr[s, h] = s0_ref[s, h].T if has_s0 else jnp.zeros((DK_A, DK_A), F32)

    gw = gw_ref[...]

    def body(c, carry):
        if n_chunks == 1:
            sl = slice(None)
        else:
            sl = pl.ds(pl.multiple_of(c * chunk, chunk), chunk)
        cols = lambda part, h: slice(part * d_a + h * DK_A, part * d_a + (h + 1) * DK_A)
        part = lambda p: [_pad_rows(ph_ref[s, sl, cols(p, h)], chunk) for s, h in units]
        os_, st_new = _hgrn_chunks(part(0), part(1), part(2),
                                   [lb_ref[:, h * DK_A:(h + 1) * DK_A] for _, h in units],
                                   [st_scr[s, h] for s, h in units], rows=chunk, t_valid=t_valid)
        for (s, h), o, st in zip(units, os_, st_new):
            st_scr[s, h] = st
            hg = ph_ref[s, sl, cols(3, h)]
            o = o[:t_valid]
            o = o * lax.rsqrt(jnp.mean(o * o, axis=-1, keepdims=True) + EPS) * gw
            o_ref[s, sl, h * DK_A:(h + 1) * DK_A] = o * (hg * _sigmoid(hg))
        return carry

    if n_chunks == 1:
        body(0, 0)
    else:
        lax.fori_loop(0, n_chunks, body, 0)

    @pl.when(ti == pl.num_programs(1) - 1)
    def _():
        for s, h in units:
            s_ref[s, h] = st_scr[s, h].T


def _hgrn(ph, lb, gw, s0, *, batch, seq, tt, sb):
    has_s0 = s0 is not None
    chunk = min(HGRN_CHUNK, max(seq, SUBLANES))
    tt = min(tt, seq)
    n_chunks = max(tt // chunk, 1)
    t_valid = min(chunk, seq)
    d_a = H_A * DK_A
    st_spec = pl.BlockSpec((sb, H_A, DK_A, DK_A), lambda b, i: (b, 0, 0, 0))
    in_specs = [pl.BlockSpec((sb, tt, 4 * d_a), lambda b, i: (b, i, 0)),
                pl.BlockSpec((1, d_a), lambda b, i: (0, 0)),
                pl.BlockSpec((1, DK_A), lambda b, i: (0, 0))]
    args = [ph, lb, gw]
    if has_s0:
        in_specs.append(st_spec)
        args.append(s0)
    return pl.pallas_call(
        functools.partial(_hgrn_kernel, chunk=chunk, n_chunks=n_chunks, t_valid=t_valid, has_s0=has_s0, sb=sb),
        out_shape=(jax.ShapeDtypeStruct((batch, seq, d_a), F32),
                   jax.ShapeDtypeStruct((batch, H_A, DK_A, DK_A), F32)),
        grid=(batch // sb, seq // tt),
        in_specs=in_specs,
        out_specs=(pl.BlockSpec((sb, tt, d_a), lambda b, i: (b, i, 0)), st_spec),
        scratch_shapes=[pltpu.VMEM((sb, H_A, DK_A, DK_A), F32)],
        compiler_params=_params(("arbitrary", "arbitrary")),
        name="hgrn_scan",
    )(*args)


def _rwkv_prep_kernel(*refs, seq, has_first):
    if has_first:
        (x_ref, n1_ref, wrw_ref, first_ref, mu_ref, w0_ref, w2_ref, a0_ref, a2_ref, g2_ref, kk_ref, ka_ref,
         rk_ref, seg_ref, r_o, lw_o, k_o, v_o, ah_o, bh_o, bonus_o, g_o, carry) = refs
    else:
        (x_ref, n1_ref, wrw_ref, mu_ref, w0_ref, w2_ref, a0_ref, a2_ref, g2_ref, kk_ref, ka_ref,
         rk_ref, seg_ref, r_o, lw_o, k_o, v_o, ah_o, bh_o, bonus_o, g_o, carry) = refs
    d = r_o.shape[1]
    xin = x_ref[...]
    xin = xin * lax.rsqrt(jnp.mean(xin * xin, axis=-1, keepdims=True) + EPS) * n1_ref[...]
    x = _dg(xin.astype(BF16), wrw_ref[...], NN)
    tm = x.shape[0]
    i = pl.program_id(0)

    @pl.when(i == 0)
    def _():
        carry[...] = jnp.zeros_like(carry)

    rowi = _iota((tm, 1), 0)
    prev = pltpu.roll(x, 1, 0)
    prev = jnp.where(rowi == 0, carry[0:1, :], prev)
    seq_start = ((rowi + i * tm) & (seq - 1)) == 0
    if has_first:
        prev = jnp.where(seq_start, first_ref[...], prev)
    else:
        prev = jnp.where(seq_start, 0.0, prev)
    carry[0:1, :] = x[tm - 1:tm, :]
    x = x + (prev - x) * mu_ref[...]
    r = x[:, 0:d]
    k = x[:, d:2 * d]
    v = x[:, 2 * d:3 * d]
    wlo = x[:, 3 * d:3 * d + LORA_W]
    alo = x[:, 3 * d + LORA_W:3 * d + LORA_W + LORA_A]
    glo = x[:, 3 * d + LORA_W + LORA_A:]
    seg = seg_ref[...]
    z = -(w0_ref[...] + _mm(jnp.tanh(wlo), w2_ref[...]))
    softplus = jnp.maximum(z, 0.0) + jnp.log(1.0 + jnp.exp(-jnp.abs(z)))
    lw_o[...] = -jnp.exp(-softplus - 0.5)
    a = _sigmoid(a0_ref[...] + _mm(alo, a2_ref[...]))
    g_o[...] = _mm(_sigmoid(glo), g2_ref[...])
    kk = k * kk_ref[...]
    kk = kk / jnp.maximum(jnp.sqrt(_seg_sum(kk * kk, seg)), 1e-12)
    k2 = k * (1.0 + (a - 1.0) * ka_ref[...])
    r_o[...] = r
    k_o[...] = k2
    v_o[...] = v
    ah_o[...] = -kk
    bh_o[...] = kk * a
    bonus_o[...] = _seg_sum(r * k2 * rk_ref[...], seg) * v


def _rwkv_prep(x2, first, p, *, seq, tm):
    n, dm = x2.shape
    wcols = p["w_rw"].shape[1]
    d = H_B * HD_B
    has_first = first is not None
    row = lambda c: pl.BlockSpec((tm, c), lambda i: (i, 0))
    full = lambda a: pl.BlockSpec(a.shape, lambda i: (0, 0))
    small = [p["mu"], p["w0"], p["w2"], p["a0"], p["a2"], p["g2"], p["k_k"], p["k_a"], p["r_k"], p["seg"]]
    args = [x2, p["n1"], p["w_rw"]] + ([first] if has_first else []) + small
    in_specs = ([row(dm), full(p["n1"]), full(p["w_rw"])] + ([row(wcols)] if has_first else [])
                + [full(a) for a in small])
    return pl.pallas_call(
        functools.partial(_rwkv_prep_kernel, seq=seq, has_first=has_first),
        out_shape=tuple(jax.ShapeDtypeStruct((n, d), F32) for _ in range(8)),
        grid=(n // tm,),
        in_specs=in_specs,
        out_specs=tuple(row(d) for _ in range(8)),
        scratch_shapes=[pltpu.VMEM((SUBLANES, wcols), F32)],
        compiler_params=_params(("arbitrary",)),
        name="rwkv_prep",
    )(*args)


def _tri_inv(a_bds, n, rows, passes):
    mm = lambda xs, ys: [_mm(x, y, NN, passes) for x, y in zip(xs, ys)]
    row = _iota((n, n), 0)
    col = _iota((n, n), 1)
    t = row & (rows - 1)
    i = col & (rows - 1)
    eye = (row == col).astype(F32)
    bshift = RWKV_BASE.bit_length() - 1
    base = (t >> bshift) == (i >> bshift)
    d1 = [jnp.where(base, a, 0.0) for a in a_bds]
    n1 = [eye + d for d in d1]
    d2 = mm(d1, d1)
    n2 = [x + y for x, y in zip(n1, mm(n1, d2))]
    d4 = mm(d2, d2)
    tinv = [x + y for x, y in zip(n2, mm(n2, d4))]
    m = RWKV_BASE
    while m < rows:
        s = m.bit_length() - 1
        tb = t >> s
        ib = i >> s
        sib = ((tb >> 1) == (ib >> 1)) & ((tb & 1) == 1) & ((ib & 1) == 0)
        off = [jnp.where(sib, a, 0.0) for a in a_bds]
        nb = n // m

        def odd(x, m=m, nb=nb):
            return jnp.concatenate([x[j * m:(j + 1) * m] for j in range(1, nb, 2)], axis=0)

        def spread(y, m=m, nb=nb):
            zero = jnp.zeros((m, n), F32)
            return jnp.concatenate([p for j in range(nb // 2) for p in (zero, y[j * m:(j + 1) * m])], axis=0)

        upd = mm(mm([odd(x) for x in tinv], off), tinv)
        tinv = [x + spread(y) for x, y in zip(tinv, upd)]
        m *= 2
    return tinv


def _rwkv_chunks(ins, sts, *, rows, t_valid, passes):
    w = ins[0][0].shape[1]
    n = RWKV_GROUP * rows
    if t_valid < rows:
        valid = _iota((rows, 1), 0) < t_valid
        ins = [tuple(jnp.where(valid, x, 0.0) for x in u) for u in ins]
    r, lw, k, v, ah, bh = ([u[j] for u in ins] for j in range(6))
    lrow = _iota((rows, rows), 0)
    lcol = _iota((rows, rows), 1)
    tril = (lcol <= lrow).astype(BF16)
    hs = HD_B.bit_length() - 1
    rs = rows.bit_length() - 1
    bd_mask = (_iota((n, w), 0) >> rs) == (_iota((n, w), 1) >> hs)
    nmask = (_iota((n, n), 0) >> rs) == (_iota((n, n), 1) >> rs)
    t_idx = _iota((rows, 2 * n), 0)
    i_idx = _iota((rows, 2 * n), 1) & (rows - 1)
    strict = i_idx < t_idx
    incl = i_idx <= t_idx

    def expand(x):
        return jnp.where(bd_mask, jnp.concatenate([x] * RWKV_GROUP, axis=0), 0.0)

    cum = [_mm_exact_lhs(tril, x) for x in lw]
    gl = [c[rows - 1:rows] for c in cum]
    e_neg = [jnp.exp(-c) for c in cum]
    e_end = [jnp.exp(g - c) for g, c in zip(gl, cum)]
    qa = [a * jnp.exp(c - l) for a, c, l in zip(ah, cum, lw)]
    qr = [x * jnp.exp(c) for x, c in zip(r, cum)]
    ke = [jnp.concatenate([expand(b * e), expand(x * e)], axis=0) for b, x, e in zip(bh, k, e_neg)]
    kge = [jnp.concatenate([expand(b * e), expand(x * e)], axis=0) for b, x, e in zip(bh, k, e_end)]
    q2 = [jnp.concatenate([a, x], axis=0) for a, x in zip(qa, qr)]
    scores = [_mm(q, x, NT, passes) for q, x in zip(q2, ke)]
    a_a = [jnp.where(strict, s[:rows], 0.0) for s in scores]
    a_r = [jnp.where(incl, s[rows:], 0.0) for s in scores]
    v_bd = [expand(x) for x in v]
    w_kv = [_mm(a[:, n:], x, NN, passes) for a, x in zip(a_a, v_bd)]
    a_bd = [jnp.where(nmask, jnp.concatenate([a[:, :n]] * RWKV_GROUP, axis=0), 0.0) for a in a_a]
    tinv = _tri_inv(a_bd, n, rows, passes)
    carried = [_mm(q, st, NT, passes) for q, st in zip(q2, sts)]
    p1 = [x[:rows] for x in carried]
    y_carry = [x[rows:] for x in carried]
    c_bd = [_mm(t, expand(p + x), NN, passes) for t, p, x in zip(tinv, p1, w_kv)]
    cv = [jnp.concatenate([c, x], axis=0) for c, x in zip(c_bd, v_bd)]
    st_new = [st * jnp.exp(g) + _mm(c, x, TN, passes) for st, g, c, x in zip(sts, gl, cv, kge)]
    ys = [yc + _mm(a, c, NN, 1) for yc, a, c in zip(y_carry, a_r, cv)]
    return ys, st_new


def _rwkv_scan_kernel(*refs, chunk, n_chunks, t_valid, has_s0, passes, sb):
    if has_s0:
        r_ref, lw_ref, k_ref, v_ref, ah_ref, bh_ref, s0_ref, y_ref, s_ref, st_scr = refs
    else:
        r_ref, lw_ref, k_ref, v_ref, ah_ref, bh_ref, y_ref, s_ref, st_scr = refs
    ti = pl.program_id(1)
    w = RWKV_GROUP * HD_B
    groups = H_B // RWKV_GROUP
    units = [(s, g) for s in range(sb) for g in range(groups)]
    blk = lambda h: slice(h * HD_B, (h + 1) * HD_B)

    @pl.when(ti == 0)
    def _():
        st_scr[...] = jnp.zeros_like(st_scr)
        if has_s0:
            for s, g in units:
                for h in range(RWKV_GROUP):
                    st_scr[s, g, blk(h), blk(h)] = s0_ref[s, g * RWKV_GROUP + h]

    def body(c, carry):
        if n_chunks == 1:
            sl = slice(None)
        else:
            sl = pl.ds(pl.multiple_of(c * chunk, chunk), chunk)
        cols = lambda g: slice(g * w, (g + 1) * w)
        in_refs = (r_ref, lw_ref, k_ref, v_ref, ah_ref, bh_ref)
        ins = [tuple(_pad_rows(ref[s, sl, cols(g)], chunk) for ref in in_refs) for s, g in units]
        ys, st_new = _rwkv_chunks(ins, [st_scr[s, g] for s, g in units],
                                  rows=chunk, t_valid=t_valid, passes=passes)
        for (s, g), y, st in zip(units, ys, st_new):
            st_scr[s, g] = st
            y_ref[s, sl, cols(g)] = y[:t_valid]
        return carry

    if n_chunks == 1:
        body(0, 0)
    else:
        lax.fori_loop(0, n_chunks, body, 0)

    @pl.when(ti == pl.num_programs(1) - 1)
    def _():
        for s, g in units:
            for h in range(RWKV_GROUP):
                s_ref[s, g * RWKV_GROUP + h] = st_scr[s, g, blk(h), blk(h)]


def _rwkv_scan(r, lw, k, v, ah, bh, s0, *, batch, seq, tt, passes, sb):
    has_s0 = s0 is not None
    w = RWKV_GROUP * HD_B
    d_b = H_B * HD_B
    chunk = min(RWKV_CHUNK, max(seq, SUBLANES))
    tt = min(tt, seq)
    n_chunks = max(tt // chunk, 1)
    t_valid = min(chunk, seq)
    tok = pl.BlockSpec((sb, tt, d_b), lambda b, i: (b, i, 0))
    st_spec = pl.BlockSpec((sb, H_B, HD_B, HD_B), lambda b, i: (b, 0, 0, 0))
    args = [r, lw, k, v, ah, bh]
    in_specs = [tok] * 6
    if has_s0:
        args.append(s0)
        in_specs.append(st_spec)
    return pl.pallas_call(
        functools.partial(_rwkv_scan_kernel, chunk=chunk, n_chunks=n_chunks, t_valid=t_valid,
                          has_s0=has_s0, passes=passes, sb=sb),
        out_shape=(jax.ShapeDtypeStruct((batch, seq, d_b), F32),
                   jax.ShapeDtypeStruct((batch, H_B, HD_B, HD_B), F32)),
        grid=(batch // sb, seq // tt),
        in_specs=in_specs,
        out_specs=(tok, st_spec),
        scratch_shapes=[pltpu.VMEM((sb, H_B // RWKV_GROUP, w, w), F32)],
        compiler_params=_params(("arbitrary", "arbitrary")),
        name="rwkv_scan",
    )(*args)


def _merge_kernel(*refs, blocks_a):
    (lnw_ref, lnb_ref, seg_ref, wa_ref, wb_ref, wo_ref, n2_ref, rwt_ref, rb_ref,
     x1_o, xn_o, idx_o, gate_o, cnt_o) = refs[14:]
    from_b = pl.program_id(0) >= blocks_a

    @pl.when(pl.program_id(0) == 0)
    def _():
        cnt_o[...] = jnp.zeros_like(cnt_o)

    def body(group):
        y, bonus, g, oa, ga, gb, x = (ref[...] for ref in group)
        seg = seg_ref[...]
        inv = 1.0 / HD_B
        mu = _seg_sum(y, seg) * inv
        dlt = y - mu
        var = _seg_sum(dlt * dlt, seg) * inv
        yn = dlt * lax.rsqrt(var + RWKV_GN_EPS) * lnw_ref[...] + lnb_ref[...]
        ob = (yn + bonus) * g
        merged = _sigmoid(ga) * _mm(oa, wa_ref[...]) + _sigmoid(gb) * _mm(ob, wb_ref[...])
        x1 = x + _mm(merged, wo_ref[...])
        x1_o[...] = x1
        xn = x1 * lax.rsqrt(jnp.mean(x1 * x1, axis=-1, keepdims=True) + EPS) * n2_ref[...]
        _tok_store(xn_o, xn)
        logits = _mm(rwt_ref[...], xn, NT, 3) + rb_ref[...]
        tm = x1.shape[0]
        eio = _iota((N_EXPERTS, tm), 0)
        idx_rows, val_rows = [], []
        cnt = jnp.zeros((N_EXPERTS, 1), F32)
        for _ in range(TOP_K):
            mx = jnp.max(logits, axis=0, keepdims=True)
            pick = jnp.min(jnp.where(logits == mx, eio, N_EXPERTS), axis=0, keepdims=True)
            hit = eio == pick
            cnt = cnt + jnp.sum(hit.astype(F32), axis=1, keepdims=True)
            logits = jnp.where(hit, -jnp.inf, logits)
            idx_rows.append(pick)
            val_rows.append(mx)
        ex = [jnp.exp(vr - val_rows[0]) for vr in val_rows]
        den = ex[0] + ex[1] + ex[2] + ex[3]
        idx_o[...] = _stack_rows(idx_rows)
        gate_o[...] = _stack_rows([e / den for e in ex])
        cnt_o[...] += jnp.broadcast_to(cnt, cnt_o.shape)

    pl.when(jnp.logical_not(from_b))(lambda: body(refs[0:7]))
    pl.when(from_b)(lambda: body(refs[7:14]))


def _merge(tok_a, tok_b, p, *, tm):
    n_a, d = tok_a[5].shape
    n_b = tok_b[5].shape[0]
    n = n_a + n_b
    blocks_a = n_a // tm
    last_a = blocks_a - 1
    ia = lambda i: jnp.minimum(i, last_a)
    ib = lambda i: jnp.maximum(i - blocks_a, 0)

    def group_specs(tok, im):
        y, _, _, oa, gates, _ = tok
        dm = gates.shape[1] // 2
        row = lambda c: pl.BlockSpec((tm, c), lambda i: (im(i), 0))
        return [row(y.shape[1]), row(y.shape[1]), row(y.shape[1]), row(oa.shape[1]),
                pl.BlockSpec((tm, dm), lambda i: (im(i), 0)), pl.BlockSpec((tm, dm), lambda i: (im(i), 1)),
                row(d)]

    def group_args(tok):
        y, bonus, g, oa, gates, x = tok
        return [y, bonus, g, oa, gates, gates, x]

    full = lambda a: pl.BlockSpec(a.shape, lambda i: (0, 0))
    small = [p["ln_w"], p["ln_b"], p["seg"], p["wa"], p["wb"], p["wo"], p["n2"], p["rwt"], p["rb"]]
    out_row = pl.BlockSpec((tm, d), lambda i: (i, 0))
    out_col = pl.BlockSpec((SUBLANES, tm), lambda i: (0, i))
    return pl.pallas_call(
        functools.partial(_merge_kernel, blocks_a=blocks_a),
        out_shape=(jax.ShapeDtypeStruct((n, d), F32), jax.ShapeDtypeStruct((n * SUBLANES, LANES), F32),
                   jax.ShapeDtypeStruct((SUBLANES, n), jnp.int32),
                   jax.ShapeDtypeStruct((SUBLANES, n), F32),
                   jax.ShapeDtypeStruct((N_EXPERTS, LANES), F32)),
        grid=(n // tm,),
        in_specs=group_specs(tok_a, ia) + group_specs(tok_b, ib) + [full(a) for a in small],
        out_specs=(out_row, pl.BlockSpec((tm * SUBLANES, LANES), lambda i: (i, 0)), out_col, out_col,
                   pl.BlockSpec((N_EXPERTS, LANES), lambda i: (0, 0))),
        compiler_params=_params(("arbitrary",)),
        name="merge_router",
    )(*(group_args(tok_a) + group_args(tok_b) + small))


def _rank_kernel(idx_ref, pstart_ref, dest_o, carry):
    i = pl.program_id(0)

    @pl.when(i == 0)
    def _():
        carry[...] = pstart_ref[...]

    tm = idx_ref.shape[1]
    eio = _iota((N_EXPERTS, tm), 0)
    idx = idx_ref[...]
    hits = [eio == idx[j:j + 1, :] for j in range(TOP_K)]
    sel = hits[0] | hits[1] | hits[2] | hits[3]
    upper = (_iota((tm, tm), 0) < _iota((tm, tm), 1)).astype(BF16)
    rank = _dg(sel.astype(BF16), upper, NN) + carry[:, 0:1]
    rows = [jnp.sum(jnp.where(h, rank, 0.0), axis=0, keepdims=True) for h in hits]
    dest_o[...] = _stack_rows(rows).astype(jnp.int32)
    carry[...] += jnp.sum(sel.astype(F32), axis=1, keepdims=True)


def _rank(idx4, pstart, *, tm):
    n = idx4.shape[1]
    return pl.pallas_call(
        _rank_kernel,
        out_shape=jax.ShapeDtypeStruct((SUBLANES, n), jnp.int32),
        grid=(n // tm,),
        in_specs=[pl.BlockSpec((SUBLANES, tm), lambda i: (0, i)),
                  pl.BlockSpec((N_EXPERTS, LANES), lambda i: (0, 0))],
        out_specs=pl.BlockSpec((SUBLANES, tm), lambda i: (0, i)),
        scratch_shapes=[pltpu.VMEM((N_EXPERTS, LANES), F32)],
        compiler_params=_params(("arbitrary",)),
        name="moe_rank",
    )(idx4, pstart)


def _scatter_kernel(pad_lo, pad_hi, na_ref, dest_ref, x_ref, xs_hbm, zeros, sem, zsem):
    tm = x_ref.shape[0]
    zrows = zeros.shape[0]
    nblk = xs_hbm.shape[0] // zrows

    @pl.when(pl.program_id(0) == 0)
    def _():
        zeros[...] = jnp.zeros_like(zeros)
        zrun = lambda r, size: pltpu.make_async_copy(zeros.at[pl.ds(0, size)], xs_hbm.at[pl.ds(r, size)], zsem)

        def per_expert(op):
            def f(e, c):
                lo = pad_lo[e]
                n_pad = pad_hi[e] - lo
                for bit in reversed(range((zrows - 1).bit_length())):
                    @pl.when(((n_pad >> bit) & 1) == 1)
                    def _(lo=lo, bit=bit):
                        op(zrun(lo, 1 << bit))
                    lo = lo + (n_pad & (1 << bit))
                return c
            return f

        lax.fori_loop(0, N_EXPERTS, per_expert(lambda cp: cp.start()), 0)
        lax.fori_loop(na_ref[0], nblk, lambda b, c: (zrun(b * zrows, zrows).start(), c)[1], 0)
        lax.fori_loop(0, N_EXPERTS, per_expert(lambda cp: cp.wait()), 0)
        lax.fori_loop(na_ref[0], nblk, lambda b, c: (zrun(b * zrows, zrows).wait(), c)[1], 0)

    def issue(t, c):
        for j in range(TOP_K):
            pltpu.make_async_copy(x_ref.at[t], xs_hbm.at[dest_ref[j, t]], sem).start(priority=j % 2)
        return c

    lax.fori_loop(0, tm, issue, 0, unroll=DMA_UNROLL)
    for j in range(TOP_K):
        pltpu.make_async_copy(x_ref, xs_hbm.at[pl.ds(0, tm)], sem).wait()


def _scatter(pad_lo, pad_hi, n_active, dest4, x3, m_total, *, tm):
    n = x3.shape[0]
    tail = x3.shape[1:]
    return pl.pallas_call(
        _scatter_kernel,
        out_shape=jax.ShapeDtypeStruct((m_total,) + tail, x3.dtype),
        grid_spec=pltpu.PrefetchScalarGridSpec(
            num_scalar_prefetch=3,
            grid=(n // tm,),
            in_specs=[pl.BlockSpec((SUBLANES, tm), lambda i, *_: (0, i), memory_space=pltpu.SMEM),
                      pl.BlockSpec((tm,) + tail, lambda i, *_: (i, 0, 0))],
            out_specs=pl.BlockSpec(memory_space=pl.ANY),
            scratch_shapes=[pltpu.VMEM((MOE_ROWS,) + tail, x3.dtype),
                            pltpu.SemaphoreType.DMA, pltpu.SemaphoreType.DMA]),
        compiler_params=_params(("arbitrary",)),
        name="moe_scatter",
    )(pad_lo, pad_hi, n_active, dest4, x3)


def _expert_kernel(be_ref, na_ref, x_ref, wgu_ref, bgu_ref, wd_ref, bd_ref, y_ref, wgu_bf, wd_perm, wd_bf):
    b = pl.program_id(0)
    active = b < na_ref[0]
    half = LANES // 2
    dff = wd_ref.shape[1]

    @pl.when(active & ((b == 0) | (be_ref[b] != be_ref[jnp.maximum(b - 1, 0)])))
    def _():
        wgu_bf[...] = wgu_ref[0].astype(BF16)
        for c in range(wd_ref.shape[2] // LANES):
            cols = slice(c * LANES, (c + 1) * LANES)
            for p in range(dff // LANES):
                r0 = p * LANES
                wd_perm[pl.ds(r0, half, stride=2), :] = wd_ref[0, r0:r0 + half, cols]
                wd_perm[pl.ds(r0 + 1, half, stride=2), :] = wd_ref[0, r0 + half:r0 + LANES, cols]
            wd_bf[:, cols] = wd_perm[...].astype(BF16)

    @pl.when(active)
    def _():
        x = _tok_load(x_ref).astype(BF16)
        h = _dg(x, wgu_bf[...], NN) + bgu_ref[0]
        even = (_iota((1, LANES), 1) & 1) == 0
        acts = []
        for p in range(dff // LANES):
            ha = h[:, 2 * p * LANES:(2 * p + 1) * LANES]
            hb = h[:, (2 * p + 1) * LANES:(2 * p + 2) * LANES]
            hg = jnp.where(even, ha, pltpu.roll(hb, 1, 1))
            hl = jnp.where(even, pltpu.roll(ha, LANES - 1, 1), hb)
            glu = jnp.minimum(hg, SWIGLU_LIMIT)
            lin = jnp.clip(hl, -SWIGLU_LIMIT, SWIGLU_LIMIT)
            acts.append((glu * _sigmoid(SWIGLU_ALPHA * glu) * (lin + 1.0)).astype(BF16))
        act = jnp.concatenate(acts, axis=1)
        _tok_store(y_ref, _dg(act, wd_bf[...], NN) + bd_ref[0])

    @pl.when(jnp.logical_not(active))
    def _():
        y_ref[...] = jnp.zeros_like(y_ref)


def _experts(block_expert, n_active, xs2, w_gu, b_gu, w_down, b_down):
    d = w_gu.shape[1]
    dff = w_down.shape[1]
    per_tok = d // LANES
    m_total = xs2.shape[0] // per_tok
    nblk = m_total // MOE_ROWS
    wspec = lambda r, c: pl.BlockSpec((1, r, c), lambda b, be, na: (be[b], 0, 0))
    tok = pl.BlockSpec((MOE_ROWS * per_tok, LANES), lambda b, be, na: (b, 0))
    return pl.pallas_call(
        _expert_kernel,
        out_shape=jax.ShapeDtypeStruct(xs2.shape, F32),
        grid_spec=pltpu.PrefetchScalarGridSpec(
            num_scalar_prefetch=2,
            grid=(nblk,),
            in_specs=[tok, wspec(d, 2 * dff), wspec(1, 2 * dff), wspec(dff, d), wspec(1, d)],
            out_specs=tok,
            scratch_shapes=[pltpu.VMEM((d, 2 * dff), BF16), pltpu.VMEM((dff, LANES), F32),
                            pltpu.VMEM((dff, d), BF16)]),
        compiler_params=_params(("arbitrary",)),
        name="moe_experts",
    )(block_expert, n_active, xs2, w_gu, b_gu, w_down, b_down)


def _combine_kernel(dest_ref, gate_ref, ys_hbm, x_ref, w_ref, oa_ref, ob_ref, buf, mixed, sems, *, blocks_a):
    tm = x_ref.shape[0]
    half = tm // 2

    def issue(h):
        def f(t, c):
            for j in range(TOP_K):
                pltpu.make_async_copy(ys_hbm.at[dest_ref[j, t]], buf.at[j, t], sems.at[h]).start(priority=j % 2)
            return c
        return f

    def mix(t, c):
        acc = gate_ref[0, t] * buf[0, t]
        for j in range(1, TOP_K):
            acc = acc + gate_ref[j, t] * buf[j, t]
        mixed[pl.ds(pl.multiple_of(t * SUBLANES, SUBLANES), SUBLANES), :] = acc
        return c

    for h in range(2):
        lax.fori_loop(h * half, (h + 1) * half, issue(h), 0, unroll=DMA_UNROLL)
    for h in range(2):
        rows = pl.ds(h * half, half)
        for j in range(TOP_K):
            pltpu.make_async_copy(ys_hbm.at[pl.ds(0, half)], buf.at[j, rows], sems.at[h]).wait()
        lax.fori_loop(h * half, (h + 1) * half, mix, 0, unroll=DMA_UNROLL)

    x = x_ref[...] + _tok_load(mixed)
    y = x * lax.rsqrt(jnp.mean(x * x, axis=-1, keepdims=True) + EPS) * w_ref[...]
    first = pl.program_id(0) < blocks_a

    @pl.when(first)
    def _():
        oa_ref[...] = y

    @pl.when(jnp.logical_not(first))
    def _():
        ob_ref[...] = y


def _combine(dest4, gate4, ys3, x1, w, n_a, *, tm):
    n, d = x1.shape
    tail = ys3.shape[1:]
    blocks_a = n_a // tm
    smem = lambda: pl.BlockSpec((SUBLANES, tm), lambda i: (0, i), memory_space=pltpu.SMEM)
    return pl.pallas_call(
        functools.partial(_combine_kernel, blocks_a=blocks_a),
        out_shape=(jax.ShapeDtypeStruct((n_a, d), F32), jax.ShapeDtypeStruct((n - n_a, d), F32)),
        grid=(n // tm,),
        in_specs=[smem(), smem(), pl.BlockSpec(memory_space=pl.ANY),
                  pl.BlockSpec((tm, d), lambda i: (i, 0)), pl.BlockSpec((1, d), lambda i: (0, 0))],
        out_specs=(pl.BlockSpec((tm, d), lambda i: (jnp.minimum(i, blocks_a - 1), 0)),
                   pl.BlockSpec((tm, d), lambda i: (jnp.maximum(i - blocks_a, 0), 0))),
        scratch_shapes=[pltpu.VMEM((TOP_K, tm) + tail, F32), pltpu.VMEM((tm * SUBLANES, LANES), F32),
                        pltpu.SemaphoreType.DMA((2,))],
        compiler_params=_params(("arbitrary",)),
        name="moe_combine",
    )(dest4, gate4, ys3, x1, w)


def _mixers(x, s_hgrn, s_rwkv, shift_prev, wts):
    batch, seq, d = x.shape
    n = batch * seq
    x2 = x.reshape(n, d)
    tm = min(512, n)
    seqs_per_step = LONG_SEQS_PER_STEP if seq >= HGRN_CHUNK else SHORT_SEQS_PER_STEP
    ph = _norm_proj(x2, wts["n1"], wts["w_h"], normalize=True, tm=tm)
    pg = _norm_proj(x2, wts["n1"], wts["w_g"], normalize=True, tm=tm)
    oa, s_hgrn_new = _hgrn(ph.reshape(batch, seq, -1), wts["lb"], wts["gw"], s_hgrn,
                           batch=batch, seq=seq, tt=256, sb=seqs_per_step)
    first = None
    if shift_prev is not None:
        prev = _norm_proj(shift_prev, wts["n1"], wts["w_rw"], normalize=False, tm=shift_prev.shape[0])
        first = jnp.repeat(prev, seq, axis=0)
    r, lw, k, v, ah, bh, bonus, g = _rwkv_prep(x2, first, wts, seq=seq, tm=min(256, n))
    to3 = lambda a: a.reshape(batch, seq, -1)
    y, s_rwkv_new = _rwkv_scan(to3(r), to3(lw), to3(k), to3(v), to3(ah), to3(bh), s_rwkv,
                               batch=batch, seq=seq, tt=128, passes=1, sb=seqs_per_step)
    return (y.reshape(n, -1), bonus, g, oa.reshape(n, -1), pg, x2), s_hgrn_new, s_rwkv_new


def kernel(x_prompt, x_sample, state_hgrn, state_rwkv, state_shift, norm1_w, w_in, hgrn_lb_logits,
           hgrn_gnorm_w, rwkv_mu, rwkv_w0, rwkv_w2, rwkv_a0, rwkv_a2, rwkv_g2, rwkv_k_k, rwkv_k_a,
           rwkv_r_k, rwkv_ln_w, rwkv_ln_b, w_branch_a, w_branch_b, w_out, norm2_w, router_w, router_b,
           expert_w_gu, expert_b_gu, expert_w_down, expert_b_down, final_norm_w):
    bp, tp, d = x_prompt.shape
    bs, ts, _ = x_sample.shape
    n_p, n_s = bp * tp, bs * ts
    n = n_p + n_s
    d_a = H_A * DK_A
    d_b = H_B * HD_B
    rw_start = 4 * d_a
    gate_start = rw_start + 3 * d_b + LORA_W + LORA_A + LORA_G
    row2 = lambda a: a.reshape(1, -1)
    lb = jnp.cumsum(jax.nn.softmax(hgrn_lb_logits.astype(F32), axis=0), axis=0)[0]
    hid = jnp.arange(RWKV_GROUP * HD_B, dtype=jnp.int32) // HD_B
    wts = {
        "n1": row2(norm1_w[0]),
        "w_h": w_in[0][:, :rw_start].astype(BF16),
        "w_rw": w_in[0][:, rw_start:gate_start].astype(BF16),
        "w_g": w_in[0][:, gate_start:].astype(BF16),
        "lb": row2(lb), "gw": row2(hgrn_gnorm_w[0]),
        "mu": row2(rwkv_mu[0]), "w0": row2(rwkv_w0[0]), "w2": rwkv_w2[0], "a0": row2(rwkv_a0[0]),
        "a2": rwkv_a2[0], "g2": rwkv_g2[0], "k_k": row2(rwkv_k_k[0]), "k_a": row2(rwkv_k_a[0]),
        "r_k": row2(rwkv_r_k[0]), "seg": (hid[:, None] == hid[None, :]).astype(BF16),
        "ln_w": row2(rwkv_ln_w[0]), "ln_b": row2(rwkv_ln_b[0]),
        "wa": w_branch_a[0].astype(BF16), "wb": w_branch_b[0].astype(BF16), "wo": w_out[0].astype(BF16),
        "n2": row2(norm2_w[0]), "rwt": router_w[0].T, "rb": router_b[0].reshape(-1, 1),
    }
    tok_p, hgrn_p, rwkv_p = _mixers(x_prompt, None, None, None, wts)
    tok_s, hgrn_s, rwkv_s = _mixers(x_sample, state_hgrn[0], state_rwkv[0], state_shift[0], wts)
    shift = _rms_rows(jnp.concatenate([x_prompt[:, -1], x_sample[:, -1]], axis=0), wts["n1"])

    x1, xn2, idx4, gate4, counts = _merge(tok_p, tok_s, wts, tm=256)

    cnt = counts[:, 0].astype(jnp.int32)
    padded = (cnt + MOE_ROWS - 1) // MOE_ROWS * MOE_ROWS
    pend = jnp.cumsum(padded)
    pstart = pend - padded
    nblk = (n * TOP_K + N_EXPERTS * (MOE_ROWS - 1) + MOE_ROWS - 1) // MOE_ROWS
    m_total = nblk * MOE_ROWS
    block_row = jnp.arange(nblk, dtype=jnp.int32)[:, None] * MOE_ROWS
    block_expert = jnp.minimum(jnp.sum((pend[None, :] <= block_row).astype(jnp.int32), axis=1), N_EXPERTS - 1)
    n_active = (pend[-1:] // MOE_ROWS).astype(jnp.int32)
    dest4 = _rank(idx4, jnp.broadcast_to(pstart.astype(F32)[:, None], (N_EXPERTS, LANES)), tm=512)

    per_tok = d // LANES
    xs = _scatter(pstart + cnt, pend, n_active, dest4, xn2.reshape(n, per_tok, LANES), m_total, tm=512)
    ys = _experts(block_expert, n_active, xs.reshape(m_total * per_tok, LANES), expert_w_gu[0],
                  expert_b_gu[0][:, None, :], expert_w_down[0], expert_b_down[0][:, None, :])
    y_p, y_s = _combine(dest4, gate4, ys.reshape(m_total, per_tok, LANES), x1, row2(final_norm_w), n_p, tm=512)

    return (y_p.reshape(bp, tp, d), y_s.reshape(bs, ts, d),
            hgrn_p[None], rwkv_p[None], shift[None, :bp],
            hgrn_s[None], rwkv_s[None], shift[None, bp:])
```

```python
import functools

import jax
import jax.numpy as jnp
from jax import lax
from jax.experimental import pallas as pl
from jax.experimental.pallas import tpu as pltpu

F32 = jnp.float32
BF16 = jnp.bfloat16

H_A = 4
DK_A = 128
H_B = 8
HD_B = 64
LORA_W = 64
LORA_A = 64
LORA_G = 128
N_EXPERTS = 32
TOP_K = 4
SWIGLU_LIMIT = 7.0
SWIGLU_ALPHA = 1.702
EPS = 1e-6
RWKV_GN_EPS = 64e-5

LANES = 128
SUBLANES = 8
HGRN_CHUNK = 64
HGRN_SUB = 8
RWKV_CHUNK = 64
RWKV_GROUP = 4
RWKV_BASE = 8
MOE_ROWS = 384
DMA_UNROLL = 4
LONG_SEQS_PER_STEP = 4
SHORT_SEQS_PER_STEP = 4
VMEM_LIMIT = 52 * 1024 * 1024

NN = (((1,), (0,)), ((), ()))
NT = (((1,), (1,)), ((), ()))
TN = (((0,), (0,)), ((), ()))


def _params(sem):
    return pltpu.CompilerParams(dimension_semantics=sem, vmem_limit_bytes=VMEM_LIMIT)


def _dg(a, b, dims):
    return lax.dot_general(a, b, dims, preferred_element_type=F32)


def _split2(x):
    hi = x.astype(BF16)
    lo = (x - hi.astype(F32)).astype(BF16)
    return hi, lo


def _mm(a, b, dims=NN, passes=1):
    if passes == 1:
        return _dg(a.astype(BF16), b.astype(BF16), dims)
    ah, al = _split2(a)
    bh, bl = _split2(b)
    return _dg(ah, bh, dims) + (_dg(ah, bl, dims) + _dg(al, bh, dims))


def _mm_exact_lhs(sel_bf16, x, dims=NN):
    hi = x.astype(BF16)
    r1 = x - hi.astype(F32)
    mid = r1.astype(BF16)
    lo = (r1 - mid.astype(F32)).astype(BF16)
    return _dg(sel_bf16, hi, dims) + (_dg(sel_bf16, mid, dims) + _dg(sel_bf16, lo, dims))


def _seg_sum(x, seg_bf16):
    w = seg_bf16.shape[0]
    hi, lo = _split2(x)
    parts = [_dg(hi[:, c:c + w], seg_bf16, NN) + _dg(lo[:, c:c + w], seg_bf16, NN)
             for c in range(0, x.shape[1], w)]
    return parts[0] if len(parts) == 1 else jnp.concatenate(parts, axis=1)


def _iota(shape, dim):
    return lax.broadcasted_iota(jnp.int32, shape, dim)


def _sigmoid(x):
    return 1.0 / (1.0 + jnp.exp(-x))


def _tok_load(ref):
    tm = ref.shape[0] // SUBLANES
    return jnp.concatenate([ref[pl.ds(c, tm, stride=SUBLANES), :] for c in range(SUBLANES)], axis=1)


def _tok_store(ref, val):
    tm = val.shape[0]
    for c in range(SUBLANES):
        ref[pl.ds(c, tm, stride=SUBLANES), :] = val[:, c * LANES:(c + 1) * LANES]


def _stack_rows(rows):
    n = rows[0].shape[1]
    sub = _iota((SUBLANES, n), 0)
    out = jnp.zeros((SUBLANES, n), rows[0].dtype)
    for j, r in enumerate(rows):
        out = jnp.where(sub == j, jnp.broadcast_to(r, (SUBLANES, n)), out)
    return out


def _pad_rows(x, rows):
    if x.shape[0] == rows:
        return x
    return jnp.concatenate([x, jnp.zeros((rows - x.shape[0], x.shape[1]), x.dtype)], axis=0)


def _norm_proj_kernel(x_ref, nw_ref, w_ref, o_ref, *, normalize):
    x = x_ref[...]
    if normalize:
        x = x * lax.rsqrt(jnp.mean(x * x, axis=-1, keepdims=True) + EPS) * nw_ref[...]
    o_ref[...] = _dg(x.astype(BF16), w_ref[...], NN)


def _norm_proj(x, nw, w_bf16, *, normalize, tm):
    n, d = x.shape
    nc = w_bf16.shape[1]
    return pl.pallas_call(
        functools.partial(_norm_proj_kernel, normalize=normalize),
        out_shape=jax.ShapeDtypeStruct((n, nc), F32),
        grid=(n // tm,),
        in_specs=[pl.BlockSpec((tm, d), lambda i: (i, 0)),
                  pl.BlockSpec((1, d), lambda i: (0, 0)),
                  pl.BlockSpec((d, nc), lambda i: (0, 0))],
        out_specs=pl.BlockSpec((tm, nc), lambda i: (i, 0)),
        compiler_params=_params(("arbitrary",)),
        name="norm_proj",
    )(x, nw, w_bf16)


def _rms_rows_kernel(x_ref, w_ref, o_ref):
    x = x_ref[...]
    o_ref[...] = x * lax.rsqrt(jnp.mean(x * x, axis=-1, keepdims=True) + EPS) * w_ref[...]


def _rms_rows(x, w):
    return pl.pallas_call(
        _rms_rows_kernel,
        out_shape=jax.ShapeDtypeStruct(x.shape, F32),
        name="rms_rows",
    )(x, w)


def _hgrn_chunks(hqs, hfs, his, lbvs, sts, *, rows, t_valid):
    n_u = len(hqs)
    fs = [lbv + (1.0 - lbv) * _sigmoid(hf) for lbv, hf in zip(lbvs, hfs)]
    gs = [jnp.log(f) for f in fs]
    ks = [1.0 - f for f in fs]
    qs = [hq * _sigmoid(hq) for hq in hqs]
    vs = list(his)
    if t_valid < rows:
        valid = _iota((rows, 1), 0) < t_valid
        gs, ks, qs, vs = ([jnp.where(valid, x, 0.0) for x in xs] for xs in (gs, ks, qs, vs))
    row = _iota((rows, rows), 0)
    col = _iota((rows, rows), 1)
    tril = (col <= row).astype(BF16)
    bs = [_mm_exact_lhs(tril, g) for g in gs]
    sub = min(HGRN_SUB, rows)
    m = rows // 2
    scores = [None] * n_u
    while m >= sub:
        nb = rows // m
        shift = m.bit_length() - 1
        rb = row >> shift
        pair = ((rb & 1) == 1) & ((col >> shift) == rb - 1)
        qes, kes = [], []
        for q, k, b in zip(qs, ks, bs):
            starts = [jnp.zeros((1, DK_A), F32) if j == 0 else b[j * m - 1:j * m] for j in range(nb)]
            ends = [b[j * m + m - 1:j * m + m] for j in range(nb)]
            b_start = jnp.concatenate([jnp.broadcast_to(s, (m, DK_A)) for s in starts], axis=0)
            b_end = jnp.concatenate([jnp.broadcast_to(s, (m, DK_A)) for s in ends], axis=0)
            qes.append(q * jnp.exp(b - b_start))
            kes.append(k * jnp.exp(b_end - b))
        a_ms = [jnp.where(pair, _mm(qe, ke, NT), 0.0) for qe, ke in zip(qes, kes)]
        scores = [a if s is None else s + a for s, a in zip(scores, a_ms)]
        m //= 2
    os_ = [jnp.zeros((rows, DK_A), F32)] * n_u
    if scores[0] is not None:
        os_ = [_mm(s, v, NN) for s, v in zip(scores, vs)]
    rloc = _iota((rows, 1), 0) & (sub - 1)
    for d in range(min(sub, t_valid)):
        sh = (lambda x: x) if d == 0 else (lambda x: pltpu.roll(x, d, 0))
        es = [jnp.exp(jnp.where(rloc >= d, b - sh(b), -jnp.inf)) for b in bs]
        os_ = [o + jnp.sum(q * sh(k) * e, axis=-1, keepdims=True) * sh(v)
               for o, q, k, v, e in zip(os_, qs, ks, vs, es)]
    b_lasts = [b[rows - 1:rows] for b in bs]
    kds = [k * jnp.exp(bl - b) for k, bl, b in zip(ks, b_lasts, bs)]
    qbs = [q * jnp.exp(b) for q, b in zip(qs, bs)]
    os_ = [o + _mm(qb, st, NT) for o, qb, st in zip(os_, qbs, sts)]
    st_new = [st * jnp.exp(bl) + _mm(v, kd, TN) for st, bl, v, kd in zip(sts, b_lasts, vs, kds)]
    return os_, st_new


def _hgrn_kernel(*refs, chunk, n_chunks, t_valid, has_s0, sb):
    if has_s0:
        ph_ref, lb_ref, gw_ref, s0_ref, o_ref, s_ref, st_scr = refs
    else:
        ph_ref, lb_ref, gw_ref, o_ref, s_ref, st_scr = refs
    ti = pl.program_id(1)
    d_a = H_A * DK_A
    units = [(s, h) for s in range(sb) for h in range(H_A)]

    @pl.when(ti == 0)
    def _():
        for s, h in units:
            st_scr[s, h] = s0_ref[s, h].T if has_s0 else jnp.zeros((DK_A, DK_A), F32)

    gw = gw_ref[...]

    def body(c, carry):
        if n_chunks == 1:
            sl = slice(None)
        else:
            sl = pl.ds(pl.multiple_of(c * chunk, chunk), chunk)
        cols = lambda part, h: slice(part * d_a + h * DK_A, part * d_a + (h + 1) * DK_A)
        part = lambda p: [_pad_rows(ph_ref[s, sl, cols(p, h)], chunk) for s, h in units]
        os_, st_new = _hgrn_chunks(part(0), part(1), part(2),
                                   [lb_ref[:, h * DK_A:(h + 1) * DK_A] for _, h in units],
                                   [st_scr[s, h] for s, h in units], rows=chunk, t_valid=t_valid)
        for (s, h), o, st in zip(units, os_, st_new):
            st_scr[s, h] = st
            hg = ph_ref[s, sl, cols(3, h)]
            o = o[:t_valid]
            o = o * lax.rsqrt(jnp.mean(o * o, axis=-1, keepdims=True) + EPS) * gw
            o_ref[s, sl, h * DK_A:(h + 1) * DK_A] = o * (hg * _sigmoid(hg))
        return carry

    if n_chunks == 1:
        body(0, 0)
    else:
        lax.fori_loop(0, n_chunks, body, 0)

    @pl.when(ti == pl.num_programs(1) - 1)
    def _():
        for s, h in units:
            s_ref[s, h] = st_scr[s, h].T


def _hgrn(ph, lb, gw, s0, *, batch, seq, tt, sb):
    has_s0 = s0 is not None
    chunk = min(HGRN_CHUNK, max(seq, SUBLANES))
    tt = min(tt, seq)
    n_chunks = max(tt // chunk, 1)
    t_valid = min(chunk, seq)
    d_a = H_A * DK_A
    st_spec = pl.BlockSpec((sb, H_A, DK_A, DK_A), lambda b, i: (b, 0, 0, 0))
    in_specs = [pl.BlockSpec((sb, tt, 4 * d_a), lambda b, i: (b, i, 0)),
                pl.BlockSpec((1, d_a), lambda b, i: (0, 0)),
                pl.BlockSpec((1, DK_A), lambda b, i: (0, 0))]
    args = [ph, lb, gw]
    if has_s0:
        in_specs.append(st_spec)
        args.append(s0)
    return pl.pallas_call(
        functools.partial(_hgrn_kernel, chunk=chunk, n_chunks=n_chunks, t_valid=t_valid, has_s0=has_s0, sb=sb),
        out_shape=(jax.ShapeDtypeStruct((batch, seq, d_a), F32),
                   jax.ShapeDtypeStruct((batch, H_A, DK_A, DK_A), F32)),
        grid=(batch // sb, seq // tt),
        in_specs=in_specs,
        out_specs=(pl.BlockSpec((sb, tt, d_a), lambda b, i: (b, i, 0)), st_spec),
        scratch_shapes=[pltpu.VMEM((sb, H_A, DK_A, DK_A), F32)],
        compiler_params=_params(("arbitrary", "arbitrary")),
        name="hgrn_scan",
    )(*args)


def _rwkv_prep_kernel(*refs, seq, has_first):
    if has_first:
        (x_ref, n1_ref, wrw_ref, first_ref, mu_ref, w0_ref, w2_ref, a0_ref, a2_ref, g2_ref, kk_ref, ka_ref,
         rk_ref, seg_ref, r_o, lw_o, k_o, v_o, ah_o, bh_o, bonus_o, g_o, carry) = refs
    else:
        (x_ref, n1_ref, wrw_ref, mu_ref, w0_ref, w2_ref, a0_ref, a2_ref, g2_ref, kk_ref, ka_ref,
         rk_ref, seg_ref, r_o, lw_o, k_o, v_o, ah_o, bh_o, bonus_o, g_o, carry) = refs
    d = r_o.shape[1]
    xin = x_ref[...]
    xin = xin * lax.rsqrt(jnp.mean(xin * xin, axis=-1, keepdims=True) + EPS) * n1_ref[...]
    x = _dg(xin.astype(BF16), wrw_ref[...], NN)
    tm = x.shape[0]
    i = pl.program_id(0)

    @pl.when(i == 0)
    def _():
        carry[...] = jnp.zeros_like(carry)

    rowi = _iota((tm, 1), 0)
    prev = pltpu.roll(x, 1, 0)
    prev = jnp.where(rowi == 0, carry[0:1, :], prev)
    seq_start = ((rowi + i * tm) & (seq - 1)) == 0
    if has_first:
        prev = jnp.where(seq_start, first_ref[...], prev)
    else:
        prev = jnp.where(seq_start, 0.0, prev)
    carry[0:1, :] = x[tm - 1:tm, :]
    x = x + (prev - x) * mu_ref[...]
    r = x[:, 0:d]
    k = x[:, d:2 * d]
    v = x[:, 2 * d:3 * d]
    wlo = x[:, 3 * d:3 * d + LORA_W]
    alo = x[:, 3 * d + LORA_W:3 * d + LORA_W + LORA_A]
    glo = x[:, 3 * d + LORA_W + LORA_A:]
    seg = seg_ref[...]
    z = -(w0_ref[...] + _mm(jnp.tanh(wlo), w2_ref[...]))
    softplus = jnp.maximum(z, 0.0) + jnp.log(1.0 + jnp.exp(-jnp.abs(z)))
    lw_o[...] = -jnp.exp(-softplus - 0.5)
    a = _sigmoid(a0_ref[...] + _mm(alo, a2_ref[...]))
    g_o[...] = _mm(_sigmoid(glo), g2_ref[...])
    kk = k * kk_ref[...]
    kk = kk / jnp.maximum(jnp.sqrt(_seg_sum(kk * kk, seg)), 1e-12)
    k2 = k * (1.0 + (a - 1.0) * ka_ref[...])
    r_o[...] = r
    k_o[...] = k2
    v_o[...] = v
    ah_o[...] = -kk
    bh_o[...] = kk * a
    bonus_o[...] = _seg_sum(r * k2 * rk_ref[...], seg) * v


def _rwkv_prep(x2, first, p, *, seq, tm):
    n, dm = x2.shape
    wcols = p["w_rw"].shape[1]
    d = H_B * HD_B
    has_first = first is not None
    row = lambda c: pl.BlockSpec((tm, c), lambda i: (i, 0))
    full = lambda a: pl.BlockSpec(a.shape, lambda i: (0, 0))
    small = [p["mu"], p["w0"], p["w2"], p["a0"], p["a2"], p["g2"], p["k_k"], p["k_a"], p["r_k"], p["seg"]]
    args = [x2, p["n1"], p["w_rw"]] + ([first] if has_first else []) + small
    in_specs = ([row(dm), full(p["n1"]), full(p["w_rw"])] + ([row(wcols)] if has_first else [])
                + [full(a) for a in small])
    return pl.pallas_call(
        functools.partial(_rwkv_prep_kernel, seq=seq, has_first=has_first),
        out_shape=tuple(jax.ShapeDtypeStruct((n, d), F32) for _ in range(8)),
        grid=(n // tm,),
        in_specs=in_specs,
        out_specs=tuple(row(d) for _ in range(8)),
        scratch_shapes=[pltpu.VMEM((SUBLANES, wcols), F32)],
        compiler_params=_params(("arbitrary",)),
        name="rwkv_prep",
    )(*args)


def _tri_inv(a_bds, n, rows, passes):
    mm = lambda xs, ys: [_mm(x, y, NN, passes) for x, y in zip(xs, ys)]
    row = _iota((n, n), 0)
    col = _iota((n, n), 1)
    t = row & (rows - 1)
    i = col & (rows - 1)
    eye = (row == col).astype(F32)
    bshift = RWKV_BASE.bit_length() - 1
    base = (t >> bshift) == (i >> bshift)
    d1 = [jnp.where(base, a, 0.0) for a in a_bds]
    n1 = [eye + d for d in d1]
    d2 = mm(d1, d1)
    n2 = [x + y for x, y in zip(n1, mm(n1, d2))]
    d4 = mm(d2, d2)
    tinv = [x + y for x, y in zip(n2, mm(n2, d4))]
    m = RWKV_BASE
    while m < rows:
        s = m.bit_length() - 1
        tb = t >> s
        ib = i >> s
        sib = ((tb >> 1) == (ib >> 1)) & ((tb & 1) == 1) & ((ib & 1) == 0)
        off = [jnp.where(sib, a, 0.0) for a in a_bds]
        nb = n // m

        def odd(x, m=m, nb=nb):
            return jnp.concatenate([x[j * m:(j + 1) * m] for j in range(1, nb, 2)], axis=0)

        def spread(y, m=m, nb=nb):
            zero = jnp.zeros((m, n), F32)
            return jnp.concatenate([p for j in range(nb // 2) for p in (zero, y[j * m:(j + 1) * m])], axis=0)

        upd = mm(mm([odd(x) for x in tinv], off), tinv)
        tinv = [x + spread(y) for x, y in zip(tinv, upd)]
        m *= 2
    return tinv


def _rwkv_chunks(ins, sts, *, rows, t_valid, passes):
    w = ins[0][0].shape[1]
    n = RWKV_GROUP * rows
    if t_valid < rows:
        valid = _iota((rows, 1), 0) < t_valid
        ins = [tuple(jnp.where(valid, x, 0.0) for x in u) for u in ins]
    r, lw, k, v, ah, bh = ([u[j] for u in ins] for j in range(6))
    lrow = _iota((rows, rows), 0)
    lcol = _iota((rows, rows), 1)
    tril = (lcol <= lrow).astype(BF16)
    hs = HD_B.bit_length() - 1
    rs = rows.bit_length() - 1
    bd_mask = (_iota((n, w), 0) >> rs) == (_iota((n, w), 1) >> hs)
    nmask = (_iota((n, n), 0) >> rs) == (_iota((n, n), 1) >> rs)
    t_idx = _iota((rows, 2 * n), 0)
    i_idx = _iota((rows, 2 * n), 1) & (rows - 1)
    strict = i_idx < t_idx
    incl = i_idx <= t_idx

    def expand(x):
        return jnp.where(bd_mask, jnp.concatenate([x] * RWKV_GROUP, axis=0), 0.0)

    cum = [_mm_exact_lhs(tril, x) for x in lw]
    gl = [c[rows - 1:rows] for c in cum]
    e_neg = [jnp.exp(-c) for c in cum]
    e_end = [jnp.exp(g - c) for g, c in zip(gl, cum)]
    qa = [a * jnp.exp(c - l) for a, c, l in zip(ah, cum, lw)]
    qr = [x * jnp.exp(c) for x, c in zip(r, cum)]
    ke = [jnp.concatenate([expand(b * e), expand(x * e)], axis=0) for b, x, e in zip(bh, k, e_neg)]
    kge = [jnp.concatenate([expand(b * e), expand(x * e)], axis=0) for b, x, e in zip(bh, k, e_end)]
    q2 = [jnp.concatenate([a, x], axis=0) for a, x in zip(qa, qr)]
    scores = [_mm(q, x, NT, passes) for q, x in zip(q2, ke)]
    a_a = [jnp.where(strict, s[:rows], 0.0) for s in scores]
    a_r = [jnp.where(incl, s[rows:], 0.0) for s in scores]
    v_bd = [expand(x) for x in v]
    w_kv = [_mm(a[:, n:], x, NN, passes) for a, x in zip(a_a, v_bd)]
    a_bd = [jnp.where(nmask, jnp.concatenate([a[:, :n]] * RWKV_GROUP, axis=0), 0.0) for a in a_a]
    tinv = _tri_inv(a_bd, n, rows, passes)
    carried = [_mm(q, st, NT, passes) for q, st in zip(q2, sts)]
    p1 = [x[:rows] for x in carried]
    y_carry = [x[rows:] for x in carried]
    c_bd = [_mm(t, expand(p + x), NN, passes) for t, p, x in zip(tinv, p1, w_kv)]
    cv = [jnp.concatenate([c, x], axis=0) for c, x in zip(c_bd, v_bd)]
    st_new = [st * jnp.exp(g) + _mm(c, x, TN, passes) for st, g, c, x in zip(sts, gl, cv, kge)]
    ys = [yc + _mm(a, c, NN, 1) for yc, a, c in zip(y_carry, a_r, cv)]
    return ys, st_new


def _rwkv_scan_kernel(*refs, chunk, n_chunks, t_valid, has_s0, passes, sb):
    if has_s0:
        r_ref, lw_ref, k_ref, v_ref, ah_ref, bh_ref, s0_ref, y_ref, s_ref, st_scr = refs
    else:
        r_ref, lw_ref, k_ref, v_ref, ah_ref, bh_ref, y_ref, s_ref, st_scr = refs
    ti = pl.program_id(1)
    w = RWKV_GROUP * HD_B
    groups = H_B // RWKV_GROUP
    units = [(s, g) for s in range(sb) for g in range(groups)]
    blk = lambda h: slice(h * HD_B, (h + 1) * HD_B)

    @pl.when(ti == 0)
    def _():
        st_scr[...] = jnp.zeros_like(st_scr)
        if has_s0:
            for s, g in units:
                for h in range(RWKV_GROUP):
                    st_scr[s, g, blk(h), blk(h)] = s0_ref[s, g * RWKV_GROUP + h]

    def body(c, carry):
        if n_chunks == 1:
            sl = slice(None)
        else:
            sl = pl.ds(pl.multiple_of(c * chunk, chunk), chunk)
        cols = lambda g: slice(g * w, (g + 1) * w)
        in_refs = (r_ref, lw_ref, k_ref, v_ref, ah_ref, bh_ref)
        ins = [tuple(_pad_rows(ref[s, sl, cols(g)], chunk) for ref in in_refs) for s, g in units]
        ys, st_new = _rwkv_chunks(ins, [st_scr[s, g] for s, g in units],
                                  rows=chunk, t_valid=t_valid, passes=passes)
        for (s, g), y, st in zip(units, ys, st_new):
            st_scr[s, g] = st
            y_ref[s, sl, cols(g)] = y[:t_valid]
        return carry

    if n_chunks == 1:
        body(0, 0)
    else:
        lax.fori_loop(0, n_chunks, body, 0)

    @pl.when(ti == pl.num_programs(1) - 1)
    def _():
        for s, g in units:
            for h in range(RWKV_GROUP):
                s_ref[s, g * RWKV_GROUP + h] = st_scr[s, g, blk(h), blk(h)]


def _rwkv_scan(r, lw, k, v, ah, bh, s0, *, batch, seq, tt, passes, sb):
    has_s0 = s0 is not None
    w = RWKV_GROUP * HD_B
    d_b = H_B * HD_B
    chunk = min(RWKV_CHUNK, max(seq, SUBLANES))
    tt = min(tt, seq)
    n_chunks = max(tt // chunk, 1)
    t_valid = min(chunk, seq)
    tok = pl.BlockSpec((sb, tt, d_b), lambda b, i: (b, i, 0))
    st_spec = pl.BlockSpec((sb, H_B, HD_B, HD_B), lambda b, i: (b, 0, 0, 0))
    args = [r, lw, k, v, ah, bh]
    in_specs = [tok] * 6
    if has_s0:
        args.append(s0)
        in_specs.append(st_spec)
    return pl.pallas_call(
        functools.partial(_rwkv_scan_kernel, chunk=chunk, n_chunks=n_chunks, t_valid=t_valid,
                          has_s0=has_s0, passes=passes, sb=sb),
        out_shape=(jax.ShapeDtypeStruct((batch, seq, d_b), F32),
                   jax.ShapeDtypeStruct((batch, H_B, HD_B, HD_B), F32)),
        grid=(batch // sb, seq // tt),
        in_specs=in_specs,
        out_specs=(tok, st_spec),
        scratch_shapes=[pltpu.VMEM((sb, H_B // RWKV_GROUP, w, w), F32)],
        compiler_params=_params(("arbitrary", "arbitrary")),
        name="rwkv_scan",
    )(*args)


def _merge_kernel(*refs, blocks_a):
    (lnw_ref, lnb_ref, seg_ref, wa_ref, wb_ref, wo_ref, n2_ref, rwt_ref, rb_ref,
     x1_o, xn_o, idx_o, gate_o, cnt_o) = refs[14:]
    from_b = pl.program_id(0) >= blocks_a

    @pl.when(pl.program_id(0) == 0)
    def _():
        cnt_o[...] = jnp.zeros_like(cnt_o)

    def body(group):
        y, bonus, g, oa, ga, gb, x = (ref[...] for ref in group)
        seg = seg_ref[...]
        inv = 1.0 / HD_B
        mu = _seg_sum(y, seg) * inv
        dlt = y - mu
        var = _seg_sum(dlt * dlt, seg) * inv
        yn = dlt * lax.rsqrt(var + RWKV_GN_EPS) * lnw_ref[...] + lnb_ref[...]
        ob = (yn + bonus) * g
        merged = _sigmoid(ga) * _mm(oa, wa_ref[...]) + _sigmoid(gb) * _mm(ob, wb_ref[...])
        x1 = x + _mm(merged, wo_ref[...])
        x1_o[...] = x1
        xn = x1 * lax.rsqrt(jnp.mean(x1 * x1, axis=-1, keepdims=True) + EPS) * n2_ref[...]
        _tok_store(xn_o, xn)
        logits = _mm(rwt_ref[...], xn, NT, 3) + rb_ref[...]
        tm = x1.shape[0]
        eio = _iota((N_EXPERTS, tm), 0)
        idx_rows, val_rows = [], []
        cnt = jnp.zeros((N_EXPERTS, 1), F32)
        for _ in range(TOP_K):
            mx = jnp.max(logits, axis=0, keepdims=True)
            pick = jnp.min(jnp.where(logits == mx, eio, N_EXPERTS), axis=0, keepdims=True)
            hit = eio == pick
            cnt = cnt + jnp.sum(hit.astype(F32), axis=1, keepdims=True)
            logits = jnp.where(hit, -jnp.inf, logits)
            idx_rows.append(pick)
            val_rows.append(mx)
        ex = [jnp.exp(vr - val_rows[0]) for vr in val_rows]
        den = ex[0] + ex[1] + ex[2] + ex[3]
        idx_o[...] = _stack_rows(idx_rows)
        gate_o[...] = _stack_rows([e / den for e in ex])
        cnt_o[...] += jnp.broadcast_to(cnt, cnt_o.shape)

    pl.when(jnp.logical_not(from_b))(lambda: body(refs[0:7]))
    pl.when(from_b)(lambda: body(refs[7:14]))


def _merge(tok_a, tok_b, p, *, tm):
    n_a, d = tok_a[5].shape
    n_b = tok_b[5].shape[0]
    n = n_a + n_b
    blocks_a = n_a // tm
    last_a = blocks_a - 1
    ia = lambda i: jnp.minimum(i, last_a)
    ib = lambda i: jnp.maximum(i - blocks_a, 0)

    def group_specs(tok, im, mode=None):
        y, _, _, oa, gates, _ = tok
        dm = gates.shape[1] // 2
        spec = lambda c, j: pl.BlockSpec((tm, c), lambda i: (im(i), j), pipeline_mode=mode)
        return [spec(y.shape[1], 0), spec(y.shape[1], 0), spec(y.shape[1], 0), spec(oa.shape[1], 0),
                spec(dm, 0), spec(dm, 1), spec(d, 0)]

    def group_args(tok):
        y, bonus, g, oa, gates, x = tok
        return [y, bonus, g, oa, gates, gates, x]

    full = lambda a: pl.BlockSpec(a.shape, lambda i: (0, 0))
    small = [p["ln_w"], p["ln_b"], p["seg"], p["wa"], p["wb"], p["wo"], p["n2"], p["rwt"], p["rb"]]
    out_row = pl.BlockSpec((tm, d), lambda i: (i, 0))
    out_col = pl.BlockSpec((SUBLANES, tm), lambda i: (0, i))
    return pl.pallas_call(
        functools.partial(_merge_kernel, blocks_a=blocks_a),
        out_shape=(jax.ShapeDtypeStruct((n, d), F32), jax.ShapeDtypeStruct((n * SUBLANES, LANES), F32),
                   jax.ShapeDtypeStruct((SUBLANES, n), jnp.int32),
                   jax.ShapeDtypeStruct((SUBLANES, n), F32),
                   jax.ShapeDtypeStruct((N_EXPERTS, LANES), F32)),
        grid=(n // tm,),
        in_specs=(group_specs(tok_a, ia) + group_specs(tok_b, ib, pl.Buffered(1))
                  + [full(a) for a in small]),
        out_specs=(out_row, pl.BlockSpec((tm * SUBLANES, LANES), lambda i: (i, 0)), out_col, out_col,
                   pl.BlockSpec((N_EXPERTS, LANES), lambda i: (0, 0))),
        compiler_params=_params(("arbitrary",)),
        name="merge_router",
    )(*(group_args(tok_a) + group_args(tok_b) + small))


def _rank_kernel(idx_ref, pstart_ref, dest_o, carry):
    i = pl.program_id(0)

    @pl.when(i == 0)
    def _():
        carry[...] = pstart_ref[...]

    tm = idx_ref.shape[1]
    eio = _iota((N_EXPERTS, tm), 0)
    idx = idx_ref[...]
    hits = [eio == idx[j:j + 1, :] for j in range(TOP_K)]
    sel = hits[0] | hits[1] | hits[2] | hits[3]
    upper = (_iota((tm, tm), 0) < _iota((tm, tm), 1)).astype(BF16)
    rank = _dg(sel.astype(BF16), upper, NN) + carry[:, 0:1]
    rows = [jnp.sum(jnp.where(h, rank, 0.0), axis=0, keepdims=True) for h in hits]
    dest_o[...] = _stack_rows(rows).astype(jnp.int32)
    carry[...] += jnp.sum(sel.astype(F32), axis=1, keepdims=True)


def _rank(idx4, pstart, *, tm):
    n = idx4.shape[1]
    return pl.pallas_call(
        _rank_kernel,
        out_shape=jax.ShapeDtypeStruct((SUBLANES, n), jnp.int32),
        grid=(n // tm,),
        in_specs=[pl.BlockSpec((SUBLANES, tm), lambda i: (0, i)),
                  pl.BlockSpec((N_EXPERTS, LANES), lambda i: (0, 0))],
        out_specs=pl.BlockSpec((SUBLANES, tm), lambda i: (0, i)),
        scratch_shapes=[pltpu.VMEM((N_EXPERTS, LANES), F32)],
        compiler_params=_params(("arbitrary",)),
        name="moe_rank",
    )(idx4, pstart)


def _scatter_kernel(pad_lo, pad_hi, na_ref, dest_ref, x_ref, xs_hbm, zeros, sem, zsem):
    tm = x_ref.shape[0]
    zrows = zeros.shape[0]
    nblk = xs_hbm.shape[0] // zrows

    @pl.when(pl.program_id(0) == 0)
    def _():
        zeros[...] = jnp.zeros_like(zeros)
        zrun = lambda r, size: pltpu.make_async_copy(zeros.at[pl.ds(0, size)], xs_hbm.at[pl.ds(r, size)], zsem)

        def per_expert(op):
            def f(e, c):
                lo = pad_lo[e]
                n_pad = pad_hi[e] - lo
                for bit in reversed(range((zrows - 1).bit_length())):
                    @pl.when(((n_pad >> bit) & 1) == 1)
                    def _(lo=lo, bit=bit):
                        op(zrun(lo, 1 << bit))
                    lo = lo + (n_pad & (1 << bit))
                return c
            return f

        lax.fori_loop(0, N_EXPERTS, per_expert(lambda cp: cp.start()), 0)
        lax.fori_loop(na_ref[0], nblk, lambda b, c: (zrun(b * zrows, zrows).start(), c)[1], 0)
        lax.fori_loop(0, N_EXPERTS, per_expert(lambda cp: cp.wait()), 0)
        lax.fori_loop(na_ref[0], nblk, lambda b, c: (zrun(b * zrows, zrows).wait(), c)[1], 0)

    def issue(t, c):
        for j in range(TOP_K):
            pltpu.make_async_copy(x_ref.at[t], xs_hbm.at[dest_ref[j, t]], sem).start(priority=j % 2)
        return c

    lax.fori_loop(0, tm, issue, 0, unroll=DMA_UNROLL)
    for j in range(TOP_K):
        pltpu.make_async_copy(x_ref, xs_hbm.at[pl.ds(0, tm)], sem).wait()


def _scatter(pad_lo, pad_hi, n_active, dest4, x3, m_total, *, tm):
    n = x3.shape[0]
    tail = x3.shape[1:]
    return pl.pallas_call(
        _scatter_kernel,
        out_shape=jax.ShapeDtypeStruct((m_total,) + tail, x3.dtype),
        grid_spec=pltpu.PrefetchScalarGridSpec(
            num_scalar_prefetch=3,
            grid=(n // tm,),
            in_specs=[pl.BlockSpec((SUBLANES, tm), lambda i, *_: (0, i), memory_space=pltpu.SMEM),
                      pl.BlockSpec((tm,) + tail, lambda i, *_: (i, 0, 0))],
            out_specs=pl.BlockSpec(memory_space=pl.ANY),
            scratch_shapes=[pltpu.VMEM((MOE_ROWS,) + tail, x3.dtype),
                            pltpu.SemaphoreType.DMA, pltpu.SemaphoreType.DMA]),
        compiler_params=_params(("arbitrary",)),
        name="moe_scatter",
    )(pad_lo, pad_hi, n_active, dest4, x3)


def _expert_kernel(be_ref, na_ref, x_ref, wgu_ref, bgu_ref, wd_ref, bd_ref, y_ref, wgu_bf, wd_perm, wd_bf):
    b = pl.program_id(0)
    active = b < na_ref[0]
    half = LANES // 2
    dff = wd_ref.shape[1]

    @pl.when(active & ((b == 0) | (be_ref[b] != be_ref[jnp.maximum(b - 1, 0)])))
    def _():
        wgu_bf[...] = wgu_ref[0].astype(BF16)
        for c in range(wd_ref.shape[2] // LANES):
            cols = slice(c * LANES, (c + 1) * LANES)
            for p in range(dff // LANES):
                r0 = p * LANES
                wd_perm[pl.ds(r0, half, stride=2), :] = wd_ref[0, r0:r0 + half, cols]
                wd_perm[pl.ds(r0 + 1, half, stride=2), :] = wd_ref[0, r0 + half:r0 + LANES, cols]
            wd_bf[:, cols] = wd_perm[...].astype(BF16)

    @pl.when(active)
    def _():
        x = _tok_load(x_ref).astype(BF16)
        h = _dg(x, wgu_bf[...], NN) + bgu_ref[0]
        even = (_iota((1, LANES), 1) & 1) == 0
        acts = []
        for p in range(dff // LANES):
            ha = h[:, 2 * p * LANES:(2 * p + 1) * LANES]
            hb = h[:, (2 * p + 1) * LANES:(2 * p + 2) * LANES]
            hg = jnp.where(even, ha, pltpu.roll(hb, 1, 1))
            hl = jnp.where(even, pltpu.roll(ha, LANES - 1, 1), hb)
            glu = jnp.minimum(hg, SWIGLU_LIMIT)
            lin = jnp.clip(hl, -SWIGLU_LIMIT, SWIGLU_LIMIT)
            acts.append((glu * _sigmoid(SWIGLU_ALPHA * glu) * (lin + 1.0)).astype(BF16))
        act = jnp.concatenate(acts, axis=1)
        _tok_store(y_ref, _dg(act, wd_bf[...], NN) + bd_ref[0])

    @pl.when(jnp.logical_not(active))
    def _():
        y_ref[...] = jnp.zeros_like(y_ref)


def _experts(block_expert, n_active, xs2, w_gu, b_gu, w_down, b_down):
    d = w_gu.shape[1]
    dff = w_down.shape[1]
    per_tok = d // LANES
    m_total = xs2.shape[0] // per_tok
    nblk = m_total // MOE_ROWS
    wspec = lambda r, c: pl.BlockSpec((1, r, c), lambda b, be, na: (be[b], 0, 0))
    tok = pl.BlockSpec((MOE_ROWS * per_tok, LANES), lambda b, be, na: (b, 0))
    return pl.pallas_call(
        _expert_kernel,
        out_shape=jax.ShapeDtypeStruct(xs2.shape, F32),
        grid_spec=pltpu.PrefetchScalarGridSpec(
            num_scalar_prefetch=2,
            grid=(nblk,),
            in_specs=[tok, wspec(d, 2 * dff), wspec(1, 2 * dff), wspec(dff, d), wspec(1, d)],
            out_specs=tok,
            scratch_shapes=[pltpu.VMEM((d, 2 * dff), BF16), pltpu.VMEM((dff, LANES), F32),
                            pltpu.VMEM((dff, d), BF16)]),
        compiler_params=_params(("arbitrary",)),
        name="moe_experts",
    )(block_expert, n_active, xs2, w_gu, b_gu, w_down, b_down)


def _combine_kernel(dest_ref, next_ref, gate_ref, ys_hbm, x_ref, w_ref, oa_ref, ob_ref, buf, mixed, sems,
                    *, blocks_a):
    tm = x_ref.shape[0]
    i = pl.program_id(0)
    slot = i % 2

    def issue(idx_ref, s):
        def f(t, c):
            for j in range(TOP_K):
                pltpu.make_async_copy(ys_hbm.at[idx_ref[j, t]], buf.at[s, j, t], sems.at[s]).start(priority=j % 2)
            return c
        lax.fori_loop(0, tm, f, 0, unroll=DMA_UNROLL)

    @pl.when(i == 0)
    def _():
        issue(dest_ref, slot)

    @pl.when(i + 1 < pl.num_programs(0))
    def _():
        issue(next_ref, 1 - slot)

    for j in range(TOP_K):
        pltpu.make_async_copy(ys_hbm.at[pl.ds(0, tm)], buf.at[slot, j], sems.at[slot]).wait()

    def mix(t, c):
        acc = gate_ref[0, t] * buf[slot, 0, t]
        for j in range(1, TOP_K):
            acc = acc + gate_ref[j, t] * buf[slot, j, t]
        mixed[pl.ds(pl.multiple_of(t * SUBLANES, SUBLANES), SUBLANES), :] = acc
        return c

    lax.fori_loop(0, tm, mix, 0, unroll=DMA_UNROLL)
    x = x_ref[...] + _tok_load(mixed)
    y = x * lax.rsqrt(jnp.mean(x * x, axis=-1, keepdims=True) + EPS) * w_ref[...]
    first = pl.program_id(0) < blocks_a

    @pl.when(first)
    def _():
        oa_ref[...] = y

    @pl.when(jnp.logical_not(first))
    def _():
        ob_ref[...] = y


def _combine(dest4, gate4, ys3, x1, w, n_a, *, tm):
    n, d = x1.shape
    tail = ys3.shape[1:]
    blocks_a = n_a // tm
    last = n // tm - 1
    smem = lambda im: pl.BlockSpec((SUBLANES, tm), lambda i: (0, im(i)), memory_space=pltpu.SMEM)
    return pl.pallas_call(
        functools.partial(_combine_kernel, blocks_a=blocks_a),
        out_shape=(jax.ShapeDtypeStruct((n_a, d), F32), jax.ShapeDtypeStruct((n - n_a, d), F32)),
        grid=(n // tm,),
        in_specs=[smem(lambda i: i), smem(lambda i: jnp.minimum(i + 1, last)), smem(lambda i: i),
                  pl.BlockSpec(memory_space=pl.ANY),
                  pl.BlockSpec((tm, d), lambda i: (i, 0)), pl.BlockSpec((1, d), lambda i: (0, 0))],
        out_specs=(pl.BlockSpec((tm, d), lambda i: (jnp.minimum(i, blocks_a - 1), 0)),
                   pl.BlockSpec((tm, d), lambda i: (jnp.maximum(i - blocks_a, 0), 0))),
        scratch_shapes=[pltpu.VMEM((2, TOP_K, tm) + tail, F32), pltpu.VMEM((tm * SUBLANES, LANES), F32),
                        pltpu.SemaphoreType.DMA((2,))],
        compiler_params=_params(("arbitrary",)),
        name="moe_combine",
    )(dest4, dest4, gate4, ys3, x1, w)


def _mixers(x, s_hgrn, s_rwkv, shift_prev, wts):
    batch, seq, d = x.shape
    n = batch * seq
    x2 = x.reshape(n, d)
    tm = min(512, n)
    seqs_per_step = LONG_SEQS_PER_STEP if seq >= HGRN_CHUNK else SHORT_SEQS_PER_STEP
    ph = _norm_proj(x2, wts["n1"], wts["w_h"], normalize=True, tm=tm)
    pg = _norm_proj(x2, wts["n1"], wts["w_g"], normalize=True, tm=tm)
    oa, s_hgrn_new = _hgrn(ph.reshape(batch, seq, -1), wts["lb"], wts["gw"], s_hgrn,
                           batch=batch, seq=seq, tt=256, sb=seqs_per_step)
    first = None
    if shift_prev is not None:
        prev = _norm_proj(shift_prev, wts["n1"], wts["w_rw"], normalize=False, tm=shift_prev.shape[0])
        first = jnp.repeat(prev, seq, axis=0)
    r, lw, k, v, ah, bh, bonus, g = _rwkv_prep(x2, first, wts, seq=seq, tm=min(512, n))
    to3 = lambda a: a.reshape(batch, seq, -1)
    y, s_rwkv_new = _rwkv_scan(to3(r), to3(lw), to3(k), to3(v), to3(ah), to3(bh), s_rwkv,
                               batch=batch, seq=seq, tt=128, passes=1, sb=seqs_per_step)
    return (y.reshape(n, -1), bonus, g, oa.reshape(n, -1), pg, x2), s_hgrn_new, s_rwkv_new


def kernel(x_prompt, x_sample, state_hgrn, state_rwkv, state_shift, norm1_w, w_in, hgrn_lb_logits,
           hgrn_gnorm_w, rwkv_mu, rwkv_w0, rwkv_w2, rwkv_a0, rwkv_a2, rwkv_g2, rwkv_k_k, rwkv_k_a,
           rwkv_r_k, rwkv_ln_w, rwkv_ln_b, w_branch_a, w_branch_b, w_out, norm2_w, router_w, router_b,
           expert_w_gu, expert_b_gu, expert_w_down, expert_b_down, final_norm_w):
    bp, tp, d = x_prompt.shape
    bs, ts, _ = x_sample.shape
    n_p, n_s = bp * tp, bs * ts
    n = n_p + n_s
    d_a = H_A * DK_A
    d_b = H_B * HD_B
    rw_start = 4 * d_a
    gate_start = rw_start + 3 * d_b + LORA_W + LORA_A + LORA_G
    row2 = lambda a: a.reshape(1, -1)
    lb = jnp.cumsum(jax.nn.softmax(hgrn_lb_logits.astype(F32), axis=0), axis=0)[0]
    hid = jnp.arange(RWKV_GROUP * HD_B, dtype=jnp.int32) // HD_B
    wts = {
        "n1": row2(norm1_w[0]),
        "w_h": w_in[0][:, :rw_start].astype(BF16),
        "w_rw": w_in[0][:, rw_start:gate_start].astype(BF16),
        "w_g": w_in[0][:, gate_start:].astype(BF16),
        "lb": row2(lb), "gw": row2(hgrn_gnorm_w[0]),
        "mu": row2(rwkv_mu[0]), "w0": row2(rwkv_w0[0]), "w2": rwkv_w2[0], "a0": row2(rwkv_a0[0]),
        "a2": rwkv_a2[0], "g2": rwkv_g2[0], "k_k": row2(rwkv_k_k[0]), "k_a": row2(rwkv_k_a[0]),
        "r_k": row2(rwkv_r_k[0]), "seg": (hid[:, None] == hid[None, :]).astype(BF16),
        "ln_w": row2(rwkv_ln_w[0]), "ln_b": row2(rwkv_ln_b[0]),
        "wa": w_branch_a[0].astype(BF16), "wb": w_branch_b[0].astype(BF16), "wo": w_out[0].astype(BF16),
        "n2": row2(norm2_w[0]), "rwt": router_w[0].T, "rb": router_b[0].reshape(-1, 1),
    }
    tok_p, hgrn_p, rwkv_p = _mixers(x_prompt, None, None, None, wts)
    tok_s, hgrn_s, rwkv_s = _mixers(x_sample, state_hgrn[0], state_rwkv[0], state_shift[0], wts)
    shift = _rms_rows(jnp.concatenate([x_prompt[:, -1], x_sample[:, -1]], axis=0), wts["n1"])

    x1, xn2, idx4, gate4, counts = _merge(tok_p, tok_s, wts, tm=512)

    cnt = counts[:, 0].astype(jnp.int32)
    padded = (cnt + MOE_ROWS - 1) // MOE_ROWS * MOE_ROWS
    pend = jnp.cumsum(padded)
    pstart = pend - padded
    nblk = (n * TOP_K + N_EXPERTS * (MOE_ROWS - 1) + MOE_ROWS - 1) // MOE_ROWS
    m_total = nblk * MOE_ROWS
    block_row = jnp.arange(nblk, dtype=jnp.int32)[:, None] * MOE_ROWS
    block_expert = jnp.minimum(jnp.sum((pend[None, :] <= block_row).astype(jnp.int32), axis=1), N_EXPERTS - 1)
    n_active = (pend[-1:] // MOE_ROWS).astype(jnp.int32)
    dest4 = _rank(idx4, jnp.broadcast_to(pstart.astype(F32)[:, None], (N_EXPERTS, LANES)), tm=512)

    per_tok = d // LANES
    xs = _scatter(pstart + cnt, pend, n_active, dest4, xn2.reshape(n, per_tok, LANES), m_total, tm=512)
    ys = _experts(block_expert, n_active, xs.reshape(m_total * per_tok, LANES), expert_w_gu[0],
                  expert_b_gu[0][:, None, :], expert_w_down[0], expert_b_down[0][:, None, :])
    y_p, y_s = _combine(dest4, gate4, ys.reshape(m_total, per_tok, LANES), x1, row2(final_norm_w), n_p, tm=512)

    return (y_p.reshape(bp, tp, d), y_s.reshape(bs, ts, d),
            hgrn_p[None], rwkv_p[None], shift[None, :bp],
            hgrn_s[None], rwkv_s[None], shift[None, bp:])
```

```python
import functools

import jax
import jax.numpy as jnp
from jax import lax
from jax.experimental import pallas as pl
from jax.experimental.pallas import tpu as pltpu

F32 = jnp.float32
BF16 = jnp.bfloat16

H_A = 4
DK_A = 128
H_B = 8
HD_B = 64
LORA_W = 64
LORA_A = 64
LORA_G = 128
N_EXPERTS = 32
TOP_K = 4
SWIGLU_LIMIT = 7.0
SWIGLU_ALPHA = 1.702
EPS = 1e-6
RWKV_GN_EPS = 64e-5

LANES = 128
SUBLANES = 8
HGRN_CHUNK = 64
HGRN_SUB = 8
RWKV_CHUNK = 64
RWKV_GROUP = 4
RWKV_BASE = 8
MOE_ROWS = 384
DMA_UNROLL = 4
LONG_SEQS_PER_STEP = 4
SHORT_SEQS_PER_STEP = 4
VMEM_LIMIT = 52 * 1024 * 1024

NN = (((1,), (0,)), ((), ()))
NT = (((1,), (1,)), ((), ()))
TN = (((0,), (0,)), ((), ()))


def _params(sem):
    return pltpu.CompilerParams(dimension_semantics=sem, vmem_limit_bytes=VMEM_LIMIT)


def _dg(a, b, dims):
    return lax.dot_general(a, b, dims, preferred_element_type=F32)


def _split2(x):
    hi = x.astype(BF16)
    lo = (x - hi.astype(F32)).astype(BF16)
    return hi, lo


def _mm(a, b, dims=NN, passes=1):
    if passes == 1:
        return _dg(a.astype(BF16), b.astype(BF16), dims)
    ah, al = _split2(a)
    bh, bl = _split2(b)
    return _dg(ah, bh, dims) + (_dg(ah, bl, dims) + _dg(al, bh, dims))


def _mm_exact_lhs(sel_bf16, x, dims=NN):
    hi = x.astype(BF16)
    r1 = x - hi.astype(F32)
    mid = r1.astype(BF16)
    lo = (r1 - mid.astype(F32)).astype(BF16)
    return _dg(sel_bf16, hi, dims) + (_dg(sel_bf16, mid, dims) + _dg(sel_bf16, lo, dims))


def _seg_sum(x, seg_bf16):
    w = seg_bf16.shape[0]
    hi, lo = _split2(x)
    parts = [_dg(hi[:, c:c + w], seg_bf16, NN) + _dg(lo[:, c:c + w], seg_bf16, NN)
             for c in range(0, x.shape[1], w)]
    return parts[0] if len(parts) == 1 else jnp.concatenate(parts, axis=1)


def _iota(shape, dim):
    return lax.broadcasted_iota(jnp.int32, shape, dim)


def _sigmoid(x):
    return 1.0 / (1.0 + jnp.exp(-x))


def _tok_load(ref):
    tm = ref.shape[0] // SUBLANES
    return jnp.concatenate([ref[pl.ds(c, tm, stride=SUBLANES), :] for c in range(SUBLANES)], axis=1)


def _tok_store(ref, val):
    tm = val.shape[0]
    for c in range(SUBLANES):
        ref[pl.ds(c, tm, stride=SUBLANES), :] = val[:, c * LANES:(c + 1) * LANES]


def _stack_rows(rows):
    n = rows[0].shape[1]
    sub = _iota((SUBLANES, n), 0)
    out = jnp.zeros((SUBLANES, n), rows[0].dtype)
    for j, r in enumerate(rows):
        out = jnp.where(sub == j, jnp.broadcast_to(r, (SUBLANES, n)), out)
    return out


def _pad_rows(x, rows):
    if x.shape[0] == rows:
        return x
    return jnp.concatenate([x, jnp.zeros((rows - x.shape[0], x.shape[1]), x.dtype)], axis=0)


def _norm_proj_kernel(x_ref, nw_ref, w_ref, o_ref, *, normalize):
    x = x_ref[...]
    if normalize:
        x = x * lax.rsqrt(jnp.mean(x * x, axis=-1, keepdims=True) + EPS) * nw_ref[...]
    o_ref[...] = _dg(x.astype(BF16), w_ref[...], NN)


def _norm_proj(x, nw, w_bf16, *, normalize, tm):
    n, d = x.shape
    nc = w_bf16.shape[1]
    return pl.pallas_call(
        functools.partial(_norm_proj_kernel, normalize=normalize),
        out_shape=jax.ShapeDtypeStruct((n, nc), F32),
        grid=(n // tm,),
        in_specs=[pl.BlockSpec((tm, d), lambda i: (i, 0)),
                  pl.BlockSpec((1, d), lambda i: (0, 0)),
                  pl.BlockSpec((d, nc), lambda i: (0, 0))],
        out_specs=pl.BlockSpec((tm, nc), lambda i: (i, 0)),
        compiler_params=_params(("arbitrary",)),
        name="norm_proj",
    )(x, nw, w_bf16)


def _rms_rows_kernel(x_ref, w_ref, o_ref):
    x = x_ref[...]
    o_ref[...] = x * lax.rsqrt(jnp.mean(x * x, axis=-1, keepdims=True) + EPS) * w_ref[...]


def _rms_rows(x, w):
    return pl.pallas_call(
        _rms_rows_kernel,
        out_shape=jax.ShapeDtypeStruct(x.shape, F32),
        name="rms_rows",
    )(x, w)


def _hgrn_chunks(hqs, hfs, his, lbvs, sts, *, rows, t_valid):
    n_u = len(hqs)
    fs = [lbv + (1.0 - lbv) * _sigmoid(hf) for lbv, hf in zip(lbvs, hfs)]
    gs = [jnp.log(f) for f in fs]
    ks = [1.0 - f for f in fs]
    qs = [hq * _sigmoid(hq) for hq in hqs]
    vs = list(his)
    if t_valid < rows:
        valid = _iota((rows, 1), 0) < t_valid
        gs, ks, qs, vs = ([jnp.where(valid, x, 0.0) for x in xs] for xs in (gs, ks, qs, vs))
    row = _iota((rows, rows), 0)
    col = _iota((rows, rows), 1)
    tril = (col <= row).astype(BF16)
    bs = [_mm_exact_lhs(tril, g) for g in gs]
    sub = min(HGRN_SUB, rows)
    m = rows // 2
    scores = [None] * n_u
    while m >= sub:
        nb = rows // m
        shift = m.bit_length() - 1
        rb = row >> shift
        pair = ((rb & 1) == 1) & ((col >> shift) == rb - 1)
        qes, kes = [], []
        for q, k, b in zip(qs, ks, bs):
            starts = [jnp.zeros((1, DK_A), F32) if j == 0 else b[j * m - 1:j * m] for j in range(nb)]
            ends = [b[j * m + m - 1:j * m + m] for j in range(nb)]
            b_start = jnp.concatenate([jnp.broadcast_to(s, (m, DK_A)) for s in starts], axis=0)
            b_end = jnp.concatenate([jnp.broadcast_to(s, (m, DK_A)) for s in ends], axis=0)
            qes.append(q * jnp.exp(b - b_start))
            kes.append(k * jnp.exp(b_end - b))
        a_ms = [jnp.where(pair, _mm(qe, ke, NT), 0.0) for qe, ke in zip(qes, kes)]
        scores = [a if s is None else s + a for s, a in zip(scores, a_ms)]
        m //= 2
    os_ = [jnp.zeros((rows, DK_A), F32)] * n_u
    if scores[0] is not None:
        os_ = [_mm(s, v, NN) for s, v in zip(scores, vs)]
    rloc = _iota((rows, 1), 0) & (sub - 1)
    os_ = [o + jnp.sum(q * k, axis=-1, keepdims=True) * v for o, q, k, v in zip(os_, qs, ks, vs)]
    for d in range(1, min(sub, t_valid)):
        sh = lambda x: pltpu.roll(x, d, 0)
        es = [jnp.exp(jnp.where(rloc >= d, b - sh(b), -jnp.inf)) for b in bs]
        os_ = [o + jnp.sum(q * sh(k) * e, axis=-1, keepdims=True) * sh(v)
               for o, q, k, v, e in zip(os_, qs, ks, vs, es)]
    b_lasts = [b[rows - 1:rows] for b in bs]
    kds = [k * jnp.exp(bl - b) for k, bl, b in zip(ks, b_lasts, bs)]
    qbs = [q * jnp.exp(b) for q, b in zip(qs, bs)]
    os_ = [o + _mm(qb, st, NT) for o, qb, st in zip(os_, qbs, sts)]
    st_new = [st * jnp.exp(bl) + _mm(v, kd, TN) for st, bl, v, kd in zip(sts, b_lasts, vs, kds)]
    return os_, st_new


def _hgrn_kernel(*refs, chunk, n_chunks, t_valid, has_s0, sb):
    if has_s0:
        ph_ref, lb_ref, gw_ref, s0_ref, o_ref, s_ref, st_scr = refs
    else:
        ph_ref, lb_ref, gw_ref, o_ref, s_ref, st_scr = refs
    ti = pl.program_id(1)
    d_a = H_A * DK_A
    units = [(s, h) for s in range(sb) for h in range(H_A)]

    @pl.when(ti == 0)
    def _():
        for s, h in units:
            st_scr[s, h] = s0_ref[s, h].T if has_s0 else jnp.zeros((DK_A, DK_A), F32)

    gw = gw_ref[...]

    def body(c, carry):
        if n_chunks == 1:
            sl = slice(None)
        else:
            sl = pl.ds(pl.multiple_of(c * chunk, chunk), chunk)
        cols = lambda part, h: slice(part * d_a + h * DK_A, part * d_a + (h + 1) * DK_A)
        part = lambda p: [_pad_rows(ph_ref[s, sl, cols(p, h)], chunk) for s, h in units]
        os_, st_new = _hgrn_chunks(part(0), part(1), part(2),
                                   [lb_ref[:, h * DK_A:(h + 1) * DK_A] for _, h in units],
                                   [st_scr[s, h] for s, h in units], rows=chunk, t_valid=t_valid)
        for (s, h), o, st in zip(units, os_, st_new):
            st_scr[s, h] = st
            hg = ph_ref[s, sl, cols(3, h)]
            o = o[:t_valid]
            o = o * lax.rsqrt(jnp.mean(o * o, axis=-1, keepdims=True) + EPS) * gw
            o_ref[s, sl, h * DK_A:(h + 1) * DK_A] = o * (hg * _sigmoid(hg))
        return carry

    if n_chunks == 1:
        body(0, 0)
    else:
        lax.fori_loop(0, n_chunks, body, 0)

    @pl.when(ti == pl.num_programs(1) - 1)
    def _():
        for s, h in units:
            s_ref[s, h] = st_scr[s, h].T


def _hgrn(ph, lb, gw, s0, *, batch, seq, tt, sb):
    has_s0 = s0 is not None
    chunk = min(HGRN_CHUNK, max(seq, SUBLANES))
    tt = min(tt, seq)
    n_chunks = max(tt // chunk, 1)
    t_valid = min(chunk, seq)
    d_a = H_A * DK_A
    st_spec = pl.BlockSpec((sb, H_A, DK_A, DK_A), lambda b, i: (b, 0, 0, 0))
    in_specs = [pl.BlockSpec((sb, tt, 4 * d_a), lambda b, i: (b, i, 0)),
                pl.BlockSpec((1, d_a), lambda b, i: (0, 0)),
                pl.BlockSpec((1, DK_A), lambda b, i: (0, 0))]
    args = [ph, lb, gw]
    if has_s0:
        in_specs.append(st_spec)
        args.append(s0)
    return pl.pallas_call(
        functools.partial(_hgrn_kernel, chunk=chunk, n_chunks=n_chunks, t_valid=t_valid, has_s0=has_s0, sb=sb),
        out_shape=(jax.ShapeDtypeStruct((batch, seq, d_a), F32),
                   jax.ShapeDtypeStruct((batch, H_A, DK_A, DK_A), F32)),
        grid=(batch // sb, seq // tt),
        in_specs=in_specs,
        out_specs=(pl.BlockSpec((sb, tt, d_a), lambda b, i: (b, i, 0)), st_spec),
        scratch_shapes=[pltpu.VMEM((sb, H_A, DK_A, DK_A), F32)],
        compiler_params=_params(("arbitrary", "arbitrary")),
        name="hgrn_scan",
    )(*args)


def _rwkv_prep_kernel(*refs, seq, has_first):
    if has_first:
        (x_ref, n1_ref, wrw_ref, first_ref, mu_ref, w0_ref, w2_ref, a0_ref, a2_ref, g2_ref, kk_ref, ka_ref,
         rk_ref, seg_ref, r_o, lw_o, k_o, v_o, ah_o, bh_o, bonus_o, g_o, carry) = refs
    else:
        (x_ref, n1_ref, wrw_ref, mu_ref, w0_ref, w2_ref, a0_ref, a2_ref, g2_ref, kk_ref, ka_ref,
         rk_ref, seg_ref, r_o, lw_o, k_o, v_o, ah_o, bh_o, bonus_o, g_o, carry) = refs
    d = r_o.shape[1]
    xin = x_ref[...]
    xin = xin * lax.rsqrt(jnp.mean(xin * xin, axis=-1, keepdims=True) + EPS) * n1_ref[...]
    x = _dg(xin.astype(BF16), wrw_ref[...], NN)
    tm = x.shape[0]
    i = pl.program_id(0)

    @pl.when(i == 0)
    def _():
        carry[...] = jnp.zeros_like(carry)

    rowi = _iota((tm, 1), 0)
    prev = pltpu.roll(x, 1, 0)
    prev = jnp.where(rowi == 0, carry[0:1, :], prev)
    seq_start = ((rowi + i * tm) & (seq - 1)) == 0
    if has_first:
        prev = jnp.where(seq_start, first_ref[...], prev)
    else:
        prev = jnp.where(seq_start, 0.0, prev)
    carry[0:1, :] = x[tm - 1:tm, :]
    x = x + (prev - x) * mu_ref[...]
    r = x[:, 0:d]
    k = x[:, d:2 * d]
    v = x[:, 2 * d:3 * d]
    wlo = x[:, 3 * d:3 * d + LORA_W]
    alo = x[:, 3 * d + LORA_W:3 * d + LORA_W + LORA_A]
    glo = x[:, 3 * d + LORA_W + LORA_A:]
    seg = seg_ref[...]
    z = -(w0_ref[...] + _mm(jnp.tanh(wlo), w2_ref[...]))
    softplus = jnp.maximum(z, 0.0) + jnp.log(1.0 + jnp.exp(-jnp.abs(z)))
    lw_o[...] = -jnp.exp(-softplus - 0.5)
    a = _sigmoid(a0_ref[...] + _mm(alo, a2_ref[...]))
    g_o[...] = _mm(_sigmoid(glo), g2_ref[...])
    kk = k * kk_ref[...]
    kk = kk / jnp.maximum(jnp.sqrt(_seg_sum(kk * kk, seg)), 1e-12)
    k2 = k * (1.0 + (a - 1.0) * ka_ref[...])
    r_o[...] = r
    k_o[...] = k2
    v_o[...] = v
    ah_o[...] = -kk
    bh_o[...] = kk * a
    bonus_o[...] = _seg_sum(r * k2 * rk_ref[...], seg) * v


def _rwkv_prep(x2, first, p, *, seq, tm):
    n, dm = x2.shape
    wcols = p["w_rw"].shape[1]
    d = H_B * HD_B
    has_first = first is not None
    row = lambda c: pl.BlockSpec((tm, c), lambda i: (i, 0))
    full = lambda a: pl.BlockSpec(a.shape, lambda i: (0, 0))
    small = [p["mu"], p["w0"], p["w2"], p["a0"], p["a2"], p["g2"], p["k_k"], p["k_a"], p["r_k"], p["seg"]]
    args = [x2, p["n1"], p["w_rw"]] + ([first] if has_first else []) + small
    in_specs = ([row(dm), full(p["n1"]), full(p["w_rw"])] + ([row(wcols)] if has_first else [])
                + [full(a) for a in small])
    return pl.pallas_call(
        functools.partial(_rwkv_prep_kernel, seq=seq, has_first=has_first),
        out_shape=tuple(jax.ShapeDtypeStruct((n, d), F32) for _ in range(8)),
        grid=(n // tm,),
        in_specs=in_specs,
        out_specs=tuple(row(d) for _ in range(8)),
        scratch_shapes=[pltpu.VMEM((SUBLANES, wcols), F32)],
        compiler_params=_params(("arbitrary",)),
        name="rwkv_prep",
    )(*args)


def _tri_inv(a_bds, n, rows, passes):
    mm = lambda xs, ys: [_mm(x, y, NN, passes) for x, y in zip(xs, ys)]
    row = _iota((n, n), 0)
    col = _iota((n, n), 1)
    t = row & (rows - 1)
    i = col & (rows - 1)
    eye = (row == col).astype(F32)
    bshift = RWKV_BASE.bit_length() - 1
    base = (t >> bshift) == (i >> bshift)
    d1 = [jnp.where(base, a, 0.0) for a in a_bds]
    n1 = [eye + d for d in d1]
    d2 = mm(d1, d1)
    both = mm([jnp.concatenate([x, y], axis=0) for x, y in zip(n1, d2)], d2)
    n2 = [x + y[:n] for x, y in zip(n1, both)]
    d4 = [y[n:] for y in both]
    tinv = [x + y for x, y in zip(n2, mm(n2, d4))]
    m = RWKV_BASE
    while m < rows:
        s = m.bit_length() - 1
        tb = t >> s
        ib = i >> s
        sib = ((tb >> 1) == (ib >> 1)) & ((tb & 1) == 1) & ((ib & 1) == 0)
        off = [jnp.where(sib, a, 0.0) for a in a_bds]
        nb = n // m

        def odd(x, m=m, nb=nb):
            return jnp.concatenate([x[j * m:(j + 1) * m] for j in range(1, nb, 2)], axis=0)

        def spread(y, m=m, nb=nb):
            zero = jnp.zeros((m, n), F32)
            return jnp.concatenate([p for j in range(nb // 2) for p in (zero, y[j * m:(j + 1) * m])], axis=0)

        upd = mm(mm([odd(x) for x in tinv], off), tinv)
        tinv = [x + spread(y) for x, y in zip(tinv, upd)]
        m *= 2
    return tinv


def _rwkv_chunks(ins, sts, *, rows, t_valid, passes):
    w = ins[0][0].shape[1]
    n = RWKV_GROUP * rows
    if t_valid < rows:
        valid = _iota((rows, 1), 0) < t_valid
        ins = [tuple(jnp.where(valid, x, 0.0) for x in u) for u in ins]
    r, lw, k, v, ah, bh = ([u[j] for u in ins] for j in range(6))
    lrow = _iota((rows, rows), 0)
    lcol = _iota((rows, rows), 1)
    tril = (lcol <= lrow).astype(BF16)
    hs = HD_B.bit_length() - 1
    rs = rows.bit_length() - 1
    bd_mask = (_iota((n, w), 0) >> rs) == (_iota((n, w), 1) >> hs)
    nmask = (_iota((n, n), 0) >> rs) == (_iota((n, n), 1) >> rs)
    t_idx = _iota((rows, 2 * n), 0)
    i_idx = _iota((rows, 2 * n), 1) & (rows - 1)
    strict = i_idx < t_idx
    incl = i_idx <= t_idx

    def expand(x):
        return jnp.where(bd_mask, jnp.concatenate([x] * RWKV_GROUP, axis=0), 0.0)

    cum = [_mm_exact_lhs(tril, x) for x in lw]
    gl = [c[rows - 1:rows] for c in cum]
    e_neg = [jnp.exp(-c) for c in cum]
    e_end = [jnp.exp(g - c) for g, c in zip(gl, cum)]
    qa = [a * jnp.exp(c - l) for a, c, l in zip(ah, cum, lw)]
    qr = [x * jnp.exp(c) for x, c in zip(r, cum)]
    ke = [jnp.concatenate([expand(b * e), expand(x * e)], axis=0) for b, x, e in zip(bh, k, e_neg)]
    kge = [jnp.concatenate([expand(b * e), expand(x * e)], axis=0) for b, x, e in zip(bh, k, e_end)]
    q2 = [jnp.concatenate([a, x], axis=0) for a, x in zip(qa, qr)]
    scores = [_mm(q, x, NT, passes) for q, x in zip(q2, ke)]
    a_a = [jnp.where(strict, s[:rows], 0.0) for s in scores]
    a_r = [jnp.where(incl, s[rows:], 0.0) for s in scores]
    v_bd = [expand(x) for x in v]
    akv = [_mm(jnp.concatenate([a[:, n:], b[:, n:]], axis=0), x, NN, passes)
           for a, b, x in zip(a_a, a_r, v_bd)]
    w_kv = [x[:rows] for x in akv]
    a_bd =[jnp.where(nmask, jnp.concatenate([a[:, :n]] * RWKV_GROUP, axis=0), 0.0) for a in a_a]
    tinv = _tri_inv(a_bd, n, rows, passes)
    carried = [_mm(q, st, NT, passes) for q, st in zip(q2, sts)]
    p1 = [x[:rows] for x in carried]
    y_carry = [x[rows:] for x in carried]
    c_bd = [_mm(t, expand(p + x), NN, passes) for t, p, x in zip(tinv, p1, w_kv)]
    cv = [jnp.concatenate([c, x], axis=0) for c, x in zip(c_bd, v_bd)]
    st_new = [st * jnp.exp(g) + _mm(c, x, TN, passes) for st, g, c, x in zip(sts, gl, cv, kge)]
    ys = [yc + x[rows:] + _mm(a[:, :n], c, NN, passes) for yc, x, a, c in zip(y_carry, akv, a_r, c_bd)]
    return ys, st_new


def _rwkv_scan_kernel(*refs, chunk, n_chunks, t_valid, has_s0, passes, sb):
    if has_s0:
        r_ref, lw_ref, k_ref, v_ref, ah_ref, bh_ref, s0_ref, y_ref, s_ref, st_scr = refs
    else:
        r_ref, lw_ref, k_ref, v_ref, ah_ref, bh_ref, y_ref, s_ref, st_scr = refs
    ti = pl.program_id(1)
    w = RWKV_GROUP * HD_B
    groups = H_B // RWKV_GROUP
    units = [(s, g) for s in range(sb) for g in range(groups)]
    blk = lambda h: slice(h * HD_B, (h + 1) * HD_B)

    @pl.when(ti == 0)
    def _():
        st_scr[...] = jnp.zeros_like(st_scr)
        if has_s0:
            for s, g in units:
                for h in range(RWKV_GROUP):
                    st_scr[s, g, blk(h), blk(h)] = s0_ref[s, g * RWKV_GROUP + h]

    def body(c, carry):
        if n_chunks == 1:
            sl = slice(None)
        else:
            sl = pl.ds(pl.multiple_of(c * chunk, chunk), chunk)
        cols = lambda g: slice(g * w, (g + 1) * w)
        in_refs = (r_ref, lw_ref, k_ref, v_ref, ah_ref, bh_ref)
        ins = [tuple(_pad_rows(ref[s, sl, cols(g)], chunk) for ref in in_refs) for s, g in units]
        ys, st_new = _rwkv_chunks(ins, [st_scr[s, g] for s, g in units],
                                  rows=chunk, t_valid=t_valid, passes=passes)
        for (s, g), y, st in zip(units, ys, st_new):
            st_scr[s, g] = st
            y_ref[s, sl, cols(g)] = y[:t_valid]
        return carry

    if n_chunks == 1:
        body(0, 0)
    else:
        lax.fori_loop(0, n_chunks, body, 0)

    @pl.when(ti == pl.num_programs(1) - 1)
    def _():
        for s, g in units:
            for h in range(RWKV_GROUP):
                s_ref[s, g * RWKV_GROUP + h] = st_scr[s, g, blk(h), blk(h)]


def _rwkv_scan(r, lw, k, v, ah, bh, s0, *, batch, seq, tt, passes, sb):
    has_s0 = s0 is not None
    w = RWKV_GROUP * HD_B
    d_b = H_B * HD_B
    chunk = min(RWKV_CHUNK, max(seq, SUBLANES))
    tt = min(tt, seq)
    n_chunks = max(tt // chunk, 1)
    t_valid = min(chunk, seq)
    tok = pl.BlockSpec((sb, tt, d_b), lambda b, i: (b, i, 0))
    st_spec = pl.BlockSpec((sb, H_B, HD_B, HD_B), lambda b, i: (b, 0, 0, 0))
    args = [r, lw, k, v, ah, bh]
    in_specs = [tok] * 6
    if has_s0:
        args.append(s0)
        in_specs.append(st_spec)
    return pl.pallas_call(
        functools.partial(_rwkv_scan_kernel, chunk=chunk, n_chunks=n_chunks, t_valid=t_valid,
                          has_s0=has_s0, passes=passes, sb=sb),
        out_shape=(jax.ShapeDtypeStruct((batch, seq, d_b), F32),
                   jax.ShapeDtypeStruct((batch, H_B, HD_B, HD_B), F32)),
        grid=(batch // sb, seq // tt),
        in_specs=in_specs,
        out_specs=(tok, st_spec),
        scratch_shapes=[pltpu.VMEM((sb, H_B // RWKV_GROUP, w, w), F32)],
        compiler_params=_params(("arbitrary", "arbitrary")),
        name="rwkv_scan",
    )(*args)


def _merge_kernel(*refs, blocks_a):
    (lnw_ref, lnb_ref, seg_ref, wa_ref, wb_ref, wo_ref, n2_ref, rwt_ref, rb_ref,
     x1_o, xn_o, idx_o, gate_o, cnt_o) = refs[14:]
    from_b = pl.program_id(0) >= blocks_a

    @pl.when(pl.program_id(0) == 0)
    def _():
        cnt_o[...] = jnp.zeros_like(cnt_o)

    def body(group):
        y, bonus, g, oa, ga, gb, x = (ref[...] for ref in group)
        seg = seg_ref[...]
        inv = 1.0 / HD_B
        mu = _seg_sum(y, seg) * inv
        dlt = y - mu
        var = _seg_sum(dlt * dlt, seg) * inv
        yn = dlt * lax.rsqrt(var + RWKV_GN_EPS) * lnw_ref[...] + lnb_ref[...]
        ob = (yn + bonus) * g
        merged = _sigmoid(ga) * _mm(oa, wa_ref[...]) + _sigmoid(gb) * _mm(ob, wb_ref[...])
        x1 = x + _mm(merged, wo_ref[...])
        x1_o[...] = x1
        xn = x1 * lax.rsqrt(jnp.mean(x1 * x1, axis=-1, keepdims=True) + EPS) * n2_ref[...]
        _tok_store(xn_o, xn)
        logits = _mm(rwt_ref[...], xn, NT, 3) + rb_ref[...]
        tm = x1.shape[0]
        eio = _iota((N_EXPERTS, tm), 0)
        idx_rows, val_rows = [], []
        cnt = jnp.zeros((N_EXPERTS, 1), F32)
        for _ in range(TOP_K):
            mx = jnp.max(logits, axis=0, keepdims=True)
            pick = jnp.min(jnp.where(logits == mx, eio, N_EXPERTS), axis=0, keepdims=True)
            hit = eio == pick
            cnt = cnt + jnp.sum(hit.astype(F32), axis=1, keepdims=True)
            logits = jnp.where(hit, -jnp.inf, logits)
            idx_rows.append(pick)
            val_rows.append(mx)
        ex = [jnp.exp(vr - val_rows[0]) for vr in val_rows]
        den = ex[0] + ex[1] + ex[2] + ex[3]
        idx_o[...] = _stack_rows(idx_rows)
        gate_o[...] = _stack_rows([e / den for e in ex])
        cnt_o[...] += jnp.broadcast_to(cnt, cnt_o.shape)

    pl.when(jnp.logical_not(from_b))(lambda: body(refs[0:7]))
    pl.when(from_b)(lambda: body(refs[7:14]))


def _merge(tok_a, tok_b, p, *, tm):
    n_a, d = tok_a[5].shape
    n_b = tok_b[5].shape[0]
    n = n_a + n_b
    blocks_a = n_a // tm
    last_a = blocks_a - 1
    ia = lambda i: jnp.minimum(i, last_a)
    ib = lambda i: jnp.maximum(i - blocks_a, 0)

    def group_specs(tok, im, mode=None):
        y, _, _, oa, gates, _ = tok
        dm = gates.shape[1] // 2
        spec = lambda c, j: pl.BlockSpec((tm, c), lambda i: (im(i), j), pipeline_mode=mode)
        return [spec(y.shape[1], 0), spec(y.shape[1], 0), spec(y.shape[1], 0), spec(oa.shape[1], 0),
                spec(dm, 0), spec(dm, 1), spec(d, 0)]

    def group_args(tok):
        y, bonus, g, oa, gates, x = tok
        return [y, bonus, g, oa, gates, gates, x]

    full = lambda a: pl.BlockSpec(a.shape, lambda i: (0, 0))
    small = [p["ln_w"], p["ln_b"], p["seg"], p["wa"], p["wb"], p["wo"], p["n2"], p["rwt"], p["rb"]]
    out_row = pl.BlockSpec((tm, d), lambda i: (i, 0))
    out_col = pl.BlockSpec((SUBLANES, tm), lambda i: (0, i))
    return pl.pallas_call(
        functools.partial(_merge_kernel, blocks_a=blocks_a),
        out_shape=(jax.ShapeDtypeStruct((n, d), F32), jax.ShapeDtypeStruct((n * SUBLANES, LANES), F32),
                   jax.ShapeDtypeStruct((SUBLANES, n), jnp.int32),
                   jax.ShapeDtypeStruct((SUBLANES, n), F32),
                   jax.ShapeDtypeStruct((N_EXPERTS, LANES), F32)),
        grid=(n // tm,),
        in_specs=(group_specs(tok_a, ia) + group_specs(tok_b, ib, pl.Buffered(1))
                  + [full(a) for a in small]),
        out_specs=(out_row, pl.BlockSpec((tm * SUBLANES, LANES), lambda i: (i, 0)), out_col, out_col,
                   pl.BlockSpec((N_EXPERTS, LANES), lambda i: (0, 0))),
        compiler_params=_params(("arbitrary",)),
        name="merge_router",
    )(*(group_args(tok_a) + group_args(tok_b) + small))


def _rank_kernel(idx_ref, pstart_ref, dest_o, carry):
    i = pl.program_id(0)

    @pl.when(i == 0)
    def _():
        carry[...] = pstart_ref[...]

    tm = idx_ref.shape[1]
    eio = _iota((N_EXPERTS, tm), 0)
    idx = idx_ref[...]
    hits = [eio == idx[j:j + 1, :] for j in range(TOP_K)]
    sel = hits[0] | hits[1] | hits[2] | hits[3]
    upper = (_iota((tm, tm), 0) < _iota((tm, tm), 1)).astype(BF16)
    rank = _dg(sel.astype(BF16), upper, NN) + carry[:, 0:1]
    rows = [jnp.sum(jnp.where(h, rank, 0.0), axis=0, keepdims=True) for h in hits]
    dest_o[...] = _stack_rows(rows).astype(jnp.int32)
    carry[...] += jnp.sum(sel.astype(F32), axis=1, keepdims=True)


def _rank(idx4, pstart, *, tm):
    n = idx4.shape[1]
    return pl.pallas_call(
        _rank_kernel,
        out_shape=jax.ShapeDtypeStruct((SUBLANES, n), jnp.int32),
        grid=(n // tm,),
        in_specs=[pl.BlockSpec((SUBLANES, tm), lambda i: (0, i)),
                  pl.BlockSpec((N_EXPERTS, LANES), lambda i: (0, 0))],
        out_specs=pl.BlockSpec((SUBLANES, tm), lambda i: (0, i)),
        scratch_shapes=[pltpu.VMEM((N_EXPERTS, LANES), F32)],
        compiler_params=_params(("arbitrary",)),
        name="moe_rank",
    )(idx4, pstart)


def _scatter_kernel(pad_lo, pad_hi, na_ref, dest_ref, x_ref, xs_hbm, zeros, sem, zsem):
    tm = x_ref.shape[0]
    zrows = zeros.shape[0]
    nblk = xs_hbm.shape[0] // zrows

    @pl.when(pl.program_id(0) == 0)
    def _():
        zeros[...] = jnp.zeros_like(zeros)
        zrun = lambda r, size: pltpu.make_async_copy(zeros.at[pl.ds(0, size)], xs_hbm.at[pl.ds(r, size)], zsem)

        def per_expert(op):
            def f(e, c):
                lo = pad_lo[e]
                n_pad = pad_hi[e] - lo
                for bit in reversed(range((zrows - 1).bit_length())):
                    @pl.when(((n_pad >> bit) & 1) == 1)
                    def _(lo=lo, bit=bit):
                        op(zrun(lo, 1 << bit))
                    lo = lo + (n_pad & (1 << bit))
                return c
            return f

        lax.fori_loop(0, N_EXPERTS, per_expert(lambda cp: cp.start()), 0)
        lax.fori_loop(na_ref[0], nblk, lambda b, c: (zrun(b * zrows, zrows).start(), c)[1], 0)
        lax.fori_loop(0, N_EXPERTS, per_expert(lambda cp: cp.wait()), 0)
        lax.fori_loop(na_ref[0], nblk, lambda b, c: (zrun(b * zrows, zrows).wait(), c)[1], 0)

    def issue(t, c):
        for j in range(TOP_K):
            pltpu.make_async_copy(x_ref.at[t], xs_hbm.at[dest_ref[j, t]], sem).start(priority=j % 2)
        return c

    lax.fori_loop(0, tm, issue, 0, unroll=DMA_UNROLL)
    for j in range(TOP_K):
        pltpu.make_async_copy(x_ref, xs_hbm.at[pl.ds(0, tm)], sem).wait()


def _scatter(pad_lo, pad_hi, n_active, dest4, x3, m_total, *, tm):
    n = x3.shape[0]
    tail = x3.shape[1:]
    return pl.pallas_call(
        _scatter_kernel,
        out_shape=jax.ShapeDtypeStruct((m_total,) + tail, x3.dtype),
        grid_spec=pltpu.PrefetchScalarGridSpec(
            num_scalar_prefetch=3,
            grid=(n // tm,),
            in_specs=[pl.BlockSpec((SUBLANES, tm), lambda i, *_: (0, i), memory_space=pltpu.SMEM),
                      pl.BlockSpec((tm,) + tail, lambda i, *_: (i, 0, 0))],
            out_specs=pl.BlockSpec(memory_space=pl.ANY),
            scratch_shapes=[pltpu.VMEM((MOE_ROWS,) + tail, x3.dtype),
                            pltpu.SemaphoreType.DMA, pltpu.SemaphoreType.DMA]),
        compiler_params=_params(("arbitrary",)),
        name="moe_scatter",
    )(pad_lo, pad_hi, n_active, dest4, x3)


def _expert_kernel(be_ref, na_ref, x_ref, wgu_ref, bgu_ref, wd_ref, bd_ref, y_ref, wgu_bf, wd_perm, wd_bf):
    b = pl.program_id(0)
    active = b < na_ref[0]
    half = LANES // 2
    dff = wd_ref.shape[1]

    @pl.when(active & ((b == 0) | (be_ref[b] != be_ref[jnp.maximum(b - 1, 0)])))
    def _():
        wgu_bf[...] = wgu_ref[0].astype(BF16)
        for c in range(wd_ref.shape[2] // LANES):
            cols = slice(c * LANES, (c + 1) * LANES)
            for p in range(dff // LANES):
                r0 = p * LANES
                wd_perm[pl.ds(r0, half, stride=2), :] = wd_ref[0, r0:r0 + half, cols]
                wd_perm[pl.ds(r0 + 1, half, stride=2), :] = wd_ref[0, r0 + half:r0 + LANES, cols]
            wd_bf[:, cols] = wd_perm[...].astype(BF16)

    @pl.when(active)
    def _():
        x = _tok_load(x_ref).astype(BF16)
        h = _dg(x, wgu_bf[...], NN) + bgu_ref[0]
        even = (_iota((1, LANES), 1) & 1) == 0
        acts = []
        for p in range(dff // LANES):
            ha = h[:, 2 * p * LANES:(2 * p + 1) * LANES]
            hb = h[:, (2 * p + 1) * LANES:(2 * p + 2) * LANES]
            hg = jnp.where(even, ha, pltpu.roll(hb, 1, 1))
            hl = jnp.where(even, pltpu.roll(ha, LANES - 1, 1), hb)
            glu = jnp.minimum(hg, SWIGLU_LIMIT)
            lin = jnp.clip(hl, -SWIGLU_LIMIT, SWIGLU_LIMIT)
            acts.append((glu * _sigmoid(SWIGLU_ALPHA * glu) * (lin + 1.0)).astype(BF16))
        act = jnp.concatenate(acts, axis=1)
        _tok_store(y_ref, _dg(act, wd_bf[...], NN) + bd_ref[0])

    @pl.when(jnp.logical_not(active))
    def _():
        y_ref[...] = jnp.zeros_like(y_ref)


def _experts(block_expert, n_active, xs2, w_gu, b_gu, w_down, b_down):
    d = w_gu.shape[1]
    dff = w_down.shape[1]
    per_tok = d // LANES
    m_total = xs2.shape[0] // per_tok
    nblk = m_total // MOE_ROWS
    wspec = lambda r, c: pl.BlockSpec((1, r, c), lambda b, be, na: (be[b], 0, 0))
    tok = pl.BlockSpec((MOE_ROWS * per_tok, LANES), lambda b, be, na: (b, 0))
    return pl.pallas_call(
        _expert_kernel,
        out_shape=jax.ShapeDtypeStruct(xs2.shape, F32),
        grid_spec=pltpu.PrefetchScalarGridSpec(
            num_scalar_prefetch=2,
            grid=(nblk,),
            in_specs=[tok, wspec(d, 2 * dff), wspec(1, 2 * dff), wspec(dff, d), wspec(1, d)],
            out_specs=tok,
            scratch_shapes=[pltpu.VMEM((d, 2 * dff), BF16), pltpu.VMEM((dff, LANES), F32),
                            pltpu.VMEM((dff, d), BF16)]),
        compiler_params=_params(("arbitrary",)),
        name="moe_experts",
    )(block_expert, n_active, xs2, w_gu, b_gu, w_down, b_down)


def _combine_kernel(dest_ref, next_ref, gate_ref, ys_hbm, x_ref, w_ref, oa_ref, ob_ref, buf, mixed, sems,
                    *, blocks_a):
    tm = x_ref.shape[0]
    i = pl.program_id(0)
    slot = i % 2

    def issue(idx_ref, s):
        def f(t, c):
            for j in range(TOP_K):
                pltpu.make_async_copy(ys_hbm.at[idx_ref[j, t]], buf.at[s, j, t], sems.at[s]).start(priority=j % 2)
            return c
        lax.fori_loop(0, tm, f, 0, unroll=DMA_UNROLL)

    @pl.when(i == 0)
    def _():
        issue(dest_ref, slot)

    @pl.when(i + 1 < pl.num_programs(0))
    def _():
        issue(next_ref, 1 - slot)

    for j in range(TOP_K):
        pltpu.make_async_copy(ys_hbm.at[pl.ds(0, tm)], buf.at[slot, j], sems.at[slot]).wait()

    def mix(t, c):
        acc = gate_ref[0, t] * buf[slot, 0, t]
        for j in range(1, TOP_K):
            acc = acc + gate_ref[j, t] * buf[slot, j, t]
        mixed[pl.ds(pl.multiple_of(t * SUBLANES, SUBLANES), SUBLANES), :] = acc
        return c

    lax.fori_loop(0, tm, mix, 0, unroll=DMA_UNROLL)
    x = x_ref[...] + _tok_load(mixed)
    y = x * lax.rsqrt(jnp.mean(x * x, axis=-1, keepdims=True) + EPS) * w_ref[...]
    first = pl.program_id(0) < blocks_a

    @pl.when(first)
    def _():
        oa_ref[...] = y

    @pl.when(jnp.logical_not(first))
    def _():
        ob_ref[...] = y


def _combine(dest4, gate4, ys3, x1, w, n_a, *, tm):
    n, d = x1.shape
    tail = ys3.shape[1:]
    blocks_a = n_a // tm
    last = n // tm - 1
    smem = lambda im: pl.BlockSpec((SUBLANES, tm), lambda i: (0, im(i)), memory_space=pltpu.SMEM)
    return pl.pallas_call(
        functools.partial(_combine_kernel, blocks_a=blocks_a),
        out_shape=(jax.ShapeDtypeStruct((n_a, d), F32), jax.ShapeDtypeStruct((n - n_a, d), F32)),
        grid=(n // tm,),
        in_specs=[smem(lambda i: i), smem(lambda i: jnp.minimum(i + 1, last)), smem(lambda i: i),
                  pl.BlockSpec(memory_space=pl.ANY),
                  pl.BlockSpec((tm, d), lambda i: (i, 0)), pl.BlockSpec((1, d), lambda i: (0, 0))],
        out_specs=(pl.BlockSpec((tm, d), lambda i: (jnp.minimum(i, blocks_a - 1), 0)),
                   pl.BlockSpec((tm, d), lambda i: (jnp.maximum(i - blocks_a, 0), 0))),
        scratch_shapes=[pltpu.VMEM((2, TOP_K, tm) + tail, F32), pltpu.VMEM((tm * SUBLANES, LANES), F32),
                        pltpu.SemaphoreType.DMA((2,))],
        compiler_params=_params(("arbitrary",)),
        name="moe_combine",
    )(dest4, dest4, gate4, ys3, x1, w)


def _mixers(x, s_hgrn, s_rwkv, shift_prev, wts):
    batch, seq, d = x.shape
    n = batch * seq
    x2 = x.reshape(n, d)
    tm = min(512, n)
    seqs_per_step = LONG_SEQS_PER_STEP if seq >= HGRN_CHUNK else SHORT_SEQS_PER_STEP
    ph = _norm_proj(x2, wts["n1"], wts["w_h"], normalize=True, tm=tm)
    pg = _norm_proj(x2, wts["n1"], wts["w_g"], normalize=True, tm=tm)
    oa, s_hgrn_new = _hgrn(ph.reshape(batch, seq, -1), wts["lb"], wts["gw"], s_hgrn,
                           batch=batch, seq=seq, tt=256, sb=seqs_per_step)
    first = None
    if shift_prev is not None:
        prev = _norm_proj(shift_prev, wts["n1"], wts["w_rw"], normalize=False, tm=shift_prev.shape[0])
        first = jnp.repeat(prev, seq, axis=0)
    r, lw, k, v, ah, bh, bonus, g = _rwkv_prep(x2, first, wts, seq=seq, tm=min(512, n))
    to3 = lambda a: a.reshape(batch, seq, -1)
    y, s_rwkv_new = _rwkv_scan(to3(r), to3(lw), to3(k), to3(v), to3(ah), to3(bh), s_rwkv,
                               batch=batch, seq=seq, tt=128, passes=1, sb=seqs_per_step)
    return (y.reshape(n, -1), bonus, g, oa.reshape(n, -1), pg, x2), s_hgrn_new, s_rwkv_new


def kernel(x_prompt, x_sample, state_hgrn, state_rwkv, state_shift, norm1_w, w_in, hgrn_lb_logits,
           hgrn_gnorm_w, rwkv_mu, rwkv_w0, rwkv_w2, rwkv_a0, rwkv_a2, rwkv_g2, rwkv_k_k, rwkv_k_a,
           rwkv_r_k, rwkv_ln_w, rwkv_ln_b, w_branch_a, w_branch_b, w_out, norm2_w, router_w, router_b,
           expert_w_gu, expert_b_gu, expert_w_down, expert_b_down, final_norm_w):
    bp, tp, d = x_prompt.shape
    bs, ts, _ = x_sample.shape
    n_p, n_s = bp * tp, bs * ts
    n = n_p + n_s
    d_a = H_A * DK_A
    d_b = H_B * HD_B
    rw_start = 4 * d_a
    gate_start = rw_start + 3 * d_b + LORA_W + LORA_A + LORA_G
    row2 = lambda a: a.reshape(1, -1)
    lb = jnp.cumsum(jax.nn.softmax(hgrn_lb_logits.astype(F32), axis=0), axis=0)[0]
    hid = jnp.arange(RWKV_GROUP * HD_B, dtype=jnp.int32) // HD_B
    wts = {
        "n1": row2(norm1_w[0]),
        "w_h": w_in[0][:, :rw_start].astype(BF16),
        "w_rw": w_in[0][:, rw_start:gate_start].astype(BF16),
        "w_g": w_in[0][:, gate_start:].astype(BF16),
        "lb": row2(lb), "gw": row2(hgrn_gnorm_w[0]),
        "mu": row2(rwkv_mu[0]), "w0": row2(rwkv_w0[0]), "w2": rwkv_w2[0], "a0": row2(rwkv_a0[0]),
        "a2": rwkv_a2[0], "g2": rwkv_g2[0], "k_k": row2(rwkv_k_k[0]), "k_a": row2(rwkv_k_a[0]),
        "r_k": row2(rwkv_r_k[0]), "seg": (hid[:, None] == hid[None, :]).astype(BF16),
        "ln_w": row2(rwkv_ln_w[0]), "ln_b": row2(rwkv_ln_b[0]),
        "wa": w_branch_a[0].astype(BF16), "wb": w_branch_b[0].astype(BF16), "wo": w_out[0].astype(BF16),
        "n2": row2(norm2_w[0]), "rwt": router_w[0].T, "rb": router_b[0].reshape(-1, 1),
    }
    tok_p, hgrn_p, rwkv_p = _mixers(x_prompt, None, None, None, wts)
    tok_s, hgrn_s, rwkv_s = _mixers(x_sample, state_hgrn[0], state_rwkv[0], state_shift[0], wts)
    shift = _rms_rows(jnp.concatenate([x_prompt[:, -1], x_sample[:, -1]], axis=0), wts["n1"])

    x1, xn2, idx4, gate4, counts = _merge(tok_p, tok_s, wts, tm=512)

    cnt = counts[:, 0].astype(jnp.int32)
    padded = (cnt + MOE_ROWS - 1) // MOE_ROWS * MOE_ROWS
    pend = jnp.cumsum(padded)
    pstart = pend - padded
    nblk = (n * TOP_K + N_EXPERTS * (MOE_ROWS - 1) + MOE_ROWS - 1) // MOE_ROWS
    m_total = nblk * MOE_ROWS
    block_row = jnp.arange(nblk, dtype=jnp.int32)[:, None] * MOE_ROWS
    block_expert = jnp.minimum(jnp.sum((pend[None, :] <= block_row).astype(jnp.int32), axis=1), N_EXPERTS - 1)
    n_active = (pend[-1:] // MOE_ROWS).astype(jnp.int32)
    dest4 = _rank(idx4, jnp.broadcast_to(pstart.astype(F32)[:, None], (N_EXPERTS, LANES)), tm=512)

    per_tok = d // LANES
    xs = _scatter(pstart + cnt, pend, n_active, dest4, xn2.reshape(n, per_tok, LANES), m_total, tm=512)
    ys = _experts(block_expert, n_active, xs.reshape(m_total * per_tok, LANES), expert_w_gu[0],
                  expert_b_gu[0][:, None, :], expert_w_down[0], expert_b_down[0][:, None, :])
    y_p, y_s = _combine(dest4, gate4, ys.reshape(m_total, per_tok, LANES), x1, row2(final_norm_w), n_p, tm=512)

    return (y_p.reshape(bp, tp, d), y_s.reshape(bs, ts, d),
            hgrn_p[None], rwkv_p[None], shift[None, :bp],
            hgrn_s[None], rwkv_s[None], shift[None, bp:])
```

```python
import functools

import jax
import jax.numpy as jnp
from jax import lax
from jax.experimental import pallas as pl
from jax.experimental.pallas import tpu as pltpu

F32 = jnp.float32
BF16 = jnp.bfloat16

H_A = 4
DK_A = 128
H_B = 8
HD_B = 64
LORA_W = 64
LORA_A = 64
LORA_G = 128
N_EXPERTS = 32
TOP_K = 4
SWIGLU_LIMIT = 7.0
SWIGLU_ALPHA = 1.702
EPS = 1e-6
RWKV_GN_EPS = 64e-5

LANES = 128
SUBLANES = 8
HGRN_CHUNK = 64
HGRN_SUB = 8
RWKV_CHUNK = 64
RWKV_GROUP = 4
RWKV_BASE = 8
MOE_ROWS = 768
PROJ_ROWS = 1024
DMA_UNROLL = 4
LONG_SEQS_PER_STEP = 4
SHORT_SEQS_PER_STEP = 4
VMEM_LIMIT = 56 * 1024 * 1024

NN = (((1,), (0,)), ((), ()))
NT = (((1,), (1,)), ((), ()))
TN = (((0,), (0,)), ((), ()))


def _params(sem):
    return pltpu.CompilerParams(dimension_semantics=sem, vmem_limit_bytes=VMEM_LIMIT)


def _dg(a, b, dims):
    return lax.dot_general(a, b, dims, preferred_element_type=F32)


def _split2(x):
    hi = x.astype(BF16)
    lo = (x - hi.astype(F32)).astype(BF16)
    return hi, lo


def _mm(a, b, dims=NN, passes=1):
    if passes == 1:
        return _dg(a.astype(BF16), b.astype(BF16), dims)
    ah, al = _split2(a)
    bh, bl = _split2(b)
    return _dg(ah, bh, dims) + (_dg(ah, bl, dims) + _dg(al, bh, dims))


def _mm_exact_lhs(sel_bf16, x, dims=NN):
    hi = x.astype(BF16)
    r1 = x - hi.astype(F32)
    mid = r1.astype(BF16)
    lo = (r1 - mid.astype(F32)).astype(BF16)
    return _dg(sel_bf16, hi, dims) + (_dg(sel_bf16, mid, dims) + _dg(sel_bf16, lo, dims))


def _seg_sum(x, seg_bf16):
    w = seg_bf16.shape[0]
    hi, lo = _split2(x)
    parts = [_dg(hi[:, c:c + w], seg_bf16, NN) + _dg(lo[:, c:c + w], seg_bf16, NN)
             for c in range(0, x.shape[1], w)]
    return parts[0] if len(parts) == 1 else jnp.concatenate(parts, axis=1)


def _iota(shape, dim):
    return lax.broadcasted_iota(jnp.int32, shape, dim)


def _sigmoid(x):
    return 1.0 / (1.0 + jnp.exp(-x))


def _tok_load(ref):
    tm = ref.shape[0] // SUBLANES
    return jnp.concatenate([ref[pl.ds(c, tm, stride=SUBLANES), :] for c in range(SUBLANES)], axis=1)


def _tok_store(ref, val):
    tm = val.shape[0]
    for c in range(SUBLANES):
        ref[pl.ds(c, tm, stride=SUBLANES), :] = val[:, c * LANES:(c + 1) * LANES]


def _stack_rows(rows):
    n = rows[0].shape[1]
    sub = _iota((SUBLANES, n), 0)
    out = jnp.zeros((SUBLANES, n), rows[0].dtype)
    for j, r in enumerate(rows):
        out = jnp.where(sub == j, jnp.broadcast_to(r, (SUBLANES, n)), out)
    return out


def _pad_rows(x, rows):
    if x.shape[0] == rows:
        return x
    return jnp.concatenate([x, jnp.zeros((rows - x.shape[0], x.shape[1]), x.dtype)], axis=0)


def _norm_proj_kernel(x_ref, nw_ref, w_ref, o_ref, *, normalize):
    x = x_ref[...]
    if normalize:
        x = x * lax.rsqrt(jnp.mean(x * x, axis=-1, keepdims=True) + EPS) * nw_ref[...]
    o_ref[...] = _dg(x.astype(BF16), w_ref[...], NN)


def _norm_proj(x, nw, w_bf16, *, normalize, tm):
    n, d = x.shape
    nc = w_bf16.shape[1]
    return pl.pallas_call(
        functools.partial(_norm_proj_kernel, normalize=normalize),
        out_shape=jax.ShapeDtypeStruct((n, nc), F32),
        grid=(n // tm,),
        in_specs=[pl.BlockSpec((tm, d), lambda i: (i, 0)),
                  pl.BlockSpec((1, d), lambda i: (0, 0)),
                  pl.BlockSpec((d, nc), lambda i: (0, 0))],
        out_specs=pl.BlockSpec((tm, nc), lambda i: (i, 0)),
        compiler_params=_params(("arbitrary",)),
        name="norm_proj",
    )(x, nw, w_bf16)


def _rms_rows_kernel(x_ref, w_ref, o_ref):
    x = x_ref[...]
    o_ref[...] = x * lax.rsqrt(jnp.mean(x * x, axis=-1, keepdims=True) + EPS) * w_ref[...]


def _rms_rows(x, w):
    return pl.pallas_call(
        _rms_rows_kernel,
        out_shape=jax.ShapeDtypeStruct(x.shape, F32),
        name="rms_rows",
    )(x, w)


def _hgrn_chunks(hqs, hfs, his, lbvs, sts, *, rows, t_valid):
    n_u = len(hqs)
    fs = [lbv + (1.0 - lbv) * _sigmoid(hf) for lbv, hf in zip(lbvs, hfs)]
    gs = [jnp.log(f) for f in fs]
    ks = [1.0 - f for f in fs]
    qs = [hq * _sigmoid(hq) for hq in hqs]
    vs = list(his)
    if t_valid < rows:
        valid = _iota((rows, 1), 0) < t_valid
        gs, ks, qs, vs = ([jnp.where(valid, x, 0.0) for x in xs] for xs in (gs, ks, qs, vs))
    row = _iota((rows, rows), 0)
    col = _iota((rows, rows), 1)
    tril = (col <= row).astype(BF16)
    bs = [_mm_exact_lhs(tril, g) for g in gs]
    sub = min(HGRN_SUB, rows)
    m = rows // 2
    scores = [None] * n_u
    while m >= sub:
        nb = rows // m
        shift = m.bit_length() - 1
        rb = row >> shift
        pair = ((rb & 1) == 1) & ((col >> shift) == rb - 1)
        qes, kes = [], []
        for q, k, b in zip(qs, ks, bs):
            starts = [jnp.zeros((1, DK_A), F32) if j == 0 else b[j * m - 1:j * m] for j in range(nb)]
            ends = [b[j * m + m - 1:j * m + m] for j in range(nb)]
            b_start = jnp.concatenate([jnp.broadcast_to(s, (m, DK_A)) for s in starts], axis=0)
            b_end = jnp.concatenate([jnp.broadcast_to(s, (m, DK_A)) for s in ends], axis=0)
            qes.append(q * jnp.exp(b - b_start))
            kes.append(k * jnp.exp(b_end - b))
        a_ms = [jnp.where(pair, _mm(qe, ke, NT), 0.0) for qe, ke in zip(qes, kes)]
        scores = [a if s is None else s + a for s, a in zip(scores, a_ms)]
        m //= 2
    os_ = [jnp.zeros((rows, DK_A), F32)] * n_u
    if scores[0] is not None:
        os_ = [_mm(s, v, NN) for s, v in zip(scores, vs)]
    rloc = _iota((rows, 1), 0) & (sub - 1)
    os_ = [o + jnp.sum(q * k, axis=-1, keepdims=True) * v for o, q, k, v in zip(os_, qs, ks, vs)]
    for d in range(1, min(sub, t_valid)):
        sh = lambda x: pltpu.roll(x, d, 0)
        es = [jnp.exp(jnp.where(rloc >= d, b - sh(b), -jnp.inf)) for b in bs]
        os_ = [o + jnp.sum(q * sh(k) * e, axis=-1, keepdims=True) * sh(v)
               for o, q, k, v, e in zip(os_, qs, ks, vs, es)]
    b_lasts = [b[rows - 1:rows] for b in bs]
    kds = [k * jnp.exp(bl - b) for k, bl, b in zip(ks, b_lasts, bs)]
    qbs = [q * jnp.exp(b) for q, b in zip(qs, bs)]
    os_ = [o + _mm(qb, st, NT) for o, qb, st in zip(os_, qbs, sts)]
    st_new = [st * jnp.exp(bl) + _mm(v, kd, TN) for st, bl, v, kd in zip(sts, b_lasts, vs, kds)]
    return os_, st_new


def _hgrn_kernel(*refs, chunk, n_chunks, t_valid, has_s0, sb):
    if has_s0:
        ph_ref, lb_ref, gw_ref, s0_ref, o_ref, s_ref, st_scr = refs
    else:
        ph_ref, lb_ref, gw_ref, o_ref, s_ref, st_scr = refs
    ti = pl.program_id(1)
    d_a = H_A * DK_A
    units = [(s, h) for s in range(sb) for h in range(H_A)]

    @pl.when(ti == 0)
    def _():
        for s, h in units:
            st_scr[s, h] = s0_ref[s, h].T if has_s0 else jnp.zeros((DK_A, DK_A), F32)

    gw = gw_ref[...]

    def body(c, carry):
        if n_chunks == 1:
            sl = slice(None)
        else:
            sl = pl.ds(pl.multiple_of(c * chunk, chunk), chunk)
        cols = lambda part, h: slice(part * d_a + h * DK_A, part * d_a + (h + 1) * DK_A)
        part = lambda p: [_pad_rows(ph_ref[s, sl, cols(p, h)], chunk) for s, h in units]
        os_, st_new = _hgrn_chunks(part(0), part(1), part(2),
                                   [lb_ref[:, h * DK_A:(h + 1) * DK_A] for _, h in units],
                                   [st_scr[s, h] for s, h in units], rows=chunk, t_valid=t_valid)
        for (s, h), o, st in zip(units, os_, st_new):
            st_scr[s, h] = st
            hg = ph_ref[s, sl, cols(3, h)]
            o = o[:t_valid]
            o = o * lax.rsqrt(jnp.mean(o * o, axis=-1, keepdims=True) + EPS) * gw
            o_ref[s, sl, h * DK_A:(h + 1) * DK_A] = o * (hg * _sigmoid(hg))
        return carry

    if n_chunks == 1:
        body(0, 0)
    else:
        lax.fori_loop(0, n_chunks, body, 0)

    @pl.when(ti == pl.num_programs(1) - 1)
    def _():
        for s, h in units:
            s_ref[s, h] = st_scr[s, h].T


def _hgrn(ph, lb, gw, s0, *, batch, seq, tt, sb):
    has_s0 = s0 is not None
    chunk = min(HGRN_CHUNK, max(seq, SUBLANES))
    tt = min(tt, seq)
    n_chunks = max(tt // chunk, 1)
    t_valid = min(chunk, seq)
    d_a = H_A * DK_A
    st_spec = pl.BlockSpec((sb, H_A, DK_A, DK_A), lambda b, i: (b, 0, 0, 0))
    in_specs = [pl.BlockSpec((sb, tt, 4 * d_a), lambda b, i: (b, i, 0)),
                pl.BlockSpec((1, d_a), lambda b, i: (0, 0)),
                pl.BlockSpec((1, DK_A), lambda b, i: (0, 0))]
    args = [ph, lb, gw]
    if has_s0:
        in_specs.append(st_spec)
        args.append(s0)
    return pl.pallas_call(
        functools.partial(_hgrn_kernel, chunk=chunk, n_chunks=n_chunks, t_valid=t_valid, has_s0=has_s0, sb=sb),
        out_shape=(jax.ShapeDtypeStruct((batch, seq, d_a), F32),
                   jax.ShapeDtypeStruct((batch, H_A, DK_A, DK_A), F32)),
        grid=(batch // sb, seq // tt),
        in_specs=in_specs,
        out_specs=(pl.BlockSpec((sb, tt, d_a), lambda b, i: (b, i, 0)), st_spec),
        scratch_shapes=[pltpu.VMEM((sb, H_A, DK_A, DK_A), F32)],
        compiler_params=_params(("arbitrary", "arbitrary")),
        name="hgrn_scan",
    )(*args)


def _rwkv_prep_kernel(*refs, seq, has_first):
    if has_first:
        (x_ref, n1_ref, wrw_ref, first_ref, mu_ref, w0_ref, w2_ref, a0_ref, a2_ref, g2_ref, kk_ref, ka_ref,
         rk_ref, seg_ref, r_o, lw_o, k_o, v_o, ah_o, bh_o, bonus_o, g_o, carry) = refs
    else:
        (x_ref, n1_ref, wrw_ref, mu_ref, w0_ref, w2_ref, a0_ref, a2_ref, g2_ref, kk_ref, ka_ref,
         rk_ref, seg_ref, r_o, lw_o, k_o, v_o, ah_o, bh_o, bonus_o, g_o, carry) = refs
    d = r_o.shape[1]
    xin = x_ref[...]
    xin = xin * lax.rsqrt(jnp.mean(xin * xin, axis=-1, keepdims=True) + EPS) * n1_ref[...]
    x = _dg(xin.astype(BF16), wrw_ref[...], NN)
    tm = x.shape[0]
    i = pl.program_id(0)

    @pl.when(i == 0)
    def _():
        carry[...] = jnp.zeros_like(carry)

    rowi = _iota((tm, 1), 0)
    prev = pltpu.roll(x, 1, 0)
    prev = jnp.where(rowi == 0, carry[0:1, :], prev)
    seq_start = ((rowi + i * tm) & (seq - 1)) == 0
    if has_first:
        prev = jnp.where(seq_start, first_ref[...], prev)
    else:
        prev = jnp.where(seq_start, 0.0, prev)
    carry[0:1, :] = x[tm - 1:tm, :]
    x = x + (prev - x) * mu_ref[...]
    r = x[:, 0:d]
    k = x[:, d:2 * d]
    v = x[:, 2 * d:3 * d]
    wlo = x[:, 3 * d:3 * d + LORA_W]
    alo = x[:, 3 * d + LORA_W:3 * d + LORA_W + LORA_A]
    glo = x[:, 3 * d + LORA_W + LORA_A:]
    seg = seg_ref[...]
    z = -(w0_ref[...] + _mm(jnp.tanh(wlo), w2_ref[...]))
    softplus = jnp.maximum(z, 0.0) + jnp.log(1.0 + jnp.exp(-jnp.abs(z)))
    lw_o[...] = -jnp.exp(-softplus - 0.5)
    a = _sigmoid(a0_ref[...] + _mm(alo, a2_ref[...]))
    g_o[...] = _mm(_sigmoid(glo), g2_ref[...])
    kk = k * kk_ref[...]
    kk = kk / jnp.maximum(jnp.sqrt(_seg_sum(kk * kk, seg)), 1e-12)
    k2 = k * (1.0 + (a - 1.0) * ka_ref[...])
    r_o[...] = r
    k_o[...] = k2
    v_o[...] = v
    ah_o[...] = -kk
    bh_o[...] = kk * a
    bonus_o[...] = _seg_sum(r * k2 * rk_ref[...], seg) * v


def _rwkv_prep(x2, first, p, *, seq, tm):
    n, dm = x2.shape
    wcols = p["w_rw"].shape[1]
    d = H_B * HD_B
    has_first = first is not None
    row = lambda c: pl.BlockSpec((tm, c), lambda i: (i, 0))
    full = lambda a: pl.BlockSpec(a.shape, lambda i: (0, 0))
    small = [p["mu"], p["w0"], p["w2"], p["a0"], p["a2"], p["g2"], p["k_k"], p["k_a"], p["r_k"], p["seg"]]
    args = [x2, p["n1"], p["w_rw"]] + ([first] if has_first else []) + small
    in_specs = ([row(dm), full(p["n1"]), full(p["w_rw"])] + ([row(wcols)] if has_first else [])
                + [full(a) for a in small])
    return pl.pallas_call(
        functools.partial(_rwkv_prep_kernel, seq=seq, has_first=has_first),
        out_shape=tuple(jax.ShapeDtypeStruct((n, d), F32) for _ in range(8)),
        grid=(n // tm,),
        in_specs=in_specs,
        out_specs=tuple(row(d) for _ in range(8)),
        scratch_shapes=[pltpu.VMEM((SUBLANES, wcols), F32)],
        compiler_params=_params(("arbitrary",)),
        name="rwkv_prep",
    )(*args)


def _tri_inv(a_bds, n, rows, passes):
    mm = lambda xs, ys: [_mm(x, y, NN, passes) for x, y in zip(xs, ys)]
    row = _iota((n, n), 0)
    col = _iota((n, n), 1)
    t = row & (rows - 1)
    i = col & (rows - 1)
    eye = (row == col).astype(F32)
    bshift = RWKV_BASE.bit_length() - 1
    base = (t >> bshift) == (i >> bshift)
    d1 = [jnp.where(base, a, 0.0) for a in a_bds]
    n1 = [eye + d for d in d1]
    d2 = mm(d1, d1)
    both = mm([jnp.concatenate([x, y], axis=0) for x, y in zip(n1, d2)], d2)
    n2 = [x + y[:n] for x, y in zip(n1, both)]
    d4 = [y[n:] for y in both]
    tinv = [x + y for x, y in zip(n2, mm(n2, d4))]
    m = RWKV_BASE
    while m < rows:
        s = m.bit_length() - 1
        tb = t >> s
        ib = i >> s
        sib = ((tb >> 1) == (ib >> 1)) & ((tb & 1) == 1) & ((ib & 1) == 0)
        off = [jnp.where(sib, a, 0.0) for a in a_bds]
        nb = n // m

        def odd(x, m=m, nb=nb):
            return jnp.concatenate([x[j * m:(j + 1) * m] for j in range(1, nb, 2)], axis=0)

        def spread(y, m=m, nb=nb):
            zero = jnp.zeros((m, n), F32)
            return jnp.concatenate([p for j in range(nb // 2) for p in (zero, y[j * m:(j + 1) * m])], axis=0)

        upd = mm(mm([odd(x) for x in tinv], off), tinv)
        tinv = [x + spread(y) for x, y in zip(tinv, upd)]
        m *= 2
    return tinv


def _rwkv_chunks(ins, sts, *, rows, t_valid, passes):
    w = ins[0][0].shape[1]
    n = RWKV_GROUP * rows
    if t_valid < rows:
        valid = _iota((rows, 1), 0) < t_valid
        ins = [tuple(jnp.where(valid, x, 0.0) for x in u) for u in ins]
    r, lw, k, v, ah, bh = ([u[j] for u in ins] for j in range(6))
    lrow = _iota((rows, rows), 0)
    lcol = _iota((rows, rows), 1)
    tril = (lcol <= lrow).astype(BF16)
    hs = HD_B.bit_length() - 1
    rs = rows.bit_length() - 1
    bd_mask = (_iota((n, w), 0) >> rs) == (_iota((n, w), 1) >> hs)
    nmask = (_iota((n, n), 0) >> rs) == (_iota((n, n), 1) >> rs)
    t_idx = _iota((rows, 2 * n), 0)
    i_idx = _iota((rows, 2 * n), 1) & (rows - 1)
    strict = i_idx < t_idx
    incl = i_idx <= t_idx

    def expand(x):
        return jnp.where(bd_mask, jnp.concatenate([x] * RWKV_GROUP, axis=0), 0.0)

    cum = [_mm_exact_lhs(tril, x) for x in lw]
    gl = [c[rows - 1:rows] for c in cum]
    e_neg = [jnp.exp(-c) for c in cum]
    e_end = [jnp.exp(g - c) for g, c in zip(gl, cum)]
    qa = [a * jnp.exp(c - l) for a, c, l in zip(ah, cum, lw)]
    qr = [x * jnp.exp(c) for x, c in zip(r, cum)]
    ke = [jnp.concatenate([expand(b * e), expand(x * e)], axis=0) for b, x, e in zip(bh, k, e_neg)]
    kge = [jnp.concatenate([expand(b * e), expand(x * e)], axis=0) for b, x, e in zip(bh, k, e_end)]
    q2 = [jnp.concatenate([a, x], axis=0) for a, x in zip(qa, qr)]
    scores = [_mm(q, x, NT, passes) for q, x in zip(q2, ke)]
    a_a = [jnp.where(strict, s[:rows], 0.0) for s in scores]
    a_r = [jnp.where(incl, s[rows:], 0.0) for s in scores]
    v_bd = [expand(x) for x in v]
    akv = [_mm(jnp.concatenate([a[:, n:], b[:, n:]], axis=0), x, NN, passes)
           for a, b, x in zip(a_a, a_r, v_bd)]
    w_kv = [x[:rows] for x in akv]
    a_bd =[jnp.where(nmask, jnp.concatenate([a[:, :n]] * RWKV_GROUP, axis=0), 0.0) for a in a_a]
    tinv = _tri_inv(a_bd, n, rows, passes)
    carried = [_mm(q, st, NT, passes) for q, st in zip(q2, sts)]
    p1 = [x[:rows] for x in carried]
    y_carry = [x[rows:] for x in carried]
    c_bd = [_mm(t, expand(p + x), NN, passes) for t, p, x in zip(tinv, p1, w_kv)]
    cv = [jnp.concatenate([c, x], axis=0) for c, x in zip(c_bd, v_bd)]
    st_new = [st * jnp.exp(g) + _mm(c, x, TN, passes) for st, g, c, x in zip(sts, gl, cv, kge)]
    ys = [yc + x[rows:] + _mm(a[:, :n], c, NN, passes) for yc, x, a, c in zip(y_carry, akv, a_r, c_bd)]
    return ys, st_new


def _rwkv_scan_kernel(*refs, chunk, n_chunks, t_valid, has_s0, passes, sb):
    if has_s0:
        r_ref, lw_ref, k_ref, v_ref, ah_ref, bh_ref, s0_ref, y_ref, s_ref, st_scr = refs
    else:
        r_ref, lw_ref, k_ref, v_ref, ah_ref, bh_ref, y_ref, s_ref, st_scr = refs
    ti = pl.program_id(1)
    w = RWKV_GROUP * HD_B
    groups = H_B // RWKV_GROUP
    units = [(s, g) for s in range(sb) for g in range(groups)]
    blk = lambda h: slice(h * HD_B, (h + 1) * HD_B)

    @pl.when(ti == 0)
    def _():
        st_scr[...] = jnp.zeros_like(st_scr)
        if has_s0:
            for s, g in units:
                for h in range(RWKV_GROUP):
                    st_scr[s, g, blk(h), blk(h)] = s0_ref[s, g * RWKV_GROUP + h]

    def body(c, carry):
        if n_chunks == 1:
            sl = slice(None)
        else:
            sl = pl.ds(pl.multiple_of(c * chunk, chunk), chunk)
        cols = lambda g: slice(g * w, (g + 1) * w)
        in_refs = (r_ref, lw_ref, k_ref, v_ref, ah_ref, bh_ref)
        ins = [tuple(_pad_rows(ref[s, sl, cols(g)], chunk) for ref in in_refs) for s, g in units]
        ys, st_new = _rwkv_chunks(ins, [st_scr[s, g] for s, g in units],
                                  rows=chunk, t_valid=t_valid, passes=passes)
        for (s, g), y, st in zip(units, ys, st_new):
            st_scr[s, g] = st
            y_ref[s, sl, cols(g)] = y[:t_valid]
        return carry

    if n_chunks == 1:
        body(0, 0)
    else:
        lax.fori_loop(0, n_chunks, body, 0)

    @pl.when(ti == pl.num_programs(1) - 1)
    def _():
        for s, g in units:
            for h in range(RWKV_GROUP):
                s_ref[s, g * RWKV_GROUP + h] = st_scr[s, g, blk(h), blk(h)]


def _rwkv_scan(r, lw, k, v, ah, bh, s0, *, batch, seq, tt, passes, sb):
    has_s0 = s0 is not None
    w = RWKV_GROUP * HD_B
    d_b = H_B * HD_B
    chunk = min(RWKV_CHUNK, max(seq, SUBLANES))
    tt = min(tt, seq)
    n_chunks = max(tt // chunk, 1)
    t_valid = min(chunk, seq)
    tok = pl.BlockSpec((sb, tt, d_b), lambda b, i: (b, i, 0))
    st_spec = pl.BlockSpec((sb, H_B, HD_B, HD_B), lambda b, i: (b, 0, 0, 0))
    args = [r, lw, k, v, ah, bh]
    in_specs = [tok] * 6
    if has_s0:
        args.append(s0)
        in_specs.append(st_spec)
    return pl.pallas_call(
        functools.partial(_rwkv_scan_kernel, chunk=chunk, n_chunks=n_chunks, t_valid=t_valid,
                          has_s0=has_s0, passes=passes, sb=sb),
        out_shape=(jax.ShapeDtypeStruct((batch, seq, d_b), F32),
                   jax.ShapeDtypeStruct((batch, H_B, HD_B, HD_B), F32)),
        grid=(batch // sb, seq // tt),
        in_specs=in_specs,
        out_specs=(tok, st_spec),
        scratch_shapes=[pltpu.VMEM((sb, H_B // RWKV_GROUP, w, w), F32)],
        compiler_params=_params(("arbitrary", "arbitrary")),
        name="rwkv_scan",
    )(*args)


def _merge_kernel(*refs, blocks_a):
    (lnw_ref, lnb_ref, seg_ref, wa_ref, wb_ref, wo_ref, n2_ref, rwt_ref, rb_ref,
     x1_o, xn_o, idx_o, gate_o, cnt_o) = refs[14:]
    from_b = pl.program_id(0) >= blocks_a

    @pl.when(pl.program_id(0) == 0)
    def _():
        cnt_o[...] = jnp.zeros_like(cnt_o)

    def body(group):
        y, bonus, g, oa, ga, gb, x = (ref[...] for ref in group)
        seg = seg_ref[...]
        inv = 1.0 / HD_B
        mu = _seg_sum(y, seg) * inv
        dlt = y - mu
        var = _seg_sum(dlt * dlt, seg) * inv
        yn = dlt * lax.rsqrt(var + RWKV_GN_EPS) * lnw_ref[...] + lnb_ref[...]
        ob = (yn + bonus) * g
        merged = _sigmoid(ga) * _mm(oa, wa_ref[...]) + _sigmoid(gb) * _mm(ob, wb_ref[...])
        x1 = x + _mm(merged, wo_ref[...])
        x1_o[...] = x1
        xn = x1 * lax.rsqrt(jnp.mean(x1 * x1, axis=-1, keepdims=True) + EPS) * n2_ref[...]
        _tok_store(xn_o, xn)
        logits = _mm(rwt_ref[...], xn, NT, 3) + rb_ref[...]
        tm = x1.shape[0]
        eio = _iota((N_EXPERTS, tm), 0)
        idx_rows, val_rows = [], []
        cnt = jnp.zeros((N_EXPERTS, 1), F32)
        for _ in range(TOP_K):
            mx = jnp.max(logits, axis=0, keepdims=True)
            pick = jnp.min(jnp.where(logits == mx, eio, N_EXPERTS), axis=0, keepdims=True)
            hit = eio == pick
            cnt = cnt + jnp.sum(hit.astype(F32), axis=1, keepdims=True)
            logits = jnp.where(hit, -jnp.inf, logits)
            idx_rows.append(pick)
            val_rows.append(mx)
        ex = [jnp.exp(vr - val_rows[0]) for vr in val_rows]
        den = ex[0] + ex[1] + ex[2] + ex[3]
        idx_o[...] = _stack_rows(idx_rows)
        gate_o[...] = _stack_rows([e / den for e in ex])
        cnt_o[...] += jnp.broadcast_to(cnt, cnt_o.shape)

    pl.when(jnp.logical_not(from_b))(lambda: body(refs[0:7]))
    pl.when(from_b)(lambda: body(refs[7:14]))


def _merge(tok_a, tok_b, p, *, tm):
    n_a, d = tok_a[5].shape
    n_b = tok_b[5].shape[0]
    n = n_a + n_b
    blocks_a = n_a // tm
    last_a = blocks_a - 1
    ia = lambda i: jnp.minimum(i, last_a)
    ib = lambda i: jnp.maximum(i - blocks_a, 0)

    def group_specs(tok, im, mode=None):
        y, _, _, oa, gates, _ = tok
        dm = gates.shape[1] // 2
        spec = lambda c, j: pl.BlockSpec((tm, c), lambda i: (im(i), j), pipeline_mode=mode)
        return [spec(y.shape[1], 0), spec(y.shape[1], 0), spec(y.shape[1], 0), spec(oa.shape[1], 0),
                spec(dm, 0), spec(dm, 1), spec(d, 0)]

    def group_args(tok):
        y, bonus, g, oa, gates, x = tok
        return [y, bonus, g, oa, gates, gates, x]

    full = lambda a: pl.BlockSpec(a.shape, lambda i: (0, 0))
    small = [p["ln_w"], p["ln_b"], p["seg"], p["wa"], p["wb"], p["wo"], p["n2"], p["rwt"], p["rb"]]
    out_row = pl.BlockSpec((tm, d), lambda i: (i, 0))
    out_col = pl.BlockSpec((SUBLANES, tm), lambda i: (0, i))
    return pl.pallas_call(
        functools.partial(_merge_kernel, blocks_a=blocks_a),
        out_shape=(jax.ShapeDtypeStruct((n, d), F32), jax.ShapeDtypeStruct((n * SUBLANES, LANES), F32),
                   jax.ShapeDtypeStruct((SUBLANES, n), jnp.int32),
                   jax.ShapeDtypeStruct((SUBLANES, n), F32),
                   jax.ShapeDtypeStruct((N_EXPERTS, LANES), F32)),
        grid=(n // tm,),
        in_specs=(group_specs(tok_a, ia) + group_specs(tok_b, ib, pl.Buffered(1))
                  + [full(a) for a in small]),
        out_specs=(out_row, pl.BlockSpec((tm * SUBLANES, LANES), lambda i: (i, 0)), out_col, out_col,
                   pl.BlockSpec((N_EXPERTS, LANES), lambda i: (0, 0))),
        compiler_params=_params(("arbitrary",)),
        name="merge_router",
    )(*(group_args(tok_a) + group_args(tok_b) + small))


def _rank_kernel(idx_ref, pstart_ref, dest_o, carry):
    i = pl.program_id(0)

    @pl.when(i == 0)
    def _():
        carry[...] = pstart_ref[...]

    tm = idx_ref.shape[1]
    eio = _iota((N_EXPERTS, tm), 0)
    idx = idx_ref[...]
    hits = [eio == idx[j:j + 1, :] for j in range(TOP_K)]
    sel = hits[0] | hits[1] | hits[2] | hits[3]
    upper = (_iota((tm, tm), 0) < _iota((tm, tm), 1)).astype(BF16)
    rank = _dg(sel.astype(BF16), upper, NN) + carry[:, 0:1]
    rows = [jnp.sum(jnp.where(h, rank, 0.0), axis=0, keepdims=True) for h in hits]
    dest_o[...] = _stack_rows(rows).astype(jnp.int32)
    carry[...] += jnp.sum(sel.astype(F32), axis=1, keepdims=True)


def _rank(idx4, pstart, *, tm):
    n = idx4.shape[1]
    return pl.pallas_call(
        _rank_kernel,
        out_shape=jax.ShapeDtypeStruct((SUBLANES, n), jnp.int32),
        grid=(n // tm,),
        in_specs=[pl.BlockSpec((SUBLANES, tm), lambda i: (0, i)),
                  pl.BlockSpec((N_EXPERTS, LANES), lambda i: (0, 0))],
        out_specs=pl.BlockSpec((SUBLANES, tm), lambda i: (0, i)),
        scratch_shapes=[pltpu.VMEM((N_EXPERTS, LANES), F32)],
        compiler_params=_params(("arbitrary",)),
        name="moe_rank",
    )(idx4, pstart)


def _scatter_kernel(pad_lo, pad_hi, na_ref, dest_ref, x_ref, xs_hbm, zeros, sem, zsem):
    tm = x_ref.shape[0]
    zrows = zeros.shape[0]
    nblk = xs_hbm.shape[0] // zrows

    @pl.when(pl.program_id(0) == 0)
    def _():
        zeros[...] = jnp.zeros_like(zeros)
        zrun = lambda r, size: pltpu.make_async_copy(zeros.at[pl.ds(0, size)], xs_hbm.at[pl.ds(r, size)], zsem)

        def per_expert(op):
            def f(e, c):
                lo = pad_lo[e]
                n_pad = pad_hi[e] - lo
                for bit in reversed(range((zrows - 1).bit_length())):
                    @pl.when(((n_pad >> bit) & 1) == 1)
                    def _(lo=lo, bit=bit):
                        op(zrun(lo, 1 << bit))
                    lo = lo + (n_pad & (1 << bit))
                return c
            return f

        lax.fori_loop(0, N_EXPERTS, per_expert(lambda cp: cp.start()), 0)
        lax.fori_loop(na_ref[0], nblk, lambda b, c: (zrun(b * zrows, zrows).start(), c)[1], 0)
        lax.fori_loop(0, N_EXPERTS, per_expert(lambda cp: cp.wait()), 0)
        lax.fori_loop(na_ref[0], nblk, lambda b, c: (zrun(b * zrows, zrows).wait(), c)[1], 0)

    def issue(t, c):
        for j in range(TOP_K):
            pltpu.make_async_copy(x_ref.at[t], xs_hbm.at[dest_ref[j, t]], sem).start(priority=j % 2)
        return c

    lax.fori_loop(0, tm, issue, 0, unroll=DMA_UNROLL)
    for j in range(TOP_K):
        pltpu.make_async_copy(x_ref, xs_hbm.at[pl.ds(0, tm)], sem).wait()


def _scatter(pad_lo, pad_hi, n_active, dest4, x3, m_total, *, tm):
    n = x3.shape[0]
    tail = x3.shape[1:]
    return pl.pallas_call(
        _scatter_kernel,
        out_shape=jax.ShapeDtypeStruct((m_total,) + tail, x3.dtype),
        grid_spec=pltpu.PrefetchScalarGridSpec(
            num_scalar_prefetch=3,
            grid=(n // tm,),
            in_specs=[pl.BlockSpec((SUBLANES, tm), lambda i, *_: (0, i), memory_space=pltpu.SMEM),
                      pl.BlockSpec((tm,) + tail, lambda i, *_: (i, 0, 0))],
            out_specs=pl.BlockSpec(memory_space=pl.ANY),
            scratch_shapes=[pltpu.VMEM((MOE_ROWS,) + tail, x3.dtype),
                            pltpu.SemaphoreType.DMA, pltpu.SemaphoreType.DMA]),
        compiler_params=_params(("arbitrary",)),
        name="moe_scatter",
    )(pad_lo, pad_hi, n_active, dest4, x3)


def _expert_kernel(be_ref, na_ref, x_ref, wgu_ref, bgu_ref, wd_ref, bd_ref, y_ref, wgu_bf, wd_perm, wd_bf):
    b = pl.program_id(0)
    active = b < na_ref[0]
    half = LANES // 2
    dff = wd_ref.shape[1]

    @pl.when(active & ((b == 0) | (be_ref[b] != be_ref[jnp.maximum(b - 1, 0)])))
    def _():
        wgu_bf[...] = wgu_ref[0].astype(BF16)
        for c in range(wd_ref.shape[2] // LANES):
            cols = slice(c * LANES, (c + 1) * LANES)
            for p in range(dff // LANES):
                r0 = p * LANES
                wd_perm[pl.ds(r0, half, stride=2), :] = wd_ref[0, r0:r0 + half, cols]
                wd_perm[pl.ds(r0 + 1, half, stride=2), :] = wd_ref[0, r0 + half:r0 + LANES, cols]
            wd_bf[:, cols] = wd_perm[...].astype(BF16)

    @pl.when(active)
    def _():
        x = _tok_load(x_ref).astype(BF16)
        h = _dg(x, wgu_bf[...], NN) + bgu_ref[0]
        even = (_iota((1, LANES), 1) & 1) == 0
        acts = []
        for p in range(dff // LANES):
            ha = h[:, 2 * p * LANES:(2 * p + 1) * LANES]
            hb = h[:, (2 * p + 1) * LANES:(2 * p + 2) * LANES]
            hg = jnp.where(even, ha, pltpu.roll(hb, 1, 1))
            hl = jnp.where(even, pltpu.roll(ha, LANES - 1, 1), hb)
            glu = jnp.minimum(hg, SWIGLU_LIMIT)
            lin = jnp.clip(hl, -SWIGLU_LIMIT, SWIGLU_LIMIT)
            acts.append((glu * _sigmoid(SWIGLU_ALPHA * glu) * (lin + 1.0)).astype(BF16))
        act = jnp.concatenate(acts, axis=1)
        _tok_store(y_ref, _dg(act, wd_bf[...], NN) + bd_ref[0])

    @pl.when(jnp.logical_not(active))
    def _():
        y_ref[...] = jnp.zeros_like(y_ref)


def _experts(block_expert, n_active, xs2, w_gu, b_gu, w_down, b_down):
    d = w_gu.shape[1]
    dff = w_down.shape[1]
    per_tok = d // LANES
    m_total = xs2.shape[0] // per_tok
    nblk = m_total // MOE_ROWS
    wspec = lambda r, c: pl.BlockSpec((1, r, c), lambda b, be, na: (be[b], 0, 0))
    tok = pl.BlockSpec((MOE_ROWS * per_tok, LANES), lambda b, be, na: (b, 0))
    return pl.pallas_call(
        _expert_kernel,
        out_shape=jax.ShapeDtypeStruct(xs2.shape, F32),
        grid_spec=pltpu.PrefetchScalarGridSpec(
            num_scalar_prefetch=2,
            grid=(nblk,),
            in_specs=[tok, wspec(d, 2 * dff), wspec(1, 2 * dff), wspec(dff, d), wspec(1, d)],
            out_specs=tok,
            scratch_shapes=[pltpu.VMEM((d, 2 * dff), BF16), pltpu.VMEM((dff, LANES), F32),
                            pltpu.VMEM((dff, d), BF16)]),
        compiler_params=_params(("arbitrary",)),
        name="moe_experts",
    )(block_expert, n_active, xs2, w_gu, b_gu, w_down, b_down)


def _combine_kernel(dest_ref, next_ref, gate_ref, ys_hbm, x_ref, w_ref, oa_ref, ob_ref, buf, mixed, sems,
                    *, blocks_a):
    tm = x_ref.shape[0]
    i = pl.program_id(0)
    slot = i % 2

    def issue(idx_ref, s):
        def f(t, c):
            for j in range(TOP_K):
                pltpu.make_async_copy(ys_hbm.at[idx_ref[j, t]], buf.at[s, j, t], sems.at[s]).start(priority=j % 2)
            return c
        lax.fori_loop(0, tm, f, 0, unroll=DMA_UNROLL)

    @pl.when(i == 0)
    def _():
        issue(dest_ref, slot)

    @pl.when(i + 1 < pl.num_programs(0))
    def _():
        issue(next_ref, 1 - slot)

    for j in range(TOP_K):
        pltpu.make_async_copy(ys_hbm.at[pl.ds(0, tm)], buf.at[slot, j], sems.at[slot]).wait()

    def mix(t, c):
        acc = gate_ref[0, t] * buf[slot, 0, t]
        for j in range(1, TOP_K):
            acc = acc + gate_ref[j, t] * buf[slot, j, t]
        mixed[pl.ds(pl.multiple_of(t * SUBLANES, SUBLANES), SUBLANES), :] = acc
        return c

    lax.fori_loop(0, tm, mix, 0, unroll=DMA_UNROLL)
    x = x_ref[...] + _tok_load(mixed)
    y = x * lax.rsqrt(jnp.mean(x * x, axis=-1, keepdims=True) + EPS) * w_ref[...]
    first = pl.program_id(0) < blocks_a

    @pl.when(first)
    def _():
        oa_ref[...] = y

    @pl.when(jnp.logical_not(first))
    def _():
        ob_ref[...] = y


def _combine(dest4, gate4, ys3, x1, w, n_a, *, tm):
    n, d = x1.shape
    tail = ys3.shape[1:]
    blocks_a = n_a // tm
    last = n // tm - 1
    smem = lambda im: pl.BlockSpec((SUBLANES, tm), lambda i: (0, im(i)), memory_space=pltpu.SMEM)
    return pl.pallas_call(
        functools.partial(_combine_kernel, blocks_a=blocks_a),
        out_shape=(jax.ShapeDtypeStruct((n_a, d), F32), jax.ShapeDtypeStruct((n - n_a, d), F32)),
        grid=(n // tm,),
        in_specs=[smem(lambda i: i), smem(lambda i: jnp.minimum(i + 1, last)), smem(lambda i: i),
                  pl.BlockSpec(memory_space=pl.ANY),
                  pl.BlockSpec((tm, d), lambda i: (i, 0)), pl.BlockSpec((1, d), lambda i: (0, 0))],
        out_specs=(pl.BlockSpec((tm, d), lambda i: (jnp.minimum(i, blocks_a - 1), 0)),
                   pl.BlockSpec((tm, d), lambda i: (jnp.maximum(i - blocks_a, 0), 0))),
        scratch_shapes=[pltpu.VMEM((2, TOP_K, tm) + tail, F32), pltpu.VMEM((tm * SUBLANES, LANES), F32),
                        pltpu.SemaphoreType.DMA((2,))],
        compiler_params=_params(("arbitrary",)),
        name="moe_combine",
    )(dest4, dest4, gate4, ys3, x1, w)


def _mixers(x, s_hgrn, s_rwkv, shift_prev, wts):
    batch, seq, d = x.shape
    n = batch * seq
    x2 = x.reshape(n, d)
    tm = min(PROJ_ROWS, n)
    seqs_per_step = LONG_SEQS_PER_STEP if seq >= HGRN_CHUNK else SHORT_SEQS_PER_STEP
    ph = _norm_proj(x2, wts["n1"], wts["w_h"], normalize=True, tm=tm)
    pg = _norm_proj(x2, wts["n1"], wts["w_g"], normalize=True, tm=tm)
    oa, s_hgrn_new = _hgrn(ph.reshape(batch, seq, -1), wts["lb"], wts["gw"], s_hgrn,
                           batch=batch, seq=seq, tt=256, sb=seqs_per_step)
    first = None
    if shift_prev is not None:
        prev = _norm_proj(shift_prev, wts["n1"], wts["w_rw"], normalize=False, tm=shift_prev.shape[0])
        first = jnp.repeat(prev, seq, axis=0)
    r, lw, k, v, ah, bh, bonus, g = _rwkv_prep(x2, first, wts, seq=seq, tm=min(512, n))
    to3 = lambda a: a.reshape(batch, seq, -1)
    y, s_rwkv_new = _rwkv_scan(to3(r), to3(lw), to3(k), to3(v), to3(ah), to3(bh), s_rwkv,
                               batch=batch, seq=seq, tt=128, passes=1, sb=seqs_per_step)
    return (y.reshape(n, -1), bonus, g, oa.reshape(n, -1), pg, x2), s_hgrn_new, s_rwkv_new


def kernel(x_prompt, x_sample, state_hgrn, state_rwkv, state_shift, norm1_w, w_in, hgrn_lb_logits,
           hgrn_gnorm_w, rwkv_mu, rwkv_w0, rwkv_w2, rwkv_a0, rwkv_a2, rwkv_g2, rwkv_k_k, rwkv_k_a,
           rwkv_r_k, rwkv_ln_w, rwkv_ln_b, w_branch_a, w_branch_b, w_out, norm2_w, router_w, router_b,
           expert_w_gu, expert_b_gu, expert_w_down, expert_b_down, final_norm_w):
    bp, tp, d = x_prompt.shape
    bs, ts, _ = x_sample.shape
    n_p, n_s = bp * tp, bs * ts
    n = n_p + n_s
    d_a = H_A * DK_A
    d_b = H_B * HD_B
    rw_start = 4 * d_a
    gate_start = rw_start + 3 * d_b + LORA_W + LORA_A + LORA_G
    row2 = lambda a: a.reshape(1, -1)
    lb = jnp.cumsum(jax.nn.softmax(hgrn_lb_logits.astype(F32), axis=0), axis=0)[0]
    hid = jnp.arange(RWKV_GROUP * HD_B, dtype=jnp.int32) // HD_B
    wts = {
        "n1": row2(norm1_w[0]),
        "w_h": w_in[0][:, :rw_start].astype(BF16),
        "w_rw": w_in[0][:, rw_start:gate_start].astype(BF16),
        "w_g": w_in[0][:, gate_start:].astype(BF16),
        "lb": row2(lb), "gw": row2(hgrn_gnorm_w[0]),
        "mu": row2(rwkv_mu[0]), "w0": row2(rwkv_w0[0]), "w2": rwkv_w2[0], "a0": row2(rwkv_a0[0]),
        "a2": rwkv_a2[0], "g2": rwkv_g2[0], "k_k": row2(rwkv_k_k[0]), "k_a": row2(rwkv_k_a[0]),
        "r_k": row2(rwkv_r_k[0]), "seg": (hid[:, None] == hid[None, :]).astype(BF16),
        "ln_w": row2(rwkv_ln_w[0]), "ln_b": row2(rwkv_ln_b[0]),
        "wa": w_branch_a[0].astype(BF16), "wb": w_branch_b[0].astype(BF16), "wo": w_out[0].astype(BF16),
        "n2": row2(norm2_w[0]), "rwt": router_w[0].T, "rb": router_b[0].reshape(-1, 1),
    }
    tok_p, hgrn_p, rwkv_p = _mixers(x_prompt, None, None, None, wts)
    tok_s, hgrn_s, rwkv_s = _mixers(x_sample, state_hgrn[0], state_rwkv[0], state_shift[0], wts)
    shift = _rms_rows(jnp.concatenate([x_prompt[:, -1], x_sample[:, -1]], axis=0), wts["n1"])

    x1, xn2, idx4, gate4, counts = _merge(tok_p, tok_s, wts, tm=512)

    cnt = counts[:, 0].astype(jnp.int32)
    padded = (cnt + MOE_ROWS - 1) // MOE_ROWS * MOE_ROWS
    pend = jnp.cumsum(padded)
    pstart = pend - padded
    nblk = (n * TOP_K + N_EXPERTS * (MOE_ROWS - 1) + MOE_ROWS - 1) // MOE_ROWS
    m_total = nblk * MOE_ROWS
    block_row = jnp.arange(nblk, dtype=jnp.int32)[:, None] * MOE_ROWS
    block_expert = jnp.minimum(jnp.sum((pend[None, :] <= block_row).astype(jnp.int32), axis=1), N_EXPERTS - 1)
    n_active = (pend[-1:] // MOE_ROWS).astype(jnp.int32)
    dest4 = _rank(idx4, jnp.broadcast_to(pstart.astype(F32)[:, None], (N_EXPERTS, LANES)), tm=512)

    per_tok = d // LANES
    xs = _scatter(pstart + cnt, pend, n_active, dest4, xn2.reshape(n, per_tok, LANES), m_total, tm=512)
    ys = _experts(block_expert, n_active, xs.reshape(m_total * per_tok, LANES), expert_w_gu[0],
                  expert_b_gu[0][:, None, :], expert_w_down[0], expert_b_down[0][:, None, :])
    y_p, y_s = _combine(dest4, gate4, ys.reshape(m_total, per_tok, LANES), x1, row2(final_norm_w), n_p, tm=512)

    return (y_p.reshape(bp, tp, d), y_s.reshape(bs, ts, d),
            hgrn_p[None], rwkv_p[None], shift[None, :bp],
            hgrn_s[None], rwkv_s[None], shift[None, bp:])
```

```python
import functools

import jax
import jax.numpy as jnp
from jax import lax
from jax.experimental import pallas as pl
from jax.experimental.pallas import tpu as pltpu

F32 = jnp.float32
BF16 = jnp.bfloat16

H_A = 4
DK_A = 128
H_B = 8
HD_B = 64
LORA_W = 64
LORA_A = 64
LORA_G = 128
N_EXPERTS = 32
TOP_K = 4
SWIGLU_LIMIT = 7.0
SWIGLU_ALPHA = 1.702
EPS = 1e-6
RWKV_GN_EPS = 64e-5

LANES = 128
SUBLANES = 8
HGRN_CHUNK = 64
HGRN_SUB = 8
RWKV_CHUNK = 64
RWKV_GROUP = 4
RWKV_BASE = 8
MOE_ROWS = 768
PROJ_ROWS = 1024
TOKEN_TILE = 512
HGRN_STEP_TOKENS = 256
RWKV_STEP_TOKENS = 128
RWKV_PASSES = 1
DMA_UNROLL = 4
LONG_SEQS_PER_STEP = 4
SHORT_SEQS_PER_STEP = 4
VMEM_LIMIT = 56 * 1024 * 1024

NN = (((1,), (0,)), ((), ()))
NT = (((1,), (1,)), ((), ()))
TN = (((0,), (0,)), ((), ()))


def _params(sem):
    return pltpu.CompilerParams(dimension_semantics=sem, vmem_limit_bytes=VMEM_LIMIT)


def _dg(a, b, dims):
    return lax.dot_general(a, b, dims, preferred_element_type=F32)


def _split2(x):
    hi = x.astype(BF16)
    lo = (x - hi.astype(F32)).astype(BF16)
    return hi, lo


def _mm(a, b, dims=NN, passes=1):
    if passes == 1:
        return _dg(a.astype(BF16), b.astype(BF16), dims)
    ah, al = _split2(a)
    bh, bl = _split2(b)
    return _dg(ah, bh, dims) + (_dg(ah, bl, dims) + _dg(al, bh, dims))


def _mm_exact_lhs(sel_bf16, x, dims=NN):
    hi = x.astype(BF16)
    r1 = x - hi.astype(F32)
    mid = r1.astype(BF16)
    lo = (r1 - mid.astype(F32)).astype(BF16)
    return _dg(sel_bf16, hi, dims) + (_dg(sel_bf16, mid, dims) + _dg(sel_bf16, lo, dims))


def _seg_sum(x, seg_bf16):
    w = seg_bf16.shape[0]
    hi, lo = _split2(x)
    parts = [_dg(hi[:, c:c + w], seg_bf16, NN) + _dg(lo[:, c:c + w], seg_bf16, NN)
             for c in range(0, x.shape[1], w)]
    return parts[0] if len(parts) == 1 else jnp.concatenate(parts, axis=1)


def _iota(shape, dim):
    return lax.broadcasted_iota(jnp.int32, shape, dim)


def _sigmoid(x):
    return 1.0 / (1.0 + jnp.exp(-x))


def _tok_load(ref):
    tm = ref.shape[0] // SUBLANES
    return jnp.concatenate([ref[pl.ds(c, tm, stride=SUBLANES), :] for c in range(SUBLANES)], axis=1)


def _tok_store(ref, val):
    tm = val.shape[0]
    for c in range(SUBLANES):
        ref[pl.ds(c, tm, stride=SUBLANES), :] = val[:, c * LANES:(c + 1) * LANES]


def _stack_rows(rows):
    n = rows[0].shape[1]
    sub = _iota((SUBLANES, n), 0)
    out = jnp.zeros((SUBLANES, n), rows[0].dtype)
    for j, r in enumerate(rows):
        out = jnp.where(sub == j, jnp.broadcast_to(r, (SUBLANES, n)), out)
    return out


def _pad_rows(x, rows):
    if x.shape[0] == rows:
        return x
    return jnp.concatenate([x, jnp.zeros((rows - x.shape[0], x.shape[1]), x.dtype)], axis=0)


def _norm_proj_kernel(x_ref, nw_ref, w_ref, o_ref, *, normalize):
    x = x_ref[...]
    if normalize:
        x = x * lax.rsqrt(jnp.mean(x * x, axis=-1, keepdims=True) + EPS) * nw_ref[...]
    o_ref[...] = _dg(x.astype(BF16), w_ref[...], NN)


def _norm_proj(x, nw, w_bf16, *, normalize, tm):
    n, d = x.shape
    nc = w_bf16.shape[1]
    return pl.pallas_call(
        functools.partial(_norm_proj_kernel, normalize=normalize),
        out_shape=jax.ShapeDtypeStruct((n, nc), F32),
        grid=(n // tm,),
        in_specs=[pl.BlockSpec((tm, d), lambda i: (i, 0)),
                  pl.BlockSpec((1, d), lambda i: (0, 0)),
                  pl.BlockSpec((d, nc), lambda i: (0, 0))],
        out_specs=pl.BlockSpec((tm, nc), lambda i: (i, 0)),
        compiler_params=_params(("arbitrary",)),
        name="norm_proj",
    )(x, nw, w_bf16)


def _rms_rows_kernel(x_ref, w_ref, o_ref):
    x = x_ref[...]
    o_ref[...] = x * lax.rsqrt(jnp.mean(x * x, axis=-1, keepdims=True) + EPS) * w_ref[...]


def _rms_rows(x, w):
    return pl.pallas_call(
        _rms_rows_kernel,
        out_shape=jax.ShapeDtypeStruct(x.shape, F32),
        name="rms_rows",
    )(x, w)


def _hgrn_chunks(hqs, hfs, his, lbvs, sts, *, rows, t_valid):
    n_u = len(hqs)
    fs = [lbv + (1.0 - lbv) * _sigmoid(hf) for lbv, hf in zip(lbvs, hfs)]
    gs = [jnp.log(f) for f in fs]
    ks = [1.0 - f for f in fs]
    qs = [hq * _sigmoid(hq) for hq in hqs]
    vs = list(his)
    if t_valid < rows:
        valid = _iota((rows, 1), 0) < t_valid
        gs, ks, qs, vs = ([jnp.where(valid, x, 0.0) for x in xs] for xs in (gs, ks, qs, vs))
    row = _iota((rows, rows), 0)
    col = _iota((rows, rows), 1)
    tril = (col <= row).astype(BF16)
    bs = [_mm_exact_lhs(tril, g) for g in gs]
    sub = min(HGRN_SUB, rows)
    m = rows // 2
    scores = [None] * n_u
    while m >= sub:
        nb = rows // m
        shift = m.bit_length() - 1
        rb = row >> shift
        pair = ((rb & 1) == 1) & ((col >> shift) == rb - 1)
        qes, kes = [], []
        for q, k, b in zip(qs, ks, bs):
            starts = [jnp.zeros((1, DK_A), F32) if j == 0 else b[j * m - 1:j * m] for j in range(nb)]
            ends = [b[j * m + m - 1:j * m + m] for j in range(nb)]
            b_start = jnp.concatenate([jnp.broadcast_to(s, (m, DK_A)) for s in starts], axis=0)
            b_end = jnp.concatenate([jnp.broadcast_to(s, (m, DK_A)) for s in ends], axis=0)
            qes.append(q * jnp.exp(b - b_start))
            kes.append(k * jnp.exp(b_end - b))
        a_ms = [jnp.where(pair, _mm(qe, ke, NT), 0.0) for qe, ke in zip(qes, kes)]
        scores = [a if s is None else s + a for s, a in zip(scores, a_ms)]
        m //= 2
    os_ = [jnp.zeros((rows, DK_A), F32)] * n_u
    if scores[0] is not None:
        os_ = [_mm(s, v, NN) for s, v in zip(scores, vs)]
    rloc = _iota((rows, 1), 0) & (sub - 1)
    os_ = [o + jnp.sum(q * k, axis=-1, keepdims=True) * v for o, q, k, v in zip(os_, qs, ks, vs)]
    for d in range(1, min(sub, t_valid)):
        sh = lambda x: pltpu.roll(x, d, 0)
        es = [jnp.exp(jnp.where(rloc >= d, b - sh(b), -jnp.inf)) for b in bs]
        os_ = [o + jnp.sum(q * sh(k) * e, axis=-1, keepdims=True) * sh(v)
               for o, q, k, v, e in zip(os_, qs, ks, vs, es)]
    b_lasts = [b[rows - 1:rows] for b in bs]
    kds = [k * jnp.exp(bl - b) for k, bl, b in zip(ks, b_lasts, bs)]
    qbs = [q * jnp.exp(b) for q, b in zip(qs, bs)]
    os_ = [o + _mm(qb, st, NT) for o, qb, st in zip(os_, qbs, sts)]
    st_new = [st * jnp.exp(bl) + _mm(v, kd, TN) for st, bl, v, kd in zip(sts, b_lasts, vs, kds)]
    return os_, st_new


def _hgrn_kernel(*refs, chunk, n_chunks, t_valid, has_s0, sb):
    if has_s0:
        ph_ref, lb_ref, gw_ref, s0_ref, o_ref, s_ref, st_scr = refs
    else:
        ph_ref, lb_ref, gw_ref, o_ref, s_ref, st_scr = refs
    ti = pl.program_id(1)
    d_a = H_A * DK_A
    units = [(s, h) for s in range(sb) for h in range(H_A)]

    @pl.when(ti == 0)
    def _():
        for s, h in units:
            st_scr[s, h] = s0_ref[s, h].T if has_s0 else jnp.zeros((DK_A, DK_A), F32)

    gw = gw_ref[...]

    def body(c, carry):
        if n_chunks == 1:
            sl = slice(None)
        else:
            sl = pl.ds(pl.multiple_of(c * chunk, chunk), chunk)
        cols = lambda part, h: slice(part * d_a + h * DK_A, part * d_a + (h + 1) * DK_A)
        part = lambda p: [_pad_rows(ph_ref[s, sl, cols(p, h)], chunk) for s, h in units]
        os_, st_new = _hgrn_chunks(part(0), part(1), part(2),
                                   [lb_ref[:, h * DK_A:(h + 1) * DK_A] for _, h in units],
                                   [st_scr[s, h] for s, h in units], rows=chunk, t_valid=t_valid)
        for (s, h), o, st in zip(units, os_, st_new):
            st_scr[s, h] = st
            hg = ph_ref[s, sl, cols(3, h)]
            o = o[:t_valid]
            o = o * lax.rsqrt(jnp.mean(o * o, axis=-1, keepdims=True) + EPS) * gw
            o_ref[s, sl, h * DK_A:(h + 1) * DK_A] = o * (hg * _sigmoid(hg))
        return carry

    if n_chunks == 1:
        body(0, 0)
    else:
        lax.fori_loop(0, n_chunks, body, 0)

    @pl.when(ti == pl.num_programs(1) - 1)
    def _():
        for s, h in units:
            s_ref[s, h] = st_scr[s, h].T


def _hgrn(ph, lb, gw, s0, *, batch, seq, tt, sb):
    has_s0 = s0 is not None
    chunk = min(HGRN_CHUNK, max(seq, SUBLANES))
    tt = min(tt, seq)
    n_chunks = max(tt // chunk, 1)
    t_valid = min(chunk, seq)
    d_a = H_A * DK_A
    st_spec = pl.BlockSpec((sb, H_A, DK_A, DK_A), lambda b, i: (b, 0, 0, 0))
    in_specs = [pl.BlockSpec((sb, tt, 4 * d_a), lambda b, i: (b, i, 0)),
                pl.BlockSpec((1, d_a), lambda b, i: (0, 0)),
                pl.BlockSpec((1, DK_A), lambda b, i: (0, 0))]
    args = [ph, lb, gw]
    if has_s0:
        in_specs.append(st_spec)
        args.append(s0)
    return pl.pallas_call(
        functools.partial(_hgrn_kernel, chunk=chunk, n_chunks=n_chunks, t_valid=t_valid, has_s0=has_s0, sb=sb),
        out_shape=(jax.ShapeDtypeStruct((batch, seq, d_a), F32),
                   jax.ShapeDtypeStruct((batch, H_A, DK_A, DK_A), F32)),
        grid=(batch // sb, seq // tt),
        in_specs=in_specs,
        out_specs=(pl.BlockSpec((sb, tt, d_a), lambda b, i: (b, i, 0)), st_spec),
        scratch_shapes=[pltpu.VMEM((sb, H_A, DK_A, DK_A), F32)],
        compiler_params=_params(("arbitrary", "arbitrary")),
        name="hgrn_scan",
    )(*args)


def _rwkv_prep_kernel(*refs, seq, has_first):
    if has_first:
        (x_ref, n1_ref, wrw_ref, first_ref, mu_ref, w0_ref, w2_ref, a0_ref, a2_ref, g2_ref, kk_ref, ka_ref,
         rk_ref, seg_ref, r_o, lw_o, k_o, v_o, ah_o, bh_o, bonus_o, g_o, carry) = refs
    else:
        (x_ref, n1_ref, wrw_ref, mu_ref, w0_ref, w2_ref, a0_ref, a2_ref, g2_ref, kk_ref, ka_ref,
         rk_ref, seg_ref, r_o, lw_o, k_o, v_o, ah_o, bh_o, bonus_o, g_o, carry) = refs
    d = r_o.shape[1]
    xin = x_ref[...]
    xin = xin * lax.rsqrt(jnp.mean(xin * xin, axis=-1, keepdims=True) + EPS) * n1_ref[...]
    x = _dg(xin.astype(BF16), wrw_ref[...], NN)
    tm = x.shape[0]
    i = pl.program_id(0)

    @pl.when(i == 0)
    def _():
        carry[...] = jnp.zeros_like(carry)

    rowi = _iota((tm, 1), 0)
    prev = pltpu.roll(x, 1, 0)
    prev = jnp.where(rowi == 0, carry[0:1, :], prev)
    seq_start = ((rowi + i * tm) & (seq - 1)) == 0
    if has_first:
        prev = jnp.where(seq_start, first_ref[...], prev)
    else:
        prev = jnp.where(seq_start, 0.0, prev)
    carry[0:1, :] = x[tm - 1:tm, :]
    x = x + (prev - x) * mu_ref[...]
    r = x[:, 0:d]
    k = x[:, d:2 * d]
    v = x[:, 2 * d:3 * d]
    wlo = x[:, 3 * d:3 * d + LORA_W]
    alo = x[:, 3 * d + LORA_W:3 * d + LORA_W + LORA_A]
    glo = x[:, 3 * d + LORA_W + LORA_A:]
    seg = seg_ref[...]
    z = -(w0_ref[...] + _mm(jnp.tanh(wlo), w2_ref[...]))
    softplus = jnp.maximum(z, 0.0) + jnp.log(1.0 + jnp.exp(-jnp.abs(z)))
    lw_o[...] = -jnp.exp(-softplus - 0.5)
    a = _sigmoid(a0_ref[...] + _mm(alo, a2_ref[...]))
    g_o[...] = _mm(_sigmoid(glo), g2_ref[...])
    kk = k * kk_ref[...]
    kk = kk / jnp.maximum(jnp.sqrt(_seg_sum(kk * kk, seg)), 1e-12)
    k2 = k * (1.0 + (a - 1.0) * ka_ref[...])
    r_o[...] = r
    k_o[...] = k2
    v_o[...] = v
    ah_o[...] = -kk
    bh_o[...] = kk * a
    bonus_o[...] = _seg_sum(r * k2 * rk_ref[...], seg) * v


def _rwkv_prep(x2, first, p, *, seq, tm):
    n, dm = x2.shape
    wcols = p["w_rw"].shape[1]
    d = H_B * HD_B
    has_first = first is not None
    row = lambda c: pl.BlockSpec((tm, c), lambda i: (i, 0))
    full = lambda a: pl.BlockSpec(a.shape, lambda i: (0, 0))
    small = [p["mu"], p["w0"], p["w2"], p["a0"], p["a2"], p["g2"], p["k_k"], p["k_a"], p["r_k"], p["seg"]]
    args = [x2, p["n1"], p["w_rw"]] + ([first] if has_first else []) + small
    in_specs = ([row(dm), full(p["n1"]), full(p["w_rw"])] + ([row(wcols)] if has_first else [])
                + [full(a) for a in small])
    return pl.pallas_call(
        functools.partial(_rwkv_prep_kernel, seq=seq, has_first=has_first),
        out_shape=tuple(jax.ShapeDtypeStruct((n, d), F32) for _ in range(8)),
        grid=(n // tm,),
        in_specs=in_specs,
        out_specs=tuple(row(d) for _ in range(8)),
        scratch_shapes=[pltpu.VMEM((SUBLANES, wcols), F32)],
        compiler_params=_params(("arbitrary",)),
        name="rwkv_prep",
    )(*args)


def _tri_inv(a_bds, n, rows, passes):
    mm = lambda xs, ys: [_mm(x, y, NN, passes) for x, y in zip(xs, ys)]
    row = _iota((n, n), 0)
    col = _iota((n, n), 1)
    t = row & (rows - 1)
    i = col & (rows - 1)
    eye = (row == col).astype(F32)
    bshift = RWKV_BASE.bit_length() - 1
    base = (t >> bshift) == (i >> bshift)
    d1 = [jnp.where(base, a, 0.0) for a in a_bds]
    n1 = [eye + d for d in d1]
    d2 = mm(d1, d1)
    both = mm([jnp.concatenate([x, y], axis=0) for x, y in zip(n1, d2)], d2)
    n2 = [x + y[:n] for x, y in zip(n1, both)]
    d4 = [y[n:] for y in both]
    tinv = [x + y for x, y in zip(n2, mm(n2, d4))]
    m = RWKV_BASE
    while m < rows:
        s = m.bit_length() - 1
        tb = t >> s
        ib = i >> s
        sib = ((tb >> 1) == (ib >> 1)) & ((tb & 1) == 1) & ((ib & 1) == 0)
        off = [jnp.where(sib, a, 0.0) for a in a_bds]
        nb = n // m

        def odd(x, m=m, nb=nb):
            return jnp.concatenate([x[j * m:(j + 1) * m] for j in range(1, nb, 2)], axis=0)

        def spread(y, m=m, nb=nb):
            zero = jnp.zeros((m, n), F32)
            return jnp.concatenate([p for j in range(nb // 2) for p in (zero, y[j * m:(j + 1) * m])], axis=0)

        upd = mm(mm([odd(x) for x in tinv], off), tinv)
        tinv = [x + spread(y) for x, y in zip(tinv, upd)]
        m *= 2
    return tinv


def _rwkv_chunks(ins, sts, *, rows, t_valid, passes):
    w = ins[0][0].shape[1]
    n = RWKV_GROUP * rows
    if t_valid < rows:
        valid = _iota((rows, 1), 0) < t_valid
        ins = [tuple(jnp.where(valid, x, 0.0) for x in u) for u in ins]
    r, lw, k, v, ah, bh = ([u[j] for u in ins] for j in range(6))
    lrow = _iota((rows, rows), 0)
    lcol = _iota((rows, rows), 1)
    tril = (lcol <= lrow).astype(BF16)
    hs = HD_B.bit_length() - 1
    rs = rows.bit_length() - 1
    bd_mask = (_iota((n, w), 0) >> rs) == (_iota((n, w), 1) >> hs)
    nmask = (_iota((n, n), 0) >> rs) == (_iota((n, n), 1) >> rs)
    t_idx = _iota((rows, 2 * n), 0)
    i_idx = _iota((rows, 2 * n), 1) & (rows - 1)
    strict = i_idx < t_idx
    incl = i_idx <= t_idx

    def expand(x):
        return jnp.where(bd_mask, jnp.concatenate([x] * RWKV_GROUP, axis=0), 0.0)

    cum = [_mm_exact_lhs(tril, x) for x in lw]
    gl = [c[rows - 1:rows] for c in cum]
    e_neg = [jnp.exp(-c) for c in cum]
    e_end = [jnp.exp(g - c) for g, c in zip(gl, cum)]
    qa = [a * jnp.exp(c - l) for a, c, l in zip(ah, cum, lw)]
    qr = [x * jnp.exp(c) for x, c in zip(r, cum)]
    ke = [jnp.concatenate([expand(b * e), expand(x * e)], axis=0) for b, x, e in zip(bh, k, e_neg)]
    kge = [jnp.concatenate([expand(b * e), expand(x * e)], axis=0) for b, x, e in zip(bh, k, e_end)]
    q2 = [jnp.concatenate([a, x], axis=0) for a, x in zip(qa, qr)]
    scores = [_mm(q, x, NT, passes) for q, x in zip(q2, ke)]
    a_a = [jnp.where(strict, s[:rows], 0.0) for s in scores]
    a_r = [jnp.where(incl, s[rows:], 0.0) for s in scores]
    v_bd = [expand(x) for x in v]
    akv = [_mm(jnp.concatenate([a[:, n:], b[:, n:]], axis=0), x, NN, passes)
           for a, b, x in zip(a_a, a_r, v_bd)]
    w_kv = [x[:rows] for x in akv]
    a_bd =[jnp.where(nmask, jnp.concatenate([a[:, :n]] * RWKV_GROUP, axis=0), 0.0) for a in a_a]
    tinv = _tri_inv(a_bd, n, rows, passes)
    carried = [_mm(q, st, NT, passes) for q, st in zip(q2, sts)]
    p1 = [x[:rows] for x in carried]
    y_carry = [x[rows:] for x in carried]
    c_bd = [_mm(t, expand(p + x), NN, passes) for t, p, x in zip(tinv, p1, w_kv)]
    cv = [jnp.concatenate([c, x], axis=0) for c, x in zip(c_bd, v_bd)]
    st_new = [st * jnp.exp(g) + _mm(c, x, TN, passes) for st, g, c, x in zip(sts, gl, cv, kge)]
    ys = [yc + x[rows:] + _mm(a[:, :n], c, NN, passes) for yc, x, a, c in zip(y_carry, akv, a_r, c_bd)]
    return ys, st_new


def _rwkv_scan_kernel(*refs, chunk, n_chunks, t_valid, has_s0, passes, sb):
    if has_s0:
        r_ref, lw_ref, k_ref, v_ref, ah_ref, bh_ref, s0_ref, y_ref, s_ref, st_scr = refs
    else:
        r_ref, lw_ref, k_ref, v_ref, ah_ref, bh_ref, y_ref, s_ref, st_scr = refs
    ti = pl.program_id(1)
    w = RWKV_GROUP * HD_B
    groups = H_B // RWKV_GROUP
    units = [(s, g) for s in range(sb) for g in range(groups)]
    blk = lambda h: slice(h * HD_B, (h + 1) * HD_B)

    @pl.when(ti == 0)
    def _():
        st_scr[...] = jnp.zeros_like(st_scr)
        if has_s0:
            for s, g in units:
                for h in range(RWKV_GROUP):
                    st_scr[s, g, blk(h), blk(h)] = s0_ref[s, g * RWKV_GROUP + h]

    def body(c, carry):
        if n_chunks == 1:
            sl = slice(None)
        else:
            sl = pl.ds(pl.multiple_of(c * chunk, chunk), chunk)
        cols = lambda g: slice(g * w, (g + 1) * w)
        in_refs = (r_ref, lw_ref, k_ref, v_ref, ah_ref, bh_ref)
        ins = [tuple(_pad_rows(ref[s, sl, cols(g)], chunk) for ref in in_refs) for s, g in units]
        ys, st_new = _rwkv_chunks(ins, [st_scr[s, g] for s, g in units],
                                  rows=chunk, t_valid=t_valid, passes=passes)
        for (s, g), y, st in zip(units, ys, st_new):
            st_scr[s, g] = st
            y_ref[s, sl, cols(g)] = y[:t_valid]
        return carry

    if n_chunks == 1:
        body(0, 0)
    else:
        lax.fori_loop(0, n_chunks, body, 0)

    @pl.when(ti == pl.num_programs(1) - 1)
    def _():
        for s, g in units:
            for h in range(RWKV_GROUP):
                s_ref[s, g * RWKV_GROUP + h] = st_scr[s, g, blk(h), blk(h)]


def _rwkv_scan(r, lw, k, v, ah, bh, s0, *, batch, seq, tt, passes, sb):
    has_s0 = s0 is not None
    w = RWKV_GROUP * HD_B
    d_b = H_B * HD_B
    chunk = min(RWKV_CHUNK, max(seq, SUBLANES))
    tt = min(tt, seq)
    n_chunks = max(tt // chunk, 1)
    t_valid = min(chunk, seq)
    tok = pl.BlockSpec((sb, tt, d_b), lambda b, i: (b, i, 0))
    st_spec = pl.BlockSpec((sb, H_B, HD_B, HD_B), lambda b, i: (b, 0, 0, 0))
    args = [r, lw, k, v, ah, bh]
    in_specs = [tok] * 6
    if has_s0:
        args.append(s0)
        in_specs.append(st_spec)
    return pl.pallas_call(
        functools.partial(_rwkv_scan_kernel, chunk=chunk, n_chunks=n_chunks, t_valid=t_valid,
                          has_s0=has_s0, passes=passes, sb=sb),
        out_shape=(jax.ShapeDtypeStruct((batch, seq, d_b), F32),
                   jax.ShapeDtypeStruct((batch, H_B, HD_B, HD_B), F32)),
        grid=(batch // sb, seq // tt),
        in_specs=in_specs,
        out_specs=(tok, st_spec),
        scratch_shapes=[pltpu.VMEM((sb, H_B // RWKV_GROUP, w, w), F32)],
        compiler_params=_params(("arbitrary", "arbitrary")),
        name="rwkv_scan",
    )(*args)


def _merge_kernel(*refs, blocks_a):
    (lnw_ref, lnb_ref, seg_ref, wa_ref, wb_ref, wo_ref, n2_ref, rwt_ref, rb_ref,
     x1_o, xn_o, idx_o, gate_o, cnt_o) = refs[14:]
    from_b = pl.program_id(0) >= blocks_a

    @pl.when(pl.program_id(0) == 0)
    def _():
        cnt_o[...] = jnp.zeros_like(cnt_o)

    def body(group):
        y, bonus, g, oa, ga, gb, x = (ref[...] for ref in group)
        seg = seg_ref[...]
        inv = 1.0 / HD_B
        mu = _seg_sum(y, seg) * inv
        dlt = y - mu
        var = _seg_sum(dlt * dlt, seg) * inv
        yn = dlt * lax.rsqrt(var + RWKV_GN_EPS) * lnw_ref[...] + lnb_ref[...]
        ob = (yn + bonus) * g
        merged = _sigmoid(ga) * _mm(oa, wa_ref[...]) + _sigmoid(gb) * _mm(ob, wb_ref[...])
        x1 = x + _mm(merged, wo_ref[...])
        x1_o[...] = x1
        xn = x1 * lax.rsqrt(jnp.mean(x1 * x1, axis=-1, keepdims=True) + EPS) * n2_ref[...]
        _tok_store(xn_o, xn)
        logits = _mm(rwt_ref[...], xn, NT, 3) + rb_ref[...]
        tm = x1.shape[0]
        eio = _iota((N_EXPERTS, tm), 0)
        idx_rows, val_rows = [], []
        cnt = jnp.zeros((N_EXPERTS, 1), F32)
        for _ in range(TOP_K):
            mx = jnp.max(logits, axis=0, keepdims=True)
            pick = jnp.min(jnp.where(logits == mx, eio, N_EXPERTS), axis=0, keepdims=True)
            hit = eio == pick
            cnt = cnt + jnp.sum(hit.astype(F32), axis=1, keepdims=True)
            logits = jnp.where(hit, -jnp.inf, logits)
            idx_rows.append(pick)
            val_rows.append(mx)
        ex = [jnp.exp(vr - val_rows[0]) for vr in val_rows]
        den = ex[0] + ex[1] + ex[2] + ex[3]
        idx_o[...] = _stack_rows(idx_rows)
        gate_o[...] = _stack_rows([e / den for e in ex])
        cnt_o[...] += jnp.broadcast_to(cnt, cnt_o.shape)

    pl.when(jnp.logical_not(from_b))(lambda: body(refs[0:7]))
    pl.when(from_b)(lambda: body(refs[7:14]))


def _merge(tok_a, tok_b, p, *, tm):
    n_a, d = tok_a[5].shape
    n_b = tok_b[5].shape[0]
    n = n_a + n_b
    blocks_a = n_a // tm
    last_a = blocks_a - 1
    ia = lambda i: jnp.minimum(i, last_a)
    ib = lambda i: jnp.maximum(i - blocks_a, 0)

    def group_specs(tok, im, mode=None):
        y, _, _, oa, gates, _ = tok
        dm = gates.shape[1] // 2
        spec = lambda c, j: pl.BlockSpec((tm, c), lambda i: (im(i), j), pipeline_mode=mode)
        return [spec(y.shape[1], 0), spec(y.shape[1], 0), spec(y.shape[1], 0), spec(oa.shape[1], 0),
                spec(dm, 0), spec(dm, 1), spec(d, 0)]

    def group_args(tok):
        y, bonus, g, oa, gates, x = tok
        return [y, bonus, g, oa, gates, gates, x]

    full = lambda a: pl.BlockSpec(a.shape, lambda i: (0, 0))
    small = [p["ln_w"], p["ln_b"], p["seg"], p["wa"], p["wb"], p["wo"], p["n2"], p["rwt"], p["rb"]]
    out_row = pl.BlockSpec((tm, d), lambda i: (i, 0))
    out_col = pl.BlockSpec((SUBLANES, tm), lambda i: (0, i))
    return pl.pallas_call(
        functools.partial(_merge_kernel, blocks_a=blocks_a),
        out_shape=(jax.ShapeDtypeStruct((n, d), F32), jax.ShapeDtypeStruct((n * SUBLANES, LANES), F32),
                   jax.ShapeDtypeStruct((SUBLANES, n), jnp.int32),
                   jax.ShapeDtypeStruct((SUBLANES, n), F32),
                   jax.ShapeDtypeStruct((N_EXPERTS, LANES), F32)),
        grid=(n // tm,),
        in_specs=(group_specs(tok_a, ia) + group_specs(tok_b, ib, pl.Buffered(1))
                  + [full(a) for a in small]),
        out_specs=(out_row, pl.BlockSpec((tm * SUBLANES, LANES), lambda i: (i, 0)), out_col, out_col,
                   pl.BlockSpec((N_EXPERTS, LANES), lambda i: (0, 0))),
        compiler_params=_params(("arbitrary",)),
        name="merge_router",
    )(*(group_args(tok_a) + group_args(tok_b) + small))


def _rank_kernel(idx_ref, pstart_ref, dest_o, carry):
    i = pl.program_id(0)

    @pl.when(i == 0)
    def _():
        carry[...] = pstart_ref[...]

    tm = idx_ref.shape[1]
    eio = _iota((N_EXPERTS, tm), 0)
    idx = idx_ref[...]
    hits = [eio == idx[j:j + 1, :] for j in range(TOP_K)]
    sel = hits[0] | hits[1] | hits[2] | hits[3]
    upper = (_iota((tm, tm), 0) < _iota((tm, tm), 1)).astype(BF16)
    rank = _dg(sel.astype(BF16), upper, NN) + carry[:, 0:1]
    rows = [jnp.sum(jnp.where(h, rank, 0.0), axis=0, keepdims=True) for h in hits]
    dest_o[...] = _stack_rows(rows).astype(jnp.int32)
    carry[...] += jnp.sum(sel.astype(F32), axis=1, keepdims=True)


def _rank(idx4, pstart, *, tm):
    n = idx4.shape[1]
    return pl.pallas_call(
        _rank_kernel,
        out_shape=jax.ShapeDtypeStruct((SUBLANES, n), jnp.int32),
        grid=(n // tm,),
        in_specs=[pl.BlockSpec((SUBLANES, tm), lambda i: (0, i)),
                  pl.BlockSpec((N_EXPERTS, LANES), lambda i: (0, 0))],
        out_specs=pl.BlockSpec((SUBLANES, tm), lambda i: (0, i)),
        scratch_shapes=[pltpu.VMEM((N_EXPERTS, LANES), F32)],
        compiler_params=_params(("arbitrary",)),
        name="moe_rank",
    )(idx4, pstart)


def _scatter_kernel(pad_lo, pad_hi, na_ref, dest_ref, x_ref, xs_hbm, zeros, sem, zsem):
    tm = x_ref.shape[0]
    zrows = zeros.shape[0]
    nblk = xs_hbm.shape[0] // zrows

    @pl.when(pl.program_id(0) == 0)
    def _():
        zeros[...] = jnp.zeros_like(zeros)
        zrun = lambda r, size: pltpu.make_async_copy(zeros.at[pl.ds(0, size)], xs_hbm.at[pl.ds(r, size)], zsem)

        def per_expert(op):
            def f(e, c):
                lo = pad_lo[e]
                n_pad = pad_hi[e] - lo
                for bit in reversed(range((zrows - 1).bit_length())):
                    @pl.when(((n_pad >> bit) & 1) == 1)
                    def _(lo=lo, bit=bit):
                        op(zrun(lo, 1 << bit))
                    lo = lo + (n_pad & (1 << bit))
                return c
            return f

        lax.fori_loop(0, N_EXPERTS, per_expert(lambda cp: cp.start()), 0)
        lax.fori_loop(na_ref[0], nblk, lambda b, c: (zrun(b * zrows, zrows).start(), c)[1], 0)
        lax.fori_loop(0, N_EXPERTS, per_expert(lambda cp: cp.wait()), 0)
        lax.fori_loop(na_ref[0], nblk, lambda b, c: (zrun(b * zrows, zrows).wait(), c)[1], 0)

    def issue(t, c):
        for j in range(TOP_K):
            pltpu.make_async_copy(x_ref.at[t], xs_hbm.at[dest_ref[j, t]], sem).start(priority=j % 2)
        return c

    lax.fori_loop(0, tm, issue, 0, unroll=DMA_UNROLL)
    for j in range(TOP_K):
        pltpu.make_async_copy(x_ref, xs_hbm.at[pl.ds(0, tm)], sem).wait()


def _scatter(pad_lo, pad_hi, n_active, dest4, x3, m_total, *, tm):
    n = x3.shape[0]
    tail = x3.shape[1:]
    return pl.pallas_call(
        _scatter_kernel,
        out_shape=jax.ShapeDtypeStruct((m_total,) + tail, x3.dtype),
        grid_spec=pltpu.PrefetchScalarGridSpec(
            num_scalar_prefetch=3,
            grid=(n // tm,),
            in_specs=[pl.BlockSpec((SUBLANES, tm), lambda i, *_: (0, i), memory_space=pltpu.SMEM),
                      pl.BlockSpec((tm,) + tail, lambda i, *_: (i, 0, 0))],
            out_specs=pl.BlockSpec(memory_space=pl.ANY),
            scratch_shapes=[pltpu.VMEM((MOE_ROWS,) + tail, x3.dtype),
                            pltpu.SemaphoreType.DMA, pltpu.SemaphoreType.DMA]),
        compiler_params=_params(("arbitrary",)),
        name="moe_scatter",
    )(pad_lo, pad_hi, n_active, dest4, x3)


def _expert_kernel(be_ref, na_ref, x_ref, wgu_ref, bgu_ref, wd_ref, bd_ref, y_ref, wgu_bf, wd_perm, wd_bf):
    b = pl.program_id(0)
    active = b < na_ref[0]
    half = LANES // 2
    dff = wd_ref.shape[1]

    @pl.when(active & ((b == 0) | (be_ref[b] != be_ref[jnp.maximum(b - 1, 0)])))
    def _():
        wgu_bf[...] = wgu_ref[0].astype(BF16)
        for c in range(wd_ref.shape[2] // LANES):
            cols = slice(c * LANES, (c + 1) * LANES)
            for p in range(dff // LANES):
                r0 = p * LANES
                wd_perm[pl.ds(r0, half, stride=2), :] = wd_ref[0, r0:r0 + half, cols]
                wd_perm[pl.ds(r0 + 1, half, stride=2), :] = wd_ref[0, r0 + half:r0 + LANES, cols]
            wd_bf[:, cols] = wd_perm[...].astype(BF16)

    @pl.when(active)
    def _():
        x = _tok_load(x_ref).astype(BF16)
        h = _dg(x, wgu_bf[...], NN) + bgu_ref[0]
        even = (_iota((1, LANES), 1) & 1) == 0
        acts = []
        for p in range(dff // LANES):
            ha = h[:, 2 * p * LANES:(2 * p + 1) * LANES]
            hb = h[:, (2 * p + 1) * LANES:(2 * p + 2) * LANES]
            hg = jnp.where(even, ha, pltpu.roll(hb, 1, 1))
            hl = jnp.where(even, pltpu.roll(ha, LANES - 1, 1), hb)
            glu = jnp.minimum(hg, SWIGLU_LIMIT)
            lin = jnp.clip(hl, -SWIGLU_LIMIT, SWIGLU_LIMIT)
            acts.append((glu * _sigmoid(SWIGLU_ALPHA * glu) * (lin + 1.0)).astype(BF16))
        act = jnp.concatenate(acts, axis=1)
        _tok_store(y_ref, _dg(act, wd_bf[...], NN) + bd_ref[0])

    @pl.when(jnp.logical_not(active))
    def _():
        y_ref[...] = jnp.zeros_like(y_ref)


def _experts(block_expert, n_active, xs2, w_gu, b_gu, w_down, b_down):
    d = w_gu.shape[1]
    dff = w_down.shape[1]
    per_tok = d // LANES
    m_total = xs2.shape[0] // per_tok
    nblk = m_total // MOE_ROWS
    wspec = lambda r, c: pl.BlockSpec((1, r, c), lambda b, be, na: (be[b], 0, 0))
    tok = pl.BlockSpec((MOE_ROWS * per_tok, LANES), lambda b, be, na: (b, 0))
    return pl.pallas_call(
        _expert_kernel,
        out_shape=jax.ShapeDtypeStruct(xs2.shape, F32),
        grid_spec=pltpu.PrefetchScalarGridSpec(
            num_scalar_prefetch=2,
            grid=(nblk,),
            in_specs=[tok, wspec(d, 2 * dff), wspec(1, 2 * dff), wspec(dff, d), wspec(1, d)],
            out_specs=tok,
            scratch_shapes=[pltpu.VMEM((d, 2 * dff), BF16), pltpu.VMEM((dff, LANES), F32),
                            pltpu.VMEM((dff, d), BF16)]),
        compiler_params=_params(("arbitrary",)),
        name="moe_experts",
    )(block_expert, n_active, xs2, w_gu, b_gu, w_down, b_down)


def _combine_kernel(dest_ref, next_ref, gate_ref, ys_hbm, x_ref, w_ref, oa_ref, ob_ref, buf, mixed, sems,
                    *, blocks_a):
    tm = x_ref.shape[0]
    i = pl.program_id(0)
    slot = i % 2

    def issue(idx_ref, s):
        def f(t, c):
            for j in range(TOP_K):
                pltpu.make_async_copy(ys_hbm.at[idx_ref[j, t]], buf.at[s, j, t], sems.at[s]).start(priority=j % 2)
            return c
        lax.fori_loop(0, tm, f, 0, unroll=DMA_UNROLL)

    @pl.when(i == 0)
    def _():
        issue(dest_ref, slot)

    @pl.when(i + 1 < pl.num_programs(0))
    def _():
        issue(next_ref, 1 - slot)

    for j in range(TOP_K):
        pltpu.make_async_copy(ys_hbm.at[pl.ds(0, tm)], buf.at[slot, j], sems.at[slot]).wait()

    def mix(t, c):
        acc = gate_ref[0, t] * buf[slot, 0, t]
        for j in range(1, TOP_K):
            acc = acc + gate_ref[j, t] * buf[slot, j, t]
        mixed[pl.ds(pl.multiple_of(t * SUBLANES, SUBLANES), SUBLANES), :] = acc
        return c

    lax.fori_loop(0, tm, mix, 0, unroll=DMA_UNROLL)
    x = x_ref[...] + _tok_load(mixed)
    y = x * lax.rsqrt(jnp.mean(x * x, axis=-1, keepdims=True) + EPS) * w_ref[...]
    first = pl.program_id(0) < blocks_a

    @pl.when(first)
    def _():
        oa_ref[...] = y

    @pl.when(jnp.logical_not(first))
    def _():
        ob_ref[...] = y


def _combine(dest4, gate4, ys3, x1, w, n_a, *, tm):
    n, d = x1.shape
    tail = ys3.shape[1:]
    blocks_a = n_a // tm
    last = n // tm - 1
    smem = lambda im: pl.BlockSpec((SUBLANES, tm), lambda i: (0, im(i)), memory_space=pltpu.SMEM)
    return pl.pallas_call(
        functools.partial(_combine_kernel, blocks_a=blocks_a),
        out_shape=(jax.ShapeDtypeStruct((n_a, d), F32), jax.ShapeDtypeStruct((n - n_a, d), F32)),
        grid=(n // tm,),
        in_specs=[smem(lambda i: i), smem(lambda i: jnp.minimum(i + 1, last)), smem(lambda i: i),
                  pl.BlockSpec(memory_space=pl.ANY),
                  pl.BlockSpec((tm, d), lambda i: (i, 0)), pl.BlockSpec((1, d), lambda i: (0, 0))],
        out_specs=(pl.BlockSpec((tm, d), lambda i: (jnp.minimum(i, blocks_a - 1), 0)),
                   pl.BlockSpec((tm, d), lambda i: (jnp.maximum(i - blocks_a, 0), 0))),
        scratch_shapes=[pltpu.VMEM((2, TOP_K, tm) + tail, F32), pltpu.VMEM((tm * SUBLANES, LANES), F32),
                        pltpu.SemaphoreType.DMA((2,))],
        compiler_params=_params(("arbitrary",)),
        name="moe_combine",
    )(dest4, dest4, gate4, ys3, x1, w)


def _mixers(x, s_hgrn, s_rwkv, shift_prev, wts):
    batch, seq, d = x.shape
    n = batch * seq
    x2 = x.reshape(n, d)
    tm = min(PROJ_ROWS, n)
    seqs_per_step = LONG_SEQS_PER_STEP if seq >= HGRN_CHUNK else SHORT_SEQS_PER_STEP
    ph = _norm_proj(x2, wts["n1"], wts["w_h"], normalize=True, tm=tm)
    pg = _norm_proj(x2, wts["n1"], wts["w_g"], normalize=True, tm=tm)
    oa, s_hgrn_new = _hgrn(ph.reshape(batch, seq, -1), wts["lb"], wts["gw"], s_hgrn,
                           batch=batch, seq=seq, tt=HGRN_STEP_TOKENS, sb=seqs_per_step)
    first = None
    if shift_prev is not None:
        prev = _norm_proj(shift_prev, wts["n1"], wts["w_rw"], normalize=False, tm=shift_prev.shape[0])
        first = jnp.repeat(prev, seq, axis=0)
    r, lw, k, v, ah, bh, bonus, g = _rwkv_prep(x2, first, wts, seq=seq, tm=min(PROJ_ROWS, n))
    to3 = lambda a: a.reshape(batch, seq, -1)
    y, s_rwkv_new = _rwkv_scan(to3(r), to3(lw), to3(k), to3(v), to3(ah), to3(bh), s_rwkv,
                               batch=batch, seq=seq, tt=RWKV_STEP_TOKENS, passes=RWKV_PASSES, sb=seqs_per_step)
    return (y.reshape(n, -1), bonus, g, oa.reshape(n, -1), pg, x2), s_hgrn_new, s_rwkv_new


def kernel(x_prompt, x_sample, state_hgrn, state_rwkv, state_shift, norm1_w, w_in, hgrn_lb_logits,
           hgrn_gnorm_w, rwkv_mu, rwkv_w0, rwkv_w2, rwkv_a0, rwkv_a2, rwkv_g2, rwkv_k_k, rwkv_k_a,
           rwkv_r_k, rwkv_ln_w, rwkv_ln_b, w_branch_a, w_branch_b, w_out, norm2_w, router_w, router_b,
           expert_w_gu, expert_b_gu, expert_w_down, expert_b_down, final_norm_w):
    bp, tp, d = x_prompt.shape
    bs, ts, _ = x_sample.shape
    n_p, n_s = bp * tp, bs * ts
    n = n_p + n_s
    d_a = H_A * DK_A
    d_b = H_B * HD_B
    rw_start = 4 * d_a
    gate_start = rw_start + 3 * d_b + LORA_W + LORA_A + LORA_G
    row2 = lambda a: a.reshape(1, -1)
    lb = jnp.cumsum(jax.nn.softmax(hgrn_lb_logits.astype(F32), axis=0), axis=0)[0]
    hid = jnp.arange(RWKV_GROUP * HD_B, dtype=jnp.int32) // HD_B
    wts = {
        "n1": row2(norm1_w[0]),
        "w_h": w_in[0][:, :rw_start].astype(BF16),
        "w_rw": w_in[0][:, rw_start:gate_start].astype(BF16),
        "w_g": w_in[0][:, gate_start:].astype(BF16),
        "lb": row2(lb), "gw": row2(hgrn_gnorm_w[0]),
        "mu": row2(rwkv_mu[0]), "w0": row2(rwkv_w0[0]), "w2": rwkv_w2[0], "a0": row2(rwkv_a0[0]),
        "a2": rwkv_a2[0], "g2": rwkv_g2[0], "k_k": row2(rwkv_k_k[0]), "k_a": row2(rwkv_k_a[0]),
        "r_k": row2(rwkv_r_k[0]), "seg": (hid[:, None] == hid[None, :]).astype(BF16),
        "ln_w": row2(rwkv_ln_w[0]), "ln_b": row2(rwkv_ln_b[0]),
        "wa": w_branch_a[0].astype(BF16), "wb": w_branch_b[0].astype(BF16), "wo": w_out[0].astype(BF16),
        "n2": row2(norm2_w[0]), "rwt": router_w[0].T, "rb": router_b[0].reshape(-1, 1),
    }
    tok_p, hgrn_p, rwkv_p = _mixers(x_prompt, None, None, None, wts)
    tok_s, hgrn_s, rwkv_s = _mixers(x_sample, state_hgrn[0], state_rwkv[0], state_shift[0], wts)
    shift = _rms_rows(jnp.concatenate([x_prompt[:, -1], x_sample[:, -1]], axis=0), wts["n1"])

    x1, xn2, idx4, gate4, counts = _merge(tok_p, tok_s, wts, tm=TOKEN_TILE)

    cnt = counts[:, 0].astype(jnp.int32)
    padded = (cnt + MOE_ROWS - 1) // MOE_ROWS * MOE_ROWS
    pend = jnp.cumsum(padded)
    pstart = pend - padded
    nblk = (n * TOP_K + N_EXPERTS * (MOE_ROWS - 1) + MOE_ROWS - 1) // MOE_ROWS
    m_total = nblk * MOE_ROWS
    block_row = jnp.arange(nblk, dtype=jnp.int32)[:, None] * MOE_ROWS
    block_expert = jnp.minimum(jnp.sum((pend[None, :] <= block_row).astype(jnp.int32), axis=1), N_EXPERTS - 1)
    n_active = (pend[-1:] // MOE_ROWS).astype(jnp.int32)
    dest4 = _rank(idx4, jnp.broadcast_to(pstart.astype(F32)[:, None], (N_EXPERTS, LANES)), tm=TOKEN_TILE)

    per_tok = d // LANES
    xs = _scatter(pstart + cnt, pend, n_active, dest4, xn2.reshape(n, per_tok, LANES), m_total, tm=TOKEN_TILE)
    ys = _experts(block_expert, n_active, xs.reshape(m_total * per_tok, LANES), expert_w_gu[0],
                  expert_b_gu[0][:, None, :], expert_w_down[0], expert_b_down[0][:, None, :])
    y_p, y_s = _combine(dest4, gate4, ys.reshape(m_total, per_tok, LANES), x1, row2(final_norm_w), n_p, tm=TOKEN_TILE)

    return (y_p.reshape(bp, tp, d), y_s.reshape(bs, ts, d),
            hgrn_p[None], rwkv_p[None], shift[None, :bp],
            hgrn_s[None], rwkv_s[None], shift[None, bp:])
```

```python
import functools

import jax
import jax.numpy as jnp
from jax import lax
from jax.experimental import pallas as pl
from jax.experimental.pallas import tpu as pltpu

F32 = jnp.float32
BF16 = jnp.bfloat16

H_A = 4
DK_A = 128
H_B = 8
HD_B = 64
LORA_W = 64
LORA_A = 64
LORA_G = 128
N_EXPERTS = 32
TOP_K = 4
SWIGLU_LIMIT = 7.0
SWIGLU_ALPHA = 1.702
EPS = 1e-6
RWKV_GN_EPS = 64e-5

LANES = 128
SUBLANES = 8
HGRN_CHUNK = 64
HGRN_SUB = 8
RWKV_CHUNK = 64
RWKV_GROUP = 4
RWKV_BASE = 8
MOE_ROWS = 768
PROJ_ROWS = 1024
TOKEN_TILE = 512
HGRN_STEP_TOKENS = 256
RWKV_STEP_TOKENS = 128
RWKV_PASSES = 1
DMA_UNROLL = 8
LONG_SEQS_PER_STEP = 4
SHORT_SEQS_PER_STEP = 4
VMEM_LIMIT = 56 * 1024 * 1024

NN = (((1,), (0,)), ((), ()))
NT = (((1,), (1,)), ((), ()))
TN = (((0,), (0,)), ((), ()))


def _params(sem):
    return pltpu.CompilerParams(dimension_semantics=sem, vmem_limit_bytes=VMEM_LIMIT)


def _dg(a, b, dims):
    return lax.dot_general(a, b, dims, preferred_element_type=F32)


def _split2(x):
    hi = x.astype(BF16)
    lo = (x - hi.astype(F32)).astype(BF16)
    return hi, lo


def _mm(a, b, dims=NN, passes=1):
    if passes == 1:
        return _dg(a.astype(BF16), b.astype(BF16), dims)
    ah, al = _split2(a)
    bh, bl = _split2(b)
    return _dg(ah, bh, dims) + (_dg(ah, bl, dims) + _dg(al, bh, dims))


def _mm_exact_lhs(sel_bf16, x, dims=NN):
    hi = x.astype(BF16)
    r1 = x - hi.astype(F32)
    mid = r1.astype(BF16)
    lo = (r1 - mid.astype(F32)).astype(BF16)
    return _dg(sel_bf16, hi, dims) + (_dg(sel_bf16, mid, dims) + _dg(sel_bf16, lo, dims))


def _seg_sum(x, seg_bf16):
    w = seg_bf16.shape[0]
    hi, lo = _split2(x)
    parts = [_dg(hi[:, c:c + w], seg_bf16, NN) + _dg(lo[:, c:c + w], seg_bf16, NN)
             for c in range(0, x.shape[1], w)]
    return parts[0] if len(parts) == 1 else jnp.concatenate(parts, axis=1)


def _iota(shape, dim):
    return lax.broadcasted_iota(jnp.int32, shape, dim)


def _sigmoid(x):
    return 1.0 / (1.0 + jnp.exp(-x))


def _tok_load(ref):
    tm = ref.shape[0] // SUBLANES
    return jnp.concatenate([ref[pl.ds(c, tm, stride=SUBLANES), :] for c in range(SUBLANES)], axis=1)


def _tok_store(ref, val):
    tm = val.shape[0]
    for c in range(SUBLANES):
        ref[pl.ds(c, tm, stride=SUBLANES), :] = val[:, c * LANES:(c + 1) * LANES]


def _stack_rows(rows):
    n = rows[0].shape[1]
    sub = _iota((SUBLANES, n), 0)
    out = jnp.zeros((SUBLANES, n), rows[0].dtype)
    for j, r in enumerate(rows):
        out = jnp.where(sub == j, jnp.broadcast_to(r, (SUBLANES, n)), out)
    return out


def _pad_rows(x, rows):
    if x.shape[0] == rows:
        return x
    return jnp.concatenate([x, jnp.zeros((rows - x.shape[0], x.shape[1]), x.dtype)], axis=0)


def _norm_proj_kernel(x_ref, nw_ref, w_ref, o_ref, *, normalize):
    x = x_ref[...]
    if normalize:
        x = x * lax.rsqrt(jnp.mean(x * x, axis=-1, keepdims=True) + EPS) * nw_ref[...]
    o_ref[...] = _dg(x.astype(BF16), w_ref[...], NN)


def _norm_proj(x, nw, w_bf16, *, normalize, tm):
    n, d = x.shape
    nc = w_bf16.shape[1]
    return pl.pallas_call(
        functools.partial(_norm_proj_kernel, normalize=normalize),
        out_shape=jax.ShapeDtypeStruct((n, nc), F32),
        grid=(n // tm,),
        in_specs=[pl.BlockSpec((tm, d), lambda i: (i, 0)),
                  pl.BlockSpec((1, d), lambda i: (0, 0)),
                  pl.BlockSpec((d, nc), lambda i: (0, 0))],
        out_specs=pl.BlockSpec((tm, nc), lambda i: (i, 0)),
        compiler_params=_params(("arbitrary",)),
        name="norm_proj",
    )(x, nw, w_bf16)


def _rms_rows_kernel(x_ref, w_ref, o_ref):
    x = x_ref[...]
    o_ref[...] = x * lax.rsqrt(jnp.mean(x * x, axis=-1, keepdims=True) + EPS) * w_ref[...]


def _rms_rows(x, w):
    return pl.pallas_call(
        _rms_rows_kernel,
        out_shape=jax.ShapeDtypeStruct(x.shape, F32),
        name="rms_rows",
    )(x, w)


def _hgrn_chunks(hqs, hfs, his, lbvs, sts, *, rows, t_valid):
    n_u = len(hqs)
    fs = [lbv + (1.0 - lbv) * _sigmoid(hf) for lbv, hf in zip(lbvs, hfs)]
    gs = [jnp.log(f) for f in fs]
    ks = [1.0 - f for f in fs]
    qs = [hq * _sigmoid(hq) for hq in hqs]
    vs = list(his)
    if t_valid < rows:
        valid = _iota((rows, 1), 0) < t_valid
        gs, ks, qs, vs = ([jnp.where(valid, x, 0.0) for x in xs] for xs in (gs, ks, qs, vs))
    row = _iota((rows, rows), 0)
    col = _iota((rows, rows), 1)
    tril = (col <= row).astype(BF16)
    bs = [_mm_exact_lhs(tril, g) for g in gs]
    sub = min(HGRN_SUB, rows)
    m = rows // 2
    scores = [None] * n_u
    while m >= sub:
        nb = rows // m
        shift = m.bit_length() - 1
        rb = row >> shift
        pair = ((rb & 1) == 1) & ((col >> shift) == rb - 1)
        qes, kes = [], []
        for q, k, b in zip(qs, ks, bs):
            starts = [jnp.zeros((1, DK_A), F32) if j == 0 else b[j * m - 1:j * m] for j in range(nb)]
            ends = [b[j * m + m - 1:j * m + m] for j in range(nb)]
            b_start = jnp.concatenate([jnp.broadcast_to(s, (m, DK_A)) for s in starts], axis=0)
            b_end = jnp.concatenate([jnp.broadcast_to(s, (m, DK_A)) for s in ends], axis=0)
            qes.append(q * jnp.exp(b - b_start))
            kes.append(k * jnp.exp(b_end - b))
        a_ms = [jnp.where(pair, _mm(qe, ke, NT), 0.0) for qe, ke in zip(qes, kes)]
        scores = [a if s is None else s + a for s, a in zip(scores, a_ms)]
        m //= 2
    os_ = [jnp.zeros((rows, DK_A), F32)] * n_u
    if scores[0] is not None:
        os_ = [_mm(s, v, NN) for s, v in zip(scores, vs)]
    rloc = _iota((rows, 1), 0) & (sub - 1)
    os_ = [o + jnp.sum(q * k, axis=-1, keepdims=True) * v for o, q, k, v in zip(os_, qs, ks, vs)]
    for d in range(1, min(sub, t_valid)):
        sh = lambda x: pltpu.roll(x, d, 0)
        es = [jnp.exp(jnp.where(rloc >= d, b - sh(b), -jnp.inf)) for b in bs]
        os_ = [o + jnp.sum(q * sh(k) * e, axis=-1, keepdims=True) * sh(v)
               for o, q, k, v, e in zip(os_, qs, ks, vs, es)]
    b_lasts = [b[rows - 1:rows] for b in bs]
    kds = [k * jnp.exp(bl - b) for k, bl, b in zip(ks, b_lasts, bs)]
    qbs = [q * jnp.exp(b) for q, b in zip(qs, bs)]
    os_ = [o + _mm(qb, st, NT) for o, qb, st in zip(os_, qbs, sts)]
    st_new = [st * jnp.exp(bl) + _mm(v, kd, TN) for st, bl, v, kd in zip(sts, b_lasts, vs, kds)]
    return os_, st_new


def _hgrn_kernel(*refs, chunk, n_chunks, t_valid, has_s0, sb):
    if has_s0:
        ph_ref, lb_ref, gw_ref, s0_ref, o_ref, s_ref, st_scr = refs
    else:
        ph_ref, lb_ref, gw_ref, o_ref, s_ref, st_scr = refs
    ti = pl.program_id(1)
    d_a = H_A * DK_A
    units = [(s, h) for s in range(sb) for h in range(H_A)]

    @pl.when(ti == 0)
    def _():
        for s, h in units:
            st_scr[s, h] = s0_ref[s, h].T if has_s0 else jnp.zeros((DK_A, DK_A), F32)

    gw = gw_ref[...]

    def body(c, carry):
        if n_chunks == 1:
            sl = slice(None)
        else:
            sl = pl.ds(pl.multiple_of(c * chunk, chunk), chunk)
        cols = lambda part, h: slice(part * d_a + h * DK_A, part * d_a + (h + 1) * DK_A)
        part = lambda p: [_pad_rows(ph_ref[s, sl, cols(p, h)], chunk) for s, h in units]
        os_, st_new = _hgrn_chunks(part(0), part(1), part(2),
                                   [lb_ref[:, h * DK_A:(h + 1) * DK_A] for _, h in units],
                                   [st_scr[s, h] for s, h in units], rows=chunk, t_valid=t_valid)
        for (s, h), o, st in zip(units, os_, st_new):
            st_scr[s, h] = st
            hg = ph_ref[s, sl, cols(3, h)]
            o = o[:t_valid]
            o = o * lax.rsqrt(jnp.mean(o * o, axis=-1, keepdims=True) + EPS) * gw
            o_ref[s, sl, h * DK_A:(h + 1) * DK_A] = o * (hg * _sigmoid(hg))
        return carry

    if n_chunks == 1:
        body(0, 0)
    else:
        lax.fori_loop(0, n_chunks, body, 0)

    @pl.when(ti == pl.num_programs(1) - 1)
    def _():
        for s, h in units:
            s_ref[s, h] = st_scr[s, h].T


def _hgrn(ph, lb, gw, s0, *, batch, seq, tt, sb):
    has_s0 = s0 is not None
    chunk = min(HGRN_CHUNK, max(seq, SUBLANES))
    tt = min(tt, seq)
    n_chunks = max(tt // chunk, 1)
    t_valid = min(chunk, seq)
    d_a = H_A * DK_A
    st_spec = pl.BlockSpec((sb, H_A, DK_A, DK_A), lambda b, i: (b, 0, 0, 0))
    in_specs = [pl.BlockSpec((sb, tt, 4 * d_a), lambda b, i: (b, i, 0)),
                pl.BlockSpec((1, d_a), lambda b, i: (0, 0)),
                pl.BlockSpec((1, DK_A), lambda b, i: (0, 0))]
    args = [ph, lb, gw]
    if has_s0:
        in_specs.append(st_spec)
        args.append(s0)
    return pl.pallas_call(
        functools.partial(_hgrn_kernel, chunk=chunk, n_chunks=n_chunks, t_valid=t_valid, has_s0=has_s0, sb=sb),
        out_shape=(jax.ShapeDtypeStruct((batch, seq, d_a), F32),
                   jax.ShapeDtypeStruct((batch, H_A, DK_A, DK_A), F32)),
        grid=(batch // sb, seq // tt),
        in_specs=in_specs,
        out_specs=(pl.BlockSpec((sb, tt, d_a), lambda b, i: (b, i, 0)), st_spec),
        scratch_shapes=[pltpu.VMEM((sb, H_A, DK_A, DK_A), F32)],
        compiler_params=_params(("arbitrary", "arbitrary")),
        name="hgrn_scan",
    )(*args)


def _rwkv_prep_kernel(*refs, seq, has_first):
    if has_first:
        (x_ref, n1_ref, wrw_ref, first_ref, mu_ref, w0_ref, w2_ref, a0_ref, a2_ref, g2_ref, kk_ref, ka_ref,
         rk_ref, seg_ref, r_o, lw_o, k_o, v_o, ah_o, bh_o, bonus_o, g_o, carry) = refs
    else:
        (x_ref, n1_ref, wrw_ref, mu_ref, w0_ref, w2_ref, a0_ref, a2_ref, g2_ref, kk_ref, ka_ref,
         rk_ref, seg_ref, r_o, lw_o, k_o, v_o, ah_o, bh_o, bonus_o, g_o, carry) = refs
    d = r_o.shape[1]
    xin = x_ref[...]
    xin = xin * lax.rsqrt(jnp.mean(xin * xin, axis=-1, keepdims=True) + EPS) * n1_ref[...]
    x = _dg(xin.astype(BF16), wrw_ref[...], NN)
    tm = x.shape[0]
    i = pl.program_id(0)

    @pl.when(i == 0)
    def _():
        carry[...] = jnp.zeros_like(carry)

    rowi = _iota((tm, 1), 0)
    prev = pltpu.roll(x, 1, 0)
    prev = jnp.where(rowi == 0, carry[0:1, :], prev)
    seq_start = ((rowi + i * tm) & (seq - 1)) == 0
    if has_first:
        prev = jnp.where(seq_start, first_ref[...], prev)
    else:
        prev = jnp.where(seq_start, 0.0, prev)
    carry[0:1, :] = x[tm - 1:tm, :]
    x = x + (prev - x) * mu_ref[...]
    r = x[:, 0:d]
    k = x[:, d:2 * d]
    v = x[:, 2 * d:3 * d]
    wlo = x[:, 3 * d:3 * d + LORA_W]
    alo = x[:, 3 * d + LORA_W:3 * d + LORA_W + LORA_A]
    glo = x[:, 3 * d + LORA_W + LORA_A:]
    seg = seg_ref[...]
    z = -(w0_ref[...] + _mm(jnp.tanh(wlo), w2_ref[...]))
    softplus = jnp.maximum(z, 0.0) + jnp.log(1.0 + jnp.exp(-jnp.abs(z)))
    lw_o[...] = -jnp.exp(-softplus - 0.5)
    a = _sigmoid(a0_ref[...] + _mm(alo, a2_ref[...]))
    g_o[...] = _mm(_sigmoid(glo), g2_ref[...])
    kk = k * kk_ref[...]
    kk = kk / jnp.maximum(jnp.sqrt(_seg_sum(kk * kk, seg)), 1e-12)
    k2 = k * (1.0 + (a - 1.0) * ka_ref[...])
    r_o[...] = r
    k_o[...] = k2
    v_o[...] = v
    ah_o[...] = -kk
    bh_o[...] = kk * a
    bonus_o[...] = _seg_sum(r * k2 * rk_ref[...], seg) * v


def _rwkv_prep(x2, first, p, *, seq, tm):
    n, dm = x2.shape
    wcols = p["w_rw"].shape[1]
    d = H_B * HD_B
    has_first = first is not None
    row = lambda c: pl.BlockSpec((tm, c), lambda i: (i, 0))
    full = lambda a: pl.BlockSpec(a.shape, lambda i: (0, 0))
    small = [p["mu"], p["w0"], p["w2"], p["a0"], p["a2"], p["g2"], p["k_k"], p["k_a"], p["r_k"], p["seg"]]
    args = [x2, p["n1"], p["w_rw"]] + ([first] if has_first else []) + small
    in_specs = ([row(dm), full(p["n1"]), full(p["w_rw"])] + ([row(wcols)] if has_first else [])
                + [full(a) for a in small])
    return pl.pallas_call(
        functools.partial(_rwkv_prep_kernel, seq=seq, has_first=has_first),
        out_shape=tuple(jax.ShapeDtypeStruct((n, d), F32) for _ in range(8)),
        grid=(n // tm,),
        in_specs=in_specs,
        out_specs=tuple(row(d) for _ in range(8)),
        scratch_shapes=[pltpu.VMEM((SUBLANES, wcols), F32)],
        compiler_params=_params(("arbitrary",)),
        name="rwkv_prep",
    )(*args)


def _tri_inv(a_bds, n, rows, passes):
    mm = lambda xs, ys: [_mm(x, y, NN, passes) for x, y in zip(xs, ys)]
    row = _iota((n, n), 0)
    col = _iota((n, n), 1)
    t = row & (rows - 1)
    i = col & (rows - 1)
    eye = (row == col).astype(F32)
    bshift = RWKV_BASE.bit_length() - 1
    base = (t >> bshift) == (i >> bshift)
    d1 = [jnp.where(base, a, 0.0) for a in a_bds]
    n1 = [eye + d for d in d1]
    d2 = mm(d1, d1)
    both = mm([jnp.concatenate([x, y], axis=0) for x, y in zip(n1, d2)], d2)
    n2 = [x + y[:n] for x, y in zip(n1, both)]
    d4 = [y[n:] for y in both]
    tinv = [x + y for x, y in zip(n2, mm(n2, d4))]
    m = RWKV_BASE
    while m < rows:
        s = m.bit_length() - 1
        tb = t >> s
        ib = i >> s
        sib = ((tb >> 1) == (ib >> 1)) & ((tb & 1) == 1) & ((ib & 1) == 0)
        off = [jnp.where(sib, a, 0.0) for a in a_bds]
        nb = n // m

        def odd(x, m=m, nb=nb):
            return jnp.concatenate([x[j * m:(j + 1) * m] for j in range(1, nb, 2)], axis=0)

        def spread(y, m=m, nb=nb):
            zero = jnp.zeros((m, n), F32)
            return jnp.concatenate([p for j in range(nb // 2) for p in (zero, y[j * m:(j + 1) * m])], axis=0)

        upd = mm(mm([odd(x) for x in tinv], off), tinv)
        tinv = [x + spread(y) for x, y in zip(tinv, upd)]
        m *= 2
    return tinv


def _rwkv_chunks(ins, sts, *, rows, t_valid, passes):
    w = ins[0][0].shape[1]
    n = RWKV_GROUP * rows
    if t_valid < rows:
        valid = _iota((rows, 1), 0) < t_valid
        ins = [tuple(jnp.where(valid, x, 0.0) for x in u) for u in ins]
    r, lw, k, v, ah, bh = ([u[j] for u in ins] for j in range(6))
    lrow = _iota((rows, rows), 0)
    lcol = _iota((rows, rows), 1)
    tril = (lcol <= lrow).astype(BF16)
    hs = HD_B.bit_length() - 1
    rs = rows.bit_length() - 1
    bd_mask = (_iota((n, w), 0) >> rs) == (_iota((n, w), 1) >> hs)
    nmask = (_iota((n, n), 0) >> rs) == (_iota((n, n), 1) >> rs)
    t_idx = _iota((rows, 2 * n), 0)
    i_idx = _iota((rows, 2 * n), 1) & (rows - 1)
    strict = i_idx < t_idx
    incl = i_idx <= t_idx

    def expand(x):
        return jnp.where(bd_mask, jnp.concatenate([x] * RWKV_GROUP, axis=0), 0.0)

    cum = [_mm_exact_lhs(tril, x) for x in lw]
    gl = [c[rows - 1:rows] for c in cum]
    e_neg = [jnp.exp(-c) for c in cum]
    e_end = [jnp.exp(g - c) for g, c in zip(gl, cum)]
    qa = [a * jnp.exp(c - l) for a, c, l in zip(ah, cum, lw)]
    qr = [x * jnp.exp(c) for x, c in zip(r, cum)]
    ke = [jnp.concatenate([expand(b * e), expand(x * e)], axis=0) for b, x, e in zip(bh, k, e_neg)]
    kge = [jnp.concatenate([expand(b * e), expand(x * e)], axis=0) for b, x, e in zip(bh, k, e_end)]
    q2 = [jnp.concatenate([a, x], axis=0) for a, x in zip(qa, qr)]
    scores = [_mm(q, x, NT, passes) for q, x in zip(q2, ke)]
    a_a = [jnp.where(strict, s[:rows], 0.0) for s in scores]
    a_r = [jnp.where(incl, s[rows:], 0.0) for s in scores]
    v_bd = [expand(x) for x in v]
    akv = [_mm(jnp.concatenate([a[:, n:], b[:, n:]], axis=0), x, NN, passes)
           for a, b, x in zip(a_a, a_r, v_bd)]
    w_kv = [x[:rows] for x in akv]
    a_bd =[jnp.where(nmask, jnp.concatenate([a[:, :n]] * RWKV_GROUP, axis=0), 0.0) for a in a_a]
    tinv = _tri_inv(a_bd, n, rows, passes)
    carried = [_mm(q, st, NT, passes) for q, st in zip(q2, sts)]
    p1 = [x[:rows] for x in carried]
    y_carry = [x[rows:] for x in carried]
    c_bd = [_mm(t, expand(p + x), NN, passes) for t, p, x in zip(tinv, p1, w_kv)]
    cv = [jnp.concatenate([c, x], axis=0) for c, x in zip(c_bd, v_bd)]
    st_new = [st * jnp.exp(g) + _mm(c, x, TN, passes) for st, g, c, x in zip(sts, gl, cv, kge)]
    ys = [yc + x[rows:] + _mm(a[:, :n], c, NN, passes) for yc, x, a, c in zip(y_carry, akv, a_r, c_bd)]
    return ys, st_new


def _rwkv_scan_kernel(*refs, chunk, n_chunks, t_valid, has_s0, passes, sb):
    if has_s0:
        r_ref, lw_ref, k_ref, v_ref, ah_ref, bh_ref, s0_ref, y_ref, s_ref, st_scr = refs
    else:
        r_ref, lw_ref, k_ref, v_ref, ah_ref, bh_ref, y_ref, s_ref, st_scr = refs
    ti = pl.program_id(1)
    w = RWKV_GROUP * HD_B
    groups = H_B // RWKV_GROUP
    units = [(s, g) for s in range(sb) for g in range(groups)]
    blk = lambda h: slice(h * HD_B, (h + 1) * HD_B)

    @pl.when(ti == 0)
    def _():
        st_scr[...] = jnp.zeros_like(st_scr)
        if has_s0:
            for s, g in units:
                for h in range(RWKV_GROUP):
                    st_scr[s, g, blk(h), blk(h)] = s0_ref[s, g * RWKV_GROUP + h]

    def body(c, carry):
        if n_chunks == 1:
            sl = slice(None)
        else:
            sl = pl.ds(pl.multiple_of(c * chunk, chunk), chunk)
        cols = lambda g: slice(g * w, (g + 1) * w)
        in_refs = (r_ref, lw_ref, k_ref, v_ref, ah_ref, bh_ref)
        ins = [tuple(_pad_rows(ref[s, sl, cols(g)], chunk) for ref in in_refs) for s, g in units]
        ys, st_new = _rwkv_chunks(ins, [st_scr[s, g] for s, g in units],
                                  rows=chunk, t_valid=t_valid, passes=passes)
        for (s, g), y, st in zip(units, ys, st_new):
            st_scr[s, g] = st
            y_ref[s, sl, cols(g)] = y[:t_valid]
        return carry

    if n_chunks == 1:
        body(0, 0)
    else:
        lax.fori_loop(0, n_chunks, body, 0)

    @pl.when(ti == pl.num_programs(1) - 1)
    def _():
        for s, g in units:
            for h in range(RWKV_GROUP):
                s_ref[s, g * RWKV_GROUP + h] = st_scr[s, g, blk(h), blk(h)]


def _rwkv_scan(r, lw, k, v, ah, bh, s0, *, batch, seq, tt, passes, sb):
    has_s0 = s0 is not None
    w = RWKV_GROUP * HD_B
    d_b = H_B * HD_B
    chunk = min(RWKV_CHUNK, max(seq, SUBLANES))
    tt = min(tt, seq)
    n_chunks = max(tt // chunk, 1)
    t_valid = min(chunk, seq)
    tok = pl.BlockSpec((sb, tt, d_b), lambda b, i: (b, i, 0))
    st_spec = pl.BlockSpec((sb, H_B, HD_B, HD_B), lambda b, i: (b, 0, 0, 0))
    args = [r, lw, k, v, ah, bh]
    in_specs = [tok] * 6
    if has_s0:
        args.append(s0)
        in_specs.append(st_spec)
    return pl.pallas_call(
        functools.partial(_rwkv_scan_kernel, chunk=chunk, n_chunks=n_chunks, t_valid=t_valid,
                          has_s0=has_s0, passes=passes, sb=sb),
        out_shape=(jax.ShapeDtypeStruct((batch, seq, d_b), F32),
                   jax.ShapeDtypeStruct((batch, H_B, HD_B, HD_B), F32)),
        grid=(batch // sb, seq // tt),
        in_specs=in_specs,
        out_specs=(tok, st_spec),
        scratch_shapes=[pltpu.VMEM((sb, H_B // RWKV_GROUP, w, w), F32)],
        compiler_params=_params(("arbitrary", "arbitrary")),
        name="rwkv_scan",
    )(*args)


def _merge_kernel(*refs, blocks_a):
    (lnw_ref, lnb_ref, seg_ref, wa_ref, wb_ref, wo_ref, n2_ref, rwt_ref, rb_ref,
     x1_o, xn_o, idx_o, gate_o, cnt_o) = refs[14:]
    from_b = pl.program_id(0) >= blocks_a

    @pl.when(pl.program_id(0) == 0)
    def _():
        cnt_o[...] = jnp.zeros_like(cnt_o)

    def body(group):
        y, bonus, g, oa, ga, gb, x = (ref[...] for ref in group)
        seg = seg_ref[...]
        inv = 1.0 / HD_B
        mu = _seg_sum(y, seg) * inv
        dlt = y - mu
        var = _seg_sum(dlt * dlt, seg) * inv
        yn = dlt * lax.rsqrt(var + RWKV_GN_EPS) * lnw_ref[...] + lnb_ref[...]
        ob = (yn + bonus) * g
        merged = _sigmoid(ga) * _mm(oa, wa_ref[...]) + _sigmoid(gb) * _mm(ob, wb_ref[...])
        x1 = x + _mm(merged, wo_ref[...])
        x1_o[...] = x1
        xn = x1 * lax.rsqrt(jnp.mean(x1 * x1, axis=-1, keepdims=True) + EPS) * n2_ref[...]
        _tok_store(xn_o, xn)
        logits = _mm(rwt_ref[...], xn, NT, 3) + rb_ref[...]
        tm = x1.shape[0]
        eio = _iota((N_EXPERTS, tm), 0)
        idx_rows, val_rows = [], []
        cnt = jnp.zeros((N_EXPERTS, 1), F32)
        for _ in range(TOP_K):
            mx = jnp.max(logits, axis=0, keepdims=True)
            pick = jnp.min(jnp.where(logits == mx, eio, N_EXPERTS), axis=0, keepdims=True)
            hit = eio == pick
            cnt = cnt + jnp.sum(hit.astype(F32), axis=1, keepdims=True)
            logits = jnp.where(hit, -jnp.inf, logits)
            idx_rows.append(pick)
            val_rows.append(mx)
        ex = [jnp.exp(vr - val_rows[0]) for vr in val_rows]
        den = ex[0] + ex[1] + ex[2] + ex[3]
        idx_o[...] = _stack_rows(idx_rows)
        gate_o[...] = _stack_rows([e / den for e in ex])
        cnt_o[...] += jnp.broadcast_to(cnt, cnt_o.shape)

    pl.when(jnp.logical_not(from_b))(lambda: body(refs[0:7]))
    pl.when(from_b)(lambda: body(refs[7:14]))


def _merge(tok_a, tok_b, p, *, tm):
    n_a, d = tok_a[5].shape
    n_b = tok_b[5].shape[0]
    n = n_a + n_b
    blocks_a = n_a // tm
    last_a = blocks_a - 1
    ia = lambda i: jnp.minimum(i, last_a)
    ib = lambda i: jnp.maximum(i - blocks_a, 0)

    def group_specs(tok, im, mode=None):
        y, _, _, oa, gates, _ = tok
        dm = gates.shape[1] // 2
        spec = lambda c, j: pl.BlockSpec((tm, c), lambda i: (im(i), j), pipeline_mode=mode)
        return [spec(y.shape[1], 0), spec(y.shape[1], 0), spec(y.shape[1], 0), spec(oa.shape[1], 0),
                spec(dm, 0), spec(dm, 1), spec(d, 0)]

    def group_args(tok):
        y, bonus, g, oa, gates, x = tok
        return [y, bonus, g, oa, gates, gates, x]

    full = lambda a: pl.BlockSpec(a.shape, lambda i: (0, 0))
    small = [p["ln_w"], p["ln_b"], p["seg"], p["wa"], p["wb"], p["wo"], p["n2"], p["rwt"], p["rb"]]
    out_row = pl.BlockSpec((tm, d), lambda i: (i, 0))
    out_col = pl.BlockSpec((SUBLANES, tm), lambda i: (0, i))
    return pl.pallas_call(
        functools.partial(_merge_kernel, blocks_a=blocks_a),
        out_shape=(jax.ShapeDtypeStruct((n, d), F32), jax.ShapeDtypeStruct((n * SUBLANES, LANES), F32),
                   jax.ShapeDtypeStruct((SUBLANES, n), jnp.int32),
                   jax.ShapeDtypeStruct((SUBLANES, n), F32),
                   jax.ShapeDtypeStruct((N_EXPERTS, LANES), F32)),
        grid=(n // tm,),
        in_specs=(group_specs(tok_a, ia) + group_specs(tok_b, ib, pl.Buffered(1))
                  + [full(a) for a in small]),
        out_specs=(out_row, pl.BlockSpec((tm * SUBLANES, LANES), lambda i: (i, 0)), out_col, out_col,
                   pl.BlockSpec((N_EXPERTS, LANES), lambda i: (0, 0))),
        compiler_params=_params(("arbitrary",)),
        name="merge_router",
    )(*(group_args(tok_a) + group_args(tok_b) + small))


def _rank_kernel(idx_ref, pstart_ref, dest_o, carry):
    i = pl.program_id(0)

    @pl.when(i == 0)
    def _():
        carry[...] = pstart_ref[...]

    tm = idx_ref.shape[1]
    eio = _iota((N_EXPERTS, tm), 0)
    idx = idx_ref[...]
    hits = [eio == idx[j:j + 1, :] for j in range(TOP_K)]
    sel = hits[0] | hits[1] | hits[2] | hits[3]
    upper = (_iota((tm, tm), 0) < _iota((tm, tm), 1)).astype(BF16)
    rank = _dg(sel.astype(BF16), upper, NN) + carry[:, 0:1]
    rows = [jnp.sum(jnp.where(h, rank, 0.0), axis=0, keepdims=True) for h in hits]
    dest_o[...] = _stack_rows(rows).astype(jnp.int32)
    carry[...] += jnp.sum(sel.astype(F32), axis=1, keepdims=True)


def _rank(idx4, pstart, *, tm):
    n = idx4.shape[1]
    return pl.pallas_call(
        _rank_kernel,
        out_shape=jax.ShapeDtypeStruct((SUBLANES, n), jnp.int32),
        grid=(n // tm,),
        in_specs=[pl.BlockSpec((SUBLANES, tm), lambda i: (0, i)),
                  pl.BlockSpec((N_EXPERTS, LANES), lambda i: (0, 0))],
        out_specs=pl.BlockSpec((SUBLANES, tm), lambda i: (0, i)),
        scratch_shapes=[pltpu.VMEM((N_EXPERTS, LANES), F32)],
        compiler_params=_params(("arbitrary",)),
        name="moe_rank",
    )(idx4, pstart)


def _scatter_kernel(pad_lo, pad_hi, na_ref, dest_ref, x_ref, xs_hbm, zeros, sem, zsem):
    tm = x_ref.shape[0]
    zrows = zeros.shape[0]
    nblk = xs_hbm.shape[0] // zrows

    @pl.when(pl.program_id(0) == 0)
    def _():
        zeros[...] = jnp.zeros_like(zeros)
        zrun = lambda r, size: pltpu.make_async_copy(zeros.at[pl.ds(0, size)], xs_hbm.at[pl.ds(r, size)], zsem)

        def per_expert(op):
            def f(e, c):
                lo = pad_lo[e]
                n_pad = pad_hi[e] - lo
                for bit in reversed(range((zrows - 1).bit_length())):
                    @pl.when(((n_pad >> bit) & 1) == 1)
                    def _(lo=lo, bit=bit):
                        op(zrun(lo, 1 << bit))
                    lo = lo + (n_pad & (1 << bit))
                return c
            return f

        lax.fori_loop(0, N_EXPERTS, per_expert(lambda cp: cp.start()), 0)
        lax.fori_loop(na_ref[0], nblk, lambda b, c: (zrun(b * zrows, zrows).start(), c)[1], 0)
        lax.fori_loop(0, N_EXPERTS, per_expert(lambda cp: cp.wait()), 0)
        lax.fori_loop(na_ref[0], nblk, lambda b, c: (zrun(b * zrows, zrows).wait(), c)[1], 0)

    def issue(t, c):
        for j in range(TOP_K):
            pltpu.make_async_copy(x_ref.at[t], xs_hbm.at[dest_ref[j, t]], sem).start(priority=j % 2)
        return c

    lax.fori_loop(0, tm, issue, 0, unroll=DMA_UNROLL)
    for j in range(TOP_K):
        pltpu.make_async_copy(x_ref, xs_hbm.at[pl.ds(0, tm)], sem).wait()


def _scatter(pad_lo, pad_hi, n_active, dest4, x3, m_total, *, tm):
    n = x3.shape[0]
    tail = x3.shape[1:]
    return pl.pallas_call(
        _scatter_kernel,
        out_shape=jax.ShapeDtypeStruct((m_total,) + tail, x3.dtype),
        grid_spec=pltpu.PrefetchScalarGridSpec(
            num_scalar_prefetch=3,
            grid=(n // tm,),
            in_specs=[pl.BlockSpec((SUBLANES, tm), lambda i, *_: (0, i), memory_space=pltpu.SMEM),
                      pl.BlockSpec((tm,) + tail, lambda i, *_: (i, 0, 0))],
            out_specs=pl.BlockSpec(memory_space=pl.ANY),
            scratch_shapes=[pltpu.VMEM((MOE_ROWS,) + tail, x3.dtype),
                            pltpu.SemaphoreType.DMA, pltpu.SemaphoreType.DMA]),
        compiler_params=_params(("arbitrary",)),
        name="moe_scatter",
    )(pad_lo, pad_hi, n_active, dest4, x3)


def _expert_kernel(be_ref, na_ref, x_ref, wgu_ref, bgu_ref, wd_ref, bd_ref, y_ref, wgu_bf, wd_perm, wd_bf):
    b = pl.program_id(0)
    active = b < na_ref[0]
    half = LANES // 2
    dff = wd_ref.shape[1]

    @pl.when(active & ((b == 0) | (be_ref[b] != be_ref[jnp.maximum(b - 1, 0)])))
    def _():
        wgu_bf[...] = wgu_ref[0].astype(BF16)
        for c in range(wd_ref.shape[2] // LANES):
            cols = slice(c * LANES, (c + 1) * LANES)
            for p in range(dff // LANES):
                r0 = p * LANES
                wd_perm[pl.ds(r0, half, stride=2), :] = wd_ref[0, r0:r0 + half, cols]
                wd_perm[pl.ds(r0 + 1, half, stride=2), :] = wd_ref[0, r0 + half:r0 + LANES, cols]
            wd_bf[:, cols] = wd_perm[...].astype(BF16)

    @pl.when(active)
    def _():
        x = _tok_load(x_ref).astype(BF16)
        h = _dg(x, wgu_bf[...], NN) + bgu_ref[0]
        even = (_iota((1, LANES), 1) & 1) == 0
        acts = []
        for p in range(dff // LANES):
            ha = h[:, 2 * p * LANES:(2 * p + 1) * LANES]
            hb = h[:, (2 * p + 1) * LANES:(2 * p + 2) * LANES]
            hg = jnp.where(even, ha, pltpu.roll(hb, 1, 1))
            hl = jnp.where(even, pltpu.roll(ha, LANES - 1, 1), hb)
            glu = jnp.minimum(hg, SWIGLU_LIMIT)
            lin = jnp.clip(hl, -SWIGLU_LIMIT, SWIGLU_LIMIT)
            acts.append((glu * _sigmoid(SWIGLU_ALPHA * glu) * (lin + 1.0)).astype(BF16))
        act = jnp.concatenate(acts, axis=1)
        _tok_store(y_ref, _dg(act, wd_bf[...], NN) + bd_ref[0])

    @pl.when(jnp.logical_not(active))
    def _():
        y_ref[...] = jnp.zeros_like(y_ref)


def _experts(block_expert, n_active, xs2, w_gu, b_gu, w_down, b_down):
    d = w_gu.shape[1]
    dff = w_down.shape[1]
    per_tok = d // LANES
    m_total = xs2.shape[0] // per_tok
    nblk = m_total // MOE_ROWS
    wspec = lambda r, c: pl.BlockSpec((1, r, c), lambda b, be, na: (be[b], 0, 0))
    tok = pl.BlockSpec((MOE_ROWS * per_tok, LANES), lambda b, be, na: (b, 0))
    return pl.pallas_call(
        _expert_kernel,
        out_shape=jax.ShapeDtypeStruct(xs2.shape, F32),
        grid_spec=pltpu.PrefetchScalarGridSpec(
            num_scalar_prefetch=2,
            grid=(nblk,),
            in_specs=[tok, wspec(d, 2 * dff), wspec(1, 2 * dff), wspec(dff, d), wspec(1, d)],
            out_specs=tok,
            scratch_shapes=[pltpu.VMEM((d, 2 * dff), BF16), pltpu.VMEM((dff, LANES), F32),
                            pltpu.VMEM((dff, d), BF16)]),
        compiler_params=_params(("arbitrary",)),
        name="moe_experts",
    )(block_expert, n_active, xs2, w_gu, b_gu, w_down, b_down)


def _combine_kernel(dest_ref, next_ref, gate_ref, ys_hbm, x_ref, w_ref, oa_ref, ob_ref, buf, mixed, sems,
                    *, blocks_a):
    tm = x_ref.shape[0]
    i = pl.program_id(0)
    slot = i % 2

    def issue(idx_ref, s):
        def f(t, c):
            for j in range(TOP_K):
                pltpu.make_async_copy(ys_hbm.at[idx_ref[j, t]], buf.at[s, j, t], sems.at[s]).start(priority=j % 2)
            return c
        lax.fori_loop(0, tm, f, 0, unroll=DMA_UNROLL)

    @pl.when(i == 0)
    def _():
        issue(dest_ref, slot)

    @pl.when(i + 1 < pl.num_programs(0))
    def _():
        issue(next_ref, 1 - slot)

    for j in range(TOP_K):
        pltpu.make_async_copy(ys_hbm.at[pl.ds(0, tm)], buf.at[slot, j], sems.at[slot]).wait()

    def mix(t, c):
        acc = gate_ref[0, t] * buf[slot, 0, t]
        for j in range(1, TOP_K):
            acc = acc + gate_ref[j, t] * buf[slot, j, t]
        mixed[pl.ds(pl.multiple_of(t * SUBLANES, SUBLANES), SUBLANES), :] = acc
        return c

    lax.fori_loop(0, tm, mix, 0, unroll=DMA_UNROLL)
    x = x_ref[...] + _tok_load(mixed)
    y = x * lax.rsqrt(jnp.mean(x * x, axis=-1, keepdims=True) + EPS) * w_ref[...]
    first = pl.program_id(0) < blocks_a

    @pl.when(first)
    def _():
        oa_ref[...] = y

    @pl.when(jnp.logical_not(first))
    def _():
        ob_ref[...] = y


def _combine(dest4, gate4, ys3, x1, w, n_a, *, tm):
    n, d = x1.shape
    tail = ys3.shape[1:]
    blocks_a = n_a // tm
    last = n // tm - 1
    smem = lambda im: pl.BlockSpec((SUBLANES, tm), lambda i: (0, im(i)), memory_space=pltpu.SMEM)
    return pl.pallas_call(
        functools.partial(_combine_kernel, blocks_a=blocks_a),
        out_shape=(jax.ShapeDtypeStruct((n_a, d), F32), jax.ShapeDtypeStruct((n - n_a, d), F32)),
        grid=(n // tm,),
        in_specs=[smem(lambda i: i), smem(lambda i: jnp.minimum(i + 1, last)), smem(lambda i: i),
                  pl.BlockSpec(memory_space=pl.ANY),
                  pl.BlockSpec((tm, d), lambda i: (i, 0)), pl.BlockSpec((1, d), lambda i: (0, 0))],
        out_specs=(pl.BlockSpec((tm, d), lambda i: (jnp.minimum(i, blocks_a - 1), 0)),
                   pl.BlockSpec((tm, d), lambda i: (jnp.maximum(i - blocks_a, 0), 0))),
        scratch_shapes=[pltpu.VMEM((2, TOP_K, tm) + tail, F32), pltpu.VMEM((tm * SUBLANES, LANES), F32),
                        pltpu.SemaphoreType.DMA((2,))],
        compiler_params=_params(("arbitrary",)),
        name="moe_combine",
    )(dest4, dest4, gate4, ys3, x1, w)


def _mixers(x, s_hgrn, s_rwkv, shift_prev, wts):
    batch, seq, d = x.shape
    n = batch * seq
    x2 = x.reshape(n, d)
    tm = min(PROJ_ROWS, n)
    seqs_per_step = LONG_SEQS_PER_STEP if seq >= HGRN_CHUNK else SHORT_SEQS_PER_STEP
    ph = _norm_proj(x2, wts["n1"], wts["w_h"], normalize=True, tm=tm)
    pg = _norm_proj(x2, wts["n1"], wts["w_g"], normalize=True, tm=tm)
    oa, s_hgrn_new = _hgrn(ph.reshape(batch, seq, -1), wts["lb"], wts["gw"], s_hgrn,
                           batch=batch, seq=seq, tt=HGRN_STEP_TOKENS, sb=seqs_per_step)
    first = None
    if shift_prev is not None:
        prev = _norm_proj(shift_prev, wts["n1"], wts["w_rw"], normalize=False, tm=shift_prev.shape[0])
        first = jnp.repeat(prev, seq, axis=0)
    r, lw, k, v, ah, bh, bonus, g = _rwkv_prep(x2, first, wts, seq=seq, tm=min(PROJ_ROWS, n))
    to3 = lambda a: a.reshape(batch, seq, -1)
    y, s_rwkv_new = _rwkv_scan(to3(r), to3(lw), to3(k), to3(v), to3(ah), to3(bh), s_rwkv,
                               batch=batch, seq=seq, tt=RWKV_STEP_TOKENS, passes=RWKV_PASSES, sb=seqs_per_step)
    return (y.reshape(n, -1), bonus, g, oa.reshape(n, -1), pg, x2), s_hgrn_new, s_rwkv_new


def kernel(x_prompt, x_sample, state_hgrn, state_rwkv, state_shift, norm1_w, w_in, hgrn_lb_logits,
           hgrn_gnorm_w, rwkv_mu, rwkv_w0, rwkv_w2, rwkv_a0, rwkv_a2, rwkv_g2, rwkv_k_k, rwkv_k_a,
           rwkv_r_k, rwkv_ln_w, rwkv_ln_b, w_branch_a, w_branch_b, w_out, norm2_w, router_w, router_b,
           expert_w_gu, expert_b_gu, expert_w_down, expert_b_down, final_norm_w):
    bp, tp, d = x_prompt.shape
    bs, ts, _ = x_sample.shape
    n_p, n_s = bp * tp, bs * ts
    n = n_p + n_s
    d_a = H_A * DK_A
    d_b = H_B * HD_B
    rw_start = 4 * d_a
    gate_start = rw_start + 3 * d_b + LORA_W + LORA_A + LORA_G
    row2 = lambda a: a.reshape(1, -1)
    lb = jnp.cumsum(jax.nn.softmax(hgrn_lb_logits.astype(F32), axis=0), axis=0)[0]
    hid = jnp.arange(RWKV_GROUP * HD_B, dtype=jnp.int32) // HD_B
    wts = {
        "n1": row2(norm1_w[0]),
        "w_h": w_in[0][:, :rw_start].astype(BF16),
        "w_rw": w_in[0][:, rw_start:gate_start].astype(BF16),
        "w_g": w_in[0][:, gate_start:].astype(BF16),
        "lb": row2(lb), "gw": row2(hgrn_gnorm_w[0]),
        "mu": row2(rwkv_mu[0]), "w0": row2(rwkv_w0[0]), "w2": rwkv_w2[0], "a0": row2(rwkv_a0[0]),
        "a2": rwkv_a2[0], "g2": rwkv_g2[0], "k_k": row2(rwkv_k_k[0]), "k_a": row2(rwkv_k_a[0]),
        "r_k": row2(rwkv_r_k[0]), "seg": (hid[:, None] == hid[None, :]).astype(BF16),
        "ln_w": row2(rwkv_ln_w[0]), "ln_b": row2(rwkv_ln_b[0]),
        "wa": w_branch_a[0].astype(BF16), "wb": w_branch_b[0].astype(BF16), "wo": w_out[0].astype(BF16),
        "n2": row2(norm2_w[0]), "rwt": router_w[0].T, "rb": router_b[0].reshape(-1, 1),
    }
    tok_p, hgrn_p, rwkv_p = _mixers(x_prompt, None, None, None, wts)
    tok_s, hgrn_s, rwkv_s = _mixers(x_sample, state_hgrn[0], state_rwkv[0], state_shift[0], wts)
    shift = _rms_rows(jnp.concatenate([x_prompt[:, -1], x_sample[:, -1]], axis=0), wts["n1"])

    x1, xn2, idx4, gate4, counts = _merge(tok_p, tok_s, wts, tm=TOKEN_TILE)

    cnt = counts[:, 0].astype(jnp.int32)
    padded = (cnt + MOE_ROWS - 1) // MOE_ROWS * MOE_ROWS
    pend = jnp.cumsum(padded)
    pstart = pend - padded
    nblk = (n * TOP_K + N_EXPERTS * (MOE_ROWS - 1) + MOE_ROWS - 1) // MOE_ROWS
    m_total = nblk * MOE_ROWS
    block_row = jnp.arange(nblk, dtype=jnp.int32)[:, None] * MOE_ROWS
    block_expert = jnp.minimum(jnp.sum((pend[None, :] <= block_row).astype(jnp.int32), axis=1), N_EXPERTS - 1)
    n_active = (pend[-1:] // MOE_ROWS).astype(jnp.int32)
    dest4 = _rank(idx4, jnp.broadcast_to(pstart.astype(F32)[:, None], (N_EXPERTS, LANES)), tm=TOKEN_TILE)

    per_tok = d // LANES
    xs = _scatter(pstart + cnt, pend, n_active, dest4, xn2.reshape(n, per_tok, LANES), m_total, tm=TOKEN_TILE)
    ys = _experts(block_expert, n_active, xs.reshape(m_total * per_tok, LANES), expert_w_gu[0],
                  expert_b_gu[0][:, None, :], expert_w_down[0], expert_b_down[0][:, None, :])
    y_p, y_s = _combine(dest4, gate4, ys.reshape(m_total, per_tok, LANES), x1, row2(final_norm_w), n_p, tm=TOKEN_TILE)

    return (y_p.reshape(bp, tp, d), y_s.reshape(bs, ts, d),
            hgrn_p[None], rwkv_p[None], shift[None, :bp],
            hgrn_s[None], rwkv_s[None], shift[None, bp:])
```

```python
import functools

import jax
import jax.numpy as jnp
from jax import lax
from jax.experimental import pallas as pl
from jax.experimental.pallas import tpu as pltpu

F32 = jnp.float32
BF16 = jnp.bfloat16

H_A = 4
DK_A = 128
H_B = 8
HD_B = 64
LORA_W = 64
LORA_A = 64
LORA_G = 128
N_EXPERTS = 32
TOP_K = 4
SWIGLU_LIMIT = 7.0
SWIGLU_ALPHA = 1.702
EPS = 1e-6
RWKV_GN_EPS = 64e-5

LANES = 128
SUBLANES = 8
HGRN_CHUNK = 64
HGRN_SUB = 8
RWKV_CHUNK = 64
RWKV_GROUP = 4
RWKV_BASE = 8
MOE_ROWS = 768
PROJ_ROWS = 1024
TOKEN_TILE = 512
HGRN_STEP_TOKENS = 256
RWKV_STEP_TOKENS = 128
RWKV_PASSES = 1
DMA_UNROLL = 16
LONG_SEQS_PER_STEP = 4
SHORT_SEQS_PER_STEP = 4
VMEM_LIMIT = 56 * 1024 * 1024

NN = (((1,), (0,)), ((), ()))
NT = (((1,), (1,)), ((), ()))
TN = (((0,), (0,)), ((), ()))


def _params(sem):
    return pltpu.CompilerParams(dimension_semantics=sem, vmem_limit_bytes=VMEM_LIMIT)


def _dg(a, b, dims):
    return lax.dot_general(a, b, dims, preferred_element_type=F32)


def _split2(x):
    hi = x.astype(BF16)
    lo = (x - hi.astype(F32)).astype(BF16)
    return hi, lo


def _mm(a, b, dims=NN, passes=1):
    if passes == 1:
        return _dg(a.astype(BF16), b.astype(BF16), dims)
    ah, al = _split2(a)
    bh, bl = _split2(b)
    return _dg(ah, bh, dims) + (_dg(ah, bl, dims) + _dg(al, bh, dims))


def _mm_exact_lhs(sel_bf16, x, dims=NN):
    hi = x.astype(BF16)
    r1 = x - hi.astype(F32)
    mid = r1.astype(BF16)
    lo = (r1 - mid.astype(F32)).astype(BF16)
    return _dg(sel_bf16, hi, dims) + (_dg(sel_bf16, mid, dims) + _dg(sel_bf16, lo, dims))


def _seg_sum(x, seg_bf16):
    w = seg_bf16.shape[0]
    hi, lo = _split2(x)
    parts = [_dg(hi[:, c:c + w], seg_bf16, NN) + _dg(lo[:, c:c + w], seg_bf16, NN)
             for c in range(0, x.shape[1], w)]
    return parts[0] if len(parts) == 1 else jnp.concatenate(parts, axis=1)


def _iota(shape, dim):
    return lax.broadcasted_iota(jnp.int32, shape, dim)


def _sigmoid(x):
    return 1.0 / (1.0 + jnp.exp(-x))


def _tok_load(ref):
    tm = ref.shape[0] // SUBLANES
    return jnp.concatenate([ref[pl.ds(c, tm, stride=SUBLANES), :] for c in range(SUBLANES)], axis=1)


def _tok_store(ref, val):
    tm = val.shape[0]
    for c in range(SUBLANES):
        ref[pl.ds(c, tm, stride=SUBLANES), :] = val[:, c * LANES:(c + 1) * LANES]


def _stack_rows(rows):
    n = rows[0].shape[1]
    sub = _iota((SUBLANES, n), 0)
    out = jnp.zeros((SUBLANES, n), rows[0].dtype)
    for j, r in enumerate(rows):
        out = jnp.where(sub == j, jnp.broadcast_to(r, (SUBLANES, n)), out)
    return out


def _pad_rows(x, rows):
    if x.shape[0] == rows:
        return x
    return jnp.concatenate([x, jnp.zeros((rows - x.shape[0], x.shape[1]), x.dtype)], axis=0)


def _norm_proj_kernel(x_ref, nw_ref, w_ref, o_ref, *, normalize):
    x = x_ref[...]
    if normalize:
        x = x * lax.rsqrt(jnp.mean(x * x, axis=-1, keepdims=True) + EPS) * nw_ref[...]
    o_ref[...] = _dg(x.astype(BF16), w_ref[...], NN)


def _norm_proj(x, nw, w_bf16, *, normalize, tm):
    n, d = x.shape
    nc = w_bf16.shape[1]
    return pl.pallas_call(
        functools.partial(_norm_proj_kernel, normalize=normalize),
        out_shape=jax.ShapeDtypeStruct((n, nc), F32),
        grid=(n // tm,),
        in_specs=[pl.BlockSpec((tm, d), lambda i: (i, 0)),
                  pl.BlockSpec((1, d), lambda i: (0, 0)),
                  pl.BlockSpec((d, nc), lambda i: (0, 0))],
        out_specs=pl.BlockSpec((tm, nc), lambda i: (i, 0)),
        compiler_params=_params(("arbitrary",)),
        name="norm_proj",
    )(x, nw, w_bf16)


def _rms_rows_kernel(x_ref, w_ref, o_ref):
    x = x_ref[...]
    o_ref[...] = x * lax.rsqrt(jnp.mean(x * x, axis=-1, keepdims=True) + EPS) * w_ref[...]


def _rms_rows(x, w):
    return pl.pallas_call(
        _rms_rows_kernel,
        out_shape=jax.ShapeDtypeStruct(x.shape, F32),
        name="rms_rows",
    )(x, w)


def _hgrn_chunks(hqs, hfs, his, lbvs, sts, *, rows, t_valid):
    n_u = len(hqs)
    fs = [lbv + (1.0 - lbv) * _sigmoid(hf) for lbv, hf in zip(lbvs, hfs)]
    gs = [jnp.log(f) for f in fs]
    ks = [1.0 - f for f in fs]
    qs = [hq * _sigmoid(hq) for hq in hqs]
    vs = list(his)
    if t_valid < rows:
        valid = _iota((rows, 1), 0) < t_valid
        gs, ks, qs, vs = ([jnp.where(valid, x, 0.0) for x in xs] for xs in (gs, ks, qs, vs))
    row = _iota((rows, rows), 0)
    col = _iota((rows, rows), 1)
    tril = (col <= row).astype(BF16)
    bs = [_mm_exact_lhs(tril, g) for g in gs]
    sub = min(HGRN_SUB, rows)
    m = rows // 2
    scores = [None] * n_u
    while m >= sub:
        nb = rows // m
        shift = m.bit_length() - 1
        rb = row >> shift
        pair = ((rb & 1) == 1) & ((col >> shift) == rb - 1)
        qes, kes = [], []
        for q, k, b in zip(qs, ks, bs):
            starts = [jnp.zeros((1, DK_A), F32) if j == 0 else b[j * m - 1:j * m] for j in range(nb)]
            ends = [b[j * m + m - 1:j * m + m] for j in range(nb)]
            b_start = jnp.concatenate([jnp.broadcast_to(s, (m, DK_A)) for s in starts], axis=0)
            b_end = jnp.concatenate([jnp.broadcast_to(s, (m, DK_A)) for s in ends], axis=0)
            qes.append(q * jnp.exp(b - b_start))
            kes.append(k * jnp.exp(b_end - b))
        a_ms = [jnp.where(pair, _mm(qe, ke, NT), 0.0) for qe, ke in zip(qes, kes)]
        scores = [a if s is None else s + a for s, a in zip(scores, a_ms)]
        m //= 2
    os_ = [jnp.zeros((rows, DK_A), F32)] * n_u
    if scores[0] is not None:
        os_ = [_mm(s, v, NN) for s, v in zip(scores, vs)]
    rloc = _iota((rows, 1), 0) & (sub - 1)
    os_ = [o + jnp.sum(q * k, axis=-1, keepdims=True) * v for o, q, k, v in zip(os_, qs, ks, vs)]
    for d in range(1, min(sub, t_valid)):
        sh = lambda x: pltpu.roll(x, d, 0)
        es = [jnp.exp(jnp.where(rloc >= d, b - sh(b), -jnp.inf)) for b in bs]
        os_ = [o + jnp.sum(q * sh(k) * e, axis=-1, keepdims=True) * sh(v)
               for o, q, k, v, e in zip(os_, qs, ks, vs, es)]
    b_lasts = [b[rows - 1:rows] for b in bs]
    kds = [k * jnp.exp(bl - b) for k, bl, b in zip(ks, b_lasts, bs)]
    qbs = [q * jnp.exp(b) for q, b in zip(qs, bs)]
    os_ = [o + _mm(qb, st, NT) for o, qb, st in zip(os_, qbs, sts)]
    st_new = [st * jnp.exp(bl) + _mm(v, kd, TN) for st, bl, v, kd in zip(sts, b_lasts, vs, kds)]
    return os_, st_new


def _hgrn_kernel(*refs, chunk, n_chunks, t_valid, has_s0, sb):
    if has_s0:
        ph_ref, lb_ref, gw_ref, s0_ref, o_ref, s_ref, st_scr = refs
    else:
        ph_ref, lb_ref, gw_ref, o_ref, s_ref, st_scr = refs
    ti = pl.program_id(1)
    d_a = H_A * DK_A
    units = [(s, h) for s in range(sb) for h in range(H_A)]

    @pl.when(ti == 0)
    def _():
        for s, h in units:
            st_scr[s, h] = s0_ref[s, h].T if has_s0 else jnp.zeros((DK_A, DK_A), F32)

    gw = gw_ref[...]

    def body(c, carry):
        if n_chunks == 1:
            sl = slice(None)
        else:
            sl = pl.ds(pl.multiple_of(c * chunk, chunk), chunk)
        cols = lambda part, h: slice(part * d_a + h * DK_A, part * d_a + (h + 1) * DK_A)
        part = lambda p: [_pad_rows(ph_ref[s, sl, cols(p, h)], chunk) for s, h in units]
        os_, st_new = _hgrn_chunks(part(0), part(1), part(2),
                                   [lb_ref[:, h * DK_A:(h + 1) * DK_A] for _, h in units],
                                   [st_scr[s, h] for s, h in units], rows=chunk, t_valid=t_valid)
        for (s, h), o, st in zip(units, os_, st_new):
            st_scr[s, h] = st
            hg = ph_ref[s, sl, cols(3, h)]
            o = o[:t_valid]
            o = o * lax.rsqrt(jnp.mean(o * o, axis=-1, keepdims=True) + EPS) * gw
            o_ref[s, sl, h * DK_A:(h + 1) * DK_A] = o * (hg * _sigmoid(hg))
        return carry

    if n_chunks == 1:
        body(0, 0)
    else:
        lax.fori_loop(0, n_chunks, body, 0)

    @pl.when(ti == pl.num_programs(1) - 1)
    def _():
        for s, h in units:
            s_ref[s, h] = st_scr[s, h].T


def _hgrn(ph, lb, gw, s0, *, batch, seq, tt, sb):
    has_s0 = s0 is not None
    chunk = min(HGRN_CHUNK, max(seq, SUBLANES))
    tt = min(tt, seq)
    n_chunks = max(tt // chunk, 1)
    t_valid = min(chunk, seq)
    d_a = H_A * DK_A
    st_spec = pl.BlockSpec((sb, H_A, DK_A, DK_A), lambda b, i: (b, 0, 0, 0))
    in_specs = [pl.BlockSpec((sb, tt, 4 * d_a), lambda b, i: (b, i, 0)),
                pl.BlockSpec((1, d_a), lambda b, i: (0, 0)),
                pl.BlockSpec((1, DK_A), lambda b, i: (0, 0))]
    args = [ph, lb, gw]
    if has_s0:
        in_specs.append(st_spec)
        args.append(s0)
    return pl.pallas_call(
        functools.partial(_hgrn_kernel, chunk=chunk, n_chunks=n_chunks, t_valid=t_valid, has_s0=has_s0, sb=sb),
        out_shape=(jax.ShapeDtypeStruct((batch, seq, d_a), F32),
                   jax.ShapeDtypeStruct((batch, H_A, DK_A, DK_A), F32)),
        grid=(batch // sb, seq // tt),
        in_specs=in_specs,
        out_specs=(pl.BlockSpec((sb, tt, d_a), lambda b, i: (b, i, 0)), st_spec),
        scratch_shapes=[pltpu.VMEM((sb, H_A, DK_A, DK_A), F32)],
        compiler_params=_params(("arbitrary", "arbitrary")),
        name="hgrn_scan",
    )(*args)


def _rwkv_prep_kernel(*refs, seq, has_first):
    if has_first:
        (x_ref, n1_ref, wrw_ref, first_ref, mu_ref, w0_ref, w2_ref, a0_ref, a2_ref, g2_ref, kk_ref, ka_ref,
         rk_ref, seg_ref, r_o, lw_o, k_o, v_o, ah_o, bh_o, bonus_o, g_o, carry) = refs
    else:
        (x_ref, n1_ref, wrw_ref, mu_ref, w0_ref, w2_ref, a0_ref, a2_ref, g2_ref, kk_ref, ka_ref,
         rk_ref, seg_ref, r_o, lw_o, k_o, v_o, ah_o, bh_o, bonus_o, g_o, carry) = refs
    d = r_o.shape[1]
    xin = x_ref[...]
    xin = xin * lax.rsqrt(jnp.mean(xin * xin, axis=-1, keepdims=True) + EPS) * n1_ref[...]
    x = _dg(xin.astype(BF16), wrw_ref[...], NN)
    tm = x.shape[0]
    i = pl.program_id(0)

    @pl.when(i == 0)
    def _():
        carry[...] = jnp.zeros_like(carry)

    rowi = _iota((tm, 1), 0)
    prev = pltpu.roll(x, 1, 0)
    prev = jnp.where(rowi == 0, carry[0:1, :], prev)
    seq_start = ((rowi + i * tm) & (seq - 1)) == 0
    if has_first:
        prev = jnp.where(seq_start, first_ref[...], prev)
    else:
        prev = jnp.where(seq_start, 0.0, prev)
    carry[0:1, :] = x[tm - 1:tm, :]
    x = x + (prev - x) * mu_ref[...]
    r = x[:, 0:d]
    k = x[:, d:2 * d]
    v = x[:, 2 * d:3 * d]
    wlo = x[:, 3 * d:3 * d + LORA_W]
    alo = x[:, 3 * d + LORA_W:3 * d + LORA_W + LORA_A]
    glo = x[:, 3 * d + LORA_W + LORA_A:]
    seg = seg_ref[...]
    z = -(w0_ref[...] + _mm(jnp.tanh(wlo), w2_ref[...]))
    softplus = jnp.maximum(z, 0.0) + jnp.log(1.0 + jnp.exp(-jnp.abs(z)))
    lw_o[...] = -jnp.exp(-softplus - 0.5)
    a = _sigmoid(a0_ref[...] + _mm(alo, a2_ref[...]))
    g_o[...] = _mm(_sigmoid(glo), g2_ref[...])
    kk = k * kk_ref[...]
    kk = kk / jnp.maximum(jnp.sqrt(_seg_sum(kk * kk, seg)), 1e-12)
    k2 = k * (1.0 + (a - 1.0) * ka_ref[...])
    r_o[...] = r
    k_o[...] = k2
    v_o[...] = v
    ah_o[...] = -kk
    bh_o[...] = kk * a
    bonus_o[...] = _seg_sum(r * k2 * rk_ref[...], seg) * v


def _rwkv_prep(x2, first, p, *, seq, tm):
    n, dm = x2.shape
    wcols = p["w_rw"].shape[1]
    d = H_B * HD_B
    has_first = first is not None
    row = lambda c: pl.BlockSpec((tm, c), lambda i: (i, 0))
    full = lambda a: pl.BlockSpec(a.shape, lambda i: (0, 0))
    small = [p["mu"], p["w0"], p["w2"], p["a0"], p["a2"], p["g2"], p["k_k"], p["k_a"], p["r_k"], p["seg"]]
    args = [x2, p["n1"], p["w_rw"]] + ([first] if has_first else []) + small
    in_specs = ([row(dm), full(p["n1"]), full(p["w_rw"])] + ([row(wcols)] if has_first else [])
                + [full(a) for a in small])
    return pl.pallas_call(
        functools.partial(_rwkv_prep_kernel, seq=seq, has_first=has_first),
        out_shape=tuple(jax.ShapeDtypeStruct((n, d), F32) for _ in range(8)),
        grid=(n // tm,),
        in_specs=in_specs,
        out_specs=tuple(row(d) for _ in range(8)),
        scratch_shapes=[pltpu.VMEM((SUBLANES, wcols), F32)],
        compiler_params=_params(("arbitrary",)),
        name="rwkv_prep",
    )(*args)


def _tri_inv(a_bds, n, rows, passes):
    mm = lambda xs, ys: [_mm(x, y, NN, passes) for x, y in zip(xs, ys)]
    row = _iota((n, n), 0)
    col = _iota((n, n), 1)
    t = row & (rows - 1)
    i = col & (rows - 1)
    eye = (row == col).astype(F32)
    bshift = RWKV_BASE.bit_length() - 1
    base = (t >> bshift) == (i >> bshift)
    d1 = [jnp.where(base, a, 0.0) for a in a_bds]
    n1 = [eye + d for d in d1]
    d2 = mm(d1, d1)
    both = mm([jnp.concatenate([x, y], axis=0) for x, y in zip(n1, d2)], d2)
    n2 = [x + y[:n] for x, y in zip(n1, both)]
    d4 = [y[n:] for y in both]
    tinv = [x + y for x, y in zip(n2, mm(n2, d4))]
    m = RWKV_BASE
    while m < rows:
        s = m.bit_length() - 1
        tb = t >> s
        ib = i >> s
        sib = ((tb >> 1) == (ib >> 1)) & ((tb & 1) == 1) & ((ib & 1) == 0)
        off = [jnp.where(sib, a, 0.0) for a in a_bds]
        nb = n // m

        def odd(x, m=m, nb=nb):
            return jnp.concatenate([x[j * m:(j + 1) * m] for j in range(1, nb, 2)], axis=0)

        def spread(y, m=m, nb=nb):
            zero = jnp.zeros((m, n), F32)
            return jnp.concatenate([p for j in range(nb // 2) for p in (zero, y[j * m:(j + 1) * m])], axis=0)

        upd = mm(mm([odd(x) for x in tinv], off), tinv)
        tinv = [x + spread(y) for x, y in zip(tinv, upd)]
        m *= 2
    return tinv


def _rwkv_chunks(ins, sts, *, rows, t_valid, passes):
    w = ins[0][0].shape[1]
    n = RWKV_GROUP * rows
    if t_valid < rows:
        valid = _iota((rows, 1), 0) < t_valid
        ins = [tuple(jnp.where(valid, x, 0.0) for x in u) for u in ins]
    r, lw, k, v, ah, bh = ([u[j] for u in ins] for j in range(6))
    lrow = _iota((rows, rows), 0)
    lcol = _iota((rows, rows), 1)
    tril = (lcol <= lrow).astype(BF16)
    hs = HD_B.bit_length() - 1
    rs = rows.bit_length() - 1
    bd_mask = (_iota((n, w), 0) >> rs) == (_iota((n, w), 1) >> hs)
    nmask = (_iota((n, n), 0) >> rs) == (_iota((n, n), 1) >> rs)
    t_idx = _iota((rows, 2 * n), 0)
    i_idx = _iota((rows, 2 * n), 1) & (rows - 1)
    strict = i_idx < t_idx
    incl = i_idx <= t_idx

    def expand(x):
        return jnp.where(bd_mask, jnp.concatenate([x] * RWKV_GROUP, axis=0), 0.0)

    cum = [_mm_exact_lhs(tril, x) for x in lw]
    gl = [c[rows - 1:rows] for c in cum]
    e_neg = [jnp.exp(-c) for c in cum]
    e_end = [jnp.exp(g - c) for g, c in zip(gl, cum)]
    qa = [a * jnp.exp(c - l) for a, c, l in zip(ah, cum, lw)]
    qr = [x * jnp.exp(c) for x, c in zip(r, cum)]
    ke = [jnp.concatenate([expand(b * e), expand(x * e)], axis=0) for b, x, e in zip(bh, k, e_neg)]
    kge = [jnp.concatenate([expand(b * e), expand(x * e)], axis=0) for b, x, e in zip(bh, k, e_end)]
    q2 = [jnp.concatenate([a, x], axis=0) for a, x in zip(qa, qr)]
    scores = [_mm(q, x, NT, passes) for q, x in zip(q2, ke)]
    a_a = [jnp.where(strict, s[:rows], 0.0) for s in scores]
    a_r = [jnp.where(incl, s[rows:], 0.0) for s in scores]
    v_bd = [expand(x) for x in v]
    akv = [_mm(jnp.concatenate([a[:, n:], b[:, n:]], axis=0), x, NN, passes)
           for a, b, x in zip(a_a, a_r, v_bd)]
    w_kv = [x[:rows] for x in akv]
    a_bd =[jnp.where(nmask, jnp.concatenate([a[:, :n]] * RWKV_GROUP, axis=0), 0.0) for a in a_a]
    tinv = _tri_inv(a_bd, n, rows, passes)
    carried = [_mm(q, st, NT, passes) for q, st in zip(q2, sts)]
    p1 = [x[:rows] for x in carried]
    y_carry = [x[rows:] for x in carried]
    c_bd = [_mm(t, expand(p + x), NN, passes) for t, p, x in zip(tinv, p1, w_kv)]
    cv = [jnp.concatenate([c, x], axis=0) for c, x in zip(c_bd, v_bd)]
    st_new = [st * jnp.exp(g) + _mm(c, x, TN, passes) for st, g, c, x in zip(sts, gl, cv, kge)]
    ys = [yc + x[rows:] + _mm(a[:, :n], c, NN, passes) for yc, x, a, c in zip(y_carry, akv, a_r, c_bd)]
    return ys, st_new


def _rwkv_scan_kernel(*refs, chunk, n_chunks, t_valid, has_s0, passes, sb):
    if has_s0:
        r_ref, lw_ref, k_ref, v_ref, ah_ref, bh_ref, s0_ref, y_ref, s_ref, st_scr = refs
    else:
        r_ref, lw_ref, k_ref, v_ref, ah_ref, bh_ref, y_ref, s_ref, st_scr = refs
    ti = pl.program_id(1)
    w = RWKV_GROUP * HD_B
    groups = H_B // RWKV_GROUP
    units = [(s, g) for s in range(sb) for g in range(groups)]
    blk = lambda h: slice(h * HD_B, (h + 1) * HD_B)

    @pl.when(ti == 0)
    def _():
        st_scr[...] = jnp.zeros_like(st_scr)
        if has_s0:
            for s, g in units:
                for h in range(RWKV_GROUP):
                    st_scr[s, g, blk(h), blk(h)] = s0_ref[s, g * RWKV_GROUP + h]

    def body(c, carry):
        if n_chunks == 1:
            sl = slice(None)
        else:
            sl = pl.ds(pl.multiple_of(c * chunk, chunk), chunk)
        cols = lambda g: slice(g * w, (g + 1) * w)
        in_refs = (r_ref, lw_ref, k_ref, v_ref, ah_ref, bh_ref)
        ins = [tuple(_pad_rows(ref[s, sl, cols(g)], chunk) for ref in in_refs) for s, g in units]
        ys, st_new = _rwkv_chunks(ins, [st_scr[s, g] for s, g in units],
                                  rows=chunk, t_valid=t_valid, passes=passes)
        for (s, g), y, st in zip(units, ys, st_new):
            st_scr[s, g] = st
            y_ref[s, sl, cols(g)] = y[:t_valid]
        return carry

    if n_chunks == 1:
        body(0, 0)
    else:
        lax.fori_loop(0, n_chunks, body, 0)

    @pl.when(ti == pl.num_programs(1) - 1)
    def _():
        for s, g in units:
            for h in range(RWKV_GROUP):
                s_ref[s, g * RWKV_GROUP + h] = st_scr[s, g, blk(h), blk(h)]


def _rwkv_scan(r, lw, k, v, ah, bh, s0, *, batch, seq, tt, passes, sb):
    has_s0 = s0 is not None
    w = RWKV_GROUP * HD_B
    d_b = H_B * HD_B
    chunk = min(RWKV_CHUNK, max(seq, SUBLANES))
    tt = min(tt, seq)
    n_chunks = max(tt // chunk, 1)
    t_valid = min(chunk, seq)
    tok = pl.BlockSpec((sb, tt, d_b), lambda b, i: (b, i, 0))
    st_spec = pl.BlockSpec((sb, H_B, HD_B, HD_B), lambda b, i: (b, 0, 0, 0))
    args = [r, lw, k, v, ah, bh]
    in_specs = [tok] * 6
    if has_s0:
        args.append(s0)
        in_specs.append(st_spec)
    return pl.pallas_call(
        functools.partial(_rwkv_scan_kernel, chunk=chunk, n_chunks=n_chunks, t_valid=t_valid,
                          has_s0=has_s0, passes=passes, sb=sb),
        out_shape=(jax.ShapeDtypeStruct((batch, seq, d_b), F32),
                   jax.ShapeDtypeStruct((batch, H_B, HD_B, HD_B), F32)),
        grid=(batch // sb, seq // tt),
        in_specs=in_specs,
        out_specs=(tok, st_spec),
        scratch_shapes=[pltpu.VMEM((sb, H_B // RWKV_GROUP, w, w), F32)],
        compiler_params=_params(("arbitrary", "arbitrary")),
        name="rwkv_scan",
    )(*args)


def _merge_kernel(*refs, blocks_a):
    (lnw_ref, lnb_ref, seg_ref, wa_ref, wb_ref, wo_ref, n2_ref, rwt_ref, rb_ref,
     x1_o, xn_o, idx_o, gate_o, cnt_o) = refs[14:]
    from_b = pl.program_id(0) >= blocks_a

    @pl.when(pl.program_id(0) == 0)
    def _():
        cnt_o[...] = jnp.zeros_like(cnt_o)

    def body(group):
        y, bonus, g, oa, ga, gb, x = (ref[...] for ref in group)
        seg = seg_ref[...]
        inv = 1.0 / HD_B
        mu = _seg_sum(y, seg) * inv
        dlt = y - mu
        var = _seg_sum(dlt * dlt, seg) * inv
        yn = dlt * lax.rsqrt(var + RWKV_GN_EPS) * lnw_ref[...] + lnb_ref[...]
        ob = (yn + bonus) * g
        merged = _sigmoid(ga) * _mm(oa, wa_ref[...]) + _sigmoid(gb) * _mm(ob, wb_ref[...])
        x1 = x + _mm(merged, wo_ref[...])
        x1_o[...] = x1
        xn = x1 * lax.rsqrt(jnp.mean(x1 * x1, axis=-1, keepdims=True) + EPS) * n2_ref[...]
        _tok_store(xn_o, xn)
        logits = _mm(rwt_ref[...], xn, NT, 3) + rb_ref[...]
        tm = x1.shape[0]
        eio = _iota((N_EXPERTS, tm), 0)
        idx_rows, val_rows = [], []
        cnt = jnp.zeros((N_EXPERTS, 1), F32)
        for _ in range(TOP_K):
            mx = jnp.max(logits, axis=0, keepdims=True)
            pick = jnp.min(jnp.where(logits == mx, eio, N_EXPERTS), axis=0, keepdims=True)
            hit = eio == pick
            cnt = cnt + jnp.sum(hit.astype(F32), axis=1, keepdims=True)
            logits = jnp.where(hit, -jnp.inf, logits)
            idx_rows.append(pick)
            val_rows.append(mx)
        ex = [jnp.exp(vr - val_rows[0]) for vr in val_rows]
        den = ex[0] + ex[1] + ex[2] + ex[3]
        idx_o[...] = _stack_rows(idx_rows)
        gate_o[...] = _stack_rows([e / den for e in ex])
        cnt_o[...] += jnp.broadcast_to(cnt, cnt_o.shape)

    pl.when(jnp.logical_not(from_b))(lambda: body(refs[0:7]))
    pl.when(from_b)(lambda: body(refs[7:14]))


def _merge(tok_a, tok_b, p, *, tm):
    n_a, d = tok_a[5].shape
    n_b = tok_b[5].shape[0]
    n = n_a + n_b
    blocks_a = n_a // tm
    last_a = blocks_a - 1
    ia = lambda i: jnp.minimum(i, last_a)
    ib = lambda i: jnp.maximum(i - blocks_a, 0)

    def group_specs(tok, im, mode=None):
        y, _, _, oa, gates, _ = tok
        dm = gates.shape[1] // 2
        spec = lambda c, j: pl.BlockSpec((tm, c), lambda i: (im(i), j), pipeline_mode=mode)
        return [spec(y.shape[1], 0), spec(y.shape[1], 0), spec(y.shape[1], 0), spec(oa.shape[1], 0),
                spec(dm, 0), spec(dm, 1), spec(d, 0)]

    def group_args(tok):
        y, bonus, g, oa, gates, x = tok
        return [y, bonus, g, oa, gates, gates, x]

    full = lambda a: pl.BlockSpec(a.shape, lambda i: (0, 0))
    small = [p["ln_w"], p["ln_b"], p["seg"], p["wa"], p["wb"], p["wo"], p["n2"], p["rwt"], p["rb"]]
    out_row = pl.BlockSpec((tm, d), lambda i: (i, 0))
    out_col = pl.BlockSpec((SUBLANES, tm), lambda i: (0, i))
    return pl.pallas_call(
        functools.partial(_merge_kernel, blocks_a=blocks_a),
        out_shape=(jax.ShapeDtypeStruct((n, d), F32), jax.ShapeDtypeStruct((n * SUBLANES, LANES), F32),
                   jax.ShapeDtypeStruct((SUBLANES, n), jnp.int32),
                   jax.ShapeDtypeStruct((SUBLANES, n), F32),
                   jax.ShapeDtypeStruct((N_EXPERTS, LANES), F32)),
        grid=(n // tm,),
        in_specs=(group_specs(tok_a, ia) + group_specs(tok_b, ib, pl.Buffered(1))
                  + [full(a) for a in small]),
        out_specs=(out_row, pl.BlockSpec((tm * SUBLANES, LANES), lambda i: (i, 0)), out_col, out_col,
                   pl.BlockSpec((N_EXPERTS, LANES), lambda i: (0, 0))),
        compiler_params=_params(("arbitrary",)),
        name="merge_router",
    )(*(group_args(tok_a) + group_args(tok_b) + small))


def _rank_kernel(idx_ref, pstart_ref, dest_o, carry):
    i = pl.program_id(0)

    @pl.when(i == 0)
    def _():
        carry[...] = pstart_ref[...]

    tm = idx_ref.shape[1]
    eio = _iota((N_EXPERTS, tm), 0)
    idx = idx_ref[...]
    hits = [eio == idx[j:j + 1, :] for j in range(TOP_K)]
    sel = hits[0] | hits[1] | hits[2] | hits[3]
    upper = (_iota((tm, tm), 0) < _iota((tm, tm), 1)).astype(BF16)
    rank = _dg(sel.astype(BF16), upper, NN) + carry[:, 0:1]
    rows = [jnp.sum(jnp.where(h, rank, 0.0), axis=0, keepdims=True) for h in hits]
    dest_o[...] = _stack_rows(rows).astype(jnp.int32)
    carry[...] += jnp.sum(sel.astype(F32), axis=1, keepdims=True)


def _rank(idx4, pstart, *, tm):
    n = idx4.shape[1]
    return pl.pallas_call(
        _rank_kernel,
        out_shape=jax.ShapeDtypeStruct((SUBLANES, n), jnp.int32),
        grid=(n // tm,),
        in_specs=[pl.BlockSpec((SUBLANES, tm), lambda i: (0, i)),
                  pl.BlockSpec((N_EXPERTS, LANES), lambda i: (0, 0))],
        out_specs=pl.BlockSpec((SUBLANES, tm), lambda i: (0, i)),
        scratch_shapes=[pltpu.VMEM((N_EXPERTS, LANES), F32)],
        compiler_params=_params(("arbitrary",)),
        name="moe_rank",
    )(idx4, pstart)


def _scatter_kernel(pad_lo, pad_hi, na_ref, dest_ref, x_ref, xs_hbm, zeros, sem, zsem):
    tm = x_ref.shape[0]
    zrows = zeros.shape[0]
    nblk = xs_hbm.shape[0] // zrows

    @pl.when(pl.program_id(0) == 0)
    def _():
        zeros[...] = jnp.zeros_like(zeros)
        zrun = lambda r, size: pltpu.make_async_copy(zeros.at[pl.ds(0, size)], xs_hbm.at[pl.ds(r, size)], zsem)

        def per_expert(op):
            def f(e, c):
                lo = pad_lo[e]
                n_pad = pad_hi[e] - lo
                for bit in reversed(range((zrows - 1).bit_length())):
                    @pl.when(((n_pad >> bit) & 1) == 1)
                    def _(lo=lo, bit=bit):
                        op(zrun(lo, 1 << bit))
                    lo = lo + (n_pad & (1 << bit))
                return c
            return f

        lax.fori_loop(0, N_EXPERTS, per_expert(lambda cp: cp.start()), 0)
        lax.fori_loop(na_ref[0], nblk, lambda b, c: (zrun(b * zrows, zrows).start(), c)[1], 0)
        lax.fori_loop(0, N_EXPERTS, per_expert(lambda cp: cp.wait()), 0)
        lax.fori_loop(na_ref[0], nblk, lambda b, c: (zrun(b * zrows, zrows).wait(), c)[1], 0)

    def issue(t, c):
        for j in range(TOP_K):
            pltpu.make_async_copy(x_ref.at[t], xs_hbm.at[dest_ref[j, t]], sem).start(priority=j % 2)
        return c

    lax.fori_loop(0, tm, issue, 0, unroll=DMA_UNROLL)
    for j in range(TOP_K):
        pltpu.make_async_copy(x_ref, xs_hbm.at[pl.ds(0, tm)], sem).wait()


def _scatter(pad_lo, pad_hi, n_active, dest4, x3, m_total, *, tm):
    n = x3.shape[0]
    tail = x3.shape[1:]
    return pl.pallas_call(
        _scatter_kernel,
        out_shape=jax.ShapeDtypeStruct((m_total,) + tail, x3.dtype),
        grid_spec=pltpu.PrefetchScalarGridSpec(
            num_scalar_prefetch=3,
            grid=(n // tm,),
            in_specs=[pl.BlockSpec((SUBLANES, tm), lambda i, *_: (0, i), memory_space=pltpu.SMEM),
                      pl.BlockSpec((tm,) + tail, lambda i, *_: (i, 0, 0))],
            out_specs=pl.BlockSpec(memory_space=pl.ANY),
            scratch_shapes=[pltpu.VMEM((MOE_ROWS,) + tail, x3.dtype),
                            pltpu.SemaphoreType.DMA, pltpu.SemaphoreType.DMA]),
        compiler_params=_params(("arbitrary",)),
        name="moe_scatter",
    )(pad_lo, pad_hi, n_active, dest4, x3)


def _expert_kernel(be_ref, na_ref, x_ref, wgu_ref, bgu_ref, wd_ref, bd_ref, y_ref, wgu_bf, wd_perm, wd_bf):
    b = pl.program_id(0)
    active = b < na_ref[0]
    half = LANES // 2
    dff = wd_ref.shape[1]

    @pl.when(active & ((b == 0) | (be_ref[b] != be_ref[jnp.maximum(b - 1, 0)])))
    def _():
        wgu_bf[...] = wgu_ref[0].astype(BF16)
        for c in range(wd_ref.shape[2] // LANES):
            cols = slice(c * LANES, (c + 1) * LANES)
            for p in range(dff // LANES):
                r0 = p * LANES
                wd_perm[pl.ds(r0, half, stride=2), :] = wd_ref[0, r0:r0 + half, cols]
                wd_perm[pl.ds(r0 + 1, half, stride=2), :] = wd_ref[0, r0 + half:r0 + LANES, cols]
            wd_bf[:, cols] = wd_perm[...].astype(BF16)

    @pl.when(active)
    def _():
        x = _tok_load(x_ref).astype(BF16)
        h = _dg(x, wgu_bf[...], NN) + bgu_ref[0]
        even = (_iota((1, LANES), 1) & 1) == 0
        acts = []
        for p in range(dff // LANES):
            ha = h[:, 2 * p * LANES:(2 * p + 1) * LANES]
            hb = h[:, (2 * p + 1) * LANES:(2 * p + 2) * LANES]
            hg = jnp.where(even, ha, pltpu.roll(hb, 1, 1))
            hl = jnp.where(even, pltpu.roll(ha, LANES - 1, 1), hb)
            glu = jnp.minimum(hg, SWIGLU_LIMIT)
            lin = jnp.clip(hl, -SWIGLU_LIMIT, SWIGLU_LIMIT)
            acts.append((glu * _sigmoid(SWIGLU_ALPHA * glu) * (lin + 1.0)).astype(BF16))
        act = jnp.concatenate(acts, axis=1)
        _tok_store(y_ref, _dg(act, wd_bf[...], NN) + bd_ref[0])

    @pl.when(jnp.logical_not(active))
    def _():
        y_ref[...] = jnp.zeros_like(y_ref)


def _experts(block_expert, n_active, xs2, w_gu, b_gu, w_down, b_down):
    d = w_gu.shape[1]
    dff = w_down.shape[1]
    per_tok = d // LANES
    m_total = xs2.shape[0] // per_tok
    nblk = m_total // MOE_ROWS
    wspec = lambda r, c: pl.BlockSpec((1, r, c), lambda b, be, na: (be[b], 0, 0))
    tok = pl.BlockSpec((MOE_ROWS * per_tok, LANES), lambda b, be, na: (b, 0))
    return pl.pallas_call(
        _expert_kernel,
        out_shape=jax.ShapeDtypeStruct(xs2.shape, F32),
        grid_spec=pltpu.PrefetchScalarGridSpec(
            num_scalar_prefetch=2,
            grid=(nblk,),
            in_specs=[tok, wspec(d, 2 * dff), wspec(1, 2 * dff), wspec(dff, d), wspec(1, d)],
            out_specs=tok,
            scratch_shapes=[pltpu.VMEM((d, 2 * dff), BF16), pltpu.VMEM((dff, LANES), F32),
                            pltpu.VMEM((dff, d), BF16)]),
        compiler_params=_params(("arbitrary",)),
        name="moe_experts",
    )(block_expert, n_active, xs2, w_gu, b_gu, w_down, b_down)


def _combine_kernel(dest_ref, next_ref, gate_ref, ys_hbm, x_ref, w_ref, oa_ref, ob_ref, buf, mixed, sems,
                    *, blocks_a):
    tm = x_ref.shape[0]
    i = pl.program_id(0)
    slot = i % 2

    def issue(idx_ref, s):
        def f(t, c):
            for j in range(TOP_K):
                pltpu.make_async_copy(ys_hbm.at[idx_ref[j, t]], buf.at[s, j, t], sems.at[s]).start(priority=j % 2)
            return c
        lax.fori_loop(0, tm, f, 0, unroll=DMA_UNROLL)

    @pl.when(i == 0)
    def _():
        issue(dest_ref, slot)

    @pl.when(i + 1 < pl.num_programs(0))
    def _():
        issue(next_ref, 1 - slot)

    for j in range(TOP_K):
        pltpu.make_async_copy(ys_hbm.at[pl.ds(0, tm)], buf.at[slot, j], sems.at[slot]).wait()

    def mix(t, c):
        acc = gate_ref[0, t] * buf[slot, 0, t]
        for j in range(1, TOP_K):
            acc = acc + gate_ref[j, t] * buf[slot, j, t]
        mixed[pl.ds(pl.multiple_of(t * SUBLANES, SUBLANES), SUBLANES), :] = acc
        return c

    lax.fori_loop(0, tm, mix, 0, unroll=DMA_UNROLL)
    x = x_ref[...] + _tok_load(mixed)
    y = x * lax.rsqrt(jnp.mean(x * x, axis=-1, keepdims=True) + EPS) * w_ref[...]
    first = pl.program_id(0) < blocks_a

    @pl.when(first)
    def _():
        oa_ref[...] = y

    @pl.when(jnp.logical_not(first))
    def _():
        ob_ref[...] = y


def _combine(dest4, gate4, ys3, x1, w, n_a, *, tm):
    n, d = x1.shape
    tail = ys3.shape[1:]
    blocks_a = n_a // tm
    last = n // tm - 1
    smem = lambda im: pl.BlockSpec((SUBLANES, tm), lambda i: (0, im(i)), memory_space=pltpu.SMEM)
    return pl.pallas_call(
        functools.partial(_combine_kernel, blocks_a=blocks_a),
        out_shape=(jax.ShapeDtypeStruct((n_a, d), F32), jax.ShapeDtypeStruct((n - n_a, d), F32)),
        grid=(n // tm,),
        in_specs=[smem(lambda i: i), smem(lambda i: jnp.minimum(i + 1, last)), smem(lambda i: i),
                  pl.BlockSpec(memory_space=pl.ANY),
                  pl.BlockSpec((tm, d), lambda i: (i, 0)), pl.BlockSpec((1, d), lambda i: (0, 0))],
        out_specs=(pl.BlockSpec((tm, d), lambda i: (jnp.minimum(i, blocks_a - 1), 0)),
                   pl.BlockSpec((tm, d), lambda i: (jnp.maximum(i - blocks_a, 0), 0))),
        scratch_shapes=[pltpu.VMEM((2, TOP_K, tm) + tail, F32), pltpu.VMEM((tm * SUBLANES, LANES), F32),
                        pltpu.SemaphoreType.DMA((2,))],
        compiler_params=_params(("arbitrary",)),
        name="moe_combine",
    )(dest4, dest4, gate4, ys3, x1, w)


def _mixers(x, s_hgrn, s_rwkv, shift_prev, wts):
    batch, seq, d = x.shape
    n = batch * seq
    x2 = x.reshape(n, d)
    tm = min(PROJ_ROWS, n)
    seqs_per_step = LONG_SEQS_PER_STEP if seq >= HGRN_CHUNK else SHORT_SEQS_PER_STEP
    ph = _norm_proj(x2, wts["n1"], wts["w_h"], normalize=True, tm=tm)
    pg = _norm_proj(x2, wts["n1"], wts["w_g"], normalize=True, tm=tm)
    oa, s_hgrn_new = _hgrn(ph.reshape(batch, seq, -1), wts["lb"], wts["gw"], s_hgrn,
                           batch=batch, seq=seq, tt=HGRN_STEP_TOKENS, sb=seqs_per_step)
    first = None
    if shift_prev is not None:
        prev = _norm_proj(shift_prev, wts["n1"], wts["w_rw"], normalize=False, tm=shift_prev.shape[0])
        first = jnp.repeat(prev, seq, axis=0)
    r, lw, k, v, ah, bh, bonus, g = _rwkv_prep(x2, first, wts, seq=seq, tm=min(PROJ_ROWS, n))
    to3 = lambda a: a.reshape(batch, seq, -1)
    y, s_rwkv_new = _rwkv_scan(to3(r), to3(lw), to3(k), to3(v), to3(ah), to3(bh), s_rwkv,
                               batch=batch, seq=seq, tt=RWKV_STEP_TOKENS, passes=RWKV_PASSES, sb=seqs_per_step)
    return (y.reshape(n, -1), bonus, g, oa.reshape(n, -1), pg, x2), s_hgrn_new, s_rwkv_new


def kernel(x_prompt, x_sample, state_hgrn, state_rwkv, state_shift, norm1_w, w_in, hgrn_lb_logits,
           hgrn_gnorm_w, rwkv_mu, rwkv_w0, rwkv_w2, rwkv_a0, rwkv_a2, rwkv_g2, rwkv_k_k, rwkv_k_a,
           rwkv_r_k, rwkv_ln_w, rwkv_ln_b, w_branch_a, w_branch_b, w_out, norm2_w, router_w, router_b,
           expert_w_gu, expert_b_gu, expert_w_down, expert_b_down, final_norm_w):
    bp, tp, d = x_prompt.shape
    bs, ts, _ = x_sample.shape
    n_p, n_s = bp * tp, bs * ts
    n = n_p + n_s
    d_a = H_A * DK_A
    d_b = H_B * HD_B
    rw_start = 4 * d_a
    gate_start = rw_start + 3 * d_b + LORA_W + LORA_A + LORA_G
    row2 = lambda a: a.reshape(1, -1)
    lb = jnp.cumsum(jax.nn.softmax(hgrn_lb_logits.astype(F32), axis=0), axis=0)[0]
    hid = jnp.arange(RWKV_GROUP * HD_B, dtype=jnp.int32) // HD_B
    wts = {
        "n1": row2(norm1_w[0]),
        "w_h": w_in[0][:, :rw_start].astype(BF16),
        "w_rw": w_in[0][:, rw_start:gate_start].astype(BF16),
        "w_g": w_in[0][:, gate_start:].astype(BF16),
        "lb": row2(lb), "gw": row2(hgrn_gnorm_w[0]),
        "mu": row2(rwkv_mu[0]), "w0": row2(rwkv_w0[0]), "w2": rwkv_w2[0], "a0": row2(rwkv_a0[0]),
        "a2": rwkv_a2[0], "g2": rwkv_g2[0], "k_k": row2(rwkv_k_k[0]), "k_a": row2(rwkv_k_a[0]),
        "r_k": row2(rwkv_r_k[0]), "seg": (hid[:, None] == hid[None, :]).astype(BF16),
        "ln_w": row2(rwkv_ln_w[0]), "ln_b": row2(rwkv_ln_b[0]),
        "wa": w_branch_a[0].astype(BF16), "wb": w_branch_b[0].astype(BF16), "wo": w_out[0].astype(BF16),
        "n2": row2(norm2_w[0]), "rwt": router_w[0].T, "rb": router_b[0].reshape(-1, 1),
    }
    tok_p, hgrn_p, rwkv_p = _mixers(x_prompt, None, None, None, wts)
    tok_s, hgrn_s, rwkv_s = _mixers(x_sample, state_hgrn[0], state_rwkv[0], state_shift[0], wts)
    shift = _rms_rows(jnp.concatenate([x_prompt[:, -1], x_sample[:, -1]], axis=0), wts["n1"])

    x1, xn2, idx4, gate4, counts = _merge(tok_p, tok_s, wts, tm=TOKEN_TILE)

    cnt = counts[:, 0].astype(jnp.int32)
    padded = (cnt + MOE_ROWS - 1) // MOE_ROWS * MOE_ROWS
    pend = jnp.cumsum(padded)
    pstart = pend - padded
    nblk = (n * TOP_K + N_EXPERTS * (MOE_ROWS - 1) + MOE_ROWS - 1) // MOE_ROWS
    m_total = nblk * MOE_ROWS
    block_row = jnp.arange(nblk, dtype=jnp.int32)[:, None] * MOE_ROWS
    block_expert = jnp.minimum(jnp.sum((pend[None, :] <= block_row).astype(jnp.int32), axis=1), N_EXPERTS - 1)
    n_active = (pend[-1:] // MOE_ROWS).astype(jnp.int32)
    dest4 = _rank(idx4, jnp.broadcast_to(pstart.astype(F32)[:, None], (N_EXPERTS, LANES)), tm=TOKEN_TILE)

    per_tok = d // LANES
    xs = _scatter(pstart + cnt, pend, n_active, dest4, xn2.reshape(n, per_tok, LANES), m_total, tm=TOKEN_TILE)
    ys = _experts(block_expert, n_active, xs.reshape(m_total * per_tok, LANES), expert_w_gu[0],
                  expert_b_gu[0][:, None, :], expert_w_down[0], expert_b_down[0][:, None, :])
    y_p, y_s = _combine(dest4, gate4, ys.reshape(m_total, per_tok, LANES), x1, row2(final_norm_w), n_p, tm=TOKEN_TILE)

    return (y_p.reshape(bp, tp, d), y_s.reshape(bs, ts, d),
            hgrn_p[None], rwkv_p[None], shift[None, :bp],
            hgrn_s[None], rwkv_s[None], shift[None, bp:])
```
